```python
import math
import jax
import jax.numpy as jnp
from jax import lax
import numpy as np

D_MODEL = 1024
BATCH = 4
SEQ = 8192
DEPTH = 2
DEC_BATCH = 32
DEC_SEQ = 4
PAST_LEN = 16384
PAGE_SIZE = 128

N_AB_LAYERS = (DEPTH + 1) // 2
N_C_LAYERS = DEPTH // 2
MIX_WIDTH = D_MODEL
POOL_WIDTH = MIX_WIDTH // 2
POOL_WINDOWS = (2, 4, 8, 16)
POOL_GROUP = POOL_WIDTH // len(POOL_WINDOWS)
POOL_STATE = max(POOL_WINDOWS) - 1
SSM_WIDTH = MIX_WIDTH - POOL_WIDTH
SSM_GROUP = 16
SSM_GROUPS = SSM_WIDTH // SSM_GROUP
SSM_STATE = 64
SSM_CHUNK = 128
ATT_HEADS = 16
HEAD_DIM = 64
ATT_WIDTH = ATT_HEADS * HEAD_DIM
DILATED_PATTERNS = ((128, 1), (512, 4), (2048, 16))
MAX_WINDOW = max(w for w, _ in DILATED_PATTERNS)
ATT_BLOCK = 128
FFN_HIDDEN = -(-(8 * D_MODEL) // (3 * 256)) * 256
RMS_EPS = 1e-6

kernel_name = 'hybrid_pool_s5_dilated_attn_step'


def rms_norm(x, g):
    xf = x.astype(jnp.float32)
    y = xf * lax.rsqrt(jnp.mean(xf * xf, axis=-1, keepdims=True) + RMS_EPS)
    return (y * g.astype(jnp.float32)).astype(x.dtype)


def swiglu_ffn(x, w_gate, w_up, w_down):
    return (jax.nn.silu(x @ w_gate) * (x @ w_up)) @ w_down


def pool_mixer(u, prev, pos, w_grp, scale):
    b, t, c = u.shape
    ext = jnp.concatenate([prev.astype(u.dtype), u], axis=1)
    cs = jnp.pad(jnp.cumsum(ext.astype(jnp.float32), axis=1), ((0, 0), (1, 0), (0, 0)))
    uf = u.astype(jnp.float32)
    lo = POOL_STATE + 1
    groups = []
    for g, w in enumerate(POOL_WINDOWS):
        sl = slice(g * POOL_GROUP, (g + 1) * POOL_GROUP)
        win_sum = cs[:, lo:lo + t, sl] - cs[:, lo - w:lo - w + t, sl]
        count = jnp.minimum(pos + 1, w).astype(jnp.float32)[None, :, None]
        groups.append(win_sum / count - uf[..., sl])
    d = jnp.stack(groups, axis=2)
    y = jnp.einsum('btgc,gce->btge', d, w_grp.astype(jnp.float32)).reshape(b, t, c)
    return y * scale.astype(jnp.float32), ext[:, -POOL_STATE:]


def _linear_recurrence_combine(e1, e2):
    a1, b1 = e1
    a2, b2 = e2
    return a1 * a2, a2 * b1 + b2


def s5_mixer(u, h0, lam_re, lam_im, log_dt, b_re, b_im, c_re, c_im, d_skip):
    b, t, c = u.shape
    f32 = jnp.float32
    lam = lax.complex(lam_re.astype(f32), lam_im.astype(f32))
    dt = jnp.exp(log_dt.astype(f32))[:, None]
    lam_bar = jnp.exp(lam * dt)
    b_mat = lax.complex(b_re.astype(f32), b_im.astype(f32))
    b_bar = ((lam_bar - 1.0) / lam)[..., None] * b_mat
    c_mat = lax.complex(c_re.astype(f32), c_im.astype(f32))
    uf = u.astype(f32)
    chunk = math.gcd(t, SSM_CHUNK)
    u_chunks = jnp.moveaxis(uf.reshape(b, t // chunk, chunk, SSM_GROUPS, SSM_GROUP), 1, 0)

    def step(h, u_c):
        bu = jnp.einsum('gpi,bcgi->bcgp', b_bar, u_c.astype(jnp.complex64))
        bu = bu.at[:, 0].add(lam_bar * h)
        a = jnp.broadcast_to(lam_bar, bu.shape)
        _, hs = lax.associative_scan(_linear_recurrence_combine, (a, bu), axis=1)
        y_c = jnp.einsum('gip,bcgp->bcgi', c_mat, hs).real
        return hs[:, -1], y_c

    h_last, y_chunks = lax.scan(step, h0, u_chunks)
    y = jnp.moveaxis(y_chunks, 0, 1).reshape(b, t, c) + d_skip.astype(f32) * uf
    return y, h_last


def mixer_ab(xn, pool_prev, h0, pos, w_in, pool_w, pool_scale, lam_re, lam_im, log_dt,
             b_re, b_im, c_re, c_im, d_skip, w_glu, b_glu, w_out):
    proj = xn @ w_in
    u_pool, u_ssm = proj[..., :POOL_WIDTH], proj[..., POOL_WIDTH:]
    y_pool, pool_state = pool_mixer(u_pool, pool_prev, pos, pool_w, pool_scale)
    y_ssm, h_last = s5_mixer(u_ssm, h0, lam_re, lam_im, log_dt, b_re, b_im, c_re, c_im, d_skip)
    z = jax.nn.gelu(y_ssm)
    y_ssm = z * jax.nn.sigmoid(z @ w_glu.astype(jnp.float32) + b_glu.astype(jnp.float32))
    y = jnp.concatenate([y_pool, y_ssm], axis=-1).astype(xn.dtype) @ w_out
    return y, pool_state, h_last


def _softmax_stats(sc, valid):
    sc = jnp.where(valid, sc, -jnp.inf)
    m = jnp.max(sc, axis=-1, keepdims=True)
    p = jnp.exp(sc - m)
    l = jnp.sum(p, axis=-1)
    return p, l, m[..., 0] + jnp.log(l)


def _merge_patterns(outs, lses):
    w = jax.nn.softmax(jnp.stack(lses), axis=0)
    return jnp.sum(w[..., None] * jnp.stack(outs), axis=0)


def _to_strided(a, dil):
    b, s = a.shape[:2]
    a = a.reshape((b, s // dil, dil) + a.shape[2:])
    a = jnp.moveaxis(a, 2, 1)
    return a.reshape((b * dil, s // dil) + a.shape[3:])


def _from_strided(a, b, dil):
    n, l = a.shape[:2]
    a = a.reshape((b, dil, l) + a.shape[2:])
    a = jnp.moveaxis(a, 1, 2)
    return a.reshape((b, l * dil) + a.shape[3:])


def _split_qkv(xn, w_qkv):
    b, t, _ = xn.shape
    qkv = (xn @ w_qkv).reshape(b, t, 3, ATT_HEADS, HEAD_DIM)
    return qkv[:, :, 0], qkv[:, :, 1], qkv[:, :, 2]


def dilated_branch_prompt(q, k, v, window, dil):
    b, s, h, e = q.shape
    span = window // dil
    qs, ks, vs = _to_strided(q, dil), _to_strided(k, dil), _to_strided(v, dil)
    l_sub = qs.shape[1]
    nb = -(-l_sub // ATT_BLOCK)
    extra = nb * ATT_BLOCK - l_sub
    qs = jnp.pad(qs, ((0, 0), (0, extra), (0, 0), (0, 0)))
    ks = jnp.pad(ks, ((0, 0), (ATT_BLOCK, extra), (0, 0), (0, 0)))
    vs = jnp.pad(vs, ((0, 0), (ATT_BLOCK, extra), (0, 0), (0, 0)))
    qi = jnp.arange(ATT_BLOCK)[:, None]
    kj = jnp.arange(2 * ATT_BLOCK)[None, :]
    dist = ATT_BLOCK + qi - kj
    band = (dist >= 0) & (dist <= span)
    scale = HEAD_DIM ** -0.5

    def one_block(blk):
        start = blk * ATT_BLOCK
        qb = lax.dynamic_slice_in_dim(qs, start, ATT_BLOCK, axis=1)
        kb = lax.dynamic_slice_in_dim(ks, start, 2 * ATT_BLOCK, axis=1)
        vb = lax.dynamic_slice_in_dim(vs, start, 2 * ATT_BLOCK, axis=1)
        valid = band & (start - ATT_BLOCK + kj >= 0)
        sc = jnp.einsum('nqhe,nkhe->nhqk', qb, kb, preferred_element_type=jnp.float32) * scale
        p, den, lse = _softmax_stats(sc, valid)
        o = jnp.einsum('nhqk,nkhe->nqhe', p, vb.astype(jnp.float32)) / jnp.swapaxes(den, 1, 2)[..., None]
        return o, jnp.swapaxes(lse, 1, 2)

    o_blk, lse_blk = lax.map(one_block, jnp.arange(nb))
    n = qs.shape[0]
    o = jnp.moveaxis(o_blk, 0, 1).reshape(n, nb * ATT_BLOCK, h, e)[:, :l_sub]
    lse = jnp.moveaxis(lse_blk, 0, 1).reshape(n, nb * ATT_BLOCK, h)[:, :l_sub]
    return _from_strided(o, b, dil), _from_strided(lse, b, dil)


def mixer_c_prompt(xn, w_qkv, w_o):
    q, k, v = _split_qkv(xn, w_qkv)
    outs, lses = [], []
    for window, dil in DILATED_PATTERNS:
        o, lse = dilated_branch_prompt(q, k, v, window, dil)
        outs.append(o)
        lses.append(lse)
    b, t = xn.shape[:2]
    y = _merge_patterns(outs, lses).reshape(b, t, ATT_WIDTH).astype(xn.dtype) @ w_o
    keep = min(MAX_WINDOW, t)
    return y, k[:, -keep:], v[:, -keep:]


def mixer_c_sample(xn, cache_k, cache_v, w_qkv, w_o):
    q, k, v = _split_qkv(xn, w_qkv)
    b, t = xn.shape[:2]
    buf = cache_k.shape[1]
    kk = jnp.concatenate([cache_k, k.astype(cache_k.dtype)], axis=1)
    vv = jnp.concatenate([cache_v, v.astype(cache_v.dtype)], axis=1)
    scale = HEAD_DIM ** -0.5
    outs, lses = [], []
    for window, dil in DILATED_PATTERNS:
        span = window // dil
        idx = buf + jnp.arange(t)[:, None] - dil * jnp.arange(span + 1)[None, :]
        valid = idx >= 0
        idx = jnp.maximum(idx, 0)
        kg, vg = kk[:, idx], vv[:, idx]
        sc = jnp.einsum('bthe,btjhe->bhtj', q, kg, preferred_element_type=jnp.float32) * scale
        p, den, lse = _softmax_stats(sc, valid[None, None])
        o = jnp.einsum('bhtj,btjhe->bthe', p, vg.astype(jnp.float32)) / jnp.swapaxes(den, 1, 2)[..., None]
        outs.append(o)
        lses.append(jnp.swapaxes(lse, 1, 2))
    y = _merge_patterns(outs, lses).reshape(b, t, ATT_WIDTH).astype(xn.dtype) @ w_o
    return y, k, v


def setup_inputs(seed: int = 0) -> dict:
    key = jax.random.key(seed)
    ks = jax.random.split(key, 26)
    f32 = jnp.float32

    def nrm(k, shape, scale):
        return jax.random.normal(k, shape, f32) * scale

    buf = min(MAX_WINDOW, PAST_LEN)
    return {
        'x_prompt': nrm(ks[0], (BATCH, SEQ, D_MODEL), 1.0),
        'x_sample': nrm(ks[1], (DEC_BATCH, DEC_SEQ, D_MODEL), 1.0),
        'state_pool': nrm(ks[2], (N_AB_LAYERS, DEC_BATCH, POOL_STATE, POOL_WIDTH), 1.0),
        'state_s5': nrm(ks[3], (N_AB_LAYERS, DEC_BATCH, SSM_GROUPS, SSM_STATE, 2), 0.1),
        'cache_k': nrm(ks[4], (N_C_LAYERS, DEC_BATCH, buf, ATT_HEADS, HEAD_DIM), 1.0),
        'cache_v': nrm(ks[5], (N_C_LAYERS, DEC_BATCH, buf, ATT_HEADS, HEAD_DIM), 1.0),
        'norm_gains': 1.0 + nrm(ks[6], (DEPTH, 4, D_MODEL), 0.1),
        'ab_w_in': nrm(ks[7], (N_AB_LAYERS, D_MODEL, MIX_WIDTH), D_MODEL ** -0.5),
        'ab_pool_w': nrm(ks[8], (N_AB_LAYERS, len(POOL_WINDOWS), POOL_GROUP, POOL_GROUP), POOL_GROUP ** -0.5),
        'ab_pool_scale': 1.0 + nrm(ks[9], (N_AB_LAYERS, POOL_WIDTH), 0.1),
        'ab_lambda_re': -0.5 + nrm(ks[10], (N_AB_LAYERS, SSM_GROUPS, SSM_STATE), 0.01),
        'ab_lambda_im': math.pi * jnp.arange(SSM_STATE, dtype=f32) + nrm(ks[11], (N_AB_LAYERS, SSM_GROUPS, SSM_STATE), 0.01),
        'ab_log_dt': jax.random.uniform(ks[12], (N_AB_LAYERS, SSM_GROUPS), f32, math.log(1e-3), math.log(1e-1)),
        'ab_b_re': nrm(ks[13], (N_AB_LAYERS, SSM_GROUPS, SSM_STATE, SSM_GROUP), (2 * SSM_GROUP) ** -0.5),
        'ab_b_im': nrm(ks[14], (N_AB_LAYERS, SSM_GROUPS, SSM_STATE, SSM_GROUP), (2 * SSM_GROUP) ** -0.5),
        'ab_c_re': nrm(ks[15], (N_AB_LAYERS, SSM_GROUPS, SSM_GROUP, SSM_STATE), SSM_STATE ** -0.5),
        'ab_c_im': nrm(ks[16], (N_AB_LAYERS, SSM_GROUPS, SSM_GROUP, SSM_STATE), SSM_STATE ** -0.5),
        'ab_d': nrm(ks[17], (N_AB_LAYERS, SSM_WIDTH), 1.0),
        'ab_w_glu': nrm(ks[18], (N_AB_LAYERS, SSM_WIDTH, SSM_WIDTH), SSM_WIDTH ** -0.5),
        'ab_b_glu': nrm(ks[19], (N_AB_LAYERS, SSM_WIDTH), 0.01),
        'ab_w_out': nrm(ks[20], (N_AB_LAYERS, MIX_WIDTH, D_MODEL), MIX_WIDTH ** -0.5),
        'c_w_qkv': nrm(ks[21], (N_C_LAYERS, D_MODEL, 3 * ATT_WIDTH), D_MODEL ** -0.5),
        'c_w_o': nrm(ks[22], (N_C_LAYERS, ATT_WIDTH, D_MODEL), ATT_WIDTH ** -0.5),
        'ffn_w_gate': nrm(ks[23], (DEPTH, D_MODEL, FFN_HIDDEN), D_MODEL ** -0.5),
        'ffn_w_up': nrm(ks[24], (DEPTH, D_MODEL, FFN_HIDDEN), D_MODEL ** -0.5),
        'ffn_w_down': nrm(ks[25], (DEPTH, FFN_HIDDEN, D_MODEL), FFN_HIDDEN ** -0.5),
    }


def reference(x_prompt, x_sample, state_pool, state_s5, cache_k, cache_v, norm_gains,
              ab_w_in, ab_pool_w, ab_pool_scale, ab_lambda_re, ab_lambda_im, ab_log_dt,
              ab_b_re, ab_b_im, ab_c_re, ab_c_im, ab_d, ab_w_glu, ab_b_glu, ab_w_out,
              c_w_qkv, c_w_o, ffn_w_gate, ffn_w_up, ffn_w_down):
    hp, hs = x_prompt, x_sample
    pos_p = jnp.arange(hp.shape[1])
    pos_s = PAST_LEN + jnp.arange(hs.shape[1])
    pool_p_list, s5_p_list, k_p_list, v_p_list = [], [], [], []
    pool_s_list, s5_s_list, k_s_list, v_s_list = [], [], [], []
    for layer in range(DEPTH):
        i = layer // 2
        g = norm_gains[layer]
        xp, xs = rms_norm(hp, g[0]), rms_norm(hs, g[0])
        if layer % 2 == 0:
            ab = (ab_w_in[i], ab_pool_w[i], ab_pool_scale[i], ab_lambda_re[i], ab_lambda_im[i],
                  ab_log_dt[i], ab_b_re[i], ab_b_im[i], ab_c_re[i], ab_c_im[i], ab_d[i],
                  ab_w_glu[i], ab_b_glu[i], ab_w_out[i])
            pool0 = jnp.zeros((hp.shape[0], POOL_STATE, POOL_WIDTH), hp.dtype)
            h0_p = jnp.zeros((hp.shape[0], SSM_GROUPS, SSM_STATE), jnp.complex64)
            h0_s = lax.complex(state_s5[i, ..., 0].astype(jnp.float32), state_s5[i, ..., 1].astype(jnp.float32))
            y_p, pool_p, h_p = mixer_ab(xp, pool0, h0_p, pos_p, *ab)
            y_s, pool_s, h_s = mixer_ab(xs, state_pool[i], h0_s, pos_s, *ab)
            pool_p_list.append(pool_p)
            pool_s_list.append(pool_s.astype(state_pool.dtype))
            s5_p_list.append(jnp.stack([h_p.real, h_p.imag], axis=-1).astype(state_s5.dtype))
            s5_s_list.append(jnp.stack([h_s.real, h_s.imag], axis=-1).astype(state_s5.dtype))
        else:
            y_p, k_p, v_p = mixer_c_prompt(xp, c_w_qkv[i], c_w_o[i])
            y_s, k_s, v_s = mixer_c_sample(xs, cache_k[i], cache_v[i], c_w_qkv[i], c_w_o[i])
            k_p_list.append(k_p)
            v_p_list.append(v_p)
            k_s_list.append(k_s.astype(cache_k.dtype))
            v_s_list.append(v_s.astype(cache_v.dtype))
        hp = hp + rms_norm(y_p, g[1])
        hs = hs + rms_norm(y_s, g[1])
        hp = hp + rms_norm(swiglu_ffn(rms_norm(hp, g[2]), ffn_w_gate[layer], ffn_w_up[layer], ffn_w_down[layer]), g[3])
        hs = hs + rms_norm(swiglu_ffn(rms_norm(hs, g[2]), ffn_w_gate[layer], ffn_w_up[layer], ffn_w_down[layer]), g[3])
    pool_prompt = jnp.stack(pool_p_list)
    s5_prompt = jnp.stack(s5_p_list)
    k_prompt = jnp.stack(k_p_list)
    v_prompt = jnp.stack(v_p_list)
    pool_sample = jnp.stack(pool_s_list)
    s5_sample = jnp.stack(s5_s_list)
    k_sample = jnp.stack(k_s_list)
    v_sample = jnp.stack(v_s_list)
    return (hp, hs, pool_prompt, s5_prompt, k_prompt, v_prompt, pool_sample, s5_sample, k_sample, v_sample)
```

```python
import functools
import math

import jax
import jax.numpy as jnp
from jax import lax
from jax.experimental import pallas as pl
from jax.experimental.pallas import tpu as pltpu

F32 = jnp.float32
BF16 = jnp.bfloat16

D_MODEL = 1024
PAST_LEN = 16384
POOL_WIDTH = 512
POOL_WINDOWS = (2, 4, 8, 16)
POOL_GROUP = 128
POOL_STATE = 15
POOL_LEAD = 16
SSM_WIDTH = 512
SSM_GROUP = 16
SSM_GROUPS = 32
SSM_STATE = 64
SSM_STATES = SSM_GROUPS * SSM_STATE
ATT_HEADS = 16
HEAD_DIM = 64
ATT_WIDTH = ATT_HEADS * HEAD_DIM
DILATED_PATTERNS = ((128, 1), (512, 4), (2048, 16))
MAX_WINDOW = 2048
ATT_BLOCK = 128
ATT_SPAN = 128
FFN_HIDDEN = 2816
RMS_EPS = 1e-6

SUBLANES = 8
LANES = 128
VMEM_LIMIT = 56 * 1024 * 1024


def _cparams(n_axes):
    return pltpu.CompilerParams(
        dimension_semantics=("arbitrary",) * n_axes, vmem_limit_bytes=VMEM_LIMIT)


def _rms(x, g):
    ms = jnp.mean(x * x, axis=-1, keepdims=True)
    return (x * lax.rsqrt(ms + RMS_EPS)) * g


def _const_spec(shape):
    zeros = (0,) * len(shape)
    return pl.BlockSpec(shape, lambda *_: zeros, pipeline_mode=pl.Buffered(1))


def _norm_matmul_kernel(x_ref, g_ref, w_ref, *out_refs, splits):
    xb = _rms(x_ref[...], g_ref[...]).astype(BF16)
    for (c0, width, scale), o_ref in zip(splits, out_refs):
        y = jnp.dot(xb, w_ref[:, c0:c0 + width], preferred_element_type=F32)
        if scale != 1.0:
            y = y * scale
        o_ref[...] = y.astype(o_ref.dtype)


def _norm_matmul(x, g, w, splits, dtypes, tm):
    m, d = x.shape
    n = w.shape[1]
    return pl.pallas_call(
        functools.partial(_norm_matmul_kernel, splits=splits),
        grid=(m // tm,),
        in_specs=[pl.BlockSpec((tm, d), lambda i: (i, 0)), _const_spec((1, d)), _const_spec((d, n))],
        out_specs=[pl.BlockSpec((tm, width), lambda i: (i, 0)) for _, width, _ in splits],
        out_shape=[jax.ShapeDtypeStruct((m, width), dt) for (_, width, _), dt in zip(splits, dtypes)],
        compiler_params=_cparams(1),
        name="norm_matmul",
    )(x, g, w)


def _qkv_prompt_kernel(x_ref, g_ref, w_ref, q_ref, k_ref, v_ref, tail_ref, *, q_scale):
    xb = _rms(x_ref[...], g_ref[...]).astype(BF16)
    q = jnp.dot(xb, w_ref[:, 0:ATT_WIDTH], preferred_element_type=F32)
    q_ref[...] = (q * q_scale).astype(BF16)
    k = jnp.dot(xb, w_ref[:, ATT_WIDTH:2 * ATT_WIDTH], preferred_element_type=F32)
    k_ref[...] = k.astype(BF16)
    tail_ref[0, :, 0:ATT_WIDTH] = k
    v = jnp.dot(xb, w_ref[:, 2 * ATT_WIDTH:3 * ATT_WIDTH], preferred_element_type=F32)
    v_ref[...] = v.astype(BF16)
    tail_ref[0, :, ATT_WIDTH:2 * ATT_WIDTH] = v


def _qkv_prompt(x, g, w, batch, seq, keep, tm):
    m, d = x.shape
    tiles = seq // tm
    first_tail = tiles - keep // tm

    def tail_map(i):
        return (i // tiles, jnp.maximum(i % tiles - first_tail, 0), 0)

    row_spec = pl.BlockSpec((tm, ATT_WIDTH), lambda i: (i, 0))
    return pl.pallas_call(
        functools.partial(_qkv_prompt_kernel, q_scale=HEAD_DIM ** -0.5),
        grid=(m // tm,),
        in_specs=[pl.BlockSpec((tm, d), lambda i: (i, 0)), _const_spec((1, d)),
                  _const_spec((d, 3 * ATT_WIDTH))],
        out_specs=[row_spec, row_spec, row_spec, pl.BlockSpec((1, tm, 2 * ATT_WIDTH), tail_map)],
        out_shape=[jax.ShapeDtypeStruct((m, ATT_WIDTH), BF16)] * 3
        + [jax.ShapeDtypeStruct((batch, keep, 2 * ATT_WIDTH), F32)],
        compiler_params=_cparams(1),
        name="qkv_prompt",
    )(x, g, w)


def _pool_kernel(u_ref, lead_ref, w_ref, scale_ref, y_ref, ext, *, tt, pos0, carry):
    j = pl.program_id(1)

    @pl.when(j == 0)
    def _():
        ext[0:POOL_LEAD, :] = lead_ref[0]

    u = u_ref[0]
    ext[POOL_LEAD:POOL_LEAD + tt, :] = u
    pos = pos0 + j * tt + lax.broadcasted_iota(jnp.int32, (tt, POOL_GROUP), 0)
    for grp, window in enumerate(POOL_WINDOWS):
        sl = slice(grp * POOL_GROUP, (grp + 1) * POOL_GROUP)
        u_g = u[:, sl]
        win_sum = u_g
        for k in range(1, window):
            win_sum = win_sum + ext[POOL_LEAD - k:POOL_LEAD - k + tt, sl]
        count = jnp.minimum(pos + 1, window).astype(F32)
        diff = win_sum / count - u_g
        y = jnp.dot(diff.astype(BF16), w_ref[grp], preferred_element_type=F32)
        y_ref[0, :, sl] = (y * scale_ref[:, sl]).astype(y_ref.dtype)
    if carry:
        ext[0:POOL_LEAD, :] = ext[tt:tt + POOL_LEAD, :]


def _pool_mixer(proj, lead, w, scale, tt, pos0):
    n, t, _ = proj.shape
    steps = t // tt
    return pl.pallas_call(
        functools.partial(_pool_kernel, tt=tt, pos0=pos0, carry=steps > 1),
        grid=(n, steps),
        in_specs=[pl.BlockSpec((1, tt, POOL_WIDTH), lambda b, j: (b, j, 0)),
                  pl.BlockSpec((1, POOL_LEAD, POOL_WIDTH), lambda b, j: (b, 0, 0)),
                  _const_spec((len(POOL_WINDOWS), POOL_GROUP, POOL_GROUP)),
                  _const_spec((1, POOL_WIDTH))],
        out_specs=pl.BlockSpec((1, tt, POOL_WIDTH), lambda b, j: (b, j, 0)),
        out_shape=jax.ShapeDtypeStruct((n, t, POOL_WIDTH), BF16),
        scratch_shapes=[pltpu.VMEM((POOL_LEAD + tt, POOL_WIDTH), F32)],
        compiler_params=_cparams(2),
        name="pool_mixer",
    )(proj, lead, w, scale)


S5_IN_HALF = SSM_WIDTH // 2
S5_STATE_HALF = SSM_STATES // 2
S5_OUT_TILE = LANES
S5_STATE_TILE = SSM_STATES // (SSM_WIDTH // S5_OUT_TILE)
S5_SCAN_LANES = 512


def _s5_input(ub, bre_ref, bim_ref):
    re, im = [], []
    for half in range(2):
        uk = ub[:, half * S5_IN_HALF:(half + 1) * S5_IN_HALF]
        re.append(jnp.dot(uk, bre_ref[half], preferred_element_type=F32))
        im.append(jnp.dot(uk, bim_ref[half], preferred_element_type=F32))
    return re, im


def _s5_output(h_re, h_im, cre_ref, cim_ref, tile):
    return (jnp.dot(h_re, cre_ref[tile], preferred_element_type=F32)
            - jnp.dot(h_im, cim_ref[tile], preferred_element_type=F32))


def _s5_prompt_kernel(u_ref, bre_ref, bim_ref, cre_ref, cim_ref, pw_ref, d_ref,
                      y_ref, hre_ref, him_ref, s_re, s_im, c_re, c_im, *, tt):
    j = pl.program_id(1)

    @pl.when(j == 0)
    def _():
        c_re[...] = jnp.zeros_like(c_re)
        c_im[...] = jnp.zeros_like(c_im)

    u = u_ref[0]
    bu_re, bu_im = _s5_input(u.astype(BF16), bre_ref, bim_ref)
    for half in range(2):
        s_re[:, half * S5_STATE_HALF:(half + 1) * S5_STATE_HALF] = bu_re[half]
        s_im[:, half * S5_STATE_HALF:(half + 1) * S5_STATE_HALF] = bu_im[half]

    for chunk in range(SSM_STATES // S5_SCAN_LANES):
        sl = slice(chunk * S5_SCAN_LANES, (chunk + 1) * S5_SCAN_LANES)

        def body(r, carry, sl=sl):
            in_re, in_im = carry
            row = pl.multiple_of(r * SUBLANES, SUBLANES)
            re = s_re[pl.ds(row, SUBLANES), sl]
            im = s_im[pl.ds(row, SUBLANES), sl]
            for level, shift in enumerate((1, 2, 4)):
                a_re = pw_ref[2 * level, :, sl]
                a_im = pw_ref[2 * level + 1, :, sl]
                sh_re = pltpu.roll(re, shift, 0)
                sh_im = pltpu.roll(im, shift, 0)
                re, im = (re + a_re * sh_re - a_im * sh_im,
                          im + a_re * sh_im + a_im * sh_re)
            p_re = pw_ref[6, :, sl]
            p_im = pw_ref[7, :, sl]
            re, im = (re + p_re * in_re - p_im * in_im,
                      im + p_re * in_im + p_im * in_re)
            s_re[pl.ds(row, SUBLANES), sl] = re
            s_im[pl.ds(row, SUBLANES), sl] = im
            last = SUBLANES - 1
            return (jnp.broadcast_to(re[last:last + 1, :], re.shape),
                    jnp.broadcast_to(im[last:last + 1, :], im.shape))

        out_re, out_im = lax.fori_loop(0, tt // SUBLANES, body, (c_re[:, sl], c_im[:, sl]))
        c_re[:, sl] = out_re
        c_im[:, sl] = out_im

    hre_ref[0] = c_re[...]
    him_ref[0] = c_im[...]
    for tile in range(SSM_WIDTH // S5_OUT_TILE):
        st = slice(tile * S5_STATE_TILE, (tile + 1) * S5_STATE_TILE)
        ot = slice(tile * S5_OUT_TILE, (tile + 1) * S5_OUT_TILE)
        y = _s5_output(s_re[:, st].astype(BF16), s_im[:, st].astype(BF16), cre_ref, cim_ref, tile)
        y_ref[0, :, ot] = y + d_ref[:, ot] * u[:, ot]


def _s5_prompt(proj, mats, tt):
    b, t, _ = proj.shape
    bre, bim, cre, cim, powers, d_skip = mats
    state = jax.ShapeDtypeStruct((b, SUBLANES, SSM_STATES), F32)
    state_spec = pl.BlockSpec((1, SUBLANES, SSM_STATES), lambda i, j: (i, 0, 0))
    return pl.pallas_call(
        functools.partial(_s5_prompt_kernel, tt=tt),
        grid=(b, t // tt),
        in_specs=[pl.BlockSpec((1, tt, SSM_WIDTH), lambda i, j: (i, j, 1)),
                  _const_spec(bre.shape), _const_spec(bim.shape),
                  _const_spec(cre.shape), _const_spec(cim.shape),
                  _const_spec(powers.shape), _const_spec(d_skip.shape)],
        out_specs=[pl.BlockSpec((1, tt, SSM_WIDTH), lambda i, j: (i, j, 0)), state_spec, state_spec],
        out_shape=[jax.ShapeDtypeStruct((b, t, SSM_WIDTH), F32), state, state],
        scratch_shapes=[pltpu.VMEM((tt, SSM_STATES), F32), pltpu.VMEM((tt, SSM_STATES), F32),
                        pltpu.VMEM((SUBLANES, SSM_STATES), F32), pltpu.VMEM((SUBLANES, SSM_STATES), F32)],
        compiler_params=_cparams(2),
        name="s5_prompt",
    )(proj, bre, bim, cre, cim, powers, d_skip)


def _s5_sample_kernel(u_ref, h0re_ref, h0im_ref, bre_ref, bim_ref, cre_ref, cim_ref, lam_ref, d_ref,
                      y_ref, hre_ref, him_ref, *, steps):
    h_re = h0re_ref[...]
    h_im = h0im_ref[...]
    lam_re = lam_ref[0:1, :]
    lam_im = lam_ref[1:2, :]
    for t in range(steps):
        u = u_ref[t]
        bu_re, bu_im = _s5_input(u.astype(BF16), bre_ref, bim_ref)
        bu_re = jnp.concatenate(bu_re, axis=1)
        bu_im = jnp.concatenate(bu_im, axis=1)
        h_re, h_im = (lam_re * h_re - lam_im * h_im + bu_re,
                      lam_re * h_im + lam_im * h_re + bu_im)
        hb_re = h_re.astype(BF16)
        hb_im = h_im.astype(BF16)
        for tile in range(SSM_WIDTH // S5_OUT_TILE):
            st = slice(tile * S5_STATE_TILE, (tile + 1) * S5_STATE_TILE)
            ot = slice(tile * S5_OUT_TILE, (tile + 1) * S5_OUT_TILE)
            y = _s5_output(hb_re[:, st], hb_im[:, st], cre_ref, cim_ref, tile)
            y_ref[t, :, ot] = y + d_ref[:, ot] * u[:, ot]
    hre_ref[...] = h_re
    him_ref[...] = h_im


def _s5_sample(u_tb, h0_re, h0_im, mats, lam):
    steps, n, _ = u_tb.shape
    bre, bim, cre, cim, _, d_skip = mats
    state = jax.ShapeDtypeStruct((n, SSM_STATES), F32)
    args = (u_tb, h0_re, h0_im, bre, bim, cre, cim, lam, d_skip)
    return pl.pallas_call(
        functools.partial(_s5_sample_kernel, steps=steps),
        grid=(1,),
        in_specs=[_const_spec(a.shape) for a in args],
        out_specs=[_const_spec(u_tb.shape), _const_spec((n, SSM_STATES)), _const_spec((n, SSM_STATES))],
        out_shape=[jax.ShapeDtypeStruct(u_tb.shape, F32), state, state],
        compiler_params=_cparams(1),
        name="s5_sample",
    )(*args)


def _s5_matrices(lam_re, lam_im, log_dt, b_re, b_im, c_re, c_im, d_skip):
    lam = lax.complex(lam_re, lam_im)
    dt = jnp.exp(log_dt)[:, None]
    lam_bar = jnp.exp(lam * dt)
    b_bar = ((lam_bar - 1.0) / lam)[..., None] * lax.complex(b_re, b_im)
    eye_half = jnp.eye(SSM_GROUPS // 2, dtype=F32)

    def in_blocks(x):
        x = x.reshape(2, SSM_GROUPS // 2, SSM_STATE, SSM_GROUP)
        return jnp.einsum("kgpi,gh->kgihp", x, eye_half).reshape(2, S5_IN_HALF, S5_STATE_HALF).astype(BF16)

    tiles = SSM_WIDTH // S5_OUT_TILE
    groups_per_tile = SSM_GROUPS // tiles
    eye_tile = jnp.eye(groups_per_tile, dtype=F32)

    def out_blocks(x):
        x = x.reshape(tiles, groups_per_tile, SSM_GROUP, SSM_STATE)
        return jnp.einsum("kgop,gh->kgpho", x, eye_tile).reshape(tiles, S5_STATE_TILE, S5_OUT_TILE).astype(BF16)

    lam_flat = lam_bar.reshape(1, SSM_STATES)
    rows = jnp.arange(SUBLANES)[:, None]
    planes = []
    for shift in (1, 2, 4):
        a = jnp.where(rows >= shift, lam_flat ** shift, 0.0)
        planes += [a.real, a.imag]
    carry = jnp.cumprod(jnp.broadcast_to(lam_flat, (SUBLANES, SSM_STATES)), axis=0)
    planes += [carry.real, carry.imag]
    powers = jnp.stack(planes).astype(F32)
    lam_rows = jnp.concatenate([lam_flat.real, lam_flat.imag], axis=0).astype(F32)
    mats = (in_blocks(b_bar.real), in_blocks(b_bar.imag), out_blocks(c_re), out_blocks(c_im),
            powers, d_skip.reshape(1, SSM_WIDTH))
    return mats, lam_rows


def _ab_out_kernel(yp_ref, ys_ref, wglu_ref, bglu_ref, wout_ref, h_ref, g_ref, o_ref):
    z = jax.nn.gelu(ys_ref[...])
    gate = jnp.dot(z.astype(BF16), wglu_ref[...], preferred_element_type=F32) + bglu_ref[...]
    y_ssm = z * jax.nn.sigmoid(gate)
    y = (jnp.dot(yp_ref[...], wout_ref[0:POOL_WIDTH, :], preferred_element_type=F32)
         + jnp.dot(y_ssm.astype(BF16), wout_ref[POOL_WIDTH:, :], preferred_element_type=F32))
    o_ref[...] = h_ref[...] + _rms(y, g_ref[...])


def _ab_out(y_pool, y_ssm, w_glu, b_glu, w_out, h, g, tm):
    m = h.shape[0]
    row = lambda width: pl.BlockSpec((tm, width), lambda i: (i, 0))
    return pl.pallas_call(
        _ab_out_kernel,
        grid=(m // tm,),
        in_specs=[row(POOL_WIDTH), row(SSM_WIDTH), _const_spec(w_glu.shape), _const_spec(b_glu.shape),
                  _const_spec(w_out.shape), row(D_MODEL), _const_spec(g.shape)],
        out_specs=row(D_MODEL),
        out_shape=jax.ShapeDtypeStruct((m, D_MODEL), F32),
        compiler_params=_cparams(1),
        name="ab_out",
    )(y_pool, y_ssm, w_glu, b_glu, w_out, h, g)


def _ffn_kernel(h_ref, gin_ref, wg_ref, wu_ref, wd_ref, gout_ref, o_ref):
    h = h_ref[...]
    xb = _rms(h, gin_ref[...]).astype(BF16)
    gate = jnp.dot(xb, wg_ref[...], preferred_element_type=F32)
    up = jnp.dot(xb, wu_ref[...], preferred_element_type=F32)
    act = (gate * jax.nn.sigmoid(gate) * up).astype(BF16)
    y = jnp.dot(act, wd_ref[...], preferred_element_type=F32)
    o_ref[...] = h + _rms(y, gout_ref[...])


def _ffn(h, g_in, w_gate, w_up, w_down, g_out, tm):
    m = h.shape[0]
    row = pl.BlockSpec((tm, D_MODEL), lambda i: (i, 0))
    return pl.pallas_call(
        _ffn_kernel,
        grid=(m // tm,),
        in_specs=[row, _const_spec(g_in.shape), _const_spec(w_gate.shape), _const_spec(w_up.shape),
                  _const_spec(w_down.shape), _const_spec(g_out.shape)],
        out_specs=row,
        out_shape=jax.ShapeDtypeStruct((m, D_MODEL), F32),
        compiler_params=_cparams(1),
        name="ffn",
    )(h, g_in, w_gate, w_up, w_down, g_out)


def _attn_prompt_kernel(q_ref, kc_ref, kp_ref, vc_ref, vp_ref, o_ref, lse_ref):
    blk = pl.program_id(2)
    qi = lax.broadcasted_iota(jnp.int32, (ATT_BLOCK, 2 * ATT_BLOCK), 0)
    kj = lax.broadcasted_iota(jnp.int32, (ATT_BLOCK, 2 * ATT_BLOCK), 1)
    dist = ATT_BLOCK + qi - kj
    valid = (dist >= 0) & (dist <= ATT_SPAN) & ((kj >= ATT_BLOCK) | (blk > 0))
    lane = lax.broadcasted_iota(jnp.int32, (ATT_BLOCK, LANES), 1)
    first_head = lane < HEAD_DIM
    lse_all = jnp.zeros((ATT_BLOCK, LANES), F32)
    for pair in range(ATT_HEADS // 2):
        sl = slice(pair * LANES, (pair + 1) * LANES)
        q2 = q_ref[0, :, sl]
        k2 = jnp.concatenate([kp_ref[0, :, sl], kc_ref[0, :, sl]], axis=0)
        v2 = jnp.concatenate([vp_ref[0, :, sl], vc_ref[0, :, sl]], axis=0)
        outs = []
        for half in range(2):
            qh = jnp.where(first_head if half == 0 else ~first_head, q2, jnp.zeros_like(q2))
            s = lax.dot_general(qh, k2, (((1,), (1,)), ((), ())), preferred_element_type=F32)
            s = jnp.where(valid, s, -jnp.inf)
            m = jnp.max(s, axis=-1, keepdims=True)
            p = jnp.exp(s - m)
            den = jnp.sum(p, axis=-1, keepdims=True)
            o = jnp.dot(p.astype(BF16), v2, preferred_element_type=F32)
            outs.append(o / den)
            lse_all = jnp.where(lane == 2 * pair + half, m + jnp.log(den), lse_all)
        o_ref[0, :, sl] = jnp.where(first_head, outs[0], outs[1])
    lse_ref[0] = lse_all


def _attn_prompt(q, k, v, batch, seq, dil):
    sub = seq // dil
    nblk = sub // ATT_BLOCK
    view = lambda a: a.reshape(batch, sub, dil * ATT_WIDTH)
    cur = pl.BlockSpec((1, ATT_BLOCK, ATT_WIDTH), lambda b, r, i: (b, i, r))
    prev = pl.BlockSpec((1, ATT_BLOCK, ATT_WIDTH), lambda b, r, i: (b, jnp.maximum(i - 1, 0), r))
    o, lse = pl.pallas_call(
        _attn_prompt_kernel,
        grid=(batch, dil, nblk),
        in_specs=[cur, cur, prev, cur, prev],
        out_specs=[cur, pl.BlockSpec((1, ATT_BLOCK, LANES), lambda b, r, i: (b, i, r))],
        out_shape=[jax.ShapeDtypeStruct((batch, sub, dil * ATT_WIDTH), F32),
                   jax.ShapeDtypeStruct((batch, sub, dil * LANES), F32)],
        compiler_params=_cparams(3),
        name="attn_prompt",
    )(view(q), view(k), view(k), view(v), view(v))
    return o.reshape(batch * seq, ATT_WIDTH), lse.reshape(batch * seq, LANES)


SAMPLE_KEYS = MAX_WINDOW + LANES


def _attn_sample_kernel(qblk_ref, knew_ref, vnew_ref, ck_ref, cv_ref, o_ref, kk, vv, *, steps):
    kk[0:MAX_WINDOW, :] = ck_ref[0].astype(BF16)
    vv[0:MAX_WINDOW, :] = cv_ref[0].astype(BF16)
    kk[MAX_WINDOW:SAMPLE_KEYS, :] = knew_ref[0].astype(BF16)
    vv[MAX_WINDOW:SAMPLE_KEYS, :] = vnew_ref[0].astype(BF16)
    s = jnp.dot(kk[...], qblk_ref[0], preferred_element_type=F32)
    row = lax.broadcasted_iota(jnp.int32, (SAMPLE_KEYS, LANES), 0)
    col = lax.broadcasted_iota(jnp.int32, (SAMPLE_KEYS, LANES), 1)
    dist = MAX_WINDOW + (col >> 4) - row
    mult = jnp.zeros((SAMPLE_KEYS, LANES), F32)
    for window, dil in DILATED_PATTERNS:
        hit = (dist >= 0) & (dist <= window) & ((dist & (dil - 1)) == 0)
        mult = mult + hit.astype(F32)
    s = jnp.where(mult > 0.0, s, -jnp.inf)
    m = jnp.max(s, axis=0, keepdims=True)
    p = mult * jnp.exp(s - m)
    p_t = p.T
    den = jnp.sum(p_t, axis=-1, keepdims=True)
    o = jnp.dot(p_t.astype(BF16), vv[...], preferred_element_type=F32) / den
    orow = lax.broadcasted_iota(jnp.int32, (LANES, ATT_WIDTH), 0)
    ocol = lax.broadcasted_iota(jnp.int32, (LANES, ATT_WIDTH), 1)
    o = jnp.where((ocol >> 6) == (orow & (ATT_HEADS - 1)), o, 0.0)
    for t in range(steps):
        o_ref[0, t:t + 1, :] = jnp.sum(o[t * ATT_HEADS:(t + 1) * ATT_HEADS, :], axis=0, keepdims=True)


def _attn_sample(qblk, k_new, v_new, cache_k, cache_v, steps):
    n = qblk.shape[0]
    per = lambda shape: pl.BlockSpec((1,) + shape, lambda b: (b, 0, 0))
    return pl.pallas_call(
        functools.partial(_attn_sample_kernel, steps=steps),
        grid=(n,),
        in_specs=[per((ATT_WIDTH, LANES)), per((LANES, ATT_WIDTH)), per((LANES, ATT_WIDTH)),
                  per((MAX_WINDOW, ATT_WIDTH)), per((MAX_WINDOW, ATT_WIDTH))],
        out_specs=per((steps, ATT_WIDTH)),
        out_shape=jax.ShapeDtypeStruct((n, steps, ATT_WIDTH), F32),
        scratch_shapes=[pltpu.VMEM((SAMPLE_KEYS, ATT_WIDTH), BF16), pltpu.VMEM((SAMPLE_KEYS, ATT_WIDTH), BF16)],
        compiler_params=_cparams(1),
        name="attn_sample",
    )(qblk, k_new, v_new, cache_k, cache_v)


def _attn_out_kernel(*refs, n_pat):
    o_refs = refs[:n_pat]
    n_lse = n_pat if n_pat > 1 else 0
    lse_refs = refs[n_pat:n_pat + n_lse]
    w_ref, h_ref, g_ref, out_ref = refs[n_pat + n_lse:]
    if n_pat == 1:
        a = o_refs[0][...]
    else:
        lses = [r[...] for r in lse_refs]
        top = functools.reduce(jnp.maximum, lses)
        es = [jnp.exp(l - top) for l in lses]
        total = functools.reduce(lambda x, y: x + y, es)
        wgts = [e / total for e in es]
        lane = lax.broadcasted_iota(jnp.int32, (lses[0].shape[0], LANES), 1)
        first_head = lane < HEAD_DIM
        pieces = []
        for pair in range(ATT_HEADS // 2):
            sl = slice(pair * LANES, (pair + 1) * LANES)
            acc = None
            for wgt, o_ref in zip(wgts, o_refs):
                w2 = jnp.where(first_head, wgt[:, 2 * pair:2 * pair + 1], wgt[:, 2 * pair + 1:2 * pair + 2])
                term = w2 * o_ref[:, sl]
                acc = term if acc is None else acc + term
            pieces.append(acc)
        a = jnp.concatenate(pieces, axis=1)
    y = jnp.dot(a.astype(BF16), w_ref[...], preferred_element_type=F32)
    out_ref[...] = h_ref[...] + _rms(y, g_ref[...])


def _attn_out(outs, lses, w_o, h, g, tm):
    m = h.shape[0]
    n_pat = len(outs)
    row = lambda width: pl.BlockSpec((tm, width), lambda i: (i, 0))
    return pl.pallas_call(
        functools.partial(_attn_out_kernel, n_pat=n_pat),
        grid=(m // tm,),
        in_specs=[row(ATT_WIDTH)] * n_pat + [row(LANES)] * len(lses)
        + [_const_spec(w_o.shape), row(D_MODEL), _const_spec(g.shape)],
        out_specs=row(D_MODEL),
        out_shape=jax.ShapeDtypeStruct((m, D_MODEL), F32),
        compiler_params=_cparams(1),
        name="attn_out",
    )(*outs, *lses, w_o, h, g)


def kernel(x_prompt, x_sample, state_pool, state_s5, cache_k, cache_v, norm_gains, ab_w_in, ab_pool_w, ab_pool_scale, ab_lambda_re, ab_lambda_im, ab_log_dt, ab_b_re, ab_b_im, ab_c_re, ab_c_im, ab_d, ab_w_glu, ab_b_glu, ab_w_out, c_w_qkv, c_w_o, ffn_w_gate, ffn_w_up, ffn_w_down):
    batch, seq, d = x_prompt.shape
    n_dec, t_dec, _ = x_sample.shape
    mp, ms = batch * seq, n_dec * t_dec
    tm_p = 256
    gains = norm_gains.reshape(norm_gains.shape[0], 4, 1, d)

    hp = x_prompt.reshape(mp, d)
    hs = x_sample.reshape(ms, d)

    g = gains[0]
    w_in = ab_w_in[0].astype(BF16)
    (proj_p,) = _norm_matmul(hp, g[0], w_in, ((0, d, 1.0),), (F32,), tm_p)
    (proj_s,) = _norm_matmul(hs, g[0], w_in, ((0, d, 1.0),), (F32,), ms)
    proj_p = proj_p.reshape(batch, seq, d)
    proj_s = proj_s.reshape(n_dec, t_dec, d)

    pool_w = ab_pool_w[0].astype(BF16)
    pool_scale = ab_pool_scale[0].reshape(1, POOL_WIDTH)
    lead_p = jnp.zeros((batch, POOL_LEAD, POOL_WIDTH), F32)
    lead_s = jnp.pad(state_pool[0], ((0, 0), (POOL_LEAD - POOL_STATE, 0), (0, 0)))
    ypool_p = _pool_mixer(proj_p, lead_p, pool_w, pool_scale, 256, 0)
    ypool_s = _pool_mixer(proj_s, lead_s, pool_w, pool_scale, t_dec, PAST_LEN)
    pool_prompt = proj_p[:, seq - POOL_STATE:, :POOL_WIDTH][None]
    pool_sample = jnp.concatenate([state_pool[0], proj_s[:, :, :POOL_WIDTH]], axis=1)[:, -POOL_STATE:][None]

    mats, lam_rows = _s5_matrices(ab_lambda_re[0], ab_lambda_im[0], ab_log_dt[0], ab_b_re[0], ab_b_im[0],
                                  ab_c_re[0], ab_c_im[0], ab_d[0])
    yssm_p, hre_p, him_p = _s5_prompt(proj_p, mats, 256)
    u_tb = jnp.swapaxes(proj_s[:, :, POOL_WIDTH:], 0, 1)
    h0 = state_s5[0].reshape(n_dec, SSM_STATES, 2)
    yssm_tb, hre_s, him_s = _s5_sample(u_tb, h0[..., 0], h0[..., 1], mats, lam_rows)
    yssm_s = jnp.swapaxes(yssm_tb, 0, 1)
    s5_prompt = jnp.stack([hre_p[:, 0], him_p[:, 0]], axis=-1).reshape(1, batch, SSM_GROUPS, SSM_STATE, 2)
    s5_sample = jnp.stack([hre_s, him_s], axis=-1).reshape(1, n_dec, SSM_GROUPS, SSM_STATE, 2)

    w_glu = ab_w_glu[0].astype(BF16)
    b_glu = ab_b_glu[0].reshape(1, SSM_WIDTH)
    w_out = ab_w_out[0].astype(BF16)
    hp = _ab_out(ypool_p.reshape(mp, POOL_WIDTH), yssm_p.reshape(mp, SSM_WIDTH), w_glu, b_glu, w_out, hp, g[1], tm_p)
    hs = _ab_out(ypool_s.reshape(ms, POOL_WIDTH), yssm_s.reshape(ms, SSM_WIDTH), w_glu, b_glu, w_out, hs, g[1], ms)

    wg, wu, wd = ffn_w_gate[0].astype(BF16), ffn_w_up[0].astype(BF16), ffn_w_down[0].astype(BF16)
    hp = _ffn(hp, g[2], wg, wu, wd, g[3], tm_p)
    hs = _ffn(hs, g[2], wg, wu, wd, g[3], ms)

    g = gains[1]
    w_qkv = c_w_qkv[0].astype(BF16)
    w_o = c_w_o[0].astype(BF16)
    keep = min(MAX_WINDOW, seq)
    q_p, k_p, v_p, kv_tail = _qkv_prompt(hp, g[0], w_qkv, batch, seq, keep, tm_p)
    k_prompt = kv_tail[:, :, :ATT_WIDTH].reshape(1, batch, keep, ATT_HEADS, HEAD_DIM)
    v_prompt = kv_tail[:, :, ATT_WIDTH:].reshape(1, batch, keep, ATT_HEADS, HEAD_DIM)
    outs, lses = [], []
    for _, dil in DILATED_PATTERNS:
        o, lse = _attn_prompt(q_p, k_p, v_p, batch, seq, dil)
        outs.append(o)
        lses.append(lse)
    hp = _attn_out(outs, lses, w_o, hp, g[1], tm_p)

    scale = HEAD_DIM ** -0.5
    q_s, k_s, v_s = _norm_matmul(
        hs, g[0], w_qkv,
        ((0, ATT_WIDTH, scale), (ATT_WIDTH, ATT_WIDTH, 1.0), (2 * ATT_WIDTH, ATT_WIDTH, 1.0)),
        (F32, F32, F32), ms)
    k_sample = k_s.reshape(1, n_dec, t_dec, ATT_HEADS, HEAD_DIM)
    v_sample = v_s.reshape(1, n_dec, t_dec, ATT_HEADS, HEAD_DIM)
    q4 = q_s.reshape(n_dec, t_dec, ATT_HEADS, HEAD_DIM)
    qblk = jnp.einsum("bthe,hg->bhetg", q4, jnp.eye(ATT_HEADS, dtype=F32))
    qblk = qblk.reshape(n_dec, ATT_WIDTH, t_dec * ATT_HEADS)
    qblk = jnp.pad(qblk, ((0, 0), (0, 0), (0, LANES - t_dec * ATT_HEADS))).astype(BF16)
    pad_new = lambda a: jnp.pad(a.reshape(n_dec, t_dec, ATT_WIDTH), ((0, 0), (0, LANES - t_dec), (0, 0)))
    o_s = _attn_sample(qblk, pad_new(k_s), pad_new(v_s), cache_k[0].reshape(n_dec, MAX_WINDOW, ATT_WIDTH),
                       cache_v[0].reshape(n_dec, MAX_WINDOW, ATT_WIDTH), t_dec)
    hs = _attn_out([o_s.reshape(ms, ATT_WIDTH)], [], w_o, hs, g[1], ms)

    wg, wu, wd = ffn_w_gate[1].astype(BF16), ffn_w_up[1].astype(BF16), ffn_w_down[1].astype(BF16)
    hp = _ffn(hp, g[2], wg, wu, wd, g[3], tm_p)
    hs = _ffn(hs, g[2], wg, wu, wd, g[3], ms)

    return (hp.reshape(batch, seq, d), hs.reshape(n_dec, t_dec, d), pool_prompt, s5_prompt,
            k_prompt, v_prompt, pool_sample, s5_sample, k_sample, v_sample)
```

```python
import functools
import math

import jax
import jax.numpy as jnp
from jax import lax
from jax.experimental import pallas as pl
from jax.experimental.pallas import tpu as pltpu

F32 = jnp.float32
BF16 = jnp.bfloat16

D_MODEL = 1024
PAST_LEN = 16384
POOL_WIDTH = 512
POOL_WINDOWS = (2, 4, 8, 16)
POOL_GROUP = 128
POOL_STATE = 15
POOL_LEAD = 16
SSM_WIDTH = 512
SSM_GROUP = 16
SSM_GROUPS = 32
SSM_STATE = 64
SSM_STATES = SSM_GROUPS * SSM_STATE
ATT_HEADS = 16
HEAD_DIM = 64
ATT_WIDTH = ATT_HEADS * HEAD_DIM
DILATED_PATTERNS = ((128, 1), (512, 4), (2048, 16))
MAX_WINDOW = 2048
ATT_BLOCK = 128
ATT_SPAN = 128
FFN_HIDDEN = 2816
RMS_EPS = 1e-6

SUBLANES = 8
LANES = 128
VMEM_LIMIT = 56 * 1024 * 1024


def _cparams(n_axes):
    return pltpu.CompilerParams(
        dimension_semantics=("arbitrary",) * n_axes, vmem_limit_bytes=VMEM_LIMIT)


def _rms(x, g):
    ms = jnp.mean(x * x, axis=-1, keepdims=True)
    return (x * lax.rsqrt(ms + RMS_EPS)) * g


def _const_spec(shape):
    zeros = (0,) * len(shape)
    return pl.BlockSpec(shape, lambda *_: zeros, pipeline_mode=pl.Buffered(1))


def _norm_matmul_kernel(x_ref, g_ref, w_ref, *out_refs, splits):
    xb = _rms(x_ref[...], g_ref[...]).astype(BF16)
    for (c0, width, scale), o_ref in zip(splits, out_refs):
        y = jnp.dot(xb, w_ref[:, c0:c0 + width], preferred_element_type=F32)
        if scale != 1.0:
            y = y * scale
        o_ref[...] = y.astype(o_ref.dtype)


def _norm_matmul(x, g, w, splits, dtypes, tm):
    m, d = x.shape
    n = w.shape[1]
    return pl.pallas_call(
        functools.partial(_norm_matmul_kernel, splits=splits),
        grid=(m // tm,),
        in_specs=[pl.BlockSpec((tm, d), lambda i: (i, 0)), _const_spec((1, d)), _const_spec((d, n))],
        out_specs=[pl.BlockSpec((tm, width), lambda i: (i, 0)) for _, width, _ in splits],
        out_shape=[jax.ShapeDtypeStruct((m, width), dt) for (_, width, _), dt in zip(splits, dtypes)],
        compiler_params=_cparams(1),
        name="norm_matmul",
    )(x, g, w)


ATT_DILATIONS = tuple(dil for _, dil in DILATED_PATTERNS)
COL_TILES = ATT_WIDTH // LANES


def _qkv_prompt_kernel(x_ref, g_ref, w_ref, *refs, q_scale, tm):
    n_dil = len(ATT_DILATIONS)
    out_refs = refs[:3 * n_dil]
    tail_ref, planes = refs[3 * n_dil:]
    xb = _rms(x_ref[...], g_ref[...]).astype(BF16)
    for which in range(3):
        y = jnp.dot(xb, w_ref[:, which * ATT_WIDTH:(which + 1) * ATT_WIDTH], preferred_element_type=F32)
        if which == 0:
            y = y * q_scale
        else:
            tail_ref[0, :, (which - 1) * ATT_WIDTH:which * ATT_WIDTH] = y
        for c in range(COL_TILES):
            planes[c] = y[:, c * LANES:(c + 1) * LANES]
        for dil, o_ref in zip(ATT_DILATIONS, out_refs[which * n_dil:(which + 1) * n_dil]):
            if dil == 1:
                o_ref[0, 0] = y.astype(BF16)
                continue
            rows = tm // dil
            for res in range(dil):
                for c in range(COL_TILES):
                    o_ref[0, res, :, c * LANES:(c + 1) * LANES] = (
                        planes[c, pl.ds(res, rows, stride=dil), :].astype(BF16))


def _qkv_prompt(x, g, w, batch, seq, keep, tm):
    m, d = x.shape
    tiles = seq // tm
    first_tail = tiles - keep // tm

    def tail_map(i):
        return (i // tiles, jnp.maximum(i % tiles - first_tail, 0), 0)

    def dil_spec(dil):
        return pl.BlockSpec((1, dil, tm // dil, ATT_WIDTH), lambda i: (i // tiles, 0, i % tiles, 0))

    def dil_shape(dil):
        return jax.ShapeDtypeStruct((batch, dil, seq // dil, ATT_WIDTH), BF16)

    return pl.pallas_call(
        functools.partial(_qkv_prompt_kernel, q_scale=HEAD_DIM ** -0.5, tm=tm),
        grid=(m // tm,),
        in_specs=[pl.BlockSpec((tm, d), lambda i: (i, 0)), _const_spec((1, d)),
                  _const_spec((d, 3 * ATT_WIDTH))],
        out_specs=[dil_spec(dil) for _ in range(3) for dil in ATT_DILATIONS]
        + [pl.BlockSpec((1, tm, 2 * ATT_WIDTH), tail_map)],
        out_shape=[dil_shape(dil) for _ in range(3) for dil in ATT_DILATIONS]
        + [jax.ShapeDtypeStruct((batch, keep, 2 * ATT_WIDTH), F32)],
        scratch_shapes=[pltpu.VMEM((COL_TILES, tm, LANES), F32)],
        compiler_params=_cparams(1),
        name="qkv_prompt",
    )(x, g, w)


def _pool_kernel(u_ref, lead_ref, w_ref, scale_ref, y_ref, ext, *, tt, pos0, carry):
    j = pl.program_id(1)

    @pl.when(j == 0)
    def _():
        ext[0:POOL_LEAD, :] = lead_ref[0]

    u = u_ref[0]
    ext[POOL_LEAD:POOL_LEAD + tt, :] = u
    pos = pos0 + j * tt + lax.broadcasted_iota(jnp.int32, (tt, POOL_GROUP), 0)
    for grp, window in enumerate(POOL_WINDOWS):
        sl = slice(grp * POOL_GROUP, (grp + 1) * POOL_GROUP)
        u_g = u[:, sl]
        win_sum = u_g
        for k in range(1, window):
            win_sum = win_sum + ext[POOL_LEAD - k:POOL_LEAD - k + tt, sl]
        count = jnp.minimum(pos + 1, window).astype(F32)
        diff = win_sum / count - u_g
        y = jnp.dot(diff.astype(BF16), w_ref[grp], preferred_element_type=F32)
        y_ref[0, :, sl] = (y * scale_ref[:, sl]).astype(y_ref.dtype)
    if carry:
        ext[0:POOL_LEAD, :] = ext[tt:tt + POOL_LEAD, :]


def _pool_mixer(proj, lead, w, scale, tt, pos0):
    n, t, _ = proj.shape
    steps = t // tt
    return pl.pallas_call(
        functools.partial(_pool_kernel, tt=tt, pos0=pos0, carry=steps > 1),
        grid=(n, steps),
        in_specs=[pl.BlockSpec((1, tt, POOL_WIDTH), lambda b, j: (b, j, 0)),
                  pl.BlockSpec((1, POOL_LEAD, POOL_WIDTH), lambda b, j: (b, 0, 0)),
                  _const_spec((len(POOL_WINDOWS), POOL_GROUP, POOL_GROUP)),
                  _const_spec((1, POOL_WIDTH))],
        out_specs=pl.BlockSpec((1, tt, POOL_WIDTH), lambda b, j: (b, j, 0)),
        out_shape=jax.ShapeDtypeStruct((n, t, POOL_WIDTH), BF16),
        scratch_shapes=[pltpu.VMEM((POOL_LEAD + tt, POOL_WIDTH), F32)],
        compiler_params=_cparams(2),
        name="pool_mixer",
    )(proj, lead, w, scale)


S5_IN_HALF = SSM_WIDTH // 2
S5_STATE_HALF = SSM_STATES // 2
S5_OUT_TILE = LANES
S5_STATE_TILE = SSM_STATES // (SSM_WIDTH // S5_OUT_TILE)
S5_SCAN_LANES = 512


def _s5_input(ub, bre_ref, bim_ref):
    re, im = [], []
    for half in range(2):
        uk = ub[:, half * S5_IN_HALF:(half + 1) * S5_IN_HALF]
        re.append(jnp.dot(uk, bre_ref[half], preferred_element_type=F32))
        im.append(jnp.dot(uk, bim_ref[half], preferred_element_type=F32))
    return re, im


def _s5_output(h_re, h_im, cre_ref, cim_ref, tile):
    return (jnp.dot(h_re, cre_ref[tile], preferred_element_type=F32)
            - jnp.dot(h_im, cim_ref[tile], preferred_element_type=F32))


def _s5_prompt_kernel(u_ref, bre_ref, bim_ref, cre_ref, cim_ref, pw_ref, d_ref,
                      y_ref, hre_ref, him_ref, s_re, s_im, c_re, c_im, *, tt):
    j = pl.program_id(1)

    @pl.when(j == 0)
    def _():
        c_re[...] = jnp.zeros_like(c_re)
        c_im[...] = jnp.zeros_like(c_im)

    u = u_ref[0]
    bu_re, bu_im = _s5_input(u.astype(BF16), bre_ref, bim_ref)
    for half in range(2):
        s_re[:, half * S5_STATE_HALF:(half + 1) * S5_STATE_HALF] = bu_re[half]
        s_im[:, half * S5_STATE_HALF:(half + 1) * S5_STATE_HALF] = bu_im[half]

    for chunk in range(SSM_STATES // S5_SCAN_LANES):
        sl = slice(chunk * S5_SCAN_LANES, (chunk + 1) * S5_SCAN_LANES)

        def body(r, carry, sl=sl):
            in_re, in_im = carry
            row = pl.multiple_of(r * SUBLANES, SUBLANES)
            re = s_re[pl.ds(row, SUBLANES), sl]
            im = s_im[pl.ds(row, SUBLANES), sl]
            for level, shift in enumerate((1, 2, 4)):
                a_re = pw_ref[2 * level, :, sl]
                a_im = pw_ref[2 * level + 1, :, sl]
                sh_re = pltpu.roll(re, shift, 0)
                sh_im = pltpu.roll(im, shift, 0)
                re, im = (re + a_re * sh_re - a_im * sh_im,
                          im + a_re * sh_im + a_im * sh_re)
            p_re = pw_ref[6, :, sl]
            p_im = pw_ref[7, :, sl]
            re, im = (re + p_re * in_re - p_im * in_im,
                      im + p_re * in_im + p_im * in_re)
            s_re[pl.ds(row, SUBLANES), sl] = re
            s_im[pl.ds(row, SUBLANES), sl] = im
            last = SUBLANES - 1
            return (jnp.broadcast_to(re[last:last + 1, :], re.shape),
                    jnp.broadcast_to(im[last:last + 1, :], im.shape))

        out_re, out_im = lax.fori_loop(0, tt // SUBLANES, body, (c_re[:, sl], c_im[:, sl]))
        c_re[:, sl] = out_re
        c_im[:, sl] = out_im

    hre_ref[0] = c_re[...]
    him_ref[0] = c_im[...]
    for tile in range(SSM_WIDTH // S5_OUT_TILE):
        st = slice(tile * S5_STATE_TILE, (tile + 1) * S5_STATE_TILE)
        ot = slice(tile * S5_OUT_TILE, (tile + 1) * S5_OUT_TILE)
        y = _s5_output(s_re[:, st].astype(BF16), s_im[:, st].astype(BF16), cre_ref, cim_ref, tile)
        y_ref[0, :, ot] = y + d_ref[:, ot] * u[:, ot]


def _s5_prompt(proj, mats, tt):
    b, t, _ = proj.shape
    bre, bim, cre, cim, powers, d_skip = mats
    state = jax.ShapeDtypeStruct((b, SUBLANES, SSM_STATES), F32)
    state_spec = pl.BlockSpec((1, SUBLANES, SSM_STATES), lambda i, j: (i, 0, 0))
    return pl.pallas_call(
        functools.partial(_s5_prompt_kernel, tt=tt),
        grid=(b, t // tt),
        in_specs=[pl.BlockSpec((1, tt, SSM_WIDTH), lambda i, j: (i, j, 1)),
                  _const_spec(bre.shape), _const_spec(bim.shape),
                  _const_spec(cre.shape), _const_spec(cim.shape),
                  _const_spec(powers.shape), _const_spec(d_skip.shape)],
        out_specs=[pl.BlockSpec((1, tt, SSM_WIDTH), lambda i, j: (i, j, 0)), state_spec, state_spec],
        out_shape=[jax.ShapeDtypeStruct((b, t, SSM_WIDTH), F32), state, state],
        scratch_shapes=[pltpu.VMEM((tt, SSM_STATES), F32), pltpu.VMEM((tt, SSM_STATES), F32),
                        pltpu.VMEM((SUBLANES, SSM_STATES), F32), pltpu.VMEM((SUBLANES, SSM_STATES), F32)],
        compiler_params=_cparams(2),
        name="s5_prompt",
    )(proj, bre, bim, cre, cim, powers, d_skip)


def _s5_sample_kernel(u_ref, h0re_ref, h0im_ref, bre_ref, bim_ref, cre_ref, cim_ref, lam_ref, d_ref,
                      y_ref, hre_ref, him_ref, *, steps):
    h_re = h0re_ref[...]
    h_im = h0im_ref[...]
    lam_re = lam_ref[0:1, :]
    lam_im = lam_ref[1:2, :]
    for t in range(steps):
        u = u_ref[t]
        bu_re, bu_im = _s5_input(u.astype(BF16), bre_ref, bim_ref)
        bu_re = jnp.concatenate(bu_re, axis=1)
        bu_im = jnp.concatenate(bu_im, axis=1)
        h_re, h_im = (lam_re * h_re - lam_im * h_im + bu_re,
                      lam_re * h_im + lam_im * h_re + bu_im)
        hb_re = h_re.astype(BF16)
        hb_im = h_im.astype(BF16)
        for tile in range(SSM_WIDTH // S5_OUT_TILE):
            st = slice(tile * S5_STATE_TILE, (tile + 1) * S5_STATE_TILE)
            ot = slice(tile * S5_OUT_TILE, (tile + 1) * S5_OUT_TILE)
            y = _s5_output(hb_re[:, st], hb_im[:, st], cre_ref, cim_ref, tile)
            y_ref[t, :, ot] = y + d_ref[:, ot] * u[:, ot]
    hre_ref[...] = h_re
    him_ref[...] = h_im


def _s5_sample(u_tb, h0_re, h0_im, mats, lam):
    steps, n, _ = u_tb.shape
    bre, bim, cre, cim, _, d_skip = mats
    state = jax.ShapeDtypeStruct((n, SSM_STATES), F32)
    args = (u_tb, h0_re, h0_im, bre, bim, cre, cim, lam, d_skip)
    return pl.pallas_call(
        functools.partial(_s5_sample_kernel, steps=steps),
        grid=(1,),
        in_specs=[_const_spec(a.shape) for a in args],
        out_specs=[_const_spec(u_tb.shape), _const_spec((n, SSM_STATES)), _const_spec((n, SSM_STATES))],
        out_shape=[jax.ShapeDtypeStruct(u_tb.shape, F32), state, state],
        compiler_params=_cparams(1),
        name="s5_sample",
    )(*args)


def _s5_matrices(lam_re, lam_im, log_dt, b_re, b_im, c_re, c_im, d_skip):
    lam = lax.complex(lam_re, lam_im)
    dt = jnp.exp(log_dt)[:, None]
    lam_bar = jnp.exp(lam * dt)
    b_bar = ((lam_bar - 1.0) / lam)[..., None] * lax.complex(b_re, b_im)
    eye_half = jnp.eye(SSM_GROUPS // 2, dtype=F32)

    def in_blocks(x):
        x = x.reshape(2, SSM_GROUPS // 2, SSM_STATE, SSM_GROUP)
        return jnp.einsum("kgpi,gh->kgihp", x, eye_half).reshape(2, S5_IN_HALF, S5_STATE_HALF).astype(BF16)

    tiles = SSM_WIDTH // S5_OUT_TILE
    groups_per_tile = SSM_GROUPS // tiles
    eye_tile = jnp.eye(groups_per_tile, dtype=F32)

    def out_blocks(x):
        x = x.reshape(tiles, groups_per_tile, SSM_GROUP, SSM_STATE)
        return jnp.einsum("kgop,gh->kgpho", x, eye_tile).reshape(tiles, S5_STATE_TILE, S5_OUT_TILE).astype(BF16)

    lam_flat = lam_bar.reshape(1, SSM_STATES)
    rows = jnp.arange(SUBLANES)[:, None]
    planes = []
    for shift in (1, 2, 4):
        a = jnp.where(rows >= shift, lam_flat ** shift, 0.0)
        planes += [a.real, a.imag]
    carry = jnp.cumprod(jnp.broadcast_to(lam_flat, (SUBLANES, SSM_STATES)), axis=0)
    planes += [carry.real, carry.imag]
    powers = jnp.stack(planes).astype(F32)
    lam_rows = jnp.concatenate([lam_flat.real, lam_flat.imag], axis=0).astype(F32)
    mats = (in_blocks(b_bar.real), in_blocks(b_bar.imag), out_blocks(c_re), out_blocks(c_im),
            powers, d_skip.reshape(1, SSM_WIDTH))
    return mats, lam_rows


def _ab_out_kernel(yp_ref, ys_ref, wglu_ref, bglu_ref, wout_ref, h_ref, g_ref, o_ref):
    z = jax.nn.gelu(ys_ref[...])
    gate = jnp.dot(z.astype(BF16), wglu_ref[...], preferred_element_type=F32) + bglu_ref[...]
    y_ssm = z * jax.nn.sigmoid(gate)
    y = (jnp.dot(yp_ref[...], wout_ref[0:POOL_WIDTH, :], preferred_element_type=F32)
         + jnp.dot(y_ssm.astype(BF16), wout_ref[POOL_WIDTH:, :], preferred_element_type=F32))
    o_ref[...] = h_ref[...] + _rms(y, g_ref[...])


def _ab_out(y_pool, y_ssm, w_glu, b_glu, w_out, h, g, tm):
    m = h.shape[0]
    row = lambda width: pl.BlockSpec((tm, width), lambda i: (i, 0))
    return pl.pallas_call(
        _ab_out_kernel,
        grid=(m // tm,),
        in_specs=[row(POOL_WIDTH), row(SSM_WIDTH), _const_spec(w_glu.shape), _const_spec(b_glu.shape),
                  _const_spec(w_out.shape), row(D_MODEL), _const_spec(g.shape)],
        out_specs=row(D_MODEL),
        out_shape=jax.ShapeDtypeStruct((m, D_MODEL), F32),
        compiler_params=_cparams(1),
        name="ab_out",
    )(y_pool, y_ssm, w_glu, b_glu, w_out, h, g)


def _ffn_kernel(h_ref, gin_ref, wg_ref, wu_ref, wd_ref, gout_ref, o_ref):
    h = h_ref[...]
    xb = _rms(h, gin_ref[...]).astype(BF16)
    gate = jnp.dot(xb, wg_ref[...], preferred_element_type=F32)
    up = jnp.dot(xb, wu_ref[...], preferred_element_type=F32)
    act = (gate * jax.nn.sigmoid(gate) * up).astype(BF16)
    y = jnp.dot(act, wd_ref[...], preferred_element_type=F32)
    o_ref[...] = h + _rms(y, gout_ref[...])


def _ffn(h, g_in, w_gate, w_up, w_down, g_out, tm):
    m = h.shape[0]
    row = pl.BlockSpec((tm, D_MODEL), lambda i: (i, 0))
    return pl.pallas_call(
        _ffn_kernel,
        grid=(m // tm,),
        in_specs=[row, _const_spec(g_in.shape), _const_spec(w_gate.shape), _const_spec(w_up.shape),
                  _const_spec(w_down.shape), _const_spec(g_out.shape)],
        out_specs=row,
        out_shape=jax.ShapeDtypeStruct((m, D_MODEL), F32),
        compiler_params=_cparams(1),
        name="ffn",
    )(h, g_in, w_gate, w_up, w_down, g_out)


def _attn_prompt_kernel(q_ref, kc_ref, kp_ref, vc_ref, vp_ref, o_ref, lse_ref):
    blk = pl.program_id(2)
    rows = 2 * ATT_BLOCK
    qi = lax.broadcasted_iota(jnp.int32, (rows, 2 * ATT_BLOCK), 0) & (ATT_BLOCK - 1)
    kj = lax.broadcasted_iota(jnp.int32, (rows, 2 * ATT_BLOCK), 1)
    dist = ATT_BLOCK + qi - kj
    valid = (dist >= 0) & (dist <= ATT_SPAN) & ((kj >= ATT_BLOCK) | (blk > 0))
    lane = lax.broadcasted_iota(jnp.int32, (ATT_BLOCK, LANES), 1)
    first_head = lane < HEAD_DIM
    lse_all = jnp.zeros((ATT_BLOCK, LANES), F32)
    for pair in range(ATT_HEADS // 2):
        sl = slice(pair * LANES, (pair + 1) * LANES)
        q2 = q_ref[:, sl]
        zero = jnp.zeros_like(q2)
        qs = jnp.concatenate([jnp.where(first_head, q2, zero), jnp.where(first_head, zero, q2)], axis=0)
        k2 = jnp.concatenate([kp_ref[:, sl], kc_ref[:, sl]], axis=0)
        v2 = jnp.concatenate([vp_ref[:, sl], vc_ref[:, sl]], axis=0)
        s = lax.dot_general(qs, k2, (((1,), (1,)), ((), ())), preferred_element_type=F32)
        s = jnp.where(valid, s, -jnp.inf)
        m = jnp.max(s, axis=-1, keepdims=True)
        p = jnp.exp(s - m)
        den = jnp.sum(p, axis=-1, keepdims=True)
        o = jnp.dot(p.astype(BF16), v2, preferred_element_type=F32) / den
        lse = m + jnp.log(den)
        o_ref[:, sl] = jnp.where(first_head, o[:ATT_BLOCK], o[ATT_BLOCK:])
        lse_all = jnp.where(lane == 2 * pair, lse[:ATT_BLOCK],
                            jnp.where(lane == 2 * pair + 1, lse[ATT_BLOCK:], lse_all))
    lse_ref[...] = lse_all


def _attn_prompt(q, k, v, dil):
    batch, _, sub, _ = q.shape
    nblk = sub // ATT_BLOCK
    cur = pl.BlockSpec((None, None, ATT_BLOCK, ATT_WIDTH), lambda b, r, i: (b, r, i, 0))
    prev = pl.BlockSpec((None, None, ATT_BLOCK, ATT_WIDTH), lambda b, r, i: (b, r, jnp.maximum(i - 1, 0), 0))
    return pl.pallas_call(
        _attn_prompt_kernel,
        grid=(batch, dil, nblk),
        in_specs=[cur, cur, prev, cur, prev],
        out_specs=[cur, pl.BlockSpec((None, None, ATT_BLOCK, LANES), lambda b, r, i: (b, r, i, 0))],
        out_shape=[jax.ShapeDtypeStruct((batch, dil, sub, ATT_WIDTH), F32),
                   jax.ShapeDtypeStruct((batch, dil, sub, LANES), F32)],
        compiler_params=_cparams(3),
        name="attn_prompt",
    )(q, k, k, v, v)


SAMPLE_DENSE_WINDOW = DILATED_PATTERNS[1][0]
SAMPLE_FAR_DIL = DILATED_PATTERNS[2][1]
SAMPLE_NEAR = SAMPLE_DENSE_WINDOW
SAMPLE_FAR_GROUPS = (MAX_WINDOW - SAMPLE_NEAR) // SAMPLE_FAR_DIL


def _sample_key_rows(steps):
    return SAMPLE_FAR_GROUPS * steps + SAMPLE_NEAR + LANES


def _attn_sample_kernel(qblk_ref, knew_ref, vnew_ref, kfar_ref, vfar_ref, knear_ref, vnear_ref,
                        o_ref, kk, vv, *, steps):
    n_far = SAMPLE_FAR_GROUPS * steps
    n_cache = n_far + SAMPLE_NEAR
    n_keys = n_cache + LANES
    kk[0:n_far, :] = kfar_ref[0].astype(BF16)
    vv[0:n_far, :] = vfar_ref[0].astype(BF16)
    kk[n_far:n_cache, :] = knear_ref[0].astype(BF16)
    vv[n_far:n_cache, :] = vnear_ref[0].astype(BF16)
    kk[n_cache:n_keys, :] = knew_ref[0].astype(BF16)
    vv[n_cache:n_keys, :] = vnew_ref[0].astype(BF16)
    s = jnp.dot(kk[...], qblk_ref[0], preferred_element_type=F32)
    row = lax.broadcasted_iota(jnp.int32, (n_keys, LANES), 0)
    col = lax.broadcasted_iota(jnp.int32, (n_keys, LANES), 1)
    far_pos = (row >> (steps.bit_length() - 1)) * SAMPLE_FAR_DIL + (row & (steps - 1))
    pos = jnp.where(row < n_far, far_pos, row + (MAX_WINDOW - n_cache))
    dist = MAX_WINDOW + (col >> 4) - pos
    mult = jnp.zeros((n_keys, LANES), F32)
    for window, dil in DILATED_PATTERNS:
        hit = (dist >= 0) & (dist <= window) & ((dist & (dil - 1)) == 0)
        mult = mult + hit.astype(F32)
    s = jnp.where(mult > 0.0, s, -jnp.inf)
    m = jnp.max(s, axis=0, keepdims=True)
    p = mult * jnp.exp(s - m)
    p_t = p.T
    den = jnp.sum(p_t, axis=-1, keepdims=True)
    o = jnp.dot(p_t.astype(BF16), vv[...], preferred_element_type=F32) / den
    orow = lax.broadcasted_iota(jnp.int32, (LANES, ATT_WIDTH), 0)
    ocol = lax.broadcasted_iota(jnp.int32, (LANES, ATT_WIDTH), 1)
    o = jnp.where((ocol >> 6) == (orow & (ATT_HEADS - 1)), o, 0.0)
    for t in range(steps):
        o_ref[0, t:t + 1, :] = jnp.sum(o[t * ATT_HEADS:(t + 1) * ATT_HEADS, :], axis=0, keepdims=True)


def _sample_cache_rows(cache, steps):
    n = cache.shape[0]
    old = MAX_WINDOW - SAMPLE_NEAR
    far = cache[:, :old].reshape(n, SAMPLE_FAR_GROUPS, SAMPLE_FAR_DIL, ATT_WIDTH)[:, :, :steps]
    return far.reshape(n, SAMPLE_FAR_GROUPS * steps, ATT_WIDTH), cache[:, old:].reshape(n, SAMPLE_NEAR, ATT_WIDTH)


def _attn_sample(qblk, k_new, v_new, k_far, v_far, k_near, v_near, steps):
    n = qblk.shape[0]
    assert steps & (steps - 1) == 0 and steps <= SAMPLE_FAR_DIL and steps * ATT_HEADS <= LANES
    n_keys = _sample_key_rows(steps)
    per = lambda a: pl.BlockSpec((1,) + a.shape[1:], lambda b: (b, 0, 0))
    args = (qblk, k_new, v_new, k_far, v_far, k_near, v_near)
    return pl.pallas_call(
        functools.partial(_attn_sample_kernel, steps=steps),
        grid=(n,),
        in_specs=[per(a) for a in args],
        out_specs=pl.BlockSpec((1, steps, ATT_WIDTH), lambda b: (b, 0, 0)),
        out_shape=jax.ShapeDtypeStruct((n, steps, ATT_WIDTH), F32),
        scratch_shapes=[pltpu.VMEM((n_keys, ATT_WIDTH), BF16), pltpu.VMEM((n_keys, ATT_WIDTH), BF16)],
        compiler_params=_cparams(1),
        name="attn_sample",
    )(*args)


def _attn_out_kernel(*refs, tm):
    n_pat = len(ATT_DILATIONS)
    o_refs = refs[:n_pat]
    lse_refs = refs[n_pat:2 * n_pat]
    w_ref, h_ref, g_ref, out_ref, o_nat, lse_nat = refs[2 * n_pat:]
    for pat, dil in enumerate(ATT_DILATIONS):
        rows = tm // dil
        for res in range(dil):
            dst = pl.ds(res, rows, stride=dil) if dil > 1 else pl.ds(0, rows)
            lse_nat[pat, dst, :] = lse_refs[pat][0, res]
            for c in range(COL_TILES):
                o_nat[pat * COL_TILES + c, dst, :] = o_refs[pat][0, res, :, c * LANES:(c + 1) * LANES]
    lses = [lse_nat[pat] for pat in range(n_pat)]
    top = functools.reduce(jnp.maximum, lses)
    es = [jnp.exp(l - top) for l in lses]
    total = functools.reduce(lambda x, y: x + y, es)
    wgts = [e / total for e in es]
    lane = lax.broadcasted_iota(jnp.int32, (tm, LANES), 1)
    first_head = lane < HEAD_DIM
    pieces = []
    for pair in range(ATT_HEADS // 2):
        acc = None
        for pat, wgt in enumerate(wgts):
            w2 = jnp.where(first_head, wgt[:, 2 * pair:2 * pair + 1], wgt[:, 2 * pair + 1:2 * pair + 2])
            term = w2 * o_nat[pat * COL_TILES + pair]
            acc = term if acc is None else acc + term
        pieces.append(acc.astype(BF16))
    a = jnp.concatenate(pieces, axis=1)
    y = jnp.dot(a, w_ref[...], preferred_element_type=F32)
    out_ref[...] = h_ref[...] + _rms(y, g_ref[...])


def _attn_out(outs, lses, w_o, h, g, seq, tm):
    m = h.shape[0]
    tiles = seq // tm
    n_pat = len(ATT_DILATIONS)

    def dil_spec(dil, width):
        return pl.BlockSpec((1, dil, tm // dil, width), lambda i: (i // tiles, 0, i % tiles, 0))

    row = pl.BlockSpec((tm, D_MODEL), lambda i: (i, 0))
    return pl.pallas_call(
        functools.partial(_attn_out_kernel, tm=tm),
        grid=(m // tm,),
        in_specs=[dil_spec(dil, ATT_WIDTH) for dil in ATT_DILATIONS]
        + [dil_spec(dil, LANES) for dil in ATT_DILATIONS]
        + [_const_spec(w_o.shape), row, _const_spec(g.shape)],
        out_specs=row,
        out_shape=jax.ShapeDtypeStruct((m, D_MODEL), F32),
        scratch_shapes=[pltpu.VMEM((n_pat * COL_TILES, tm, LANES), F32), pltpu.VMEM((n_pat, tm, LANES), F32)],
        compiler_params=_cparams(1),
        name="attn_out",
    )(*outs, *lses, w_o, h, g)


def _proj_out_kernel(a_ref, w_ref, h_ref, g_ref, out_ref):
    y = jnp.dot(a_ref[...].astype(BF16), w_ref[...], preferred_element_type=F32)
    out_ref[...] = h_ref[...] + _rms(y, g_ref[...])


def _proj_out(a, w, h, g, tm):
    m = h.shape[0]
    return pl.pallas_call(
        _proj_out_kernel,
        grid=(m // tm,),
        in_specs=[pl.BlockSpec((tm, a.shape[1]), lambda i: (i, 0)), _const_spec(w.shape),
                  pl.BlockSpec((tm, D_MODEL), lambda i: (i, 0)), _const_spec(g.shape)],
        out_specs=pl.BlockSpec((tm, D_MODEL), lambda i: (i, 0)),
        out_shape=jax.ShapeDtypeStruct((m, D_MODEL), F32),
        compiler_params=_cparams(1),
        name="proj_out",
    )(a, w, h, g)


def kernel(x_prompt, x_sample, state_pool, state_s5, cache_k, cache_v, norm_gains, ab_w_in, ab_pool_w, ab_pool_scale, ab_lambda_re, ab_lambda_im, ab_log_dt, ab_b_re, ab_b_im, ab_c_re, ab_c_im, ab_d, ab_w_glu, ab_b_glu, ab_w_out, c_w_qkv, c_w_o, ffn_w_gate, ffn_w_up, ffn_w_down):
    batch, seq, d = x_prompt.shape
    n_dec, t_dec, _ = x_sample.shape
    mp, ms = batch * seq, n_dec * t_dec
    tm_p = 256
    gains = norm_gains.reshape(norm_gains.shape[0], 4, 1, d)

    hp = x_prompt.reshape(mp, d)
    hs = x_sample.reshape(ms, d)

    g = gains[0]
    w_in = ab_w_in[0].astype(BF16)
    (proj_p,) = _norm_matmul(hp, g[0], w_in, ((0, d, 1.0),), (F32,), tm_p)
    (proj_s,) = _norm_matmul(hs, g[0], w_in, ((0, d, 1.0),), (F32,), ms)
    proj_p = proj_p.reshape(batch, seq, d)
    proj_s = proj_s.reshape(n_dec, t_dec, d)

    pool_w = ab_pool_w[0].astype(BF16)
    pool_scale = ab_pool_scale[0].reshape(1, POOL_WIDTH)
    lead_p = jnp.zeros((batch, POOL_LEAD, POOL_WIDTH), F32)
    lead_s = jnp.pad(state_pool[0], ((0, 0), (POOL_LEAD - POOL_STATE, 0), (0, 0)))
    ypool_p = _pool_mixer(proj_p, lead_p, pool_w, pool_scale, 256, 0)
    ypool_s = _pool_mixer(proj_s, lead_s, pool_w, pool_scale, t_dec, PAST_LEN)
    pool_prompt = proj_p[:, seq - POOL_STATE:, :POOL_WIDTH][None]
    pool_sample = jnp.concatenate([state_pool[0], proj_s[:, :, :POOL_WIDTH]], axis=1)[:, -POOL_STATE:][None]

    mats, lam_rows = _s5_matrices(ab_lambda_re[0], ab_lambda_im[0], ab_log_dt[0], ab_b_re[0], ab_b_im[0],
                                  ab_c_re[0], ab_c_im[0], ab_d[0])
    yssm_p, hre_p, him_p = _s5_prompt(proj_p, mats, 256)
    u_tb = jnp.swapaxes(proj_s[:, :, POOL_WIDTH:], 0, 1)
    h0 = state_s5[0].reshape(n_dec, SSM_STATES, 2)
    yssm_tb, hre_s, him_s = _s5_sample(u_tb, h0[..., 0], h0[..., 1], mats, lam_rows)
    yssm_s = jnp.swapaxes(yssm_tb, 0, 1)
    s5_prompt = jnp.stack([hre_p[:, 0], him_p[:, 0]], axis=-1).reshape(1, batch, SSM_GROUPS, SSM_STATE, 2)
    s5_sample = jnp.stack([hre_s, him_s], axis=-1).reshape(1, n_dec, SSM_GROUPS, SSM_STATE, 2)

    w_glu = ab_w_glu[0].astype(BF16)
    b_glu = ab_b_glu[0].reshape(1, SSM_WIDTH)
    w_out = ab_w_out[0].astype(BF16)
    hp = _ab_out(ypool_p.reshape(mp, POOL_WIDTH), yssm_p.reshape(mp, SSM_WIDTH), w_glu, b_glu, w_out, hp, g[1], tm_p)
    hs = _ab_out(ypool_s.reshape(ms, POOL_WIDTH), yssm_s.reshape(ms, SSM_WIDTH), w_glu, b_glu, w_out, hs, g[1], ms)

    wg, wu, wd = ffn_w_gate[0].astype(BF16), ffn_w_up[0].astype(BF16), ffn_w_down[0].astype(BF16)
    hp = _ffn(hp, g[2], wg, wu, wd, g[3], tm_p)
    hs = _ffn(hs, g[2], wg, wu, wd, g[3], ms)

    g = gains[1]
    w_qkv = c_w_qkv[0].astype(BF16)
    w_o = c_w_o[0].astype(BF16)
    keep = min(MAX_WINDOW, seq)
    *qkv_p, kv_tail = _qkv_prompt(hp, g[0], w_qkv, batch, seq, keep, tm_p)
    k_prompt = kv_tail[:, :, :ATT_WIDTH].reshape(1, batch, keep, ATT_HEADS, HEAD_DIM)
    v_prompt = kv_tail[:, :, ATT_WIDTH:].reshape(1, batch, keep, ATT_HEADS, HEAD_DIM)
    n_dil = len(ATT_DILATIONS)
    outs, lses = [], []
    for pat, dil in enumerate(ATT_DILATIONS):
        o, lse = _attn_prompt(qkv_p[pat], qkv_p[n_dil + pat], qkv_p[2 * n_dil + pat], dil)
        outs.append(o)
        lses.append(lse)
    hp = _attn_out(outs, lses, w_o, hp, g[1], seq, tm_p)

    scale = HEAD_DIM ** -0.5
    q_s, k_s, v_s = _norm_matmul(
        hs, g[0], w_qkv,
        ((0, ATT_WIDTH, scale), (ATT_WIDTH, ATT_WIDTH, 1.0), (2 * ATT_WIDTH, ATT_WIDTH, 1.0)),
        (F32, F32, F32), ms)
    k_sample = k_s.reshape(1, n_dec, t_dec, ATT_HEADS, HEAD_DIM)
    v_sample = v_s.reshape(1, n_dec, t_dec, ATT_HEADS, HEAD_DIM)
    q4 = q_s.reshape(n_dec, t_dec, ATT_HEADS, HEAD_DIM)
    qblk = jnp.einsum("bthe,hg->bhetg", q4, jnp.eye(ATT_HEADS, dtype=F32))
    qblk = qblk.reshape(n_dec, ATT_WIDTH, t_dec * ATT_HEADS)
    qblk = jnp.pad(qblk, ((0, 0), (0, 0), (0, LANES - t_dec * ATT_HEADS))).astype(BF16)
    pad_new = lambda a: jnp.pad(a.reshape(n_dec, t_dec, ATT_WIDTH), ((0, 0), (0, LANES - t_dec), (0, 0)))
    k_far, k_near = _sample_cache_rows(cache_k[0], t_dec)
    v_far, v_near = _sample_cache_rows(cache_v[0], t_dec)
    o_s = _attn_sample(qblk, pad_new(k_s), pad_new(v_s), k_far, v_far, k_near, v_near, t_dec)
    hs = _proj_out(o_s.reshape(ms, ATT_WIDTH), w_o, hs, g[1], ms)

    wg, wu, wd = ffn_w_gate[1].astype(BF16), ffn_w_up[1].astype(BF16), ffn_w_down[1].astype(BF16)
    hp = _ffn(hp, g[2], wg, wu, wd, g[3], tm_p)
    hs = _ffn(hs, g[2], wg, wu, wd, g[3], ms)

    return (hp.reshape(batch, seq, d), hs.reshape(n_dec, t_dec, d), pool_prompt, s5_prompt,
            k_prompt, v_prompt, pool_sample, s5_sample, k_sample, v_sample)
```

```python
import functools
import math

import jax
import jax.numpy as jnp
from jax import lax
from jax.experimental import pallas as pl
from jax.experimental.pallas import tpu as pltpu

F32 = jnp.float32
BF16 = jnp.bfloat16

D_MODEL = 1024
PAST_LEN = 16384
POOL_WIDTH = 512
POOL_WINDOWS = (2, 4, 8, 16)
POOL_GROUP = 128
POOL_STATE = 15
POOL_LEAD = 16
SSM_WIDTH = 512
SSM_GROUP = 16
SSM_GROUPS = 32
SSM_STATE = 64
SSM_STATES = SSM_GROUPS * SSM_STATE
ATT_HEADS = 16
HEAD_DIM = 64
ATT_WIDTH = ATT_HEADS * HEAD_DIM
DILATED_PATTERNS = ((128, 1), (512, 4), (2048, 16))
MAX_WINDOW = 2048
ATT_BLOCK = 128
ATT_SPAN = 128
FFN_HIDDEN = 2816
RMS_EPS = 1e-6

SUBLANES = 8
LANES = 128
VMEM_LIMIT = 56 * 1024 * 1024


def _cparams(n_axes):
    return pltpu.CompilerParams(
        dimension_semantics=("arbitrary",) * n_axes, vmem_limit_bytes=VMEM_LIMIT)


def _rms(x, g):
    ms = jnp.mean(x * x, axis=-1, keepdims=True)
    return (x * lax.rsqrt(ms + RMS_EPS)) * g


def _const_spec(shape):
    zeros = (0,) * len(shape)
    return pl.BlockSpec(shape, lambda *_: zeros, pipeline_mode=pl.Buffered(1))


def _norm_matmul_kernel(x_ref, g_ref, w_ref, *out_refs, splits):
    xb = _rms(x_ref[...], g_ref[...]).astype(BF16)
    for (c0, width, scale), o_ref in zip(splits, out_refs):
        y = jnp.dot(xb, w_ref[:, c0:c0 + width], preferred_element_type=F32)
        if scale != 1.0:
            y = y * scale
        o_ref[...] = y.astype(o_ref.dtype)


def _norm_matmul(x, g, w, splits, dtypes, tm):
    m, d = x.shape
    n = w.shape[1]
    return pl.pallas_call(
        functools.partial(_norm_matmul_kernel, splits=splits),
        grid=(m // tm,),
        in_specs=[pl.BlockSpec((tm, d), lambda i: (i, 0)), _const_spec((1, d)), _const_spec((d, n))],
        out_specs=[pl.BlockSpec((tm, width), lambda i: (i, 0)) for _, width, _ in splits],
        out_shape=[jax.ShapeDtypeStruct((m, width), dt) for (_, width, _), dt in zip(splits, dtypes)],
        compiler_params=_cparams(1),
        name="norm_matmul",
    )(x, g, w)


ATT_DILATIONS = tuple(dil for _, dil in DILATED_PATTERNS)
COL_TILES = ATT_WIDTH // LANES


def _qkv_prompt_kernel(x_ref, g_ref, w_ref, perm_ref, *refs, q_scale, tm):
    n_dil = len(ATT_DILATIONS)
    out_refs = refs[:3 * n_dil]
    tail_ref = refs[3 * n_dil]
    xb = _rms(x_ref[...], g_ref[...]).astype(BF16)
    for which in range(3):
        y = jnp.dot(xb, w_ref[:, which * ATT_WIDTH:(which + 1) * ATT_WIDTH], preferred_element_type=F32)
        if which == 0:
            y = y * q_scale
        else:
            tail_ref[0, :, (which - 1) * ATT_WIDTH:which * ATT_WIDTH] = y
        yb = y.astype(BF16)
        for pat, (dil, o_ref) in enumerate(zip(ATT_DILATIONS, out_refs[which * n_dil:(which + 1) * n_dil])):
            if dil == 1:
                o_ref[0, 0] = yb
                continue
            moved = jnp.dot(perm_ref[pat - 1], yb, preferred_element_type=F32).astype(BF16)
            rows = tm // dil
            for res in range(dil):
                o_ref[0, res] = moved[res * rows:(res + 1) * rows]


def _qkv_prompt(x, g, w, batch, seq, keep, tm):
    m, d = x.shape
    tiles = seq // tm
    first_tail = tiles - keep // tm

    def tail_map(i):
        return (i // tiles, jnp.maximum(i % tiles - first_tail, 0), 0)

    def dil_spec(dil):
        return pl.BlockSpec((1, dil, tm // dil, ATT_WIDTH), lambda i: (i // tiles, 0, i % tiles, 0))

    def dil_shape(dil):
        return jax.ShapeDtypeStruct((batch, dil, seq // dil, ATT_WIDTH), BF16)

    assert ATT_DILATIONS[0] == 1
    src = jnp.arange(tm)[None, :]
    dst = jnp.arange(tm)[:, None]
    perm = jnp.stack([(src == (dst % (tm // dil)) * dil + dst // (tm // dil)).astype(BF16)
                      for dil in ATT_DILATIONS[1:]])
    return pl.pallas_call(
        functools.partial(_qkv_prompt_kernel, q_scale=HEAD_DIM ** -0.5, tm=tm),
        grid=(m // tm,),
        in_specs=[pl.BlockSpec((tm, d), lambda i: (i, 0)), _const_spec((1, d)),
                  _const_spec((d, 3 * ATT_WIDTH)), _const_spec(perm.shape)],
        out_specs=[dil_spec(dil) for _ in range(3) for dil in ATT_DILATIONS]
        + [pl.BlockSpec((1, tm, 2 * ATT_WIDTH), tail_map)],
        out_shape=[dil_shape(dil) for _ in range(3) for dil in ATT_DILATIONS]
        + [jax.ShapeDtypeStruct((batch, keep, 2 * ATT_WIDTH), F32)],
        compiler_params=_cparams(1),
        name="qkv_prompt",
    )(x, g, w, perm)


def _pool_kernel(u_ref, lead_ref, w_ref, scale_ref, y_ref, ext, *, tt, pos0, carry):
    j = pl.program_id(1)

    @pl.when(j == 0)
    def _():
        ext[0:POOL_LEAD, :] = lead_ref[0]

    u = u_ref[0]
    ext[POOL_LEAD:POOL_LEAD + tt, :] = u
    pos = pos0 + j * tt + lax.broadcasted_iota(jnp.int32, (tt, POOL_GROUP), 0)
    for grp, window in enumerate(POOL_WINDOWS):
        sl = slice(grp * POOL_GROUP, (grp + 1) * POOL_GROUP)
        u_g = u[:, sl]
        win_sum = u_g
        for k in range(1, window):
            win_sum = win_sum + ext[POOL_LEAD - k:POOL_LEAD - k + tt, sl]
        count = jnp.minimum(pos + 1, window).astype(F32)
        diff = win_sum / count - u_g
        y = jnp.dot(diff.astype(BF16), w_ref[grp], preferred_element_type=F32)
        y_ref[0, :, sl] = (y * scale_ref[:, sl]).astype(y_ref.dtype)
    if carry:
        ext[0:POOL_LEAD, :] = ext[tt:tt + POOL_LEAD, :]


def _pool_mixer(proj, lead, w, scale, tt, pos0):
    n, t, _ = proj.shape
    steps = t // tt
    return pl.pallas_call(
        functools.partial(_pool_kernel, tt=tt, pos0=pos0, carry=steps > 1),
        grid=(n, steps),
        in_specs=[pl.BlockSpec((1, tt, POOL_WIDTH), lambda b, j: (b, j, 0)),
                  pl.BlockSpec((1, POOL_LEAD, POOL_WIDTH), lambda b, j: (b, 0, 0)),
                  _const_spec((len(POOL_WINDOWS), POOL_GROUP, POOL_GROUP)),
                  _const_spec((1, POOL_WIDTH))],
        out_specs=pl.BlockSpec((1, tt, POOL_WIDTH), lambda b, j: (b, j, 0)),
        out_shape=jax.ShapeDtypeStruct((n, t, POOL_WIDTH), BF16),
        scratch_shapes=[pltpu.VMEM((POOL_LEAD + tt, POOL_WIDTH), F32)],
        compiler_params=_cparams(2),
        name="pool_mixer",
    )(proj, lead, w, scale)


S5_IN_HALF = SSM_WIDTH // 2
S5_STATE_HALF = SSM_STATES // 2
S5_OUT_TILE = LANES
S5_STATE_TILE = SSM_STATES // (SSM_WIDTH // S5_OUT_TILE)
S5_SCAN_LANES = 512


def _s5_input(ub, bre_ref, bim_ref):
    re, im = [], []
    for half in range(2):
        uk = ub[:, half * S5_IN_HALF:(half + 1) * S5_IN_HALF]
        re.append(jnp.dot(uk, bre_ref[half], preferred_element_type=F32))
        im.append(jnp.dot(uk, bim_ref[half], preferred_element_type=F32))
    return re, im


def _s5_output(h_re, h_im, cre_ref, cim_ref, tile):
    return (jnp.dot(h_re, cre_ref[tile], preferred_element_type=F32)
            - jnp.dot(h_im, cim_ref[tile], preferred_element_type=F32))


def _s5_prompt_kernel(u_ref, bre_ref, bim_ref, cre_ref, cim_ref, pw_ref, d_ref,
                      y_ref, hre_ref, him_ref, s_re, s_im, c_re, c_im, *, tt):
    j = pl.program_id(1)

    @pl.when(j == 0)
    def _():
        c_re[...] = jnp.zeros_like(c_re)
        c_im[...] = jnp.zeros_like(c_im)

    u = u_ref[0]
    bu_re, bu_im = _s5_input(u.astype(BF16), bre_ref, bim_ref)
    for half in range(2):
        s_re[:, half * S5_STATE_HALF:(half + 1) * S5_STATE_HALF] = bu_re[half]
        s_im[:, half * S5_STATE_HALF:(half + 1) * S5_STATE_HALF] = bu_im[half]

    for chunk in range(SSM_STATES // S5_SCAN_LANES):
        sl = slice(chunk * S5_SCAN_LANES, (chunk + 1) * S5_SCAN_LANES)

        def body(r, carry, sl=sl):
            in_re, in_im = carry
            row = pl.multiple_of(r * SUBLANES, SUBLANES)
            re = s_re[pl.ds(row, SUBLANES), sl]
            im = s_im[pl.ds(row, SUBLANES), sl]
            for level, shift in enumerate((1, 2, 4)):
                a_re = pw_ref[2 * level, :, sl]
                a_im = pw_ref[2 * level + 1, :, sl]
                sh_re = pltpu.roll(re, shift, 0)
                sh_im = pltpu.roll(im, shift, 0)
                re, im = (re + a_re * sh_re - a_im * sh_im,
                          im + a_re * sh_im + a_im * sh_re)
            p_re = pw_ref[6, :, sl]
            p_im = pw_ref[7, :, sl]
            re, im = (re + p_re * in_re - p_im * in_im,
                      im + p_re * in_im + p_im * in_re)
            s_re[pl.ds(row, SUBLANES), sl] = re
            s_im[pl.ds(row, SUBLANES), sl] = im
            last = SUBLANES - 1
            return (jnp.broadcast_to(re[last:last + 1, :], re.shape),
                    jnp.broadcast_to(im[last:last + 1, :], im.shape))

        out_re, out_im = lax.fori_loop(0, tt // SUBLANES, body, (c_re[:, sl], c_im[:, sl]))
        c_re[:, sl] = out_re
        c_im[:, sl] = out_im

    hre_ref[0] = c_re[...]
    him_ref[0] = c_im[...]
    for tile in range(SSM_WIDTH // S5_OUT_TILE):
        st = slice(tile * S5_STATE_TILE, (tile + 1) * S5_STATE_TILE)
        ot = slice(tile * S5_OUT_TILE, (tile + 1) * S5_OUT_TILE)
        y = _s5_output(s_re[:, st].astype(BF16), s_im[:, st].astype(BF16), cre_ref, cim_ref, tile)
        y_ref[0, :, ot] = y + d_ref[:, ot] * u[:, ot]


def _s5_prompt(proj, mats, tt):
    b, t, _ = proj.shape
    bre, bim, cre, cim, powers, d_skip = mats
    state = jax.ShapeDtypeStruct((b, SUBLANES, SSM_STATES), F32)
    state_spec = pl.BlockSpec((1, SUBLANES, SSM_STATES), lambda i, j: (i, 0, 0))
    return pl.pallas_call(
        functools.partial(_s5_prompt_kernel, tt=tt),
        grid=(b, t // tt),
        in_specs=[pl.BlockSpec((1, tt, SSM_WIDTH), lambda i, j: (i, j, 1)),
                  _const_spec(bre.shape), _const_spec(bim.shape),
                  _const_spec(cre.shape), _const_spec(cim.shape),
                  _const_spec(powers.shape), _const_spec(d_skip.shape)],
        out_specs=[pl.BlockSpec((1, tt, SSM_WIDTH), lambda i, j: (i, j, 0)), state_spec, state_spec],
        out_shape=[jax.ShapeDtypeStruct((b, t, SSM_WIDTH), F32), state, state],
        scratch_shapes=[pltpu.VMEM((tt, SSM_STATES), F32), pltpu.VMEM((tt, SSM_STATES), F32),
                        pltpu.VMEM((SUBLANES, SSM_STATES), F32), pltpu.VMEM((SUBLANES, SSM_STATES), F32)],
        compiler_params=_cparams(2),
        name="s5_prompt",
    )(proj, bre, bim, cre, cim, powers, d_skip)


def _s5_sample_kernel(u_ref, h0re_ref, h0im_ref, bre_ref, bim_ref, cre_ref, cim_ref, lam_ref, d_ref,
                      y_ref, hre_ref, him_ref, *, steps):
    h_re = h0re_ref[...]
    h_im = h0im_ref[...]
    lam_re = lam_ref[0:1, :]
    lam_im = lam_ref[1:2, :]
    for t in range(steps):
        u = u_ref[t]
        bu_re, bu_im = _s5_input(u.astype(BF16), bre_ref, bim_ref)
        bu_re = jnp.concatenate(bu_re, axis=1)
        bu_im = jnp.concatenate(bu_im, axis=1)
        h_re, h_im = (lam_re * h_re - lam_im * h_im + bu_re,
                      lam_re * h_im + lam_im * h_re + bu_im)
        hb_re = h_re.astype(BF16)
        hb_im = h_im.astype(BF16)
        for tile in range(SSM_WIDTH // S5_OUT_TILE):
            st = slice(tile * S5_STATE_TILE, (tile + 1) * S5_STATE_TILE)
            ot = slice(tile * S5_OUT_TILE, (tile + 1) * S5_OUT_TILE)
            y = _s5_output(hb_re[:, st], hb_im[:, st], cre_ref, cim_ref, tile)
            y_ref[t, :, ot] = y + d_ref[:, ot] * u[:, ot]
    hre_ref[...] = h_re
    him_ref[...] = h_im


def _s5_sample(u_tb, h0_re, h0_im, mats, lam):
    steps, n, _ = u_tb.shape
    bre, bim, cre, cim, _, d_skip = mats
    state = jax.ShapeDtypeStruct((n, SSM_STATES), F32)
    args = (u_tb, h0_re, h0_im, bre, bim, cre, cim, lam, d_skip)
    return pl.pallas_call(
        functools.partial(_s5_sample_kernel, steps=steps),
        grid=(1,),
        in_specs=[_const_spec(a.shape) for a in args],
        out_specs=[_const_spec(u_tb.shape), _const_spec((n, SSM_STATES)), _const_spec((n, SSM_STATES))],
        out_shape=[jax.ShapeDtypeStruct(u_tb.shape, F32), state, state],
        compiler_params=_cparams(1),
        name="s5_sample",
    )(*args)


def _s5_matrices(lam_re, lam_im, log_dt, b_re, b_im, c_re, c_im, d_skip):
    lam = lax.complex(lam_re, lam_im)
    dt = jnp.exp(log_dt)[:, None]
    lam_bar = jnp.exp(lam * dt)
    b_bar = ((lam_bar - 1.0) / lam)[..., None] * lax.complex(b_re, b_im)
    eye_half = jnp.eye(SSM_GROUPS // 2, dtype=F32)

    def in_blocks(x):
        x = x.reshape(2, SSM_GROUPS // 2, SSM_STATE, SSM_GROUP)
        return jnp.einsum("kgpi,gh->kgihp", x, eye_half).reshape(2, S5_IN_HALF, S5_STATE_HALF).astype(BF16)

    tiles = SSM_WIDTH // S5_OUT_TILE
    groups_per_tile = SSM_GROUPS // tiles
    eye_tile = jnp.eye(groups_per_tile, dtype=F32)

    def out_blocks(x):
        x = x.reshape(tiles, groups_per_tile, SSM_GROUP, SSM_STATE)
        return jnp.einsum("kgop,gh->kgpho", x, eye_tile).reshape(tiles, S5_STATE_TILE, S5_OUT_TILE).astype(BF16)

    lam_flat = lam_bar.reshape(1, SSM_STATES)
    rows = jnp.arange(SUBLANES)[:, None]
    planes = []
    for shift in (1, 2, 4):
        a = jnp.where(rows >= shift, lam_flat ** shift, 0.0)
        planes += [a.real, a.imag]
    carry = jnp.cumprod(jnp.broadcast_to(lam_flat, (SUBLANES, SSM_STATES)), axis=0)
    planes += [carry.real, carry.imag]
    powers = jnp.stack(planes).astype(F32)
    lam_rows = jnp.concatenate([lam_flat.real, lam_flat.imag], axis=0).astype(F32)
    mats = (in_blocks(b_bar.real), in_blocks(b_bar.imag), out_blocks(c_re), out_blocks(c_im),
            powers, d_skip.reshape(1, SSM_WIDTH))
    return mats, lam_rows


def _ab_out_kernel(yp_ref, ys_ref, wglu_ref, bglu_ref, wout_ref, h_ref, g_ref, o_ref):
    z = jax.nn.gelu(ys_ref[...])
    gate = jnp.dot(z.astype(BF16), wglu_ref[...], preferred_element_type=F32) + bglu_ref[...]
    y_ssm = z * jax.nn.sigmoid(gate)
    y = (jnp.dot(yp_ref[...], wout_ref[0:POOL_WIDTH, :], preferred_element_type=F32)
         + jnp.dot(y_ssm.astype(BF16), wout_ref[POOL_WIDTH:, :], preferred_element_type=F32))
    o_ref[...] = h_ref[...] + _rms(y, g_ref[...])


def _ab_out(y_pool, y_ssm, w_glu, b_glu, w_out, h, g, tm):
    m = h.shape[0]
    row = lambda width: pl.BlockSpec((tm, width), lambda i: (i, 0))
    return pl.pallas_call(
        _ab_out_kernel,
        grid=(m // tm,),
        in_specs=[row(POOL_WIDTH), row(SSM_WIDTH), _const_spec(w_glu.shape), _const_spec(b_glu.shape),
                  _const_spec(w_out.shape), row(D_MODEL), _const_spec(g.shape)],
        out_specs=row(D_MODEL),
        out_shape=jax.ShapeDtypeStruct((m, D_MODEL), F32),
        compiler_params=_cparams(1),
        name="ab_out",
    )(y_pool, y_ssm, w_glu, b_glu, w_out, h, g)


def _ffn_kernel(h_ref, gin_ref, wg_ref, wu_ref, wd_ref, gout_ref, o_ref):
    h = h_ref[...]
    xb = _rms(h, gin_ref[...]).astype(BF16)
    gate = jnp.dot(xb, wg_ref[...], preferred_element_type=F32)
    up = jnp.dot(xb, wu_ref[...], preferred_element_type=F32)
    act = (gate * jax.nn.sigmoid(gate) * up).astype(BF16)
    y = jnp.dot(act, wd_ref[...], preferred_element_type=F32)
    o_ref[...] = h + _rms(y, gout_ref[...])


def _ffn(h, g_in, w_gate, w_up, w_down, g_out, tm):
    m = h.shape[0]
    row = pl.BlockSpec((tm, D_MODEL), lambda i: (i, 0))
    return pl.pallas_call(
        _ffn_kernel,
        grid=(m // tm,),
        in_specs=[row, _const_spec(g_in.shape), _const_spec(w_gate.shape), _const_spec(w_up.shape),
                  _const_spec(w_down.shape), _const_spec(g_out.shape)],
        out_specs=row,
        out_shape=jax.ShapeDtypeStruct((m, D_MODEL), F32),
        compiler_params=_cparams(1),
        name="ffn",
    )(h, g_in, w_gate, w_up, w_down, g_out)


def _attn_prompt_kernel(q_ref, kc_ref, kp_ref, vc_ref, vp_ref, o_ref, lse_ref):
    blk = pl.program_id(2)
    rows = 2 * ATT_BLOCK
    qi = lax.broadcasted_iota(jnp.int32, (rows, 2 * ATT_BLOCK), 0) & (ATT_BLOCK - 1)
    kj = lax.broadcasted_iota(jnp.int32, (rows, 2 * ATT_BLOCK), 1)
    dist = ATT_BLOCK + qi - kj
    valid = (dist >= 0) & (dist <= ATT_SPAN) & ((kj >= ATT_BLOCK) | (blk > 0))
    lane = lax.broadcasted_iota(jnp.int32, (ATT_BLOCK, LANES), 1)
    first_head = lane < HEAD_DIM
    lse_all = jnp.zeros((ATT_BLOCK, LANES), F32)
    for pair in range(ATT_HEADS // 2):
        sl = slice(pair * LANES, (pair + 1) * LANES)
        q2 = q_ref[:, sl]
        zero = jnp.zeros_like(q2)
        qs = jnp.concatenate([jnp.where(first_head, q2, zero), jnp.where(first_head, zero, q2)], axis=0)
        k2 = jnp.concatenate([kp_ref[:, sl], kc_ref[:, sl]], axis=0)
        v2 = jnp.concatenate([vp_ref[:, sl], vc_ref[:, sl]], axis=0)
        s = lax.dot_general(qs, k2, (((1,), (1,)), ((), ())), preferred_element_type=F32)
        s = jnp.where(valid, s, -jnp.inf)
        m = jnp.max(s, axis=-1, keepdims=True)
        p = jnp.exp(s - m)
        den = jnp.sum(p, axis=-1, keepdims=True)
        o = jnp.dot(p.astype(BF16), v2, preferred_element_type=F32) / den
        lse = m + jnp.log(den)
        o_ref[:, sl] = jnp.where(first_head, o[:ATT_BLOCK], o[ATT_BLOCK:])
        lse_all = jnp.where(lane == 2 * pair, lse[:ATT_BLOCK],
                            jnp.where(lane == 2 * pair + 1, lse[ATT_BLOCK:], lse_all))
    lse_ref[...] = lse_all


def _attn_prompt(q, k, v, dil):
    batch, _, sub, _ = q.shape
    nblk = sub // ATT_BLOCK
    cur = pl.BlockSpec((None, None, ATT_BLOCK, ATT_WIDTH), lambda b, r, i: (b, r, i, 0))
    prev = pl.BlockSpec((None, None, ATT_BLOCK, ATT_WIDTH), lambda b, r, i: (b, r, jnp.maximum(i - 1, 0), 0))
    return pl.pallas_call(
        _attn_prompt_kernel,
        grid=(batch, dil, nblk),
        in_specs=[cur, cur, prev, cur, prev],
        out_specs=[cur, pl.BlockSpec((None, None, ATT_BLOCK, LANES), lambda b, r, i: (b, r, i, 0))],
        out_shape=[jax.ShapeDtypeStruct((batch, dil, sub, ATT_WIDTH), F32),
                   jax.ShapeDtypeStruct((batch, dil, sub, LANES), F32)],
        compiler_params=_cparams(3),
        name="attn_prompt",
    )(q, k, k, v, v)


def _pattern_multiplicity(dist):
    mult = jnp.zeros(dist.shape, F32)
    for window, dil in DILATED_PATTERNS:
        hit = (dist >= 0) & (dist <= window) & ((dist & (dil - 1)) == 0)
        mult = mult + hit.astype(F32)
    return mult


def _attn_sample_kernel(q_ref, knew_ref, vnew_ref, kt_ref, vt_ref, o_ref, *, steps):
    t_cache = lax.broadcasted_iota(jnp.int32, (SUBLANES, MAX_WINDOW), 0)
    pos = lax.broadcasted_iota(jnp.int32, (SUBLANES, MAX_WINDOW), 1)
    mult = _pattern_multiplicity(MAX_WINDOW + t_cache - pos)
    t_new = lax.broadcasted_iota(jnp.int32, (SUBLANES, LANES), 0)
    j_new = lax.broadcasted_iota(jnp.int32, (SUBLANES, LANES), 1)
    mult_new = jnp.where(j_new < steps, _pattern_multiplicity(t_new - j_new), 0.0)
    nt = (((1,), (1,)), ((), ()))
    for head in range(ATT_HEADS):
        q = q_ref[0, head].astype(BF16)
        s = jnp.dot(q, kt_ref[0, head].astype(BF16), preferred_element_type=F32)
        s_new = lax.dot_general(q, knew_ref[0, head].astype(BF16), nt, preferred_element_type=F32)
        s = jnp.where(mult > 0.0, s, -jnp.inf)
        s_new = jnp.where(mult_new > 0.0, s_new, -jnp.inf)
        m = jnp.maximum(jnp.max(s, axis=-1, keepdims=True), jnp.max(s_new, axis=-1, keepdims=True))
        p = mult * jnp.exp(s - m)
        p_new = mult_new * jnp.exp(s_new - m)
        den = jnp.sum(p, axis=-1, keepdims=True) + jnp.sum(p_new, axis=-1, keepdims=True)
        o = lax.dot_general(p.astype(BF16), vt_ref[0, head].astype(BF16), nt, preferred_element_type=F32)
        o = o + jnp.dot(p_new.astype(BF16), vnew_ref[0, head].astype(BF16), preferred_element_type=F32)
        o_ref[0, :, head * HEAD_DIM:(head + 1) * HEAD_DIM] = (o / den)[:steps]


def _attn_sample(q, k_new, v_new, cache_kt, cache_vt, steps):
    n = q.shape[0]
    assert steps <= SUBLANES
    per = lambda a: pl.BlockSpec((1,) + a.shape[1:], lambda b: (b, 0, 0, 0))
    args = (q, k_new, v_new, cache_kt, cache_vt)
    return pl.pallas_call(
        functools.partial(_attn_sample_kernel, steps=steps),
        grid=(n,),
        in_specs=[per(a) for a in args],
        out_specs=pl.BlockSpec((1, steps, ATT_WIDTH), lambda b: (b, 0, 0)),
        out_shape=jax.ShapeDtypeStruct((n, steps, ATT_WIDTH), F32),
        compiler_params=_cparams(1),
        name="attn_sample",
    )(*args)


def _attn_out_kernel(*refs, tm):
    n_pat = len(ATT_DILATIONS)
    o_refs = refs[:n_pat]
    lse_refs = refs[n_pat:2 * n_pat]
    w_ref, h_ref, g_ref, out_ref, o_nat, lse_nat = refs[2 * n_pat:]
    for pat, dil in enumerate(ATT_DILATIONS):
        rows = tm // dil
        for res in range(dil):
            dst = pl.ds(res, rows, stride=dil) if dil > 1 else pl.ds(0, rows)
            lse_nat[pat, dst, :] = lse_refs[pat][0, res]
            for c in range(COL_TILES):
                o_nat[pat * COL_TILES + c, dst, :] = o_refs[pat][0, res, :, c * LANES:(c + 1) * LANES]
    lses = [lse_nat[pat] for pat in range(n_pat)]
    top = functools.reduce(jnp.maximum, lses)
    es = [jnp.exp(l - top) for l in lses]
    total = functools.reduce(lambda x, y: x + y, es)
    wgts = [e / total for e in es]
    lane = lax.broadcasted_iota(jnp.int32, (tm, LANES), 1)
    first_head = lane < HEAD_DIM
    pieces = []
    for pair in range(ATT_HEADS // 2):
        acc = None
        for pat, wgt in enumerate(wgts):
            w2 = jnp.where(first_head, wgt[:, 2 * pair:2 * pair + 1], wgt[:, 2 * pair + 1:2 * pair + 2])
            term = w2 * o_nat[pat * COL_TILES + pair]
            acc = term if acc is None else acc + term
        pieces.append(acc.astype(BF16))
    a = jnp.concatenate(pieces, axis=1)
    y = jnp.dot(a, w_ref[...], preferred_element_type=F32)
    out_ref[...] = h_ref[...] + _rms(y, g_ref[...])


def _attn_out(outs, lses, w_o, h, g, seq, tm):
    m = h.shape[0]
    tiles = seq // tm
    n_pat = len(ATT_DILATIONS)

    def dil_spec(dil, width):
        return pl.BlockSpec((1, dil, tm // dil, width), lambda i: (i // tiles, 0, i % tiles, 0))

    row = pl.BlockSpec((tm, D_MODEL), lambda i: (i, 0))
    return pl.pallas_call(
        functools.partial(_attn_out_kernel, tm=tm),
        grid=(m // tm,),
        in_specs=[dil_spec(dil, ATT_WIDTH) for dil in ATT_DILATIONS]
        + [dil_spec(dil, LANES) for dil in ATT_DILATIONS]
        + [_const_spec(w_o.shape), row, _const_spec(g.shape)],
        out_specs=row,
        out_shape=jax.ShapeDtypeStruct((m, D_MODEL), F32),
        scratch_shapes=[pltpu.VMEM((n_pat * COL_TILES, tm, LANES), F32), pltpu.VMEM((n_pat, tm, LANES), F32)],
        compiler_params=_cparams(1),
        name="attn_out",
    )(*outs, *lses, w_o, h, g)


def _proj_out_kernel(a_ref, w_ref, h_ref, g_ref, out_ref):
    y = jnp.dot(a_ref[...].astype(BF16), w_ref[...], preferred_element_type=F32)
    out_ref[...] = h_ref[...] + _rms(y, g_ref[...])


def _proj_out(a, w, h, g, tm):
    m = h.shape[0]
    return pl.pallas_call(
        _proj_out_kernel,
        grid=(m // tm,),
        in_specs=[pl.BlockSpec((tm, a.shape[1]), lambda i: (i, 0)), _const_spec(w.shape),
                  pl.BlockSpec((tm, D_MODEL), lambda i: (i, 0)), _const_spec(g.shape)],
        out_specs=pl.BlockSpec((tm, D_MODEL), lambda i: (i, 0)),
        out_shape=jax.ShapeDtypeStruct((m, D_MODEL), F32),
        compiler_params=_cparams(1),
        name="proj_out",
    )(a, w, h, g)


def kernel(x_prompt, x_sample, state_pool, state_s5, cache_k, cache_v, norm_gains, ab_w_in, ab_pool_w, ab_pool_scale, ab_lambda_re, ab_lambda_im, ab_log_dt, ab_b_re, ab_b_im, ab_c_re, ab_c_im, ab_d, ab_w_glu, ab_b_glu, ab_w_out, c_w_qkv, c_w_o, ffn_w_gate, ffn_w_up, ffn_w_down):
    batch, seq, d = x_prompt.shape
    n_dec, t_dec, _ = x_sample.shape
    mp, ms = batch * seq, n_dec * t_dec
    tm_p = 256
    gains = norm_gains.reshape(norm_gains.shape[0], 4, 1, d)

    hp = x_prompt.reshape(mp, d)
    hs = x_sample.reshape(ms, d)

    g = gains[0]
    w_in = ab_w_in[0].astype(BF16)
    (proj_p,) = _norm_matmul(hp, g[0], w_in, ((0, d, 1.0),), (F32,), tm_p)
    (proj_s,) = _norm_matmul(hs, g[0], w_in, ((0, d, 1.0),), (F32,), ms)
    proj_p = proj_p.reshape(batch, seq, d)
    proj_s = proj_s.reshape(n_dec, t_dec, d)

    pool_w = ab_pool_w[0].astype(BF16)
    pool_scale = ab_pool_scale[0].reshape(1, POOL_WIDTH)
    lead_p = jnp.zeros((batch, POOL_LEAD, POOL_WIDTH), F32)
    lead_s = jnp.pad(state_pool[0], ((0, 0), (POOL_LEAD - POOL_STATE, 0), (0, 0)))
    ypool_p = _pool_mixer(proj_p, lead_p, pool_w, pool_scale, 256, 0)
    ypool_s = _pool_mixer(proj_s, lead_s, pool_w, pool_scale, t_dec, PAST_LEN)
    pool_prompt = proj_p[:, seq - POOL_STATE:, :POOL_WIDTH][None]
    pool_sample = jnp.concatenate([state_pool[0], proj_s[:, :, :POOL_WIDTH]], axis=1)[:, -POOL_STATE:][None]

    mats, lam_rows = _s5_matrices(ab_lambda_re[0], ab_lambda_im[0], ab_log_dt[0], ab_b_re[0], ab_b_im[0],
                                  ab_c_re[0], ab_c_im[0], ab_d[0])
    yssm_p, hre_p, him_p = _s5_prompt(proj_p, mats, 256)
    u_tb = jnp.swapaxes(proj_s[:, :, POOL_WIDTH:], 0, 1)
    h0 = state_s5[0].reshape(n_dec, SSM_STATES, 2)
    yssm_tb, hre_s, him_s = _s5_sample(u_tb, h0[..., 0], h0[..., 1], mats, lam_rows)
    yssm_s = jnp.swapaxes(yssm_tb, 0, 1)
    s5_prompt = jnp.stack([hre_p[:, 0], him_p[:, 0]], axis=-1).reshape(1, batch, SSM_GROUPS, SSM_STATE, 2)
    s5_sample = jnp.stack([hre_s, him_s], axis=-1).reshape(1, n_dec, SSM_GROUPS, SSM_STATE, 2)

    w_glu = ab_w_glu[0].astype(BF16)
    b_glu = ab_b_glu[0].reshape(1, SSM_WIDTH)
    w_out = ab_w_out[0].astype(BF16)
    hp = _ab_out(ypool_p.reshape(mp, POOL_WIDTH), yssm_p.reshape(mp, SSM_WIDTH), w_glu, b_glu, w_out, hp, g[1], tm_p)
    hs = _ab_out(ypool_s.reshape(ms, POOL_WIDTH), yssm_s.reshape(ms, SSM_WIDTH), w_glu, b_glu, w_out, hs, g[1], ms)

    wg, wu, wd = ffn_w_gate[0].astype(BF16), ffn_w_up[0].astype(BF16), ffn_w_down[0].astype(BF16)
    hp = _ffn(hp, g[2], wg, wu, wd, g[3], tm_p)
    hs = _ffn(hs, g[2], wg, wu, wd, g[3], ms)

    g = gains[1]
    w_qkv = c_w_qkv[0].astype(BF16)
    w_o = c_w_o[0].astype(BF16)
    keep = min(MAX_WINDOW, seq)
    *qkv_p, kv_tail = _qkv_prompt(hp, g[0], w_qkv, batch, seq, keep, tm_p)
    k_prompt = kv_tail[:, :, :ATT_WIDTH].reshape(1, batch, keep, ATT_HEADS, HEAD_DIM)
    v_prompt = kv_tail[:, :, ATT_WIDTH:].reshape(1, batch, keep, ATT_HEADS, HEAD_DIM)
    n_dil = len(ATT_DILATIONS)
    outs, lses = [], []
    for pat, dil in enumerate(ATT_DILATIONS):
        o, lse = _attn_prompt(qkv_p[pat], qkv_p[n_dil + pat], qkv_p[2 * n_dil + pat], dil)
        outs.append(o)
        lses.append(lse)
    hp = _attn_out(outs, lses, w_o, hp, g[1], seq, tm_p)

    scale = HEAD_DIM ** -0.5
    q_s, k_s, v_s = _norm_matmul(
        hs, g[0], w_qkv,
        ((0, ATT_WIDTH, scale), (ATT_WIDTH, ATT_WIDTH, 1.0), (2 * ATT_WIDTH, ATT_WIDTH, 1.0)),
        (F32, F32, F32), ms)
    k_sample = k_s.reshape(1, n_dec, t_dec, ATT_HEADS, HEAD_DIM)
    v_sample = v_s.reshape(1, n_dec, t_dec, ATT_HEADS, HEAD_DIM)
    head_major = lambda a, rows: jnp.pad(
        jnp.swapaxes(a.reshape(n_dec, t_dec, ATT_HEADS, HEAD_DIM), 1, 2),
        ((0, 0), (0, 0), (0, rows - t_dec), (0, 0)))
    cache_kt = jnp.transpose(cache_k[0], (0, 2, 3, 1))
    cache_vt = jnp.transpose(cache_v[0], (0, 2, 3, 1))
    o_s = _attn_sample(head_major(q_s, SUBLANES), head_major(k_s, LANES), head_major(v_s, LANES),
                       cache_kt, cache_vt, t_dec)
    hs = _proj_out(o_s.reshape(ms, ATT_WIDTH), w_o, hs, g[1], ms)

    wg, wu, wd = ffn_w_gate[1].astype(BF16), ffn_w_up[1].astype(BF16), ffn_w_down[1].astype(BF16)
    hp = _ffn(hp, g[2], wg, wu, wd, g[3], tm_p)
    hs = _ffn(hs, g[2], wg, wu, wd, g[3], ms)

    return (hp.reshape(batch, seq, d), hs.reshape(n_dec, t_dec, d), pool_prompt, s5_prompt,
            k_prompt, v_prompt, pool_sample, s5_sample, k_sample, v_sample)
```

```python
import functools
import math

import jax
import jax.numpy as jnp
from jax import lax
from jax.experimental import pallas as pl
from jax.experimental.pallas import tpu as pltpu

F32 = jnp.float32
BF16 = jnp.bfloat16

D_MODEL = 1024
PAST_LEN = 16384
POOL_WIDTH = 512
POOL_WINDOWS = (2, 4, 8, 16)
POOL_GROUP = 128
POOL_STATE = 15
POOL_LEAD = 16
SSM_WIDTH = 512
SSM_GROUP = 16
SSM_GROUPS = 32
SSM_STATE = 64
SSM_STATES = SSM_GROUPS * SSM_STATE
ATT_HEADS = 16
HEAD_DIM = 64
ATT_WIDTH = ATT_HEADS * HEAD_DIM
DILATED_PATTERNS = ((128, 1), (512, 4), (2048, 16))
MAX_WINDOW = 2048
ATT_BLOCK = 128
ATT_SPAN = 128
FFN_HIDDEN = 2816
RMS_EPS = 1e-6

SUBLANES = 8
LANES = 128
VMEM_LIMIT = 56 * 1024 * 1024


def _cparams(n_axes):
    return pltpu.CompilerParams(
        dimension_semantics=("arbitrary",) * n_axes, vmem_limit_bytes=VMEM_LIMIT)


def _rms(x, g):
    ms = jnp.mean(x * x, axis=-1, keepdims=True)
    return (x * lax.rsqrt(ms + RMS_EPS)) * g


def _const_spec(shape):
    zeros = (0,) * len(shape)
    return pl.BlockSpec(shape, lambda *_: zeros, pipeline_mode=pl.Buffered(1))


def _norm_matmul_kernel(x_ref, g_ref, w_ref, *out_refs, splits):
    xb = _rms(x_ref[...], g_ref[...]).astype(BF16)
    for (c0, width, scale), o_ref in zip(splits, out_refs):
        y = jnp.dot(xb, w_ref[:, c0:c0 + width], preferred_element_type=F32)
        if scale != 1.0:
            y = y * scale
        o_ref[...] = y.astype(o_ref.dtype)


def _norm_matmul(x, g, w, splits, dtypes, tm):
    m, d = x.shape
    n = w.shape[1]
    return pl.pallas_call(
        functools.partial(_norm_matmul_kernel, splits=splits),
        grid=(m // tm,),
        in_specs=[pl.BlockSpec((tm, d), lambda i: (i, 0)), _const_spec((1, d)), _const_spec((d, n))],
        out_specs=[pl.BlockSpec((tm, width), lambda i: (i, 0)) for _, width, _ in splits],
        out_shape=[jax.ShapeDtypeStruct((m, width), dt) for (_, width, _), dt in zip(splits, dtypes)],
        compiler_params=_cparams(1),
        name="norm_matmul",
    )(x, g, w)


ATT_DILATIONS = tuple(dil for _, dil in DILATED_PATTERNS)
COL_TILES = ATT_WIDTH // LANES


def _qkv_prompt_kernel(x_ref, g_ref, w_ref, perm_ref, *refs, q_scale, tm):
    n_dil = len(ATT_DILATIONS)
    out_refs = refs[:3 * n_dil]
    tail_ref = refs[3 * n_dil]
    xb = _rms(x_ref[...], g_ref[...]).astype(BF16)
    for which in range(3):
        y = jnp.dot(xb, w_ref[:, which * ATT_WIDTH:(which + 1) * ATT_WIDTH], preferred_element_type=F32)
        if which == 0:
            y = y * q_scale
        else:
            tail_ref[0, :, (which - 1) * ATT_WIDTH:which * ATT_WIDTH] = y
        yb = y.astype(BF16)
        for pat, (dil, o_ref) in enumerate(zip(ATT_DILATIONS, out_refs[which * n_dil:(which + 1) * n_dil])):
            if dil == 1:
                o_ref[0, 0] = yb
                continue
            moved = jnp.dot(perm_ref[pat - 1], yb, preferred_element_type=F32).astype(BF16)
            rows = tm // dil
            for res in range(dil):
                o_ref[0, res] = moved[res * rows:(res + 1) * rows]


def _qkv_prompt(x, g, w, batch, seq, keep, tm):
    m, d = x.shape
    tiles = seq // tm
    first_tail = tiles - keep // tm

    def tail_map(i):
        return (i // tiles, jnp.maximum(i % tiles - first_tail, 0), 0)

    def dil_spec(dil):
        return pl.BlockSpec((1, dil, tm // dil, ATT_WIDTH), lambda i: (i // tiles, 0, i % tiles, 0))

    def dil_shape(dil):
        return jax.ShapeDtypeStruct((batch, dil, seq // dil, ATT_WIDTH), BF16)

    assert ATT_DILATIONS[0] == 1
    src = jnp.arange(tm)[None, :]
    dst = jnp.arange(tm)[:, None]
    perm = jnp.stack([(src == (dst % (tm // dil)) * dil + dst // (tm // dil)).astype(BF16)
                      for dil in ATT_DILATIONS[1:]])
    return pl.pallas_call(
        functools.partial(_qkv_prompt_kernel, q_scale=HEAD_DIM ** -0.5, tm=tm),
        grid=(m // tm,),
        in_specs=[pl.BlockSpec((tm, d), lambda i: (i, 0)), _const_spec((1, d)),
                  _const_spec((d, 3 * ATT_WIDTH)), _const_spec(perm.shape)],
        out_specs=[dil_spec(dil) for _ in range(3) for dil in ATT_DILATIONS]
        + [pl.BlockSpec((1, tm, 2 * ATT_WIDTH), tail_map)],
        out_shape=[dil_shape(dil) for _ in range(3) for dil in ATT_DILATIONS]
        + [jax.ShapeDtypeStruct((batch, keep, 2 * ATT_WIDTH), F32)],
        compiler_params=_cparams(1),
        name="qkv_prompt",
    )(x, g, w, perm)


def _pool_kernel(u_ref, lead_ref, w_ref, scale_ref, y_ref, ext, *, tt, pos0, carry):
    j = pl.program_id(1)

    @pl.when(j == 0)
    def _():
        ext[0:POOL_LEAD, :] = lead_ref[0]

    u = u_ref[0]
    ext[POOL_LEAD:POOL_LEAD + tt, :] = u
    pos = pos0 + j * tt + lax.broadcasted_iota(jnp.int32, (tt, POOL_GROUP), 0)
    for grp, window in enumerate(POOL_WINDOWS):
        sl = slice(grp * POOL_GROUP, (grp + 1) * POOL_GROUP)
        u_g = u[:, sl]
        win_sum = u_g
        for k in range(1, window):
            win_sum = win_sum + ext[POOL_LEAD - k:POOL_LEAD - k + tt, sl]
        count = jnp.minimum(pos + 1, window).astype(F32)
        diff = win_sum / count - u_g
        y = jnp.dot(diff.astype(BF16), w_ref[grp], preferred_element_type=F32)
        y_ref[0, :, sl] = (y * scale_ref[:, sl]).astype(y_ref.dtype)
    if carry:
        ext[0:POOL_LEAD, :] = ext[tt:tt + POOL_LEAD, :]


def _pool_mixer(proj, lead, w, scale, tt, pos0):
    n, t, _ = proj.shape
    steps = t // tt
    return pl.pallas_call(
        functools.partial(_pool_kernel, tt=tt, pos0=pos0, carry=steps > 1),
        grid=(n, steps),
        in_specs=[pl.BlockSpec((1, tt, POOL_WIDTH), lambda b, j: (b, j, 0)),
                  pl.BlockSpec((1, POOL_LEAD, POOL_WIDTH), lambda b, j: (b, 0, 0)),
                  _const_spec((len(POOL_WINDOWS), POOL_GROUP, POOL_GROUP)),
                  _const_spec((1, POOL_WIDTH))],
        out_specs=pl.BlockSpec((1, tt, POOL_WIDTH), lambda b, j: (b, j, 0)),
        out_shape=jax.ShapeDtypeStruct((n, t, POOL_WIDTH), BF16),
        scratch_shapes=[pltpu.VMEM((POOL_LEAD + tt, POOL_WIDTH), F32)],
        compiler_params=_cparams(2),
        name="pool_mixer",
    )(proj, lead, w, scale)


S5_IN_HALF = SSM_WIDTH // 2
S5_STATE_HALF = SSM_STATES // 2
S5_OUT_TILE = LANES
S5_STATE_TILE = SSM_STATES // (SSM_WIDTH // S5_OUT_TILE)
S5_SCAN_LANES = 512


def _s5_input(ub, bre_ref, bim_ref):
    re, im = [], []
    for half in range(2):
        uk = ub[:, half * S5_IN_HALF:(half + 1) * S5_IN_HALF]
        re.append(jnp.dot(uk, bre_ref[half], preferred_element_type=F32))
        im.append(jnp.dot(uk, bim_ref[half], preferred_element_type=F32))
    return re, im


def _s5_output(h_re, h_im, cre_ref, cim_ref, tile):
    return (jnp.dot(h_re, cre_ref[tile], preferred_element_type=F32)
            - jnp.dot(h_im, cim_ref[tile], preferred_element_type=F32))


def _s5_prompt_kernel(u_ref, bre_ref, bim_ref, cre_ref, cim_ref, pw_ref, d_ref,
                      y_ref, hre_ref, him_ref, s_re, s_im, c_re, c_im, *, tt):
    j = pl.program_id(1)

    @pl.when(j == 0)
    def _():
        c_re[...] = jnp.zeros_like(c_re)
        c_im[...] = jnp.zeros_like(c_im)

    u = u_ref[0]
    bu_re, bu_im = _s5_input(u.astype(BF16), bre_ref, bim_ref)
    for half in range(2):
        s_re[:, half * S5_STATE_HALF:(half + 1) * S5_STATE_HALF] = bu_re[half]
        s_im[:, half * S5_STATE_HALF:(half + 1) * S5_STATE_HALF] = bu_im[half]

    for chunk in range(SSM_STATES // S5_SCAN_LANES):
        sl = slice(chunk * S5_SCAN_LANES, (chunk + 1) * S5_SCAN_LANES)

        def body(r, carry, sl=sl):
            in_re, in_im = carry
            row = pl.multiple_of(r * SUBLANES, SUBLANES)
            re = s_re[pl.ds(row, SUBLANES), sl]
            im = s_im[pl.ds(row, SUBLANES), sl]
            for level, shift in enumerate((1, 2, 4)):
                a_re = pw_ref[2 * level, :, sl]
                a_im = pw_ref[2 * level + 1, :, sl]
                sh_re = pltpu.roll(re, shift, 0)
                sh_im = pltpu.roll(im, shift, 0)
                re, im = (re + a_re * sh_re - a_im * sh_im,
                          im + a_re * sh_im + a_im * sh_re)
            p_re = pw_ref[6, :, sl]
            p_im = pw_ref[7, :, sl]
            re, im = (re + p_re * in_re - p_im * in_im,
                      im + p_re * in_im + p_im * in_re)
            s_re[pl.ds(row, SUBLANES), sl] = re
            s_im[pl.ds(row, SUBLANES), sl] = im
            last = SUBLANES - 1
            return (jnp.broadcast_to(re[last:last + 1, :], re.shape),
                    jnp.broadcast_to(im[last:last + 1, :], im.shape))

        out_re, out_im = lax.fori_loop(0, tt // SUBLANES, body, (c_re[:, sl], c_im[:, sl]))
        c_re[:, sl] = out_re
        c_im[:, sl] = out_im

    hre_ref[0] = c_re[...]
    him_ref[0] = c_im[...]
    for tile in range(SSM_WIDTH // S5_OUT_TILE):
        st = slice(tile * S5_STATE_TILE, (tile + 1) * S5_STATE_TILE)
        ot = slice(tile * S5_OUT_TILE, (tile + 1) * S5_OUT_TILE)
        y = _s5_output(s_re[:, st].astype(BF16), s_im[:, st].astype(BF16), cre_ref, cim_ref, tile)
        y_ref[0, :, ot] = y + d_ref[:, ot] * u[:, ot]


def _s5_prompt(proj, mats, tt):
    b, t, _ = proj.shape
    bre, bim, cre, cim, powers, d_skip = mats
    state = jax.ShapeDtypeStruct((b, SUBLANES, SSM_STATES), F32)
    state_spec = pl.BlockSpec((1, SUBLANES, SSM_STATES), lambda i, j: (i, 0, 0))
    return pl.pallas_call(
        functools.partial(_s5_prompt_kernel, tt=tt),
        grid=(b, t // tt),
        in_specs=[pl.BlockSpec((1, tt, SSM_WIDTH), lambda i, j: (i, j, 1)),
                  _const_spec(bre.shape), _const_spec(bim.shape),
                  _const_spec(cre.shape), _const_spec(cim.shape),
                  _const_spec(powers.shape), _const_spec(d_skip.shape)],
        out_specs=[pl.BlockSpec((1, tt, SSM_WIDTH), lambda i, j: (i, j, 0)), state_spec, state_spec],
        out_shape=[jax.ShapeDtypeStruct((b, t, SSM_WIDTH), F32), state, state],
        scratch_shapes=[pltpu.VMEM((tt, SSM_STATES), F32), pltpu.VMEM((tt, SSM_STATES), F32),
                        pltpu.VMEM((SUBLANES, SSM_STATES), F32), pltpu.VMEM((SUBLANES, SSM_STATES), F32)],
        compiler_params=_cparams(2),
        name="s5_prompt",
    )(proj, bre, bim, cre, cim, powers, d_skip)


def _s5_sample_kernel(u_ref, h0re_ref, h0im_ref, bre_ref, bim_ref, cre_ref, cim_ref, lam_ref, d_ref,
                      y_ref, hre_ref, him_ref, *, steps):
    h_re = h0re_ref[...]
    h_im = h0im_ref[...]
    lam_re = lam_ref[0:1, :]
    lam_im = lam_ref[1:2, :]
    for t in range(steps):
        u = u_ref[t]
        bu_re, bu_im = _s5_input(u.astype(BF16), bre_ref, bim_ref)
        bu_re = jnp.concatenate(bu_re, axis=1)
        bu_im = jnp.concatenate(bu_im, axis=1)
        h_re, h_im = (lam_re * h_re - lam_im * h_im + bu_re,
                      lam_re * h_im + lam_im * h_re + bu_im)
        hb_re = h_re.astype(BF16)
        hb_im = h_im.astype(BF16)
        for tile in range(SSM_WIDTH // S5_OUT_TILE):
            st = slice(tile * S5_STATE_TILE, (tile + 1) * S5_STATE_TILE)
            ot = slice(tile * S5_OUT_TILE, (tile + 1) * S5_OUT_TILE)
            y = _s5_output(hb_re[:, st], hb_im[:, st], cre_ref, cim_ref, tile)
            y_ref[t, :, ot] = y + d_ref[:, ot] * u[:, ot]
    hre_ref[...] = h_re
    him_ref[...] = h_im


def _s5_sample(u_tb, h0_re, h0_im, mats, lam):
    steps, n, _ = u_tb.shape
    bre, bim, cre, cim, _, d_skip = mats
    state = jax.ShapeDtypeStruct((n, SSM_STATES), F32)
    args = (u_tb, h0_re, h0_im, bre, bim, cre, cim, lam, d_skip)
    return pl.pallas_call(
        functools.partial(_s5_sample_kernel, steps=steps),
        grid=(1,),
        in_specs=[_const_spec(a.shape) for a in args],
        out_specs=[_const_spec(u_tb.shape), _const_spec((n, SSM_STATES)), _const_spec((n, SSM_STATES))],
        out_shape=[jax.ShapeDtypeStruct(u_tb.shape, F32), state, state],
        compiler_params=_cparams(1),
        name="s5_sample",
    )(*args)


def _s5_matrices(lam_re, lam_im, log_dt, b_re, b_im, c_re, c_im, d_skip):
    lam = lax.complex(lam_re, lam_im)
    dt = jnp.exp(log_dt)[:, None]
    lam_bar = jnp.exp(lam * dt)
    b_bar = ((lam_bar - 1.0) / lam)[..., None] * lax.complex(b_re, b_im)
    eye_half = jnp.eye(SSM_GROUPS // 2, dtype=F32)

    def in_blocks(x):
        x = x.reshape(2, SSM_GROUPS // 2, SSM_STATE, SSM_GROUP)
        return jnp.einsum("kgpi,gh->kgihp", x, eye_half).reshape(2, S5_IN_HALF, S5_STATE_HALF).astype(BF16)

    tiles = SSM_WIDTH // S5_OUT_TILE
    groups_per_tile = SSM_GROUPS // tiles
    eye_tile = jnp.eye(groups_per_tile, dtype=F32)

    def out_blocks(x):
        x = x.reshape(tiles, groups_per_tile, SSM_GROUP, SSM_STATE)
        return jnp.einsum("kgop,gh->kgpho", x, eye_tile).reshape(tiles, S5_STATE_TILE, S5_OUT_TILE).astype(BF16)

    lam_flat = lam_bar.reshape(1, SSM_STATES)
    rows = jnp.arange(SUBLANES)[:, None]
    planes = []
    for shift in (1, 2, 4):
        a = jnp.where(rows >= shift, lam_flat ** shift, 0.0)
        planes += [a.real, a.imag]
    carry = jnp.cumprod(jnp.broadcast_to(lam_flat, (SUBLANES, SSM_STATES)), axis=0)
    planes += [carry.real, carry.imag]
    powers = jnp.stack(planes).astype(F32)
    lam_rows = jnp.concatenate([lam_flat.real, lam_flat.imag], axis=0).astype(F32)
    mats = (in_blocks(b_bar.real), in_blocks(b_bar.imag), out_blocks(c_re), out_blocks(c_im),
            powers, d_skip.reshape(1, SSM_WIDTH))
    return mats, lam_rows


def _ab_out_kernel(yp_ref, ys_ref, wglu_ref, bglu_ref, wout_ref, h_ref, g_ref, o_ref):
    z = jax.nn.gelu(ys_ref[...])
    gate = jnp.dot(z.astype(BF16), wglu_ref[...], preferred_element_type=F32) + bglu_ref[...]
    y_ssm = z * jax.nn.sigmoid(gate)
    y = (jnp.dot(yp_ref[...], wout_ref[0:POOL_WIDTH, :], preferred_element_type=F32)
         + jnp.dot(y_ssm.astype(BF16), wout_ref[POOL_WIDTH:, :], preferred_element_type=F32))
    o_ref[...] = h_ref[...] + _rms(y, g_ref[...])


def _ab_out(y_pool, y_ssm, w_glu, b_glu, w_out, h, g, tm):
    m = h.shape[0]
    row = lambda width: pl.BlockSpec((tm, width), lambda i: (i, 0))
    return pl.pallas_call(
        _ab_out_kernel,
        grid=(m // tm,),
        in_specs=[row(POOL_WIDTH), row(SSM_WIDTH), _const_spec(w_glu.shape), _const_spec(b_glu.shape),
                  _const_spec(w_out.shape), row(D_MODEL), _const_spec(g.shape)],
        out_specs=row(D_MODEL),
        out_shape=jax.ShapeDtypeStruct((m, D_MODEL), F32),
        compiler_params=_cparams(1),
        name="ab_out",
    )(y_pool, y_ssm, w_glu, b_glu, w_out, h, g)


def _ffn_kernel(h_ref, gin_ref, wg_ref, wu_ref, wd_ref, gout_ref, o_ref):
    h = h_ref[...]
    xb = _rms(h, gin_ref[...]).astype(BF16)
    gate = jnp.dot(xb, wg_ref[...], preferred_element_type=F32)
    up = jnp.dot(xb, wu_ref[...], preferred_element_type=F32)
    act = (gate * jax.nn.sigmoid(gate) * up).astype(BF16)
    y = jnp.dot(act, wd_ref[...], preferred_element_type=F32)
    o_ref[...] = h + _rms(y, gout_ref[...])


def _ffn(h, g_in, w_gate, w_up, w_down, g_out, tm):
    m = h.shape[0]
    row = pl.BlockSpec((tm, D_MODEL), lambda i: (i, 0))
    return pl.pallas_call(
        _ffn_kernel,
        grid=(m // tm,),
        in_specs=[row, _const_spec(g_in.shape), _const_spec(w_gate.shape), _const_spec(w_up.shape),
                  _const_spec(w_down.shape), _const_spec(g_out.shape)],
        out_specs=row,
        out_shape=jax.ShapeDtypeStruct((m, D_MODEL), F32),
        compiler_params=_cparams(1),
        name="ffn",
    )(h, g_in, w_gate, w_up, w_down, g_out)


def _attn_prompt_kernel(q_ref, kc_ref, kp_ref, vc_ref, vp_ref, o_ref, lse_ref):
    blk = pl.program_id(2)
    rows = 2 * ATT_BLOCK
    qi = lax.broadcasted_iota(jnp.int32, (rows, 2 * ATT_BLOCK), 0) & (ATT_BLOCK - 1)
    kj = lax.broadcasted_iota(jnp.int32, (rows, 2 * ATT_BLOCK), 1)
    dist = ATT_BLOCK + qi - kj
    valid = (dist >= 0) & (dist <= ATT_SPAN) & ((kj >= ATT_BLOCK) | (blk > 0))
    lane = lax.broadcasted_iota(jnp.int32, (ATT_BLOCK, LANES), 1)
    first_head = lane < HEAD_DIM
    lse_all = jnp.zeros((ATT_BLOCK, LANES), F32)
    for pair in range(ATT_HEADS // 2):
        sl = slice(pair * LANES, (pair + 1) * LANES)
        q2 = q_ref[:, sl]
        zero = jnp.zeros_like(q2)
        qs = jnp.concatenate([jnp.where(first_head, q2, zero), jnp.where(first_head, zero, q2)], axis=0)
        k2 = jnp.concatenate([kp_ref[:, sl], kc_ref[:, sl]], axis=0)
        v2 = jnp.concatenate([vp_ref[:, sl], vc_ref[:, sl]], axis=0)
        s = lax.dot_general(qs, k2, (((1,), (1,)), ((), ())), preferred_element_type=F32)
        s = jnp.where(valid, s, -jnp.inf)
        m = jnp.max(s, axis=-1, keepdims=True)
        p = jnp.exp(s - m)
        den = jnp.sum(p, axis=-1, keepdims=True)
        o = jnp.dot(p.astype(BF16), v2, preferred_element_type=F32) / den
        lse = m + jnp.log(den)
        o_ref[:, sl] = jnp.where(first_head, o[:ATT_BLOCK], o[ATT_BLOCK:]).astype(o_ref.dtype)
        lse_all = jnp.where(lane == 2 * pair, lse[:ATT_BLOCK],
                            jnp.where(lane == 2 * pair + 1, lse[ATT_BLOCK:], lse_all))
    lse_ref[...] = lse_all


def _attn_prompt(q, k, v, dil):
    batch, _, sub, _ = q.shape
    nblk = sub // ATT_BLOCK
    cur = pl.BlockSpec((None, None, ATT_BLOCK, ATT_WIDTH), lambda b, r, i: (b, r, i, 0))
    prev = pl.BlockSpec((None, None, ATT_BLOCK, ATT_WIDTH), lambda b, r, i: (b, r, jnp.maximum(i - 1, 0), 0))
    return pl.pallas_call(
        _attn_prompt_kernel,
        grid=(batch, dil, nblk),
        in_specs=[cur, cur, prev, cur, prev],
        out_specs=[cur, pl.BlockSpec((None, None, ATT_BLOCK, LANES), lambda b, r, i: (b, r, i, 0))],
        out_shape=[jax.ShapeDtypeStruct((batch, dil, sub, ATT_WIDTH), BF16),
                   jax.ShapeDtypeStruct((batch, dil, sub, LANES), F32)],
        compiler_params=_cparams(3),
        name="attn_prompt",
    )(q, k, k, v, v)


def _pattern_multiplicity(dist):
    mult = jnp.zeros(dist.shape, F32)
    for window, dil in DILATED_PATTERNS:
        hit = (dist >= 0) & (dist <= window) & ((dist & (dil - 1)) == 0)
        mult = mult + hit.astype(F32)
    return mult


def _attn_sample_kernel(q_ref, knew_ref, vnew_ref, kt_ref, vt_ref, o_ref, *, steps):
    t_cache = lax.broadcasted_iota(jnp.int32, (SUBLANES, MAX_WINDOW), 0)
    pos = lax.broadcasted_iota(jnp.int32, (SUBLANES, MAX_WINDOW), 1)
    mult = _pattern_multiplicity(MAX_WINDOW + t_cache - pos)
    t_new = lax.broadcasted_iota(jnp.int32, (SUBLANES, LANES), 0)
    j_new = lax.broadcasted_iota(jnp.int32, (SUBLANES, LANES), 1)
    mult_new = jnp.where(j_new < steps, _pattern_multiplicity(t_new - j_new), 0.0)
    nt = (((1,), (1,)), ((), ()))
    for head in range(ATT_HEADS):
        q = q_ref[0, head].astype(BF16)
        s = jnp.dot(q, kt_ref[0, head].astype(BF16), preferred_element_type=F32)
        s_new = lax.dot_general(q, knew_ref[0, head].astype(BF16), nt, preferred_element_type=F32)
        s = jnp.where(mult > 0.0, s, -jnp.inf)
        s_new = jnp.where(mult_new > 0.0, s_new, -jnp.inf)
        m = jnp.maximum(jnp.max(s, axis=-1, keepdims=True), jnp.max(s_new, axis=-1, keepdims=True))
        p = mult * jnp.exp(s - m)
        p_new = mult_new * jnp.exp(s_new - m)
        den = jnp.sum(p, axis=-1, keepdims=True) + jnp.sum(p_new, axis=-1, keepdims=True)
        o = lax.dot_general(p.astype(BF16), vt_ref[0, head].astype(BF16), nt, preferred_element_type=F32)
        o = o + jnp.dot(p_new.astype(BF16), vnew_ref[0, head].astype(BF16), preferred_element_type=F32)
        o_ref[0, :, head * HEAD_DIM:(head + 1) * HEAD_DIM] = (o / den)[:steps]


def _attn_sample(q, k_new, v_new, cache_kt, cache_vt, steps):
    n = q.shape[0]
    assert steps <= SUBLANES
    per = lambda a: pl.BlockSpec((1,) + a.shape[1:], lambda b: (b, 0, 0, 0))
    args = (q, k_new, v_new, cache_kt, cache_vt)
    return pl.pallas_call(
        functools.partial(_attn_sample_kernel, steps=steps),
        grid=(n,),
        in_specs=[per(a) for a in args],
        out_specs=pl.BlockSpec((1, steps, ATT_WIDTH), lambda b: (b, 0, 0)),
        out_shape=jax.ShapeDtypeStruct((n, steps, ATT_WIDTH), F32),
        compiler_params=_cparams(1),
        name="attn_sample",
    )(*args)


def _attn_out_kernel(*refs, tm):
    n_pat = len(ATT_DILATIONS)
    o_refs = refs[:n_pat]
    lse_refs = refs[n_pat:2 * n_pat]
    w_ref, h_ref, g_ref, out_ref, o_nat, lse_nat = refs[2 * n_pat:]
    for pat, dil in enumerate(ATT_DILATIONS):
        rows = tm // dil
        for res in range(dil):
            dst = pl.ds(res, rows, stride=dil) if dil > 1 else pl.ds(0, rows)
            lse_nat[pat, dst, :] = lse_refs[pat][0, res]
            for c in range(COL_TILES):
                o_nat[pat * COL_TILES + c, dst, :] = o_refs[pat][0, res, :, c * LANES:(c + 1) * LANES].astype(F32)
    lses = [lse_nat[pat] for pat in range(n_pat)]
    top = functools.reduce(jnp.maximum, lses)
    es = [jnp.exp(l - top) for l in lses]
    total = functools.reduce(lambda x, y: x + y, es)
    wgts = [e / total for e in es]
    lane = lax.broadcasted_iota(jnp.int32, (tm, LANES), 1)
    first_head = lane < HEAD_DIM
    pieces = []
    for pair in range(ATT_HEADS // 2):
        acc = None
        for pat, wgt in enumerate(wgts):
            w2 = jnp.where(first_head, wgt[:, 2 * pair:2 * pair + 1], wgt[:, 2 * pair + 1:2 * pair + 2])
            term = w2 * o_nat[pat * COL_TILES + pair]
            acc = term if acc is None else acc + term
        pieces.append(acc.astype(BF16))
    a = jnp.concatenate(pieces, axis=1)
    y = jnp.dot(a, w_ref[...], preferred_element_type=F32)
    out_ref[...] = h_ref[...] + _rms(y, g_ref[...])


def _attn_out(outs, lses, w_o, h, g, seq, tm):
    m = h.shape[0]
    tiles = seq // tm
    n_pat = len(ATT_DILATIONS)

    def dil_spec(dil, width):
        return pl.BlockSpec((1, dil, tm // dil, width), lambda i: (i // tiles, 0, i % tiles, 0))

    row = pl.BlockSpec((tm, D_MODEL), lambda i: (i, 0))
    return pl.pallas_call(
        functools.partial(_attn_out_kernel, tm=tm),
        grid=(m // tm,),
        in_specs=[dil_spec(dil, ATT_WIDTH) for dil in ATT_DILATIONS]
        + [dil_spec(dil, LANES) for dil in ATT_DILATIONS]
        + [_const_spec(w_o.shape), row, _const_spec(g.shape)],
        out_specs=row,
        out_shape=jax.ShapeDtypeStruct((m, D_MODEL), F32),
        scratch_shapes=[pltpu.VMEM((n_pat * COL_TILES, tm, LANES), F32), pltpu.VMEM((n_pat, tm, LANES), F32)],
        compiler_params=_cparams(1),
        name="attn_out",
    )(*outs, *lses, w_o, h, g)


def _proj_out_kernel(a_ref, w_ref, h_ref, g_ref, out_ref):
    y = jnp.dot(a_ref[...].astype(BF16), w_ref[...], preferred_element_type=F32)
    out_ref[...] = h_ref[...] + _rms(y, g_ref[...])


def _proj_out(a, w, h, g, tm):
    m = h.shape[0]
    return pl.pallas_call(
        _proj_out_kernel,
        grid=(m // tm,),
        in_specs=[pl.BlockSpec((tm, a.shape[1]), lambda i: (i, 0)), _const_spec(w.shape),
                  pl.BlockSpec((tm, D_MODEL), lambda i: (i, 0)), _const_spec(g.shape)],
        out_specs=pl.BlockSpec((tm, D_MODEL), lambda i: (i, 0)),
        out_shape=jax.ShapeDtypeStruct((m, D_MODEL), F32),
        compiler_params=_cparams(1),
        name="proj_out",
    )(a, w, h, g)


def kernel(x_prompt, x_sample, state_pool, state_s5, cache_k, cache_v, norm_gains, ab_w_in, ab_pool_w, ab_pool_scale, ab_lambda_re, ab_lambda_im, ab_log_dt, ab_b_re, ab_b_im, ab_c_re, ab_c_im, ab_d, ab_w_glu, ab_b_glu, ab_w_out, c_w_qkv, c_w_o, ffn_w_gate, ffn_w_up, ffn_w_down):
    batch, seq, d = x_prompt.shape
    n_dec, t_dec, _ = x_sample.shape
    mp, ms = batch * seq, n_dec * t_dec
    tm_p = 256
    tm_ffn = 512
    gains = norm_gains.reshape(norm_gains.shape[0], 4, 1, d)

    hp = x_prompt.reshape(mp, d)
    hs = x_sample.reshape(ms, d)

    g = gains[0]
    w_in = ab_w_in[0].astype(BF16)
    (proj_p,) = _norm_matmul(hp, g[0], w_in, ((0, d, 1.0),), (F32,), tm_p)
    (proj_s,) = _norm_matmul(hs, g[0], w_in, ((0, d, 1.0),), (F32,), ms)
    proj_p = proj_p.reshape(batch, seq, d)
    proj_s = proj_s.reshape(n_dec, t_dec, d)

    pool_w = ab_pool_w[0].astype(BF16)
    pool_scale = ab_pool_scale[0].reshape(1, POOL_WIDTH)
    lead_p = jnp.zeros((batch, POOL_LEAD, POOL_WIDTH), F32)
    lead_s = jnp.pad(state_pool[0], ((0, 0), (POOL_LEAD - POOL_STATE, 0), (0, 0)))
    ypool_p = _pool_mixer(proj_p, lead_p, pool_w, pool_scale, 256, 0)
    ypool_s = _pool_mixer(proj_s, lead_s, pool_w, pool_scale, t_dec, PAST_LEN)
    pool_prompt = proj_p[:, seq - POOL_STATE:, :POOL_WIDTH][None]
    pool_sample = jnp.concatenate([state_pool[0], proj_s[:, :, :POOL_WIDTH]], axis=1)[:, -POOL_STATE:][None]

    mats, lam_rows = _s5_matrices(ab_lambda_re[0], ab_lambda_im[0], ab_log_dt[0], ab_b_re[0], ab_b_im[0],
                                  ab_c_re[0], ab_c_im[0], ab_d[0])
    yssm_p, hre_p, him_p = _s5_prompt(proj_p, mats, 256)
    u_tb = jnp.swapaxes(proj_s[:, :, POOL_WIDTH:], 0, 1)
    h0 = state_s5[0].reshape(n_dec, SSM_STATES, 2)
    yssm_tb, hre_s, him_s = _s5_sample(u_tb, h0[..., 0], h0[..., 1], mats, lam_rows)
    yssm_s = jnp.swapaxes(yssm_tb, 0, 1)
    s5_prompt = jnp.stack([hre_p[:, 0], him_p[:, 0]], axis=-1).reshape(1, batch, SSM_GROUPS, SSM_STATE, 2)
    s5_sample = jnp.stack([hre_s, him_s], axis=-1).reshape(1, n_dec, SSM_GROUPS, SSM_STATE, 2)

    w_glu = ab_w_glu[0].astype(BF16)
    b_glu = ab_b_glu[0].reshape(1, SSM_WIDTH)
    w_out = ab_w_out[0].astype(BF16)
    hp = _ab_out(ypool_p.reshape(mp, POOL_WIDTH), yssm_p.reshape(mp, SSM_WIDTH), w_glu, b_glu, w_out, hp, g[1], tm_p)
    hs = _ab_out(ypool_s.reshape(ms, POOL_WIDTH), yssm_s.reshape(ms, SSM_WIDTH), w_glu, b_glu, w_out, hs, g[1], ms)

    wg, wu, wd = ffn_w_gate[0].astype(BF16), ffn_w_up[0].astype(BF16), ffn_w_down[0].astype(BF16)
    hp = _ffn(hp, g[2], wg, wu, wd, g[3], tm_ffn)
    hs = _ffn(hs, g[2], wg, wu, wd, g[3], ms)

    g = gains[1]
    w_qkv = c_w_qkv[0].astype(BF16)
    w_o = c_w_o[0].astype(BF16)
    keep = min(MAX_WINDOW, seq)
    *qkv_p, kv_tail = _qkv_prompt(hp, g[0], w_qkv, batch, seq, keep, tm_p)
    k_prompt = kv_tail[:, :, :ATT_WIDTH].reshape(1, batch, keep, ATT_HEADS, HEAD_DIM)
    v_prompt = kv_tail[:, :, ATT_WIDTH:].reshape(1, batch, keep, ATT_HEADS, HEAD_DIM)
    n_dil = len(ATT_DILATIONS)
    outs, lses = [], []
    for pat, dil in enumerate(ATT_DILATIONS):
        o, lse = _attn_prompt(qkv_p[pat], qkv_p[n_dil + pat], qkv_p[2 * n_dil + pat], dil)
        outs.append(o)
        lses.append(lse)
    hp = _attn_out(outs, lses, w_o, hp, g[1], seq, tm_p)

    scale = HEAD_DIM ** -0.5
    q_s, k_s, v_s = _norm_matmul(
        hs, g[0], w_qkv,
        ((0, ATT_WIDTH, scale), (ATT_WIDTH, ATT_WIDTH, 1.0), (2 * ATT_WIDTH, ATT_WIDTH, 1.0)),
        (F32, F32, F32), ms)
    k_sample = k_s.reshape(1, n_dec, t_dec, ATT_HEADS, HEAD_DIM)
    v_sample = v_s.reshape(1, n_dec, t_dec, ATT_HEADS, HEAD_DIM)
    head_major = lambda a, rows: jnp.pad(
        jnp.swapaxes(a.reshape(n_dec, t_dec, ATT_HEADS, HEAD_DIM), 1, 2),
        ((0, 0), (0, 0), (0, rows - t_dec), (0, 0)))
    cache_kt = jnp.transpose(cache_k[0], (0, 2, 3, 1))
    cache_vt = jnp.transpose(cache_v[0], (0, 2, 3, 1))
    o_s = _attn_sample(head_major(q_s, SUBLANES), head_major(k_s, LANES), head_major(v_s, LANES),
                       cache_kt, cache_vt, t_dec)
    hs = _proj_out(o_s.reshape(ms, ATT_WIDTH), w_o, hs, g[1], ms)

    wg, wu, wd = ffn_w_gate[1].astype(BF16), ffn_w_up[1].astype(BF16), ffn_w_down[1].astype(BF16)
    hp = _ffn(hp, g[2], wg, wu, wd, g[3], tm_ffn)
    hs = _ffn(hs, g[2], wg, wu, wd, g[3], ms)

    return (hp.reshape(batch, seq, d), hs.reshape(n_dec, t_dec, d), pool_prompt, s5_prompt,
            k_prompt, v_prompt, pool_sample, s5_sample, k_sample, v_sample)
```

```python
import functools
import math

import jax
import jax.numpy as jnp
from jax import lax
from jax.experimental import pallas as pl
from jax.experimental.pallas import tpu as pltpu

F32 = jnp.float32
BF16 = jnp.bfloat16

D_MODEL = 1024
PAST_LEN = 16384
POOL_WIDTH = 512
POOL_WINDOWS = (2, 4, 8, 16)
POOL_GROUP = 128
POOL_STATE = 15
POOL_LEAD = 16
SSM_WIDTH = 512
SSM_GROUP = 16
SSM_GROUPS = 32
SSM_STATE = 64
SSM_STATES = SSM_GROUPS * SSM_STATE
ATT_HEADS = 16
HEAD_DIM = 64
ATT_WIDTH = ATT_HEADS * HEAD_DIM
DILATED_PATTERNS = ((128, 1), (512, 4), (2048, 16))
MAX_WINDOW = 2048
ATT_BLOCK = 128
ATT_SPAN = 128
FFN_HIDDEN = 2816
RMS_EPS = 1e-6

SUBLANES = 8
LANES = 128
VMEM_LIMIT = 56 * 1024 * 1024


def _cparams(n_axes):
    return pltpu.CompilerParams(
        dimension_semantics=("arbitrary",) * n_axes, vmem_limit_bytes=VMEM_LIMIT)


def _rms(x, g):
    ms = jnp.mean(x * x, axis=-1, keepdims=True)
    return (x * lax.rsqrt(ms + RMS_EPS)) * g


def _const_spec(shape):
    zeros = (0,) * len(shape)
    return pl.BlockSpec(shape, lambda *_: zeros, pipeline_mode=pl.Buffered(1))


def _norm_matmul_kernel(x_ref, g_ref, w_ref, *out_refs, splits):
    xb = _rms(x_ref[...], g_ref[...]).astype(BF16)
    for (c0, width, scale), o_ref in zip(splits, out_refs):
        y = jnp.dot(xb, w_ref[:, c0:c0 + width], preferred_element_type=F32)
        if scale != 1.0:
            y = y * scale
        o_ref[...] = y.astype(o_ref.dtype)


def _norm_matmul(x, g, w, splits, dtypes, tm):
    m, d = x.shape
    n = w.shape[1]
    return pl.pallas_call(
        functools.partial(_norm_matmul_kernel, splits=splits),
        grid=(m // tm,),
        in_specs=[pl.BlockSpec((tm, d), lambda i: (i, 0)), _const_spec((1, d)), _const_spec((d, n))],
        out_specs=[pl.BlockSpec((tm, width), lambda i: (i, 0)) for _, width, _ in splits],
        out_shape=[jax.ShapeDtypeStruct((m, width), dt) for (_, width, _), dt in zip(splits, dtypes)],
        compiler_params=_cparams(1),
        name="norm_matmul",
    )(x, g, w)


ATT_DILATIONS = tuple(dil for _, dil in DILATED_PATTERNS)
COL_TILES = ATT_WIDTH // LANES


def _qkv_prompt_kernel(x_ref, g_ref, w_ref, perm_ref, *refs, q_scale, tm):
    n_dil = len(ATT_DILATIONS)
    out_refs = refs[:3 * n_dil]
    tail_ref = refs[3 * n_dil]
    xb = _rms(x_ref[...], g_ref[...]).astype(BF16)
    for which in range(3):
        y = jnp.dot(xb, w_ref[:, which * ATT_WIDTH:(which + 1) * ATT_WIDTH], preferred_element_type=F32)
        if which == 0:
            y = y * q_scale
        else:
            tail_ref[0, :, (which - 1) * ATT_WIDTH:which * ATT_WIDTH] = y
        yb = y.astype(BF16)
        for pat, (dil, o_ref) in enumerate(zip(ATT_DILATIONS, out_refs[which * n_dil:(which + 1) * n_dil])):
            if dil == 1:
                o_ref[0, 0] = yb
                continue
            moved = jnp.dot(perm_ref[pat - 1], yb, preferred_element_type=F32).astype(BF16)
            rows = tm // dil
            for res in range(dil):
                o_ref[0, res] = moved[res * rows:(res + 1) * rows]


def _qkv_prompt(x, g, w, batch, seq, keep, tm):
    m, d = x.shape
    tiles = seq // tm
    first_tail = tiles - keep // tm

    def tail_map(i):
        return (i // tiles, jnp.maximum(i % tiles - first_tail, 0), 0)

    def dil_spec(dil):
        return pl.BlockSpec((1, dil, tm // dil, ATT_WIDTH), lambda i: (i // tiles, 0, i % tiles, 0))

    def dil_shape(dil):
        return jax.ShapeDtypeStruct((batch, dil, seq // dil, ATT_WIDTH), BF16)

    assert ATT_DILATIONS[0] == 1
    src = jnp.arange(tm)[None, :]
    dst = jnp.arange(tm)[:, None]
    perm = jnp.stack([(src == (dst % (tm // dil)) * dil + dst // (tm // dil)).astype(BF16)
                      for dil in ATT_DILATIONS[1:]])
    return pl.pallas_call(
        functools.partial(_qkv_prompt_kernel, q_scale=HEAD_DIM ** -0.5, tm=tm),
        grid=(m // tm,),
        in_specs=[pl.BlockSpec((tm, d), lambda i: (i, 0)), _const_spec((1, d)),
                  _const_spec((d, 3 * ATT_WIDTH)), _const_spec(perm.shape)],
        out_specs=[dil_spec(dil) for _ in range(3) for dil in ATT_DILATIONS]
        + [pl.BlockSpec((1, tm, 2 * ATT_WIDTH), tail_map)],
        out_shape=[dil_shape(dil) for _ in range(3) for dil in ATT_DILATIONS]
        + [jax.ShapeDtypeStruct((batch, keep, 2 * ATT_WIDTH), F32)],
        compiler_params=_cparams(1),
        name="qkv_prompt",
    )(x, g, w, perm)


def _pool_kernel(u_ref, lead_ref, w_ref, scale_ref, y_ref, ext, *, tt, pos0, carry):
    j = pl.program_id(1)

    @pl.when(j == 0)
    def _():
        ext[0:POOL_LEAD, :] = lead_ref[0]

    u = u_ref[0]
    ext[POOL_LEAD:POOL_LEAD + tt, :] = u
    pos = pos0 + j * tt + lax.broadcasted_iota(jnp.int32, (tt, POOL_GROUP), 0)
    for grp, window in enumerate(POOL_WINDOWS):
        sl = slice(grp * POOL_GROUP, (grp + 1) * POOL_GROUP)
        u_g = u[:, sl]
        win_sum = u_g
        for k in range(1, window):
            win_sum = win_sum + ext[POOL_LEAD - k:POOL_LEAD - k + tt, sl]
        count = jnp.minimum(pos + 1, window).astype(F32)
        diff = win_sum / count - u_g
        y = jnp.dot(diff.astype(BF16), w_ref[grp], preferred_element_type=F32)
        y_ref[0, :, sl] = (y * scale_ref[:, sl]).astype(y_ref.dtype)
    if carry:
        ext[0:POOL_LEAD, :] = ext[tt:tt + POOL_LEAD, :]


def _pool_mixer(proj, lead, w, scale, tt, pos0):
    n, t, _ = proj.shape
    steps = t // tt
    return pl.pallas_call(
        functools.partial(_pool_kernel, tt=tt, pos0=pos0, carry=steps > 1),
        grid=(n, steps),
        in_specs=[pl.BlockSpec((1, tt, POOL_WIDTH), lambda b, j: (b, j, 0)),
                  pl.BlockSpec((1, POOL_LEAD, POOL_WIDTH), lambda b, j: (b, 0, 0)),
                  _const_spec((len(POOL_WINDOWS), POOL_GROUP, POOL_GROUP)),
                  _const_spec((1, POOL_WIDTH))],
        out_specs=pl.BlockSpec((1, tt, POOL_WIDTH), lambda b, j: (b, j, 0)),
        out_shape=jax.ShapeDtypeStruct((n, t, POOL_WIDTH), BF16),
        scratch_shapes=[pltpu.VMEM((POOL_LEAD + tt, POOL_WIDTH), F32)],
        compiler_params=_cparams(2),
        name="pool_mixer",
    )(proj, lead, w, scale)


S5_IN_HALF = SSM_WIDTH // 2
S5_STATE_HALF = SSM_STATES // 2
S5_OUT_TILE = LANES
S5_STATE_TILE = SSM_STATES // (SSM_WIDTH // S5_OUT_TILE)
S5_SCAN_LANES = 512


def _s5_input(ub, bre_ref, bim_ref):
    re, im = [], []
    for half in range(2):
        uk = ub[:, half * S5_IN_HALF:(half + 1) * S5_IN_HALF]
        re.append(jnp.dot(uk, bre_ref[half], preferred_element_type=F32))
        im.append(jnp.dot(uk, bim_ref[half], preferred_element_type=F32))
    return re, im


def _s5_output(h_re, h_im, cre_ref, cim_ref, tile):
    return (jnp.dot(h_re, cre_ref[tile], preferred_element_type=F32)
            - jnp.dot(h_im, cim_ref[tile], preferred_element_type=F32))


S5_CHUNK = 16
S5_ROW = S5_CHUNK * SSM_GROUP
S5_SUB = 256
UNIT = SSM_GROUP
UNITS = LANES // UNIT
S5_PAIRS = SSM_GROUPS // 2


def _unit_transpose(pieces):
    unit = lax.broadcasted_iota(jnp.int32, pieces[0].shape, 1) >> (UNIT.bit_length() - 1)
    outs = [None] * UNITS
    for a in range(UNITS):
        for delta in range(UNITS):
            b = (a - delta) % UNITS
            moved = pieces[a] if delta == 0 else pltpu.roll(pieces[a], delta * UNIT, 1)
            outs[b] = moved if outs[b] is None else jnp.where(unit == a, moved, outs[b])
    return outs


def _s5_prompt_kernel(u_ref, perm_ref, toep_ref, wre_ref, wim_ref, vre_ref, vim_ref, pw_ref, d_ref,
                      y_ref, hre_ref, him_ref, u2, y2, s_re, s_im, c_re, c_im, y_nat, *, tt):
    j = pl.program_id(1)
    chunks = tt // S5_CHUNK
    sub_chunks = S5_SUB // S5_CHUNK

    @pl.when(j == 0)
    def _():
        c_re[...] = jnp.zeros_like(c_re)
        c_im[...] = jnp.zeros_like(c_im)

    def relayout_in(sub, carry):
        r0 = pl.multiple_of(sub * S5_SUB, S5_SUB)
        c0 = pl.multiple_of(sub * sub_chunks, sub_chunks)
        ub = u_ref[0, pl.ds(r0, S5_SUB), :].astype(BF16)
        xs = jnp.dot(perm_ref[...], ub, preferred_element_type=F32)
        for tile in range(SSM_WIDTH // LANES):
            for half in range(S5_CHUNK // UNITS):
                pieces = [xs[(half * UNITS + a) * sub_chunks:(half * UNITS + a + 1) * sub_chunks,
                             tile * LANES:(tile + 1) * LANES] for a in range(UNITS)]
                outs = _unit_transpose(pieces)
                for gl in range(UNITS):
                    u2[tile * UNITS + gl, pl.ds(c0, sub_chunks), half * LANES:(half + 1) * LANES] = (
                        outs[gl].astype(BF16))
        return carry

    lax.fori_loop(0, tt // S5_SUB, relayout_in, 0)

    s_re[0:SUBLANES, :] = c_re[...]
    s_im[0:SUBLANES, :] = c_im[...]
    for pair in range(S5_PAIRS):
        sl = slice(pair * LANES, (pair + 1) * LANES)
        ua = u2[2 * pair]
        ub = u2[2 * pair + 1]
        s_re[SUBLANES:SUBLANES + chunks, sl] = (
            jnp.dot(ua, wre_ref[2 * pair], preferred_element_type=F32)
            + jnp.dot(ub, wre_ref[2 * pair + 1], preferred_element_type=F32))
        s_im[SUBLANES:SUBLANES + chunks, sl] = (
            jnp.dot(ua, wim_ref[2 * pair], preferred_element_type=F32)
            + jnp.dot(ub, wim_ref[2 * pair + 1], preferred_element_type=F32))

    for chunk in range(SSM_STATES // S5_SCAN_LANES):
        sl = slice(chunk * S5_SCAN_LANES, (chunk + 1) * S5_SCAN_LANES)

        def body(r, carry, sl=sl):
            in_re, in_im = carry
            row = pl.multiple_of((r + 1) * SUBLANES, SUBLANES)
            re = s_re[pl.ds(row, SUBLANES), sl]
            im = s_im[pl.ds(row, SUBLANES), sl]
            for level, shift in enumerate((1, 2, 4)):
                a_re = pw_ref[2 * level, :, sl]
                a_im = pw_ref[2 * level + 1, :, sl]
                sh_re = pltpu.roll(re, shift, 0)
                sh_im = pltpu.roll(im, shift, 0)
                re, im = (re + a_re * sh_re - a_im * sh_im,
                          im + a_re * sh_im + a_im * sh_re)
            p_re = pw_ref[6, :, sl]
            p_im = pw_ref[7, :, sl]
            re, im = (re + p_re * in_re - p_im * in_im,
                      im + p_re * in_im + p_im * in_re)
            s_re[pl.ds(row, SUBLANES), sl] = re
            s_im[pl.ds(row, SUBLANES), sl] = im
            last = SUBLANES - 1
            return (jnp.broadcast_to(re[last:last + 1, :], re.shape),
                    jnp.broadcast_to(im[last:last + 1, :], im.shape))

        out_re, out_im = lax.fori_loop(0, chunks // SUBLANES, body, (c_re[:, sl], c_im[:, sl]))
        c_re[:, sl] = out_re
        c_im[:, sl] = out_im

    hre_ref[0] = c_re[...]
    him_ref[0] = c_im[...]

    for pair in range(S5_PAIRS):
        sl = slice(pair * LANES, (pair + 1) * LANES)
        h_re = s_re[SUBLANES - 1:SUBLANES - 1 + chunks, sl].astype(BF16)
        h_im = s_im[SUBLANES - 1:SUBLANES - 1 + chunks, sl].astype(BF16)
        carried = (jnp.dot(h_re, vre_ref[pair], preferred_element_type=F32)
                   + jnp.dot(h_im, vim_ref[pair], preferred_element_type=F32))
        for k in range(2):
            grp = 2 * pair + k
            y2[grp] = (jnp.dot(u2[grp], toep_ref[grp], preferred_element_type=F32)
                       + carried[:, k * S5_ROW:(k + 1) * S5_ROW])

    def relayout_out(blk, carry):
        r0 = pl.multiple_of(blk * S5_SUB, S5_SUB)
        c0 = pl.multiple_of(blk * sub_chunks, sub_chunks)
        for tile in range(SSM_WIDTH // LANES):
            ot = slice(tile * LANES, (tile + 1) * LANES)
            for half in range(S5_CHUNK // UNITS):
                pieces = [y2[tile * UNITS + gl, pl.ds(c0, sub_chunks), half * LANES:(half + 1) * LANES]
                          for gl in range(UNITS)]
                outs = _unit_transpose(pieces)
                for a in range(UNITS):
                    y_nat[tile, pl.ds(half * UNITS + a, sub_chunks, stride=S5_CHUNK), :] = outs[a]
            y_ref[0, pl.ds(r0, S5_SUB), ot] = y_nat[tile] + d_ref[:, ot] * u_ref[0, pl.ds(r0, S5_SUB), ot]
        return carry

    lax.fori_loop(0, tt // S5_SUB, relayout_out, 0)


def _s5_prompt(proj, mats, tt):
    b, t, _ = proj.shape
    chunks = tt // S5_CHUNK
    state = jax.ShapeDtypeStruct((b, SUBLANES, SSM_STATES), F32)
    state_spec = pl.BlockSpec((1, SUBLANES, SSM_STATES), lambda i, j: (i, 0, 0))
    return pl.pallas_call(
        functools.partial(_s5_prompt_kernel, tt=tt),
        grid=(b, t // tt),
        in_specs=[pl.BlockSpec((1, tt, SSM_WIDTH), lambda i, j: (i, j, 1))]
        + [_const_spec(a.shape) for a in mats],
        out_specs=[pl.BlockSpec((1, tt, SSM_WIDTH), lambda i, j: (i, j, 0)), state_spec, state_spec],
        out_shape=[jax.ShapeDtypeStruct((b, t, SSM_WIDTH), F32), state, state],
        scratch_shapes=[pltpu.VMEM((SSM_GROUPS, chunks, S5_ROW), BF16),
                        pltpu.VMEM((SSM_GROUPS, chunks, S5_ROW), F32),
                        pltpu.VMEM((SUBLANES + chunks, SSM_STATES), F32),
                        pltpu.VMEM((SUBLANES + chunks, SSM_STATES), F32),
                        pltpu.VMEM((SUBLANES, SSM_STATES), F32), pltpu.VMEM((SUBLANES, SSM_STATES), F32),
                        pltpu.VMEM((SSM_WIDTH // LANES, S5_SUB, LANES), F32)],
        compiler_params=_cparams(2),
        name="s5_prompt",
    )(proj, *mats)


def _s5_sample_kernel(u_ref, h0re_ref, h0im_ref, bre_ref, bim_ref, cre_ref, cim_ref, lam_ref, d_ref,
                      y_ref, hre_ref, him_ref, *, steps):
    h_re = h0re_ref[...]
    h_im = h0im_ref[...]
    lam_re = lam_ref[0:1, :]
    lam_im = lam_ref[1:2, :]
    for t in range(steps):
        u = u_ref[t]
        bu_re, bu_im = _s5_input(u.astype(BF16), bre_ref, bim_ref)
        bu_re = jnp.concatenate(bu_re, axis=1)
        bu_im = jnp.concatenate(bu_im, axis=1)
        h_re, h_im = (lam_re * h_re - lam_im * h_im + bu_re,
                      lam_re * h_im + lam_im * h_re + bu_im)
        hb_re = h_re.astype(BF16)
        hb_im = h_im.astype(BF16)
        for tile in range(SSM_WIDTH // S5_OUT_TILE):
            st = slice(tile * S5_STATE_TILE, (tile + 1) * S5_STATE_TILE)
            ot = slice(tile * S5_OUT_TILE, (tile + 1) * S5_OUT_TILE)
            y = _s5_output(hb_re[:, st], hb_im[:, st], cre_ref, cim_ref, tile)
            y_ref[t, :, ot] = y + d_ref[:, ot] * u[:, ot]
    hre_ref[...] = h_re
    him_ref[...] = h_im


def _s5_sample(u_tb, h0_re, h0_im, mats, lam):
    steps, n, _ = u_tb.shape
    bre, bim, cre, cim, d_skip = mats
    state = jax.ShapeDtypeStruct((n, SSM_STATES), F32)
    args = (u_tb, h0_re, h0_im, bre, bim, cre, cim, lam, d_skip)
    return pl.pallas_call(
        functools.partial(_s5_sample_kernel, steps=steps),
        grid=(1,),
        in_specs=[_const_spec(a.shape) for a in args],
        out_specs=[_const_spec(u_tb.shape), _const_spec((n, SSM_STATES)), _const_spec((n, SSM_STATES))],
        out_shape=[jax.ShapeDtypeStruct(u_tb.shape, F32), state, state],
        compiler_params=_cparams(1),
        name="s5_sample",
    )(*args)


def _s5_matrices(lam_re, lam_im, log_dt, b_re, b_im, c_re, c_im, d_skip):
    lam = lax.complex(lam_re, lam_im)
    dt = jnp.exp(log_dt)[:, None]
    lam_bar = jnp.exp(lam * dt)
    b_bar = ((lam_bar - 1.0) / lam)[..., None] * lax.complex(b_re, b_im)
    eye_half = jnp.eye(SSM_GROUPS // 2, dtype=F32)

    def in_blocks(x):
        x = x.reshape(2, SSM_GROUPS // 2, SSM_STATE, SSM_GROUP)
        return jnp.einsum("kgpi,gh->kgihp", x, eye_half).reshape(2, S5_IN_HALF, S5_STATE_HALF).astype(BF16)

    tiles = SSM_WIDTH // S5_OUT_TILE
    groups_per_tile = SSM_GROUPS // tiles
    eye_tile = jnp.eye(groups_per_tile, dtype=F32)

    def out_blocks(x):
        x = x.reshape(tiles, groups_per_tile, SSM_GROUP, SSM_STATE)
        return jnp.einsum("kgop,gh->kgpho", x, eye_tile).reshape(tiles, S5_STATE_TILE, S5_OUT_TILE).astype(BF16)

    lam_flat = lam_bar.reshape(1, SSM_STATES)
    lam_rows = jnp.concatenate([lam_flat.real, lam_flat.imag], axis=0).astype(F32)
    d_row = d_skip.reshape(1, SSM_WIDTH)
    sample_mats = (in_blocks(b_bar.real), in_blocks(b_bar.imag), out_blocks(c_re), out_blocks(c_im), d_row)

    c_mat = lax.complex(c_re, c_im)
    lam_pow = jnp.concatenate([jnp.ones((1,) + lam_bar.shape, lam_bar.dtype),
                               jnp.cumprod(jnp.broadcast_to(lam_bar, (S5_CHUNK,) + lam_bar.shape), axis=0)])
    taps = jnp.einsum("gop,kgp,gpi->gkoi", c_mat, lam_pow[:S5_CHUNK], b_bar).real
    src = jnp.arange(S5_CHUNK)[:, None]
    dst = jnp.arange(S5_CHUNK)[None, :]
    lag = jnp.clip(dst - src, 0, S5_CHUNK - 1)
    toep = jnp.where((dst >= src)[None, :, :, None, None], taps[:, lag], 0.0)
    toep = jnp.transpose(toep, (0, 1, 4, 2, 3)).reshape(SSM_GROUPS, S5_ROW, S5_ROW).astype(BF16)
    w_state = jnp.einsum("sgp,gpi->gsip", lam_pow[S5_CHUNK - 1::-1][:S5_CHUNK], b_bar)
    w_state = w_state.reshape(SSM_GROUPS, S5_ROW, SSM_STATE)
    second = (jnp.arange(SSM_GROUPS) % 2 == 1)[:, None, None]
    zeros = jnp.zeros_like(w_state.real)

    def pair_cols(x):
        return jnp.where(second, jnp.concatenate([zeros, x], -1), jnp.concatenate([x, zeros], -1)).astype(BF16)

    v_out = jnp.einsum("gop,tgp->gpto", c_mat, lam_pow[1:]).reshape(SSM_GROUPS, SSM_STATE, S5_ROW)
    eye_pair = jnp.eye(2, dtype=F32)

    def pair_blocks(x):
        x = x.reshape(S5_PAIRS, 2, SSM_STATE, S5_ROW)
        return jnp.einsum("kapc,ab->kapbc", x, eye_pair).reshape(S5_PAIRS, LANES, 2 * S5_ROW).astype(BF16)

    lam_chunk = lam_pow[S5_CHUNK].reshape(1, SSM_STATES)
    rows = jnp.arange(SUBLANES)[:, None]
    planes = []
    for shift in (1, 2, 4):
        a = jnp.where(rows >= shift, lam_chunk ** shift, 0.0)
        planes += [a.real, a.imag]
    carry = jnp.cumprod(jnp.broadcast_to(lam_chunk, (SUBLANES, SSM_STATES)), axis=0)
    planes += [carry.real, carry.imag]
    powers = jnp.stack(planes).astype(F32)
    pos = jnp.arange(S5_SUB)
    perm = (jnp.arange(S5_SUB)[None, :] == ((pos % (S5_SUB // S5_CHUNK)) * S5_CHUNK
                                            + pos // (S5_SUB // S5_CHUNK))[:, None]).astype(BF16)
    prompt_mats = (perm, toep, pair_cols(w_state.real), pair_cols(w_state.imag),
                   pair_blocks(v_out.real), pair_blocks(-v_out.imag), powers, d_row)
    return prompt_mats, sample_mats, lam_rows


def _ab_out_kernel(yp_ref, ys_ref, wglu_ref, bglu_ref, wout_ref, h_ref, g_ref, o_ref):
    z = jax.nn.gelu(ys_ref[...])
    gate = jnp.dot(z.astype(BF16), wglu_ref[...], preferred_element_type=F32) + bglu_ref[...]
    y_ssm = z * jax.nn.sigmoid(gate)
    y = (jnp.dot(yp_ref[...], wout_ref[0:POOL_WIDTH, :], preferred_element_type=F32)
         + jnp.dot(y_ssm.astype(BF16), wout_ref[POOL_WIDTH:, :], preferred_element_type=F32))
    o_ref[...] = h_ref[...] + _rms(y, g_ref[...])


def _ab_out(y_pool, y_ssm, w_glu, b_glu, w_out, h, g, tm):
    m = h.shape[0]
    row = lambda width: pl.BlockSpec((tm, width), lambda i: (i, 0))
    return pl.pallas_call(
        _ab_out_kernel,
        grid=(m // tm,),
        in_specs=[row(POOL_WIDTH), row(SSM_WIDTH), _const_spec(w_glu.shape), _const_spec(b_glu.shape),
                  _const_spec(w_out.shape), row(D_MODEL), _const_spec(g.shape)],
        out_specs=row(D_MODEL),
        out_shape=jax.ShapeDtypeStruct((m, D_MODEL), F32),
        compiler_params=_cparams(1),
        name="ab_out",
    )(y_pool, y_ssm, w_glu, b_glu, w_out, h, g)


def _ffn_kernel(h_ref, gin_ref, wg_ref, wu_ref, wd_ref, gout_ref, o_ref):
    h = h_ref[...]
    xb = _rms(h, gin_ref[...]).astype(BF16)
    gate = jnp.dot(xb, wg_ref[...], preferred_element_type=F32)
    up = jnp.dot(xb, wu_ref[...], preferred_element_type=F32)
    act = (gate * jax.nn.sigmoid(gate) * up).astype(BF16)
    y = jnp.dot(act, wd_ref[...], preferred_element_type=F32)
    o_ref[...] = h + _rms(y, gout_ref[...])


def _ffn(h, g_in, w_gate, w_up, w_down, g_out, tm):
    m = h.shape[0]
    row = pl.BlockSpec((tm, D_MODEL), lambda i: (i, 0))
    return pl.pallas_call(
        _ffn_kernel,
        grid=(m // tm,),
        in_specs=[row, _const_spec(g_in.shape), _const_spec(w_gate.shape), _const_spec(w_up.shape),
                  _const_spec(w_down.shape), _const_spec(g_out.shape)],
        out_specs=row,
        out_shape=jax.ShapeDtypeStruct((m, D_MODEL), F32),
        compiler_params=_cparams(1),
        name="ffn",
    )(h, g_in, w_gate, w_up, w_down, g_out)


def _attn_prompt_kernel(q_ref, kc_ref, kp_ref, vc_ref, vp_ref, o_ref, lse_ref):
    blk = pl.program_id(2)
    rows = 2 * ATT_BLOCK
    qi = lax.broadcasted_iota(jnp.int32, (rows, 2 * ATT_BLOCK), 0) & (ATT_BLOCK - 1)
    kj = lax.broadcasted_iota(jnp.int32, (rows, 2 * ATT_BLOCK), 1)
    dist = ATT_BLOCK + qi - kj
    valid = (dist >= 0) & (dist <= ATT_SPAN) & ((kj >= ATT_BLOCK) | (blk > 0))
    lane = lax.broadcasted_iota(jnp.int32, (ATT_BLOCK, LANES), 1)
    first_head = lane < HEAD_DIM
    lse_all = jnp.zeros((ATT_BLOCK, LANES), F32)
    for pair in range(ATT_HEADS // 2):
        sl = slice(pair * LANES, (pair + 1) * LANES)
        q2 = q_ref[:, sl]
        zero = jnp.zeros_like(q2)
        qs = jnp.concatenate([jnp.where(first_head, q2, zero), jnp.where(first_head, zero, q2)], axis=0)
        k2 = jnp.concatenate([kp_ref[:, sl], kc_ref[:, sl]], axis=0)
        v2 = jnp.concatenate([vp_ref[:, sl], vc_ref[:, sl]], axis=0)
        s = lax.dot_general(qs, k2, (((1,), (1,)), ((), ())), preferred_element_type=F32)
        s = jnp.where(valid, s, -jnp.inf)
        m = jnp.max(s, axis=-1, keepdims=True)
        p = jnp.exp(s - m)
        den = jnp.sum(p, axis=-1, keepdims=True)
        o = jnp.dot(p.astype(BF16), v2, preferred_element_type=F32) / den
        lse = m + jnp.log(den)
        o_ref[:, sl] = jnp.where(first_head, o[:ATT_BLOCK], o[ATT_BLOCK:]).astype(o_ref.dtype)
        lse_all = jnp.where(lane == 2 * pair, lse[:ATT_BLOCK],
                            jnp.where(lane == 2 * pair + 1, lse[ATT_BLOCK:], lse_all))
    lse_ref[...] = lse_all


def _attn_prompt(q, k, v, dil):
    batch, _, sub, _ = q.shape
    nblk = sub // ATT_BLOCK
    cur = pl.BlockSpec((None, None, ATT_BLOCK, ATT_WIDTH), lambda b, r, i: (b, r, i, 0))
    prev = pl.BlockSpec((None, None, ATT_BLOCK, ATT_WIDTH), lambda b, r, i: (b, r, jnp.maximum(i - 1, 0), 0))
    return pl.pallas_call(
        _attn_prompt_kernel,
        grid=(batch, dil, nblk),
        in_specs=[cur, cur, prev, cur, prev],
        out_specs=[cur, pl.BlockSpec((None, None, ATT_BLOCK, LANES), lambda b, r, i: (b, r, i, 0))],
        out_shape=[jax.ShapeDtypeStruct((batch, dil, sub, ATT_WIDTH), BF16),
                   jax.ShapeDtypeStruct((batch, dil, sub, LANES), F32)],
        compiler_params=_cparams(3),
        name="attn_prompt",
    )(q, k, k, v, v)


def _pattern_multiplicity(dist):
    mult = jnp.zeros(dist.shape, F32)
    for window, dil in DILATED_PATTERNS:
        hit = (dist >= 0) & (dist <= window) & ((dist & (dil - 1)) == 0)
        mult = mult + hit.astype(F32)
    return mult


def _attn_sample_kernel(q_ref, knew_ref, vnew_ref, kt_ref, vt_ref, o_ref, *, steps):
    t_cache = lax.broadcasted_iota(jnp.int32, (SUBLANES, MAX_WINDOW), 0)
    pos = lax.broadcasted_iota(jnp.int32, (SUBLANES, MAX_WINDOW), 1)
    mult = _pattern_multiplicity(MAX_WINDOW + t_cache - pos)
    t_new = lax.broadcasted_iota(jnp.int32, (SUBLANES, LANES), 0)
    j_new = lax.broadcasted_iota(jnp.int32, (SUBLANES, LANES), 1)
    mult_new = jnp.where(j_new < steps, _pattern_multiplicity(t_new - j_new), 0.0)
    nt = (((1,), (1,)), ((), ()))
    for head in range(ATT_HEADS):
        q = q_ref[0, head].astype(BF16)
        s = jnp.dot(q, kt_ref[0, head].astype(BF16), preferred_element_type=F32)
        s_new = lax.dot_general(q, knew_ref[0, head].astype(BF16), nt, preferred_element_type=F32)
        s = jnp.where(mult > 0.0, s, -jnp.inf)
        s_new = jnp.where(mult_new > 0.0, s_new, -jnp.inf)
        m = jnp.maximum(jnp.max(s, axis=-1, keepdims=True), jnp.max(s_new, axis=-1, keepdims=True))
        p = mult * jnp.exp(s - m)
        p_new = mult_new * jnp.exp(s_new - m)
        den = jnp.sum(p, axis=-1, keepdims=True) + jnp.sum(p_new, axis=-1, keepdims=True)
        o = lax.dot_general(p.astype(BF16), vt_ref[0, head].astype(BF16), nt, preferred_element_type=F32)
        o = o + jnp.dot(p_new.astype(BF16), vnew_ref[0, head].astype(BF16), preferred_element_type=F32)
        o_ref[0, :, head * HEAD_DIM:(head + 1) * HEAD_DIM] = (o / den)[:steps]


def _attn_sample(q, k_new, v_new, cache_kt, cache_vt, steps):
    n = q.shape[0]
    assert steps <= SUBLANES
    per = lambda a: pl.BlockSpec((1,) + a.shape[1:], lambda b: (b, 0, 0, 0))
    args = (q, k_new, v_new, cache_kt, cache_vt)
    return pl.pallas_call(
        functools.partial(_attn_sample_kernel, steps=steps),
        grid=(n,),
        in_specs=[per(a) for a in args],
        out_specs=pl.BlockSpec((1, steps, ATT_WIDTH), lambda b: (b, 0, 0)),
        out_shape=jax.ShapeDtypeStruct((n, steps, ATT_WIDTH), F32),
        compiler_params=_cparams(1),
        name="attn_sample",
    )(*args)


def _attn_out_kernel(*refs, tm):
    n_pat = len(ATT_DILATIONS)
    o_refs = refs[:n_pat]
    lse_refs = refs[n_pat:2 * n_pat]
    w_ref, h_ref, g_ref, out_ref, o_nat, lse_nat = refs[2 * n_pat:]
    for pat, dil in enumerate(ATT_DILATIONS):
        rows = tm // dil
        for res in range(dil):
            dst = pl.ds(res, rows, stride=dil) if dil > 1 else pl.ds(0, rows)
            lse_nat[pat, dst, :] = lse_refs[pat][0, res]
            for c in range(COL_TILES):
                o_nat[pat * COL_TILES + c, dst, :] = o_refs[pat][0, res, :, c * LANES:(c + 1) * LANES].astype(F32)
    lses = [lse_nat[pat] for pat in range(n_pat)]
    top = functools.reduce(jnp.maximum, lses)
    es = [jnp.exp(l - top) for l in lses]
    total = functools.reduce(lambda x, y: x + y, es)
    wgts = [e / total for e in es]
    lane = lax.broadcasted_iota(jnp.int32, (tm, LANES), 1)
    first_head = lane < HEAD_DIM
    pieces = []
    for pair in range(ATT_HEADS // 2):
        acc = None
        for pat, wgt in enumerate(wgts):
            w2 = jnp.where(first_head, wgt[:, 2 * pair:2 * pair + 1], wgt[:, 2 * pair + 1:2 * pair + 2])
            term = w2 * o_nat[pat * COL_TILES + pair]
            acc = term if acc is None else acc + term
        pieces.append(acc.astype(BF16))
    a = jnp.concatenate(pieces, axis=1)
    y = jnp.dot(a, w_ref[...], preferred_element_type=F32)
    out_ref[...] = h_ref[...] + _rms(y, g_ref[...])


def _attn_out(outs, lses, w_o, h, g, seq, tm):
    m = h.shape[0]
    tiles = seq // tm
    n_pat = len(ATT_DILATIONS)

    def dil_spec(dil, width):
        return pl.BlockSpec((1, dil, tm // dil, width), lambda i: (i // tiles, 0, i % tiles, 0))

    row = pl.BlockSpec((tm, D_MODEL), lambda i: (i, 0))
    return pl.pallas_call(
        functools.partial(_attn_out_kernel, tm=tm),
        grid=(m // tm,),
        in_specs=[dil_spec(dil, ATT_WIDTH) for dil in ATT_DILATIONS]
        + [dil_spec(dil, LANES) for dil in ATT_DILATIONS]
        + [_const_spec(w_o.shape), row, _const_spec(g.shape)],
        out_specs=row,
        out_shape=jax.ShapeDtypeStruct((m, D_MODEL), F32),
        scratch_shapes=[pltpu.VMEM((n_pat * COL_TILES, tm, LANES), F32), pltpu.VMEM((n_pat, tm, LANES), F32)],
        compiler_params=_cparams(1),
        name="attn_out",
    )(*outs, *lses, w_o, h, g)


def _proj_out_kernel(a_ref, w_ref, h_ref, g_ref, out_ref):
    y = jnp.dot(a_ref[...].astype(BF16), w_ref[...], preferred_element_type=F32)
    out_ref[...] = h_ref[...] + _rms(y, g_ref[...])


def _proj_out(a, w, h, g, tm):
    m = h.shape[0]
    return pl.pallas_call(
        _proj_out_kernel,
        grid=(m // tm,),
        in_specs=[pl.BlockSpec((tm, a.shape[1]), lambda i: (i, 0)), _const_spec(w.shape),
                  pl.BlockSpec((tm, D_MODEL), lambda i: (i, 0)), _const_spec(g.shape)],
        out_specs=pl.BlockSpec((tm, D_MODEL), lambda i: (i, 0)),
        out_shape=jax.ShapeDtypeStruct((m, D_MODEL), F32),
        compiler_params=_cparams(1),
        name="proj_out",
    )(a, w, h, g)


def kernel(x_prompt, x_sample, state_pool, state_s5, cache_k, cache_v, norm_gains, ab_w_in, ab_pool_w, ab_pool_scale, ab_lambda_re, ab_lambda_im, ab_log_dt, ab_b_re, ab_b_im, ab_c_re, ab_c_im, ab_d, ab_w_glu, ab_b_glu, ab_w_out, c_w_qkv, c_w_o, ffn_w_gate, ffn_w_up, ffn_w_down):
    batch, seq, d = x_prompt.shape
    n_dec, t_dec, _ = x_sample.shape
    mp, ms = batch * seq, n_dec * t_dec
    tm_p = 256
    tm_ffn = 512
    gains = norm_gains.reshape(norm_gains.shape[0], 4, 1, d)

    hp = x_prompt.reshape(mp, d)
    hs = x_sample.reshape(ms, d)

    g = gains[0]
    w_in = ab_w_in[0].astype(BF16)
    (proj_p,) = _norm_matmul(hp, g[0], w_in, ((0, d, 1.0),), (F32,), tm_p)
    (proj_s,) = _norm_matmul(hs, g[0], w_in, ((0, d, 1.0),), (F32,), ms)
    proj_p = proj_p.reshape(batch, seq, d)
    proj_s = proj_s.reshape(n_dec, t_dec, d)

    pool_w = ab_pool_w[0].astype(BF16)
    pool_scale = ab_pool_scale[0].reshape(1, POOL_WIDTH)
    lead_p = jnp.zeros((batch, POOL_LEAD, POOL_WIDTH), F32)
    lead_s = jnp.pad(state_pool[0], ((0, 0), (POOL_LEAD - POOL_STATE, 0), (0, 0)))
    ypool_p = _pool_mixer(proj_p, lead_p, pool_w, pool_scale, 256, 0)
    ypool_s = _pool_mixer(proj_s, lead_s, pool_w, pool_scale, t_dec, PAST_LEN)
    pool_prompt = proj_p[:, seq - POOL_STATE:, :POOL_WIDTH][None]
    pool_sample = jnp.concatenate([state_pool[0], proj_s[:, :, :POOL_WIDTH]], axis=1)[:, -POOL_STATE:][None]

    prompt_mats, sample_mats, lam_rows = _s5_matrices(
        ab_lambda_re[0], ab_lambda_im[0], ab_log_dt[0], ab_b_re[0], ab_b_im[0], ab_c_re[0], ab_c_im[0], ab_d[0])
    yssm_p, hre_p, him_p = _s5_prompt(proj_p, prompt_mats, 2048)
    u_tb = jnp.swapaxes(proj_s[:, :, POOL_WIDTH:], 0, 1)
    h0 = state_s5[0].reshape(n_dec, SSM_STATES, 2)
    yssm_tb, hre_s, him_s = _s5_sample(u_tb, h0[..., 0], h0[..., 1], sample_mats, lam_rows)
    yssm_s = jnp.swapaxes(yssm_tb, 0, 1)
    s5_prompt = jnp.stack([hre_p[:, 0], him_p[:, 0]], axis=-1).reshape(1, batch, SSM_GROUPS, SSM_STATE, 2)
    s5_sample = jnp.stack([hre_s, him_s], axis=-1).reshape(1, n_dec, SSM_GROUPS, SSM_STATE, 2)

    w_glu = ab_w_glu[0].astype(BF16)
    b_glu = ab_b_glu[0].reshape(1, SSM_WIDTH)
    w_out = ab_w_out[0].astype(BF16)
    hp = _ab_out(ypool_p.reshape(mp, POOL_WIDTH), yssm_p.reshape(mp, SSM_WIDTH), w_glu, b_glu, w_out, hp, g[1], tm_p)
    hs = _ab_out(ypool_s.reshape(ms, POOL_WIDTH), yssm_s.reshape(ms, SSM_WIDTH), w_glu, b_glu, w_out, hs, g[1], ms)

    wg, wu, wd = ffn_w_gate[0].astype(BF16), ffn_w_up[0].astype(BF16), ffn_w_down[0].astype(BF16)
    hp = _ffn(hp, g[2], wg, wu, wd, g[3], tm_ffn)
    hs = _ffn(hs, g[2], wg, wu, wd, g[3], ms)

    g = gains[1]
    w_qkv = c_w_qkv[0].astype(BF16)
    w_o = c_w_o[0].astype(BF16)
    keep = min(MAX_WINDOW, seq)
    *qkv_p, kv_tail = _qkv_prompt(hp, g[0], w_qkv, batch, seq, keep, tm_p)
    k_prompt = kv_tail[:, :, :ATT_WIDTH].reshape(1, batch, keep, ATT_HEADS, HEAD_DIM)
    v_prompt = kv_tail[:, :, ATT_WIDTH:].reshape(1, batch, keep, ATT_HEADS, HEAD_DIM)
    n_dil = len(ATT_DILATIONS)
    outs, lses = [], []
    for pat, dil in enumerate(ATT_DILATIONS):
        o, lse = _attn_prompt(qkv_p[pat], qkv_p[n_dil + pat], qkv_p[2 * n_dil + pat], dil)
        outs.append(o)
        lses.append(lse)
    hp = _attn_out(outs, lses, w_o, hp, g[1], seq, tm_p)

    scale = HEAD_DIM ** -0.5
    q_s, k_s, v_s = _norm_matmul(
        hs, g[0], w_qkv,
        ((0, ATT_WIDTH, scale), (ATT_WIDTH, ATT_WIDTH, 1.0), (2 * ATT_WIDTH, ATT_WIDTH, 1.0)),
        (F32, F32, F32), ms)
    k_sample = k_s.reshape(1, n_dec, t_dec, ATT_HEADS, HEAD_DIM)
    v_sample = v_s.reshape(1, n_dec, t_dec, ATT_HEADS, HEAD_DIM)
    head_major = lambda a, rows: jnp.pad(
        jnp.swapaxes(a.reshape(n_dec, t_dec, ATT_HEADS, HEAD_DIM), 1, 2),
        ((0, 0), (0, 0), (0, rows - t_dec), (0, 0)))
    cache_kt = jnp.transpose(cache_k[0], (0, 2, 3, 1))
    cache_vt = jnp.transpose(cache_v[0], (0, 2, 3, 1))
    o_s = _attn_sample(head_major(q_s, SUBLANES), head_major(k_s, LANES), head_major(v_s, LANES),
                       cache_kt, cache_vt, t_dec)
    hs = _proj_out(o_s.reshape(ms, ATT_WIDTH), w_o, hs, g[1], ms)

    wg, wu, wd = ffn_w_gate[1].astype(BF16), ffn_w_up[1].astype(BF16), ffn_w_down[1].astype(BF16)
    hp = _ffn(hp, g[2], wg, wu, wd, g[3], tm_ffn)
    hs = _ffn(hs, g[2], wg, wu, wd, g[3], ms)

    return (hp.reshape(batch, seq, d), hs.reshape(n_dec, t_dec, d), pool_prompt, s5_prompt,
            k_prompt, v_prompt, pool_sample, s5_sample, k_sample, v_sample)
```

```python
import functools
import math

import jax
import jax.numpy as jnp
from jax import lax
from jax.experimental import pallas as pl
from jax.experimental.pallas import tpu as pltpu

F32 = jnp.float32
BF16 = jnp.bfloat16

D_MODEL = 1024
PAST_LEN = 16384
POOL_WIDTH = 512
POOL_WINDOWS = (2, 4, 8, 16)
POOL_GROUP = 128
POOL_STATE = 15
POOL_LEAD = 16
SSM_WIDTH = 512
SSM_GROUP = 16
SSM_GROUPS = 32
SSM_STATE = 64
SSM_STATES = SSM_GROUPS * SSM_STATE
ATT_HEADS = 16
HEAD_DIM = 64
ATT_WIDTH = ATT_HEADS * HEAD_DIM
DILATED_PATTERNS = ((128, 1), (512, 4), (2048, 16))
MAX_WINDOW = 2048
ATT_BLOCK = 128
ATT_SPAN = 128
FFN_HIDDEN = 2816
RMS_EPS = 1e-6

SUBLANES = 8
LANES = 128
VMEM_LIMIT = 56 * 1024 * 1024


def _cparams(n_axes):
    return pltpu.CompilerParams(
        dimension_semantics=("arbitrary",) * n_axes, vmem_limit_bytes=VMEM_LIMIT)


def _rms(x, g):
    ms = jnp.mean(x * x, axis=-1, keepdims=True)
    return (x * lax.rsqrt(ms + RMS_EPS)) * g


def _const_spec(shape):
    zeros = (0,) * len(shape)
    return pl.BlockSpec(shape, lambda *_: zeros, pipeline_mode=pl.Buffered(1))


def _norm_matmul_kernel(x_ref, g_ref, w_ref, *out_refs, splits):
    xb = _rms(x_ref[...], g_ref[...]).astype(BF16)
    for (c0, width, scale), o_ref in zip(splits, out_refs):
        y = jnp.dot(xb, w_ref[:, c0:c0 + width], preferred_element_type=F32)
        if scale != 1.0:
            y = y * scale
        o_ref[...] = y.astype(o_ref.dtype)


def _norm_matmul(x, g, w, splits, dtypes, tm):
    m, d = x.shape
    n = w.shape[1]
    return pl.pallas_call(
        functools.partial(_norm_matmul_kernel, splits=splits),
        grid=(m // tm,),
        in_specs=[pl.BlockSpec((tm, d), lambda i: (i, 0)), _const_spec((1, d)), _const_spec((d, n))],
        out_specs=[pl.BlockSpec((tm, width), lambda i: (i, 0)) for _, width, _ in splits],
        out_shape=[jax.ShapeDtypeStruct((m, width), dt) for (_, width, _), dt in zip(splits, dtypes)],
        compiler_params=_cparams(1),
        name="norm_matmul",
    )(x, g, w)


ATT_DILATIONS = tuple(dil for _, dil in DILATED_PATTERNS)
COL_TILES = ATT_WIDTH // LANES


def _qkv_prompt_kernel(x_ref, g_ref, w_ref, perm_ref, *refs, q_scale, tm):
    n_dil = len(ATT_DILATIONS)
    out_refs = refs[:3 * n_dil]
    tail_ref = refs[3 * n_dil]
    xb = _rms(x_ref[...], g_ref[...]).astype(BF16)
    for which in range(3):
        y = jnp.dot(xb, w_ref[:, which * ATT_WIDTH:(which + 1) * ATT_WIDTH], preferred_element_type=F32)
        if which == 0:
            y = y * q_scale
        else:
            tail_ref[0, :, (which - 1) * ATT_WIDTH:which * ATT_WIDTH] = y
        yb = y.astype(BF16)
        for pat, (dil, o_ref) in enumerate(zip(ATT_DILATIONS, out_refs[which * n_dil:(which + 1) * n_dil])):
            if dil == 1:
                o_ref[0, 0] = yb
                continue
            moved = jnp.dot(perm_ref[pat - 1], yb, preferred_element_type=F32).astype(BF16)
            rows = tm // dil
            for res in range(dil):
                o_ref[0, res] = moved[res * rows:(res + 1) * rows]


def _qkv_prompt(x, g, w, batch, seq, keep, tm):
    m, d = x.shape
    tiles = seq // tm
    first_tail = tiles - keep // tm

    def tail_map(i):
        return (i // tiles, jnp.maximum(i % tiles - first_tail, 0), 0)

    def dil_spec(dil):
        return pl.BlockSpec((1, dil, tm // dil, ATT_WIDTH), lambda i: (i // tiles, 0, i % tiles, 0))

    def dil_shape(dil):
        return jax.ShapeDtypeStruct((batch, dil, seq // dil, ATT_WIDTH), BF16)

    assert ATT_DILATIONS[0] == 1
    src = jnp.arange(tm)[None, :]
    dst = jnp.arange(tm)[:, None]
    perm = jnp.stack([(src == (dst % (tm // dil)) * dil + dst // (tm // dil)).astype(BF16)
                      for dil in ATT_DILATIONS[1:]])
    return pl.pallas_call(
        functools.partial(_qkv_prompt_kernel, q_scale=HEAD_DIM ** -0.5, tm=tm),
        grid=(m // tm,),
        in_specs=[pl.BlockSpec((tm, d), lambda i: (i, 0)), _const_spec((1, d)),
                  _const_spec((d, 3 * ATT_WIDTH)), _const_spec(perm.shape)],
        out_specs=[dil_spec(dil) for _ in range(3) for dil in ATT_DILATIONS]
        + [pl.BlockSpec((1, tm, 2 * ATT_WIDTH), tail_map)],
        out_shape=[dil_shape(dil) for _ in range(3) for dil in ATT_DILATIONS]
        + [jax.ShapeDtypeStruct((batch, keep, 2 * ATT_WIDTH), F32)],
        compiler_params=_cparams(1),
        name="qkv_prompt",
    )(x, g, w, perm)


def _pool_kernel(u_ref, lead_ref, w_ref, scale_ref, y_ref, ext, *, tt, pos0, carry):
    j = pl.program_id(1)

    @pl.when(j == 0)
    def _():
        ext[0:POOL_LEAD, :] = lead_ref[0]

    u = u_ref[0]
    ext[POOL_LEAD:POOL_LEAD + tt, :] = u
    pos = pos0 + j * tt + lax.broadcasted_iota(jnp.int32, (tt, POOL_GROUP), 0)
    for grp, window in enumerate(POOL_WINDOWS):
        sl = slice(grp * POOL_GROUP, (grp + 1) * POOL_GROUP)
        u_g = u[:, sl]
        win_sum = u_g
        for k in range(1, window):
            win_sum = win_sum + ext[POOL_LEAD - k:POOL_LEAD - k + tt, sl]
        count = jnp.minimum(pos + 1, window).astype(F32)
        diff = win_sum / count - u_g
        y = jnp.dot(diff.astype(BF16), w_ref[grp], preferred_element_type=F32)
        y_ref[0, :, sl] = (y * scale_ref[:, sl]).astype(y_ref.dtype)
    if carry:
        ext[0:POOL_LEAD, :] = ext[tt:tt + POOL_LEAD, :]


def _pool_mixer(proj, lead, w, scale, tt, pos0):
    n, t, _ = proj.shape
    steps = t // tt
    return pl.pallas_call(
        functools.partial(_pool_kernel, tt=tt, pos0=pos0, carry=steps > 1),
        grid=(n, steps),
        in_specs=[pl.BlockSpec((1, tt, POOL_WIDTH), lambda b, j: (b, j, 0)),
                  pl.BlockSpec((1, POOL_LEAD, POOL_WIDTH), lambda b, j: (b, 0, 0)),
                  _const_spec((len(POOL_WINDOWS), POOL_GROUP, POOL_GROUP)),
                  _const_spec((1, POOL_WIDTH))],
        out_specs=pl.BlockSpec((1, tt, POOL_WIDTH), lambda b, j: (b, j, 0)),
        out_shape=jax.ShapeDtypeStruct((n, t, POOL_WIDTH), BF16),
        scratch_shapes=[pltpu.VMEM((POOL_LEAD + tt, POOL_WIDTH), F32)],
        compiler_params=_cparams(2),
        name="pool_mixer",
    )(proj, lead, w, scale)


S5_IN_HALF = SSM_WIDTH // 2
S5_STATE_HALF = SSM_STATES // 2
S5_OUT_TILE = LANES
S5_STATE_TILE = SSM_STATES // (SSM_WIDTH // S5_OUT_TILE)
S5_SCAN_LANES = 512


def _s5_input(ub, bre_ref, bim_ref):
    re, im = [], []
    for half in range(2):
        uk = ub[:, half * S5_IN_HALF:(half + 1) * S5_IN_HALF]
        re.append(jnp.dot(uk, bre_ref[half], preferred_element_type=F32))
        im.append(jnp.dot(uk, bim_ref[half], preferred_element_type=F32))
    return re, im


def _s5_output(h_re, h_im, cre_ref, cim_ref, tile):
    return (jnp.dot(h_re, cre_ref[tile], preferred_element_type=F32)
            - jnp.dot(h_im, cim_ref[tile], preferred_element_type=F32))


S5_CHUNK = 16
S5_ROW = S5_CHUNK * SSM_GROUP
S5_SUB = 256
UNIT = SSM_GROUP
UNITS = LANES // UNIT
S5_PAIRS = SSM_GROUPS // 2


def _unit_transpose(pieces):
    unit = lax.broadcasted_iota(jnp.int32, pieces[0].shape, 1) >> (UNIT.bit_length() - 1)
    cur = list(pieces)
    k = UNITS // 2
    while k:
        high = (unit & k) != 0
        nxt = list(cur)
        for a in range(UNITS):
            if a & k:
                continue
            lo_piece, hi_piece = cur[a], cur[a | k]
            nxt[a] = jnp.where(high, pltpu.roll(hi_piece, k * UNIT, 1), lo_piece)
            nxt[a | k] = jnp.where(high, hi_piece, pltpu.roll(lo_piece, LANES - k * UNIT, 1))
        cur = nxt
        k //= 2
    return cur


def _s5_prompt_kernel(u_ref, perm_ref, toep_ref, wre_ref, wim_ref, vre_ref, vim_ref, pw_ref, d_ref,
                      y_ref, hre_ref, him_ref, u2, y2, s_re, s_im, c_re, c_im, y_nat, *, tt):
    j = pl.program_id(1)
    chunks = tt // S5_CHUNK
    sub_chunks = S5_SUB // S5_CHUNK

    @pl.when(j == 0)
    def _():
        c_re[...] = jnp.zeros_like(c_re)
        c_im[...] = jnp.zeros_like(c_im)

    def relayout_in(sub, carry):
        r0 = pl.multiple_of(sub * S5_SUB, S5_SUB)
        c0 = pl.multiple_of(sub * sub_chunks, sub_chunks)
        ub = u_ref[0, pl.ds(r0, S5_SUB), :].astype(BF16)
        xs = jnp.dot(perm_ref[...], ub, preferred_element_type=F32).astype(BF16)
        for tile in range(SSM_WIDTH // LANES):
            for half in range(S5_CHUNK // UNITS):
                pieces = [pltpu.bitcast(
                    xs[(half * UNITS + a) * sub_chunks:(half * UNITS + a + 1) * sub_chunks,
                       tile * LANES:(tile + 1) * LANES], jnp.uint32) for a in range(UNITS)]
                outs = _unit_transpose(pieces)
                for gl in range(UNITS):
                    u2[tile * UNITS + gl, pl.ds(c0, sub_chunks), half * LANES:(half + 1) * LANES] = (
                        pltpu.bitcast(outs[gl], BF16))
        return carry

    lax.fori_loop(0, tt // S5_SUB, relayout_in, 0)

    s_re[0:SUBLANES, :] = c_re[...]
    s_im[0:SUBLANES, :] = c_im[...]
    for pair in range(S5_PAIRS):
        sl = slice(pair * LANES, (pair + 1) * LANES)
        ua = u2[2 * pair]
        ub = u2[2 * pair + 1]
        s_re[SUBLANES:SUBLANES + chunks, sl] = (
            jnp.dot(ua, wre_ref[2 * pair], preferred_element_type=F32)
            + jnp.dot(ub, wre_ref[2 * pair + 1], preferred_element_type=F32))
        s_im[SUBLANES:SUBLANES + chunks, sl] = (
            jnp.dot(ua, wim_ref[2 * pair], preferred_element_type=F32)
            + jnp.dot(ub, wim_ref[2 * pair + 1], preferred_element_type=F32))

    for chunk in range(SSM_STATES // S5_SCAN_LANES):
        sl = slice(chunk * S5_SCAN_LANES, (chunk + 1) * S5_SCAN_LANES)

        def body(r, carry, sl=sl):
            in_re, in_im = carry
            row = pl.multiple_of((r + 1) * SUBLANES, SUBLANES)
            re = s_re[pl.ds(row, SUBLANES), sl]
            im = s_im[pl.ds(row, SUBLANES), sl]
            for level, shift in enumerate((1, 2, 4)):
                a_re = pw_ref[2 * level, :, sl]
                a_im = pw_ref[2 * level + 1, :, sl]
                sh_re = pltpu.roll(re, shift, 0)
                sh_im = pltpu.roll(im, shift, 0)
                re, im = (re + a_re * sh_re - a_im * sh_im,
                          im + a_re * sh_im + a_im * sh_re)
            p_re = pw_ref[6, :, sl]
            p_im = pw_ref[7, :, sl]
            re, im = (re + p_re * in_re - p_im * in_im,
                      im + p_re * in_im + p_im * in_re)
            s_re[pl.ds(row, SUBLANES), sl] = re
            s_im[pl.ds(row, SUBLANES), sl] = im
            last = SUBLANES - 1
            return (jnp.broadcast_to(re[last:last + 1, :], re.shape),
                    jnp.broadcast_to(im[last:last + 1, :], im.shape))

        out_re, out_im = lax.fori_loop(0, chunks // SUBLANES, body, (c_re[:, sl], c_im[:, sl]))
        c_re[:, sl] = out_re
        c_im[:, sl] = out_im

    hre_ref[0] = c_re[...]
    him_ref[0] = c_im[...]

    for pair in range(S5_PAIRS):
        sl = slice(pair * LANES, (pair + 1) * LANES)
        h_re = s_re[SUBLANES - 1:SUBLANES - 1 + chunks, sl].astype(BF16)
        h_im = s_im[SUBLANES - 1:SUBLANES - 1 + chunks, sl].astype(BF16)
        carried = (jnp.dot(h_re, vre_ref[pair], preferred_element_type=F32)
                   + jnp.dot(h_im, vim_ref[pair], preferred_element_type=F32))
        for k in range(2):
            grp = 2 * pair + k
            y2[grp] = (jnp.dot(u2[grp], toep_ref[grp], preferred_element_type=F32)
                       + carried[:, k * S5_ROW:(k + 1) * S5_ROW])

    def relayout_out(blk, carry):
        r0 = pl.multiple_of(blk * S5_SUB, S5_SUB)
        c0 = pl.multiple_of(blk * sub_chunks, sub_chunks)
        for tile in range(SSM_WIDTH // LANES):
            ot = slice(tile * LANES, (tile + 1) * LANES)
            for half in range(S5_CHUNK // UNITS):
                pieces = [y2[tile * UNITS + gl, pl.ds(c0, sub_chunks), half * LANES:(half + 1) * LANES]
                          for gl in range(UNITS)]
                outs = _unit_transpose(pieces)
                for a in range(UNITS):
                    y_nat[tile, pl.ds(half * UNITS + a, sub_chunks, stride=S5_CHUNK), :] = outs[a]
            y_ref[0, pl.ds(r0, S5_SUB), ot] = y_nat[tile] + d_ref[:, ot] * u_ref[0, pl.ds(r0, S5_SUB), ot]
        return carry

    lax.fori_loop(0, tt // S5_SUB, relayout_out, 0)


def _s5_prompt(proj, mats, tt):
    b, t, _ = proj.shape
    chunks = tt // S5_CHUNK
    state = jax.ShapeDtypeStruct((b, SUBLANES, SSM_STATES), F32)
    state_spec = pl.BlockSpec((1, SUBLANES, SSM_STATES), lambda i, j: (i, 0, 0))
    return pl.pallas_call(
        functools.partial(_s5_prompt_kernel, tt=tt),
        grid=(b, t // tt),
        in_specs=[pl.BlockSpec((1, tt, SSM_WIDTH), lambda i, j: (i, j, 1))]
        + [_const_spec(a.shape) for a in mats],
        out_specs=[pl.BlockSpec((1, tt, SSM_WIDTH), lambda i, j: (i, j, 0)), state_spec, state_spec],
        out_shape=[jax.ShapeDtypeStruct((b, t, SSM_WIDTH), F32), state, state],
        scratch_shapes=[pltpu.VMEM((SSM_GROUPS, chunks, S5_ROW), BF16),
                        pltpu.VMEM((SSM_GROUPS, chunks, S5_ROW), F32),
                        pltpu.VMEM((SUBLANES + chunks, SSM_STATES), F32),
                        pltpu.VMEM((SUBLANES + chunks, SSM_STATES), F32),
                        pltpu.VMEM((SUBLANES, SSM_STATES), F32), pltpu.VMEM((SUBLANES, SSM_STATES), F32),
                        pltpu.VMEM((SSM_WIDTH // LANES, S5_SUB, LANES), F32)],
        compiler_params=_cparams(2),
        name="s5_prompt",
    )(proj, *mats)


def _s5_sample_kernel(u_ref, h0re_ref, h0im_ref, bre_ref, bim_ref, cre_ref, cim_ref, lam_ref, d_ref,
                      y_ref, hre_ref, him_ref, *, steps):
    h_re = h0re_ref[...]
    h_im = h0im_ref[...]
    lam_re = lam_ref[0:1, :]
    lam_im = lam_ref[1:2, :]
    for t in range(steps):
        u = u_ref[t]
        bu_re, bu_im = _s5_input(u.astype(BF16), bre_ref, bim_ref)
        bu_re = jnp.concatenate(bu_re, axis=1)
        bu_im = jnp.concatenate(bu_im, axis=1)
        h_re, h_im = (lam_re * h_re - lam_im * h_im + bu_re,
                      lam_re * h_im + lam_im * h_re + bu_im)
        hb_re = h_re.astype(BF16)
        hb_im = h_im.astype(BF16)
        for tile in range(SSM_WIDTH // S5_OUT_TILE):
            st = slice(tile * S5_STATE_TILE, (tile + 1) * S5_STATE_TILE)
            ot = slice(tile * S5_OUT_TILE, (tile + 1) * S5_OUT_TILE)
            y = _s5_output(hb_re[:, st], hb_im[:, st], cre_ref, cim_ref, tile)
            y_ref[t, :, ot] = y + d_ref[:, ot] * u[:, ot]
    hre_ref[...] = h_re
    him_ref[...] = h_im


def _s5_sample(u_tb, h0_re, h0_im, mats, lam):
    steps, n, _ = u_tb.shape
    bre, bim, cre, cim, d_skip = mats
    state = jax.ShapeDtypeStruct((n, SSM_STATES), F32)
    args = (u_tb, h0_re, h0_im, bre, bim, cre, cim, lam, d_skip)
    return pl.pallas_call(
        functools.partial(_s5_sample_kernel, steps=steps),
        grid=(1,),
        in_specs=[_const_spec(a.shape) for a in args],
        out_specs=[_const_spec(u_tb.shape), _const_spec((n, SSM_STATES)), _const_spec((n, SSM_STATES))],
        out_shape=[jax.ShapeDtypeStruct(u_tb.shape, F32), state, state],
        compiler_params=_cparams(1),
        name="s5_sample",
    )(*args)


def _s5_matrices(lam_re, lam_im, log_dt, b_re, b_im, c_re, c_im, d_skip):
    lam = lax.complex(lam_re, lam_im)
    dt = jnp.exp(log_dt)[:, None]
    lam_bar = jnp.exp(lam * dt)
    b_bar = ((lam_bar - 1.0) / lam)[..., None] * lax.complex(b_re, b_im)
    eye_half = jnp.eye(SSM_GROUPS // 2, dtype=F32)

    def in_blocks(x):
        x = x.reshape(2, SSM_GROUPS // 2, SSM_STATE, SSM_GROUP)
        return jnp.einsum("kgpi,gh->kgihp", x, eye_half).reshape(2, S5_IN_HALF, S5_STATE_HALF).astype(BF16)

    tiles = SSM_WIDTH // S5_OUT_TILE
    groups_per_tile = SSM_GROUPS // tiles
    eye_tile = jnp.eye(groups_per_tile, dtype=F32)

    def out_blocks(x):
        x = x.reshape(tiles, groups_per_tile, SSM_GROUP, SSM_STATE)
        return jnp.einsum("kgop,gh->kgpho", x, eye_tile).reshape(tiles, S5_STATE_TILE, S5_OUT_TILE).astype(BF16)

    lam_flat = lam_bar.reshape(1, SSM_STATES)
    lam_rows = jnp.concatenate([lam_flat.real, lam_flat.imag], axis=0).astype(F32)
    d_row = d_skip.reshape(1, SSM_WIDTH)
    sample_mats = (in_blocks(b_bar.real), in_blocks(b_bar.imag), out_blocks(c_re), out_blocks(c_im), d_row)

    c_mat = lax.complex(c_re, c_im)
    lam_pow = jnp.concatenate([jnp.ones((1,) + lam_bar.shape, lam_bar.dtype),
                               jnp.cumprod(jnp.broadcast_to(lam_bar, (S5_CHUNK,) + lam_bar.shape), axis=0)])
    taps = jnp.einsum("gop,kgp,gpi->gkoi", c_mat, lam_pow[:S5_CHUNK], b_bar,
                      precision=lax.Precision.HIGHEST).real
    src = jnp.arange(S5_CHUNK)[None, :, None]
    dst = jnp.arange(S5_CHUNK)[None, None, :]
    lag_is = (dst - src == jnp.arange(S5_CHUNK)[:, None, None]).astype(F32)
    toep = jnp.einsum("kst,gkoi->gsito", lag_is, taps, precision=lax.Precision.HIGHEST)
    toep = toep.reshape(SSM_GROUPS, S5_ROW, S5_ROW).astype(BF16)
    w_state = jnp.einsum("sgp,gpi->gsip", lam_pow[S5_CHUNK - 1::-1][:S5_CHUNK], b_bar)
    w_state = w_state.reshape(SSM_GROUPS, S5_ROW, SSM_STATE)
    second = (jnp.arange(SSM_GROUPS) % 2 == 1)[:, None, None]
    zeros = jnp.zeros_like(w_state.real)

    def pair_cols(x):
        return jnp.where(second, jnp.concatenate([zeros, x], -1), jnp.concatenate([x, zeros], -1)).astype(BF16)

    v_out = jnp.einsum("gop,tgp->gpto", c_mat, lam_pow[1:]).reshape(SSM_GROUPS, SSM_STATE, S5_ROW)
    eye_pair = jnp.eye(2, dtype=F32)

    def pair_blocks(x):
        x = x.reshape(S5_PAIRS, 2, SSM_STATE, S5_ROW)
        return jnp.einsum("kapc,ab->kapbc", x, eye_pair).reshape(S5_PAIRS, LANES, 2 * S5_ROW).astype(BF16)

    lam_chunk = lam_pow[S5_CHUNK].reshape(1, SSM_STATES)
    rows = jnp.arange(SUBLANES)[:, None]
    planes = []
    for shift in (1, 2, 4):
        a = jnp.where(rows >= shift, lam_chunk ** shift, 0.0)
        planes += [a.real, a.imag]
    carry = jnp.cumprod(jnp.broadcast_to(lam_chunk, (SUBLANES, SSM_STATES)), axis=0)
    planes += [carry.real, carry.imag]
    powers = jnp.stack(planes).astype(F32)
    pos = jnp.arange(S5_SUB)
    perm = (jnp.arange(S5_SUB)[None, :] == ((pos % (S5_SUB // S5_CHUNK)) * S5_CHUNK
                                            + pos // (S5_SUB // S5_CHUNK))[:, None]).astype(BF16)
    prompt_mats = (perm, toep, pair_cols(w_state.real), pair_cols(w_state.imag),
                   pair_blocks(v_out.real), pair_blocks(-v_out.imag), powers, d_row)
    return prompt_mats, sample_mats, lam_rows


def _ab_out_kernel(yp_ref, ys_ref, wglu_ref, bglu_ref, wout_ref, h_ref, g_ref, o_ref):
    z = jax.nn.gelu(ys_ref[...])
    gate = jnp.dot(z.astype(BF16), wglu_ref[...], preferred_element_type=F32) + bglu_ref[...]
    y_ssm = z * jax.nn.sigmoid(gate)
    y = (jnp.dot(yp_ref[...], wout_ref[0:POOL_WIDTH, :], preferred_element_type=F32)
         + jnp.dot(y_ssm.astype(BF16), wout_ref[POOL_WIDTH:, :], preferred_element_type=F32))
    o_ref[...] = h_ref[...] + _rms(y, g_ref[...])


def _ab_out(y_pool, y_ssm, w_glu, b_glu, w_out, h, g, tm):
    m = h.shape[0]
    row = lambda width: pl.BlockSpec((tm, width), lambda i: (i, 0))
    return pl.pallas_call(
        _ab_out_kernel,
        grid=(m // tm,),
        in_specs=[row(POOL_WIDTH), row(SSM_WIDTH), _const_spec(w_glu.shape), _const_spec(b_glu.shape),
                  _const_spec(w_out.shape), row(D_MODEL), _const_spec(g.shape)],
        out_specs=row(D_MODEL),
        out_shape=jax.ShapeDtypeStruct((m, D_MODEL), F32),
        compiler_params=_cparams(1),
        name="ab_out",
    )(y_pool, y_ssm, w_glu, b_glu, w_out, h, g)


def _ffn_kernel(h_ref, gin_ref, wg_ref, wu_ref, wd_ref, gout_ref, o_ref):
    h = h_ref[...]
    xb = _rms(h, gin_ref[...]).astype(BF16)
    gate = jnp.dot(xb, wg_ref[...], preferred_element_type=F32)
    up = jnp.dot(xb, wu_ref[...], preferred_element_type=F32)
    act = (gate * jax.nn.sigmoid(gate) * up).astype(BF16)
    y = jnp.dot(act, wd_ref[...], preferred_element_type=F32)
    o_ref[...] = h + _rms(y, gout_ref[...])


def _ffn(h, g_in, w_gate, w_up, w_down, g_out, tm):
    m = h.shape[0]
    row = pl.BlockSpec((tm, D_MODEL), lambda i: (i, 0))
    return pl.pallas_call(
        _ffn_kernel,
        grid=(m // tm,),
        in_specs=[row, _const_spec(g_in.shape), _const_spec(w_gate.shape), _const_spec(w_up.shape),
                  _const_spec(w_down.shape), _const_spec(g_out.shape)],
        out_specs=row,
        out_shape=jax.ShapeDtypeStruct((m, D_MODEL), F32),
        compiler_params=_cparams(1),
        name="ffn",
    )(h, g_in, w_gate, w_up, w_down, g_out)


def _attn_prompt_kernel(q_ref, kc_ref, kp_ref, vc_ref, vp_ref, o_ref, lse_ref):
    blk = pl.program_id(2)
    rows = 2 * ATT_BLOCK
    qi = lax.broadcasted_iota(jnp.int32, (rows, 2 * ATT_BLOCK), 0) & (ATT_BLOCK - 1)
    kj = lax.broadcasted_iota(jnp.int32, (rows, 2 * ATT_BLOCK), 1)
    dist = ATT_BLOCK + qi - kj
    valid = (dist >= 0) & (dist <= ATT_SPAN) & ((kj >= ATT_BLOCK) | (blk > 0))
    lane = lax.broadcasted_iota(jnp.int32, (ATT_BLOCK, LANES), 1)
    first_head = lane < HEAD_DIM
    lse_all = jnp.zeros((ATT_BLOCK, LANES), F32)
    for pair in range(ATT_HEADS // 2):
        sl = slice(pair * LANES, (pair + 1) * LANES)
        q2 = q_ref[:, sl]
        zero = jnp.zeros_like(q2)
        qs = jnp.concatenate([jnp.where(first_head, q2, zero), jnp.where(first_head, zero, q2)], axis=0)
        k2 = jnp.concatenate([kp_ref[:, sl], kc_ref[:, sl]], axis=0)
        v2 = jnp.concatenate([vp_ref[:, sl], vc_ref[:, sl]], axis=0)
        s = lax.dot_general(qs, k2, (((1,), (1,)), ((), ())), preferred_element_type=F32)
        s = jnp.where(valid, s, -jnp.inf)
        m = jnp.max(s, axis=-1, keepdims=True)
        p = jnp.exp(s - m)
        den = jnp.sum(p, axis=-1, keepdims=True)
        o = jnp.dot(p.astype(BF16), v2, preferred_element_type=F32) / den
        lse = m + jnp.log(den)
        o_ref[:, sl] = jnp.where(first_head, o[:ATT_BLOCK], o[ATT_BLOCK:]).astype(o_ref.dtype)
        lse_all = jnp.where(lane == 2 * pair, lse[:ATT_BLOCK],
                            jnp.where(lane == 2 * pair + 1, lse[ATT_BLOCK:], lse_all))
    lse_ref[...] = lse_all


def _attn_prompt(q, k, v, dil):
    batch, _, sub, _ = q.shape
    nblk = sub // ATT_BLOCK
    cur = pl.BlockSpec((None, None, ATT_BLOCK, ATT_WIDTH), lambda b, r, i: (b, r, i, 0))
    prev = pl.BlockSpec((None, None, ATT_BLOCK, ATT_WIDTH), lambda b, r, i: (b, r, jnp.maximum(i - 1, 0), 0))
    return pl.pallas_call(
        _attn_prompt_kernel,
        grid=(batch, dil, nblk),
        in_specs=[cur, cur, prev, cur, prev],
        out_specs=[cur, pl.BlockSpec((None, None, ATT_BLOCK, LANES), lambda b, r, i: (b, r, i, 0))],
        out_shape=[jax.ShapeDtypeStruct((batch, dil, sub, ATT_WIDTH), BF16),
                   jax.ShapeDtypeStruct((batch, dil, sub, LANES), F32)],
        compiler_params=_cparams(3),
        name="attn_prompt",
    )(q, k, k, v, v)


def _pattern_multiplicity(dist):
    mult = jnp.zeros(dist.shape, F32)
    for window, dil in DILATED_PATTERNS:
        hit = (dist >= 0) & (dist <= window) & ((dist & (dil - 1)) == 0)
        mult = mult + hit.astype(F32)
    return mult


def _attn_sample_kernel(q_ref, knew_ref, vnew_ref, kt_ref, vt_ref, o_ref, *, steps):
    t_cache = lax.broadcasted_iota(jnp.int32, (SUBLANES, MAX_WINDOW), 0)
    pos = lax.broadcasted_iota(jnp.int32, (SUBLANES, MAX_WINDOW), 1)
    mult = _pattern_multiplicity(MAX_WINDOW + t_cache - pos)
    t_new = lax.broadcasted_iota(jnp.int32, (SUBLANES, LANES), 0)
    j_new = lax.broadcasted_iota(jnp.int32, (SUBLANES, LANES), 1)
    mult_new = jnp.where(j_new < steps, _pattern_multiplicity(t_new - j_new), 0.0)
    nt = (((1,), (1,)), ((), ()))
    for head in range(ATT_HEADS):
        q = q_ref[0, head].astype(BF16)
        s = jnp.dot(q, kt_ref[0, head].astype(BF16), preferred_element_type=F32)
        s_new = lax.dot_general(q, knew_ref[0, head].astype(BF16), nt, preferred_element_type=F32)
        s = jnp.where(mult > 0.0, s, -jnp.inf)
        s_new = jnp.where(mult_new > 0.0, s_new, -jnp.inf)
        m = jnp.maximum(jnp.max(s, axis=-1, keepdims=True), jnp.max(s_new, axis=-1, keepdims=True))
        p = mult * jnp.exp(s - m)
        p_new = mult_new * jnp.exp(s_new - m)
        den = jnp.sum(p, axis=-1, keepdims=True) + jnp.sum(p_new, axis=-1, keepdims=True)
        o = lax.dot_general(p.astype(BF16), vt_ref[0, head].astype(BF16), nt, preferred_element_type=F32)
        o = o + jnp.dot(p_new.astype(BF16), vnew_ref[0, head].astype(BF16), preferred_element_type=F32)
        o_ref[0, :, head * HEAD_DIM:(head + 1) * HEAD_DIM] = (o / den)[:steps]


def _attn_sample(q, k_new, v_new, cache_kt, cache_vt, steps):
    n = q.shape[0]
    assert steps <= SUBLANES
    per = lambda a: pl.BlockSpec((1,) + a.shape[1:], lambda b: (b, 0, 0, 0))
    args = (q, k_new, v_new, cache_kt, cache_vt)
    return pl.pallas_call(
        functools.partial(_attn_sample_kernel, steps=steps),
        grid=(n,),
        in_specs=[per(a) for a in args],
        out_specs=pl.BlockSpec((1, steps, ATT_WIDTH), lambda b: (b, 0, 0)),
        out_shape=jax.ShapeDtypeStruct((n, steps, ATT_WIDTH), F32),
        compiler_params=_cparams(1),
        name="attn_sample",
    )(*args)


def _attn_out_kernel(*refs, tm):
    n_pat = len(ATT_DILATIONS)
    o_refs = refs[:n_pat]
    lse_refs = refs[n_pat:2 * n_pat]
    w_ref, expand_ref, h_ref, g_ref, out_ref, o_nat, lse_nat = refs[2 * n_pat:]
    for pat, dil in enumerate(ATT_DILATIONS):
        rows = tm // dil
        for res in range(dil):
            dst = pl.ds(res, rows, stride=dil) if dil > 1 else pl.ds(0, rows)
            lse_nat[pat, dst, :] = lse_refs[pat][0, res]
            for c in range(COL_TILES):
                o_nat[pat * COL_TILES + c, dst, :] = o_refs[pat][0, res, :, c * LANES:(c + 1) * LANES].astype(F32)
    lses = [lse_nat[pat] for pat in range(n_pat)]
    top = functools.reduce(jnp.maximum, lses)
    es = [jnp.exp(l - top) for l in lses]
    total = functools.reduce(lambda x, y: x + y, es)
    spread = []
    for e in es:
        wgt = e / total
        hi = wgt.astype(BF16)
        lo = (wgt - hi.astype(F32)).astype(BF16)
        spread.append(jnp.dot(hi, expand_ref[...], preferred_element_type=F32)
                      + jnp.dot(lo, expand_ref[...], preferred_element_type=F32))
    pieces = []
    for c in range(COL_TILES):
        acc = None
        for pat in range(n_pat):
            term = spread[pat][:, c * LANES:(c + 1) * LANES] * o_nat[pat * COL_TILES + c]
            acc = term if acc is None else acc + term
        pieces.append(acc.astype(BF16))
    a = jnp.concatenate(pieces, axis=1)
    y = jnp.dot(a, w_ref[...], preferred_element_type=F32)
    out_ref[...] = h_ref[...] + _rms(y, g_ref[...])


def _attn_out(outs, lses, w_o, h, g, seq, tm):
    m = h.shape[0]
    tiles = seq // tm
    n_pat = len(ATT_DILATIONS)

    def dil_spec(dil, width):
        return pl.BlockSpec((1, dil, tm // dil, width), lambda i: (i // tiles, 0, i % tiles, 0))

    row = pl.BlockSpec((tm, D_MODEL), lambda i: (i, 0))
    expand = (jnp.arange(LANES)[:, None] == jnp.arange(ATT_WIDTH)[None, :] // HEAD_DIM).astype(BF16)
    return pl.pallas_call(
        functools.partial(_attn_out_kernel, tm=tm),
        grid=(m // tm,),
        in_specs=[dil_spec(dil, ATT_WIDTH) for dil in ATT_DILATIONS]
        + [dil_spec(dil, LANES) for dil in ATT_DILATIONS]
        + [_const_spec(w_o.shape), _const_spec(expand.shape), row, _const_spec(g.shape)],
        out_specs=row,
        out_shape=jax.ShapeDtypeStruct((m, D_MODEL), F32),
        scratch_shapes=[pltpu.VMEM((n_pat * COL_TILES, tm, LANES), F32), pltpu.VMEM((n_pat, tm, LANES), F32)],
        compiler_params=_cparams(1),
        name="attn_out",
    )(*outs, *lses, w_o, expand, h, g)


def _proj_out_kernel(a_ref, w_ref, h_ref, g_ref, out_ref):
    y = jnp.dot(a_ref[...].astype(BF16), w_ref[...], preferred_element_type=F32)
    out_ref[...] = h_ref[...] + _rms(y, g_ref[...])


def _proj_out(a, w, h, g, tm):
    m = h.shape[0]
    return pl.pallas_call(
        _proj_out_kernel,
        grid=(m // tm,),
        in_specs=[pl.BlockSpec((tm, a.shape[1]), lambda i: (i, 0)), _const_spec(w.shape),
                  pl.BlockSpec((tm, D_MODEL), lambda i: (i, 0)), _const_spec(g.shape)],
        out_specs=pl.BlockSpec((tm, D_MODEL), lambda i: (i, 0)),
        out_shape=jax.ShapeDtypeStruct((m, D_MODEL), F32),
        compiler_params=_cparams(1),
        name="proj_out",
    )(a, w, h, g)


def kernel(x_prompt, x_sample, state_pool, state_s5, cache_k, cache_v, norm_gains, ab_w_in, ab_pool_w, ab_pool_scale, ab_lambda_re, ab_lambda_im, ab_log_dt, ab_b_re, ab_b_im, ab_c_re, ab_c_im, ab_d, ab_w_glu, ab_b_glu, ab_w_out, c_w_qkv, c_w_o, ffn_w_gate, ffn_w_up, ffn_w_down):
    batch, seq, d = x_prompt.shape
    n_dec, t_dec, _ = x_sample.shape
    mp, ms = batch * seq, n_dec * t_dec
    tm_p = 256
    tm_ffn = 512
    gains = norm_gains.reshape(norm_gains.shape[0], 4, 1, d)

    hp = x_prompt.reshape(mp, d)
    hs = x_sample.reshape(ms, d)

    g = gains[0]
    w_in = ab_w_in[0].astype(BF16)
    (proj_p,) = _norm_matmul(hp, g[0], w_in, ((0, d, 1.0),), (F32,), tm_ffn)
    (proj_s,) = _norm_matmul(hs, g[0], w_in, ((0, d, 1.0),), (F32,), ms)
    proj_p = proj_p.reshape(batch, seq, d)
    proj_s = proj_s.reshape(n_dec, t_dec, d)

    pool_w = ab_pool_w[0].astype(BF16)
    pool_scale = ab_pool_scale[0].reshape(1, POOL_WIDTH)
    lead_p = jnp.zeros((batch, POOL_LEAD, POOL_WIDTH), F32)
    lead_s = jnp.pad(state_pool[0], ((0, 0), (POOL_LEAD - POOL_STATE, 0), (0, 0)))
    ypool_p = _pool_mixer(proj_p, lead_p, pool_w, pool_scale, tm_ffn, 0)
    ypool_s = _pool_mixer(proj_s, lead_s, pool_w, pool_scale, t_dec, PAST_LEN)
    pool_prompt = proj_p[:, seq - POOL_STATE:, :POOL_WIDTH][None]
    pool_sample = jnp.concatenate([state_pool[0], proj_s[:, :, :POOL_WIDTH]], axis=1)[:, -POOL_STATE:][None]

    prompt_mats, sample_mats, lam_rows = _s5_matrices(
        ab_lambda_re[0], ab_lambda_im[0], ab_log_dt[0], ab_b_re[0], ab_b_im[0], ab_c_re[0], ab_c_im[0], ab_d[0])
    yssm_p, hre_p, him_p = _s5_prompt(proj_p, prompt_mats, 2048)
    u_tb = jnp.swapaxes(proj_s[:, :, POOL_WIDTH:], 0, 1)
    h0 = state_s5[0].reshape(n_dec, SSM_STATES, 2)
    yssm_tb, hre_s, him_s = _s5_sample(u_tb, h0[..., 0], h0[..., 1], sample_mats, lam_rows)
    yssm_s = jnp.swapaxes(yssm_tb, 0, 1)
    s5_prompt = jnp.stack([hre_p[:, 0], him_p[:, 0]], axis=-1).reshape(1, batch, SSM_GROUPS, SSM_STATE, 2)
    s5_sample = jnp.stack([hre_s, him_s], axis=-1).reshape(1, n_dec, SSM_GROUPS, SSM_STATE, 2)

    w_glu = ab_w_glu[0].astype(BF16)
    b_glu = ab_b_glu[0].reshape(1, SSM_WIDTH)
    w_out = ab_w_out[0].astype(BF16)
    hp = _ab_out(ypool_p.reshape(mp, POOL_WIDTH), yssm_p.reshape(mp, SSM_WIDTH), w_glu, b_glu, w_out, hp, g[1], tm_ffn)
    hs = _ab_out(ypool_s.reshape(ms, POOL_WIDTH), yssm_s.reshape(ms, SSM_WIDTH), w_glu, b_glu, w_out, hs, g[1], ms)

    wg, wu, wd = ffn_w_gate[0].astype(BF16), ffn_w_up[0].astype(BF16), ffn_w_down[0].astype(BF16)
    hp = _ffn(hp, g[2], wg, wu, wd, g[3], tm_ffn)
    hs = _ffn(hs, g[2], wg, wu, wd, g[3], ms)

    g = gains[1]
    w_qkv = c_w_qkv[0].astype(BF16)
    w_o = c_w_o[0].astype(BF16)
    keep = min(MAX_WINDOW, seq)
    *qkv_p, kv_tail = _qkv_prompt(hp, g[0], w_qkv, batch, seq, keep, tm_p)
    k_prompt = kv_tail[:, :, :ATT_WIDTH].reshape(1, batch, keep, ATT_HEADS, HEAD_DIM)
    v_prompt = kv_tail[:, :, ATT_WIDTH:].reshape(1, batch, keep, ATT_HEADS, HEAD_DIM)
    n_dil = len(ATT_DILATIONS)
    outs, lses = [], []
    for pat, dil in enumerate(ATT_DILATIONS):
        o, lse = _attn_prompt(qkv_p[pat], qkv_p[n_dil + pat], qkv_p[2 * n_dil + pat], dil)
        outs.append(o)
        lses.append(lse)
    hp = _attn_out(outs, lses, w_o, hp, g[1], seq, tm_p)

    scale = HEAD_DIM ** -0.5
    q_s, k_s, v_s = _norm_matmul(
        hs, g[0], w_qkv,
        ((0, ATT_WIDTH, scale), (ATT_WIDTH, ATT_WIDTH, 1.0), (2 * ATT_WIDTH, ATT_WIDTH, 1.0)),
        (F32, F32, F32), ms)
    k_sample = k_s.reshape(1, n_dec, t_dec, ATT_HEADS, HEAD_DIM)
    v_sample = v_s.reshape(1, n_dec, t_dec, ATT_HEADS, HEAD_DIM)
    head_major = lambda a, rows: jnp.pad(
        jnp.swapaxes(a.reshape(n_dec, t_dec, ATT_HEADS, HEAD_DIM), 1, 2),
        ((0, 0), (0, 0), (0, rows - t_dec), (0, 0)))
    cache_kt = jnp.transpose(cache_k[0], (0, 2, 3, 1))
    cache_vt = jnp.transpose(cache_v[0], (0, 2, 3, 1))
    o_s = _attn_sample(head_major(q_s, SUBLANES), head_major(k_s, LANES), head_major(v_s, LANES),
                       cache_kt, cache_vt, t_dec)
    hs = _proj_out(o_s.reshape(ms, ATT_WIDTH), w_o, hs, g[1], ms)

    wg, wu, wd = ffn_w_gate[1].astype(BF16), ffn_w_up[1].astype(BF16), ffn_w_down[1].astype(BF16)
    hp = _ffn(hp, g[2], wg, wu, wd, g[3], tm_ffn)
    hs = _ffn(hs, g[2], wg, wu, wd, g[3], ms)

    return (hp.reshape(batch, seq, d), hs.reshape(n_dec, t_dec, d), pool_prompt, s5_prompt,
            k_prompt, v_prompt, pool_sample, s5_sample, k_sample, v_sample)
```

```python
import functools
import math

import jax
import jax.numpy as jnp
from jax import lax
from jax.experimental import pallas as pl
from jax.experimental.pallas import tpu as pltpu

F32 = jnp.float32
BF16 = jnp.bfloat16

D_MODEL = 1024
PAST_LEN = 16384
POOL_WIDTH = 512
POOL_WINDOWS = (2, 4, 8, 16)
POOL_GROUP = 128
POOL_STATE = 15
POOL_LEAD = 16
SSM_WIDTH = 512
SSM_GROUP = 16
SSM_GROUPS = 32
SSM_STATE = 64
SSM_STATES = SSM_GROUPS * SSM_STATE
ATT_HEADS = 16
HEAD_DIM = 64
ATT_WIDTH = ATT_HEADS * HEAD_DIM
DILATED_PATTERNS = ((128, 1), (512, 4), (2048, 16))
MAX_WINDOW = 2048
ATT_BLOCK = 128
ATT_SPAN = 128
FFN_HIDDEN = 2816
RMS_EPS = 1e-6

SUBLANES = 8
LANES = 128
VMEM_LIMIT = 56 * 1024 * 1024


def _cparams(n_axes):
    return pltpu.CompilerParams(
        dimension_semantics=("arbitrary",) * n_axes, vmem_limit_bytes=VMEM_LIMIT)


def _rms(x, g):
    ms = jnp.mean(x * x, axis=-1, keepdims=True)
    return (x * lax.rsqrt(ms + RMS_EPS)) * g


def _const_spec(shape):
    zeros = (0,) * len(shape)
    return pl.BlockSpec(shape, lambda *_: zeros, pipeline_mode=pl.Buffered(1))


def _norm_matmul_kernel(x_ref, g_ref, w_ref, *out_refs, splits):
    xb = _rms(x_ref[...], g_ref[...]).astype(BF16)
    for (c0, width, scale), o_ref in zip(splits, out_refs):
        y = jnp.dot(xb, w_ref[:, c0:c0 + width], preferred_element_type=F32)
        if scale != 1.0:
            y = y * scale
        o_ref[...] = y.astype(o_ref.dtype)


def _norm_matmul(x, g, w, splits, dtypes, tm):
    m, d = x.shape
    n = w.shape[1]
    return pl.pallas_call(
        functools.partial(_norm_matmul_kernel, splits=splits),
        grid=(m // tm,),
        in_specs=[pl.BlockSpec((tm, d), lambda i: (i, 0)), _const_spec((1, d)), _const_spec((d, n))],
        out_specs=[pl.BlockSpec((tm, width), lambda i: (i, 0)) for _, width, _ in splits],
        out_shape=[jax.ShapeDtypeStruct((m, width), dt) for (_, width, _), dt in zip(splits, dtypes)],
        compiler_params=_cparams(1),
        name="norm_matmul",
    )(x, g, w)


ATT_DILATIONS = tuple(dil for _, dil in DILATED_PATTERNS)
COL_TILES = ATT_WIDTH // LANES


def _qkv_prompt_kernel(x_ref, g_ref, w_ref, perm_ref, *refs, q_scale, tm):
    n_dil = len(ATT_DILATIONS)
    out_refs = refs[:3 * n_dil]
    tail_ref = refs[3 * n_dil]
    xb = _rms(x_ref[...], g_ref[...]).astype(BF16)
    for which in range(3):
        y = jnp.dot(xb, w_ref[:, which * ATT_WIDTH:(which + 1) * ATT_WIDTH], preferred_element_type=F32)
        if which == 0:
            y = y * q_scale
        else:
            tail_ref[0, :, (which - 1) * ATT_WIDTH:which * ATT_WIDTH] = y
        yb = y.astype(BF16)
        for pat, (dil, o_ref) in enumerate(zip(ATT_DILATIONS, out_refs[which * n_dil:(which + 1) * n_dil])):
            if dil == 1:
                o_ref[0, 0] = yb
                continue
            moved = jnp.dot(perm_ref[pat - 1], yb, preferred_element_type=F32).astype(BF16)
            rows = tm // dil
            for res in range(dil):
                o_ref[0, res] = moved[res * rows:(res + 1) * rows]


def _qkv_prompt(x, g, w, batch, seq, keep, tm):
    m, d = x.shape
    tiles = seq // tm
    first_tail = tiles - keep // tm

    def tail_map(i):
        return (i // tiles, jnp.maximum(i % tiles - first_tail, 0), 0)

    def dil_spec(dil):
        return pl.BlockSpec((1, dil, tm // dil, ATT_WIDTH), lambda i: (i // tiles, 0, i % tiles, 0))

    def dil_shape(dil):
        return jax.ShapeDtypeStruct((batch, dil, seq // dil, ATT_WIDTH), BF16)

    assert ATT_DILATIONS[0] == 1
    src = jnp.arange(tm)[None, :]
    dst = jnp.arange(tm)[:, None]
    perm = jnp.stack([(src == (dst % (tm // dil)) * dil + dst // (tm // dil)).astype(BF16)
                      for dil in ATT_DILATIONS[1:]])
    return pl.pallas_call(
        functools.partial(_qkv_prompt_kernel, q_scale=HEAD_DIM ** -0.5, tm=tm),
        grid=(m // tm,),
        in_specs=[pl.BlockSpec((tm, d), lambda i: (i, 0)), _const_spec((1, d)),
                  _const_spec((d, 3 * ATT_WIDTH)), _const_spec(perm.shape)],
        out_specs=[dil_spec(dil) for _ in range(3) for dil in ATT_DILATIONS]
        + [pl.BlockSpec((1, tm, 2 * ATT_WIDTH), tail_map)],
        out_shape=[dil_shape(dil) for _ in range(3) for dil in ATT_DILATIONS]
        + [jax.ShapeDtypeStruct((batch, keep, 2 * ATT_WIDTH), F32)],
        compiler_params=_cparams(1),
        name="qkv_prompt",
    )(x, g, w, perm)


def _pool_tile(u, j, lead_ref, w_ref, scale_ref, y_ref, ext, *, tt, pos0, carry):
    @pl.when(j == 0)
    def _():
        ext[0:POOL_LEAD, :] = lead_ref[0]

    ext[POOL_LEAD:POOL_LEAD + tt, :] = u
    pos = pos0 + j * tt + lax.broadcasted_iota(jnp.int32, (tt, POOL_GROUP), 0)
    for grp, window in enumerate(POOL_WINDOWS):
        sl = slice(grp * POOL_GROUP, (grp + 1) * POOL_GROUP)
        u_g = u[:, sl]
        win_sum = u_g
        for k in range(1, window):
            win_sum = win_sum + ext[POOL_LEAD - k:POOL_LEAD - k + tt, sl]
        count = jnp.minimum(pos + 1, window).astype(F32)
        diff = win_sum / count - u_g
        y = jnp.dot(diff.astype(BF16), w_ref[grp], preferred_element_type=F32)
        y_ref[0, :, sl] = (y * scale_ref[:, sl]).astype(y_ref.dtype)
    if carry:
        ext[0:POOL_LEAD, :] = ext[tt:tt + POOL_LEAD, :]


def _pool_kernel(u_ref, lead_ref, w_ref, scale_ref, y_ref, ext, *, tt, pos0, carry):
    _pool_tile(u_ref[0], pl.program_id(1), lead_ref, w_ref, scale_ref, y_ref, ext, tt=tt, pos0=pos0, carry=carry)


def _in_proj_pool_kernel(x_ref, g_ref, w_ref, lead_ref, pw_ref, scale_ref, ussm_ref, ypool_ref, tail_ref, ext,
                         *, tt, pos0):
    xb = _rms(x_ref[0], g_ref[...]).astype(BF16)
    ussm_ref[0] = jnp.dot(xb, w_ref[:, POOL_WIDTH:], preferred_element_type=F32)
    u_pool = jnp.dot(xb, w_ref[:, 0:POOL_WIDTH], preferred_element_type=F32)
    _pool_tile(u_pool, pl.program_id(1), lead_ref, pw_ref, scale_ref, ypool_ref, ext, tt=tt, pos0=pos0, carry=True)
    tail_ref[0] = ext[0:POOL_LEAD, :]


def _in_proj_pool(x, g, w, lead, pool_w, pool_scale, tt, pos0):
    n, t, d = x.shape
    seq_block = lambda width: pl.BlockSpec((1, tt, width), lambda b, j: (b, j, 0))
    per_seq = pl.BlockSpec((1, POOL_LEAD, POOL_WIDTH), lambda b, j: (b, 0, 0))
    return pl.pallas_call(
        functools.partial(_in_proj_pool_kernel, tt=tt, pos0=pos0),
        grid=(n, t // tt),
        in_specs=[seq_block(d), _const_spec(g.shape), _const_spec(w.shape), per_seq,
                  _const_spec(pool_w.shape), _const_spec(pool_scale.shape)],
        out_specs=[seq_block(SSM_WIDTH), seq_block(POOL_WIDTH), per_seq],
        out_shape=[jax.ShapeDtypeStruct((n, t, SSM_WIDTH), F32), jax.ShapeDtypeStruct((n, t, POOL_WIDTH), BF16),
                   jax.ShapeDtypeStruct((n, POOL_LEAD, POOL_WIDTH), F32)],
        scratch_shapes=[pltpu.VMEM((POOL_LEAD + tt, POOL_WIDTH), F32)],
        compiler_params=_cparams(2),
        name="in_proj_pool",
    )(x, g, w, lead, pool_w, pool_scale)


def _pool_mixer(proj, lead, w, scale, tt, pos0):
    n, t, _ = proj.shape
    steps = t // tt
    return pl.pallas_call(
        functools.partial(_pool_kernel, tt=tt, pos0=pos0, carry=steps > 1),
        grid=(n, steps),
        in_specs=[pl.BlockSpec((1, tt, POOL_WIDTH), lambda b, j: (b, j, 0)),
                  pl.BlockSpec((1, POOL_LEAD, POOL_WIDTH), lambda b, j: (b, 0, 0)),
                  _const_spec((len(POOL_WINDOWS), POOL_GROUP, POOL_GROUP)),
                  _const_spec((1, POOL_WIDTH))],
        out_specs=pl.BlockSpec((1, tt, POOL_WIDTH), lambda b, j: (b, j, 0)),
        out_shape=jax.ShapeDtypeStruct((n, t, POOL_WIDTH), BF16),
        scratch_shapes=[pltpu.VMEM((POOL_LEAD + tt, POOL_WIDTH), F32)],
        compiler_params=_cparams(2),
        name="pool_mixer",
    )(proj, lead, w, scale)


S5_IN_HALF = SSM_WIDTH // 2
S5_STATE_HALF = SSM_STATES // 2
S5_OUT_TILE = LANES
S5_STATE_TILE = SSM_STATES // (SSM_WIDTH // S5_OUT_TILE)
S5_SCAN_LANES = 512


def _s5_input(ub, bre_ref, bim_ref):
    re, im = [], []
    for half in range(2):
        uk = ub[:, half * S5_IN_HALF:(half + 1) * S5_IN_HALF]
        re.append(jnp.dot(uk, bre_ref[half], preferred_element_type=F32))
        im.append(jnp.dot(uk, bim_ref[half], preferred_element_type=F32))
    return re, im


def _s5_output(h_re, h_im, cre_ref, cim_ref, tile):
    return (jnp.dot(h_re, cre_ref[tile], preferred_element_type=F32)
            - jnp.dot(h_im, cim_ref[tile], preferred_element_type=F32))


S5_CHUNK = 16
S5_ROW = S5_CHUNK * SSM_GROUP
S5_SUB = 256
UNIT = SSM_GROUP
UNITS = LANES // UNIT
S5_PAIRS = SSM_GROUPS // 2


def _unit_transpose(pieces):
    unit = lax.broadcasted_iota(jnp.int32, pieces[0].shape, 1) >> (UNIT.bit_length() - 1)
    cur = list(pieces)
    k = UNITS // 2
    while k:
        high = (unit & k) != 0
        nxt = list(cur)
        for a in range(UNITS):
            if a & k:
                continue
            lo_piece, hi_piece = cur[a], cur[a | k]
            nxt[a] = jnp.where(high, pltpu.roll(hi_piece, k * UNIT, 1), lo_piece)
            nxt[a | k] = jnp.where(high, hi_piece, pltpu.roll(lo_piece, LANES - k * UNIT, 1))
        cur = nxt
        k //= 2
    return cur


def _s5_prompt_kernel(u_ref, perm_ref, toep_ref, wre_ref, wim_ref, vre_ref, vim_ref, pw_ref, d_ref,
                      y_ref, hre_ref, him_ref, u2, y2, s_re, s_im, c_re, c_im, y_nat, *, tt):
    j = pl.program_id(1)
    chunks = tt // S5_CHUNK
    sub_chunks = S5_SUB // S5_CHUNK

    @pl.when(j == 0)
    def _():
        c_re[...] = jnp.zeros_like(c_re)
        c_im[...] = jnp.zeros_like(c_im)

    def relayout_in(sub, carry):
        r0 = pl.multiple_of(sub * S5_SUB, S5_SUB)
        c0 = pl.multiple_of(sub * sub_chunks, sub_chunks)
        ub = u_ref[0, pl.ds(r0, S5_SUB), :].astype(BF16)
        xs = jnp.dot(perm_ref[...], ub, preferred_element_type=F32).astype(BF16)
        for tile in range(SSM_WIDTH // LANES):
            for half in range(S5_CHUNK // UNITS):
                pieces = [pltpu.bitcast(
                    xs[(half * UNITS + a) * sub_chunks:(half * UNITS + a + 1) * sub_chunks,
                       tile * LANES:(tile + 1) * LANES], jnp.uint32) for a in range(UNITS)]
                outs = _unit_transpose(pieces)
                for gl in range(UNITS):
                    u2[tile * UNITS + gl, pl.ds(c0, sub_chunks), half * LANES:(half + 1) * LANES] = (
                        pltpu.bitcast(outs[gl], BF16))
        return carry

    lax.fori_loop(0, tt // S5_SUB, relayout_in, 0, unroll=2)

    s_re[0:SUBLANES, :] = c_re[...]
    s_im[0:SUBLANES, :] = c_im[...]
    for pair in range(S5_PAIRS):
        sl = slice(pair * LANES, (pair + 1) * LANES)
        ua = u2[2 * pair]
        ub = u2[2 * pair + 1]
        s_re[SUBLANES:SUBLANES + chunks, sl] = (
            jnp.dot(ua, wre_ref[2 * pair], preferred_element_type=F32)
            + jnp.dot(ub, wre_ref[2 * pair + 1], preferred_element_type=F32))
        s_im[SUBLANES:SUBLANES + chunks, sl] = (
            jnp.dot(ua, wim_ref[2 * pair], preferred_element_type=F32)
            + jnp.dot(ub, wim_ref[2 * pair + 1], preferred_element_type=F32))

    for chunk in range(SSM_STATES // S5_SCAN_LANES):
        sl = slice(chunk * S5_SCAN_LANES, (chunk + 1) * S5_SCAN_LANES)

        def body(r, carry, sl=sl):
            in_re, in_im = carry
            row = pl.multiple_of((r + 1) * SUBLANES, SUBLANES)
            re = s_re[pl.ds(row, SUBLANES), sl]
            im = s_im[pl.ds(row, SUBLANES), sl]
            for level, shift in enumerate((1, 2, 4)):
                a_re = pw_ref[2 * level, :, sl]
                a_im = pw_ref[2 * level + 1, :, sl]
                sh_re = pltpu.roll(re, shift, 0)
                sh_im = pltpu.roll(im, shift, 0)
                re, im = (re + a_re * sh_re - a_im * sh_im,
                          im + a_re * sh_im + a_im * sh_re)
            p_re = pw_ref[6, :, sl]
            p_im = pw_ref[7, :, sl]
            re, im = (re + p_re * in_re - p_im * in_im,
                      im + p_re * in_im + p_im * in_re)
            s_re[pl.ds(row, SUBLANES), sl] = re
            s_im[pl.ds(row, SUBLANES), sl] = im
            last = SUBLANES - 1
            return (jnp.broadcast_to(re[last:last + 1, :], re.shape),
                    jnp.broadcast_to(im[last:last + 1, :], im.shape))

        out_re, out_im = lax.fori_loop(0, chunks // SUBLANES, body, (c_re[:, sl], c_im[:, sl]))
        c_re[:, sl] = out_re
        c_im[:, sl] = out_im

    hre_ref[0] = c_re[...]
    him_ref[0] = c_im[...]

    for pair in range(S5_PAIRS):
        sl = slice(pair * LANES, (pair + 1) * LANES)
        h_re = s_re[SUBLANES - 1:SUBLANES - 1 + chunks, sl].astype(BF16)
        h_im = s_im[SUBLANES - 1:SUBLANES - 1 + chunks, sl].astype(BF16)
        carried = (jnp.dot(h_re, vre_ref[pair], preferred_element_type=F32)
                   + jnp.dot(h_im, vim_ref[pair], preferred_element_type=F32))
        for k in range(2):
            grp = 2 * pair + k
            y2[grp] = (jnp.dot(u2[grp], toep_ref[grp], preferred_element_type=F32)
                       + carried[:, k * S5_ROW:(k + 1) * S5_ROW])

    def relayout_out(blk, carry):
        r0 = pl.multiple_of(blk * S5_SUB, S5_SUB)
        c0 = pl.multiple_of(blk * sub_chunks, sub_chunks)
        for tile in range(SSM_WIDTH // LANES):
            ot = slice(tile * LANES, (tile + 1) * LANES)
            for half in range(S5_CHUNK // UNITS):
                pieces = [y2[tile * UNITS + gl, pl.ds(c0, sub_chunks), half * LANES:(half + 1) * LANES]
                          for gl in range(UNITS)]
                outs = _unit_transpose(pieces)
                for a in range(UNITS):
                    y_nat[tile, pl.ds(half * UNITS + a, sub_chunks, stride=S5_CHUNK), :] = outs[a]
            y_ref[0, pl.ds(r0, S5_SUB), ot] = y_nat[tile] + d_ref[:, ot] * u_ref[0, pl.ds(r0, S5_SUB), ot]
        return carry

    lax.fori_loop(0, tt // S5_SUB, relayout_out, 0)


def _s5_prompt(u, mats, tt):
    b, t, _ = u.shape
    chunks = tt // S5_CHUNK
    state = jax.ShapeDtypeStruct((b, SUBLANES, SSM_STATES), F32)
    state_spec = pl.BlockSpec((1, SUBLANES, SSM_STATES), lambda i, j: (i, 0, 0))
    return pl.pallas_call(
        functools.partial(_s5_prompt_kernel, tt=tt),
        grid=(b, t // tt),
        in_specs=[pl.BlockSpec((1, tt, SSM_WIDTH), lambda i, j: (i, j, 0))]
        + [_const_spec(a.shape) for a in mats],
        out_specs=[pl.BlockSpec((1, tt, SSM_WIDTH), lambda i, j: (i, j, 0)), state_spec, state_spec],
        out_shape=[jax.ShapeDtypeStruct((b, t, SSM_WIDTH), F32), state, state],
        scratch_shapes=[pltpu.VMEM((SSM_GROUPS, chunks, S5_ROW), BF16),
                        pltpu.VMEM((SSM_GROUPS, chunks, S5_ROW), F32),
                        pltpu.VMEM((SUBLANES + chunks, SSM_STATES), F32),
                        pltpu.VMEM((SUBLANES + chunks, SSM_STATES), F32),
                        pltpu.VMEM((SUBLANES, SSM_STATES), F32), pltpu.VMEM((SUBLANES, SSM_STATES), F32),
                        pltpu.VMEM((SSM_WIDTH // LANES, S5_SUB, LANES), F32)],
        compiler_params=_cparams(2),
        name="s5_prompt",
    )(u, *mats)


def _s5_sample_kernel(u_ref, h0re_ref, h0im_ref, bre_ref, bim_ref, cre_ref, cim_ref, lam_ref, d_ref,
                      y_ref, hre_ref, him_ref, *, steps):
    h_re = h0re_ref[...]
    h_im = h0im_ref[...]
    lam_re = lam_ref[0:1, :]
    lam_im = lam_ref[1:2, :]
    for t in range(steps):
        u = u_ref[t]
        bu_re, bu_im = _s5_input(u.astype(BF16), bre_ref, bim_ref)
        bu_re = jnp.concatenate(bu_re, axis=1)
        bu_im = jnp.concatenate(bu_im, axis=1)
        h_re, h_im = (lam_re * h_re - lam_im * h_im + bu_re,
                      lam_re * h_im + lam_im * h_re + bu_im)
        hb_re = h_re.astype(BF16)
        hb_im = h_im.astype(BF16)
        for tile in range(SSM_WIDTH // S5_OUT_TILE):
            st = slice(tile * S5_STATE_TILE, (tile + 1) * S5_STATE_TILE)
            ot = slice(tile * S5_OUT_TILE, (tile + 1) * S5_OUT_TILE)
            y = _s5_output(hb_re[:, st], hb_im[:, st], cre_ref, cim_ref, tile)
            y_ref[t, :, ot] = y + d_ref[:, ot] * u[:, ot]
    hre_ref[...] = h_re
    him_ref[...] = h_im


def _s5_sample(u_tb, h0_re, h0_im, mats, lam):
    steps, n, _ = u_tb.shape
    bre, bim, cre, cim, d_skip = mats
    state = jax.ShapeDtypeStruct((n, SSM_STATES), F32)
    args = (u_tb, h0_re, h0_im, bre, bim, cre, cim, lam, d_skip)
    return pl.pallas_call(
        functools.partial(_s5_sample_kernel, steps=steps),
        grid=(1,),
        in_specs=[_const_spec(a.shape) for a in args],
        out_specs=[_const_spec(u_tb.shape), _const_spec((n, SSM_STATES)), _const_spec((n, SSM_STATES))],
        out_shape=[jax.ShapeDtypeStruct(u_tb.shape, F32), state, state],
        compiler_params=_cparams(1),
        name="s5_sample",
    )(*args)


def _s5_matrices(lam_re, lam_im, log_dt, b_re, b_im, c_re, c_im, d_skip):
    lam = lax.complex(lam_re, lam_im)
    dt = jnp.exp(log_dt)[:, None]
    lam_bar = jnp.exp(lam * dt)
    b_bar = ((lam_bar - 1.0) / lam)[..., None] * lax.complex(b_re, b_im)
    eye_half = jnp.eye(SSM_GROUPS // 2, dtype=F32)

    def in_blocks(x):
        x = x.reshape(2, SSM_GROUPS // 2, SSM_STATE, SSM_GROUP)
        return jnp.einsum("kgpi,gh->kgihp", x, eye_half).reshape(2, S5_IN_HALF, S5_STATE_HALF).astype(BF16)

    tiles = SSM_WIDTH // S5_OUT_TILE
    groups_per_tile = SSM_GROUPS // tiles
    eye_tile = jnp.eye(groups_per_tile, dtype=F32)

    def out_blocks(x):
        x = x.reshape(tiles, groups_per_tile, SSM_GROUP, SSM_STATE)
        return jnp.einsum("kgop,gh->kgpho", x, eye_tile).reshape(tiles, S5_STATE_TILE, S5_OUT_TILE).astype(BF16)

    lam_flat = lam_bar.reshape(1, SSM_STATES)
    lam_rows = jnp.concatenate([lam_flat.real, lam_flat.imag], axis=0).astype(F32)
    d_row = d_skip.reshape(1, SSM_WIDTH)
    sample_mats = (in_blocks(b_bar.real), in_blocks(b_bar.imag), out_blocks(c_re), out_blocks(c_im), d_row)

    c_mat = lax.complex(c_re, c_im)
    lam_pow = jnp.concatenate([jnp.ones((1,) + lam_bar.shape, lam_bar.dtype),
                               jnp.cumprod(jnp.broadcast_to(lam_bar, (S5_CHUNK,) + lam_bar.shape), axis=0)])
    taps = jnp.einsum("gop,kgp,gpi->gkoi", c_mat, lam_pow[:S5_CHUNK], b_bar,
                      precision=lax.Precision.HIGHEST).real
    src = jnp.arange(S5_CHUNK)[None, :, None]
    dst = jnp.arange(S5_CHUNK)[None, None, :]
    lag_is = (dst - src == jnp.arange(S5_CHUNK)[:, None, None]).astype(F32)
    toep = jnp.einsum("kst,gkoi->gsito", lag_is, taps, precision=lax.Precision.HIGHEST)
    toep = toep.reshape(SSM_GROUPS, S5_ROW, S5_ROW).astype(BF16)
    w_state = jnp.einsum("sgp,gpi->gsip", lam_pow[S5_CHUNK - 1::-1][:S5_CHUNK], b_bar)
    w_state = w_state.reshape(SSM_GROUPS, S5_ROW, SSM_STATE)
    second = (jnp.arange(SSM_GROUPS) % 2 == 1)[:, None, None]
    zeros = jnp.zeros_like(w_state.real)

    def pair_cols(x):
        return jnp.where(second, jnp.concatenate([zeros, x], -1), jnp.concatenate([x, zeros], -1)).astype(BF16)

    v_out = jnp.einsum("gop,tgp->gpto", c_mat, lam_pow[1:]).reshape(SSM_GROUPS, SSM_STATE, S5_ROW)
    eye_pair = jnp.eye(2, dtype=F32)

    def pair_blocks(x):
        x = x.reshape(S5_PAIRS, 2, SSM_STATE, S5_ROW)
        return jnp.einsum("kapc,ab->kapbc", x, eye_pair).reshape(S5_PAIRS, LANES, 2 * S5_ROW).astype(BF16)

    lam_chunk = lam_pow[S5_CHUNK].reshape(1, SSM_STATES)
    rows = jnp.arange(SUBLANES)[:, None]
    planes = []
    for shift in (1, 2, 4):
        a = jnp.where(rows >= shift, lam_chunk ** shift, 0.0)
        planes += [a.real, a.imag]
    carry = jnp.cumprod(jnp.broadcast_to(lam_chunk, (SUBLANES, SSM_STATES)), axis=0)
    planes += [carry.real, carry.imag]
    powers = jnp.stack(planes).astype(F32)
    pos = jnp.arange(S5_SUB)
    perm = (jnp.arange(S5_SUB)[None, :] == ((pos % (S5_SUB // S5_CHUNK)) * S5_CHUNK
                                            + pos // (S5_SUB // S5_CHUNK))[:, None]).astype(BF16)
    prompt_mats = (perm, toep, pair_cols(w_state.real), pair_cols(w_state.imag),
                   pair_blocks(v_out.real), pair_blocks(-v_out.imag), powers, d_row)
    return prompt_mats, sample_mats, lam_rows


def _ab_out_kernel(yp_ref, ys_ref, wglu_ref, bglu_ref, wout_ref, h_ref, g_ref, o_ref):
    z = jax.nn.gelu(ys_ref[...])
    gate = jnp.dot(z.astype(BF16), wglu_ref[...], preferred_element_type=F32) + bglu_ref[...]
    y_ssm = z * jax.nn.sigmoid(gate)
    y = (jnp.dot(yp_ref[...], wout_ref[0:POOL_WIDTH, :], preferred_element_type=F32)
         + jnp.dot(y_ssm.astype(BF16), wout_ref[POOL_WIDTH:, :], preferred_element_type=F32))
    o_ref[...] = h_ref[...] + _rms(y, g_ref[...])


def _ab_out(y_pool, y_ssm, w_glu, b_glu, w_out, h, g, tm):
    m = h.shape[0]
    row = lambda width: pl.BlockSpec((tm, width), lambda i: (i, 0))
    return pl.pallas_call(
        _ab_out_kernel,
        grid=(m // tm,),
        in_specs=[row(POOL_WIDTH), row(SSM_WIDTH), _const_spec(w_glu.shape), _const_spec(b_glu.shape),
                  _const_spec(w_out.shape), row(D_MODEL), _const_spec(g.shape)],
        out_specs=row(D_MODEL),
        out_shape=jax.ShapeDtypeStruct((m, D_MODEL), F32),
        compiler_params=_cparams(1),
        name="ab_out",
    )(y_pool, y_ssm, w_glu, b_glu, w_out, h, g)


def _ffn_kernel(h_ref, gin_ref, wg_ref, wu_ref, wd_ref, gout_ref, o_ref):
    h = h_ref[...]
    xb = _rms(h, gin_ref[...]).astype(BF16)
    gate = jnp.dot(xb, wg_ref[...], preferred_element_type=F32)
    up = jnp.dot(xb, wu_ref[...], preferred_element_type=F32)
    act = (gate * jax.nn.sigmoid(gate) * up).astype(BF16)
    y = jnp.dot(act, wd_ref[...], preferred_element_type=F32)
    o_ref[...] = h + _rms(y, gout_ref[...])


def _ffn(h, g_in, w_gate, w_up, w_down, g_out, tm):
    m = h.shape[0]
    row = pl.BlockSpec((tm, D_MODEL), lambda i: (i, 0))
    return pl.pallas_call(
        _ffn_kernel,
        grid=(m // tm,),
        in_specs=[row, _const_spec(g_in.shape), _const_spec(w_gate.shape), _const_spec(w_up.shape),
                  _const_spec(w_down.shape), _const_spec(g_out.shape)],
        out_specs=row,
        out_shape=jax.ShapeDtypeStruct((m, D_MODEL), F32),
        compiler_params=_cparams(1),
        name="ffn",
    )(h, g_in, w_gate, w_up, w_down, g_out)


def _attn_prompt_kernel(q_ref, kc_ref, kp_ref, vc_ref, vp_ref, o_ref, lse_ref):
    blk = pl.program_id(2)
    rows = 2 * ATT_BLOCK
    qi = lax.broadcasted_iota(jnp.int32, (rows, 2 * ATT_BLOCK), 0) & (ATT_BLOCK - 1)
    kj = lax.broadcasted_iota(jnp.int32, (rows, 2 * ATT_BLOCK), 1)
    dist = ATT_BLOCK + qi - kj
    valid = (dist >= 0) & (dist <= ATT_SPAN) & ((kj >= ATT_BLOCK) | (blk > 0))
    lane = lax.broadcasted_iota(jnp.int32, (ATT_BLOCK, LANES), 1)
    first_head = lane < HEAD_DIM
    lse_all = jnp.zeros((ATT_BLOCK, LANES), F32)
    for pair in range(ATT_HEADS // 2):
        sl = slice(pair * LANES, (pair + 1) * LANES)
        q2 = q_ref[:, sl]
        zero = jnp.zeros_like(q2)
        qs = jnp.concatenate([jnp.where(first_head, q2, zero), jnp.where(first_head, zero, q2)], axis=0)
        k2 = jnp.concatenate([kp_ref[:, sl], kc_ref[:, sl]], axis=0)
        v2 = jnp.concatenate([vp_ref[:, sl], vc_ref[:, sl]], axis=0)
        s = lax.dot_general(qs, k2, (((1,), (1,)), ((), ())), preferred_element_type=F32)
        s = jnp.where(valid, s, -jnp.inf)
        m = jnp.max(s, axis=-1, keepdims=True)
        p = jnp.exp(s - m)
        den = jnp.sum(p, axis=-1, keepdims=True)
        o = jnp.dot(p.astype(BF16), v2, preferred_element_type=F32) / den
        lse = m + jnp.log(den)
        o_ref[:, sl] = jnp.where(first_head, o[:ATT_BLOCK], o[ATT_BLOCK:]).astype(o_ref.dtype)
        lse_all = jnp.where(lane == 2 * pair, lse[:ATT_BLOCK],
                            jnp.where(lane == 2 * pair + 1, lse[ATT_BLOCK:], lse_all))
    lse_ref[...] = lse_all


def _attn_prompt(q, k, v, dil):
    batch, _, sub, _ = q.shape
    nblk = sub // ATT_BLOCK
    cur = pl.BlockSpec((None, None, ATT_BLOCK, ATT_WIDTH), lambda b, r, i: (b, r, i, 0))
    prev = pl.BlockSpec((None, None, ATT_BLOCK, ATT_WIDTH), lambda b, r, i: (b, r, jnp.maximum(i - 1, 0), 0))
    return pl.pallas_call(
        _attn_prompt_kernel,
        grid=(batch, dil, nblk),
        in_specs=[cur, cur, prev, cur, prev],
        out_specs=[cur, pl.BlockSpec((None, None, ATT_BLOCK, LANES), lambda b, r, i: (b, r, i, 0))],
        out_shape=[jax.ShapeDtypeStruct((batch, dil, sub, ATT_WIDTH), BF16),
                   jax.ShapeDtypeStruct((batch, dil, sub, LANES), F32)],
        compiler_params=_cparams(3),
        name="attn_prompt",
    )(q, k, k, v, v)


def _pattern_multiplicity(dist):
    mult = jnp.zeros(dist.shape, F32)
    for window, dil in DILATED_PATTERNS:
        hit = (dist >= 0) & (dist <= window) & ((dist & (dil - 1)) == 0)
        mult = mult + hit.astype(F32)
    return mult


def _attn_sample_kernel(q_ref, knew_ref, vnew_ref, kt_ref, vt_ref, o_ref, *, steps):
    t_cache = lax.broadcasted_iota(jnp.int32, (SUBLANES, MAX_WINDOW), 0)
    pos = lax.broadcasted_iota(jnp.int32, (SUBLANES, MAX_WINDOW), 1)
    mult = _pattern_multiplicity(MAX_WINDOW + t_cache - pos)
    t_new = lax.broadcasted_iota(jnp.int32, (SUBLANES, LANES), 0)
    j_new = lax.broadcasted_iota(jnp.int32, (SUBLANES, LANES), 1)
    mult_new = jnp.where(j_new < steps, _pattern_multiplicity(t_new - j_new), 0.0)
    nt = (((1,), (1,)), ((), ()))
    for head in range(ATT_HEADS):
        q = q_ref[0, head].astype(BF16)
        s = jnp.dot(q, kt_ref[0, head].astype(BF16), preferred_element_type=F32)
        s_new = lax.dot_general(q, knew_ref[0, head].astype(BF16), nt, preferred_element_type=F32)
        s = jnp.where(mult > 0.0, s, -jnp.inf)
        s_new = jnp.where(mult_new > 0.0, s_new, -jnp.inf)
        m = jnp.maximum(jnp.max(s, axis=-1, keepdims=True), jnp.max(s_new, axis=-1, keepdims=True))
        p = mult * jnp.exp(s - m)
        p_new = mult_new * jnp.exp(s_new - m)
        den = jnp.sum(p, axis=-1, keepdims=True) + jnp.sum(p_new, axis=-1, keepdims=True)
        o = lax.dot_general(p.astype(BF16), vt_ref[0, head].astype(BF16), nt, preferred_element_type=F32)
        o = o + jnp.dot(p_new.astype(BF16), vnew_ref[0, head].astype(BF16), preferred_element_type=F32)
        o_ref[0, :, head * HEAD_DIM:(head + 1) * HEAD_DIM] = (o / den)[:steps]


def _attn_sample(q, k_new, v_new, cache_kt, cache_vt, steps):
    n = q.shape[0]
    assert steps <= SUBLANES
    per = lambda a: pl.BlockSpec((1,) + a.shape[1:], lambda b: (b, 0, 0, 0))
    args = (q, k_new, v_new, cache_kt, cache_vt)
    return pl.pallas_call(
        functools.partial(_attn_sample_kernel, steps=steps),
        grid=(n,),
        in_specs=[per(a) for a in args],
        out_specs=pl.BlockSpec((1, steps, ATT_WIDTH), lambda b: (b, 0, 0)),
        out_shape=jax.ShapeDtypeStruct((n, steps, ATT_WIDTH), F32),
        compiler_params=_cparams(1),
        name="attn_sample",
    )(*args)


def _attn_out_kernel(*refs, tm):
    n_pat = len(ATT_DILATIONS)
    o_refs = refs[:n_pat]
    lse_refs = refs[n_pat:2 * n_pat]
    w_ref, expand_ref, h_ref, g_ref, out_ref, o_nat, lse_nat = refs[2 * n_pat:]
    for pat, dil in enumerate(ATT_DILATIONS):
        rows = tm // dil
        for res in range(dil):
            dst = pl.ds(res, rows, stride=dil) if dil > 1 else pl.ds(0, rows)
            lse_nat[pat, dst, :] = lse_refs[pat][0, res]
            for c in range(COL_TILES):
                o_nat[pat * COL_TILES + c, dst, :] = o_refs[pat][0, res, :, c * LANES:(c + 1) * LANES].astype(F32)
    lses = [lse_nat[pat] for pat in range(n_pat)]
    top = functools.reduce(jnp.maximum, lses)
    es = [jnp.exp(l - top) for l in lses]
    total = functools.reduce(lambda x, y: x + y, es)
    spread = []
    for e in es:
        wgt = e / total
        hi = wgt.astype(BF16)
        lo = (wgt - hi.astype(F32)).astype(BF16)
        spread.append(jnp.dot(hi, expand_ref[...], preferred_element_type=F32)
                      + jnp.dot(lo, expand_ref[...], preferred_element_type=F32))
    pieces = []
    for c in range(COL_TILES):
        acc = None
        for pat in range(n_pat):
            term = spread[pat][:, c * LANES:(c + 1) * LANES] * o_nat[pat * COL_TILES + c]
            acc = term if acc is None else acc + term
        pieces.append(acc.astype(BF16))
    a = jnp.concatenate(pieces, axis=1)
    y = jnp.dot(a, w_ref[...], preferred_element_type=F32)
    out_ref[...] = h_ref[...] + _rms(y, g_ref[...])


def _attn_out(outs, lses, w_o, h, g, seq, tm):
    m = h.shape[0]
    tiles = seq // tm
    n_pat = len(ATT_DILATIONS)

    def dil_spec(dil, width):
        return pl.BlockSpec((1, dil, tm // dil, width), lambda i: (i // tiles, 0, i % tiles, 0))

    row = pl.BlockSpec((tm, D_MODEL), lambda i: (i, 0))
    expand = (jnp.arange(LANES)[:, None] == jnp.arange(ATT_WIDTH)[None, :] // HEAD_DIM).astype(BF16)
    return pl.pallas_call(
        functools.partial(_attn_out_kernel, tm=tm),
        grid=(m // tm,),
        in_specs=[dil_spec(dil, ATT_WIDTH) for dil in ATT_DILATIONS]
        + [dil_spec(dil, LANES) for dil in ATT_DILATIONS]
        + [_const_spec(w_o.shape), _const_spec(expand.shape), row, _const_spec(g.shape)],
        out_specs=row,
        out_shape=jax.ShapeDtypeStruct((m, D_MODEL), F32),
        scratch_shapes=[pltpu.VMEM((n_pat * COL_TILES, tm, LANES), F32), pltpu.VMEM((n_pat, tm, LANES), F32)],
        compiler_params=_cparams(1),
        name="attn_out",
    )(*outs, *lses, w_o, expand, h, g)


def _proj_out_kernel(a_ref, w_ref, h_ref, g_ref, out_ref):
    y = jnp.dot(a_ref[...].astype(BF16), w_ref[...], preferred_element_type=F32)
    out_ref[...] = h_ref[...] + _rms(y, g_ref[...])


def _proj_out(a, w, h, g, tm):
    m = h.shape[0]
    return pl.pallas_call(
        _proj_out_kernel,
        grid=(m // tm,),
        in_specs=[pl.BlockSpec((tm, a.shape[1]), lambda i: (i, 0)), _const_spec(w.shape),
                  pl.BlockSpec((tm, D_MODEL), lambda i: (i, 0)), _const_spec(g.shape)],
        out_specs=pl.BlockSpec((tm, D_MODEL), lambda i: (i, 0)),
        out_shape=jax.ShapeDtypeStruct((m, D_MODEL), F32),
        compiler_params=_cparams(1),
        name="proj_out",
    )(a, w, h, g)


def kernel(x_prompt, x_sample, state_pool, state_s5, cache_k, cache_v, norm_gains, ab_w_in, ab_pool_w, ab_pool_scale, ab_lambda_re, ab_lambda_im, ab_log_dt, ab_b_re, ab_b_im, ab_c_re, ab_c_im, ab_d, ab_w_glu, ab_b_glu, ab_w_out, c_w_qkv, c_w_o, ffn_w_gate, ffn_w_up, ffn_w_down):
    batch, seq, d = x_prompt.shape
    n_dec, t_dec, _ = x_sample.shape
    mp, ms = batch * seq, n_dec * t_dec
    tm_p = 256
    tm_ffn = 512
    gains = norm_gains.reshape(norm_gains.shape[0], 4, 1, d)

    hp = x_prompt.reshape(mp, d)
    hs = x_sample.reshape(ms, d)

    g = gains[0]
    w_in = ab_w_in[0].astype(BF16)
    pool_w = ab_pool_w[0].astype(BF16)
    pool_scale = ab_pool_scale[0].reshape(1, POOL_WIDTH)
    lead_p = jnp.zeros((batch, POOL_LEAD, POOL_WIDTH), F32)
    lead_s = jnp.pad(state_pool[0], ((0, 0), (POOL_LEAD - POOL_STATE, 0), (0, 0)))
    ussm_p, ypool_p, pool_tail = _in_proj_pool(x_prompt, g[0], w_in, lead_p, pool_w, pool_scale, tm_ffn, 0)
    (proj_s,) = _norm_matmul(hs, g[0], w_in, ((0, d, 1.0),), (F32,), ms)
    proj_s = proj_s.reshape(n_dec, t_dec, d)
    ypool_s = _pool_mixer(proj_s, lead_s, pool_w, pool_scale, t_dec, PAST_LEN)
    pool_prompt = pool_tail[:, POOL_LEAD - POOL_STATE:][None]
    pool_sample = jnp.concatenate([state_pool[0], proj_s[:, :, :POOL_WIDTH]], axis=1)[:, -POOL_STATE:][None]

    prompt_mats, sample_mats, lam_rows = _s5_matrices(
        ab_lambda_re[0], ab_lambda_im[0], ab_log_dt[0], ab_b_re[0], ab_b_im[0], ab_c_re[0], ab_c_im[0], ab_d[0])
    yssm_p, hre_p, him_p = _s5_prompt(ussm_p, prompt_mats, 2048)
    u_tb = jnp.swapaxes(proj_s[:, :, POOL_WIDTH:], 0, 1)
    h0 = state_s5[0].reshape(n_dec, SSM_STATES, 2)
    yssm_tb, hre_s, him_s = _s5_sample(u_tb, h0[..., 0], h0[..., 1], sample_mats, lam_rows)
    yssm_s = jnp.swapaxes(yssm_tb, 0, 1)
    s5_prompt = jnp.stack([hre_p[:, 0], him_p[:, 0]], axis=-1).reshape(1, batch, SSM_GROUPS, SSM_STATE, 2)
    s5_sample = jnp.stack([hre_s, him_s], axis=-1).reshape(1, n_dec, SSM_GROUPS, SSM_STATE, 2)

    w_glu = ab_w_glu[0].astype(BF16)
    b_glu = ab_b_glu[0].reshape(1, SSM_WIDTH)
    w_out = ab_w_out[0].astype(BF16)
    hp = _ab_out(ypool_p.reshape(mp, POOL_WIDTH), yssm_p.reshape(mp, SSM_WIDTH), w_glu, b_glu, w_out, hp, g[1], tm_ffn)
    hs = _ab_out(ypool_s.reshape(ms, POOL_WIDTH), yssm_s.reshape(ms, SSM_WIDTH), w_glu, b_glu, w_out, hs, g[1], ms)

    wg, wu, wd = ffn_w_gate[0].astype(BF16), ffn_w_up[0].astype(BF16), ffn_w_down[0].astype(BF16)
    hp = _ffn(hp, g[2], wg, wu, wd, g[3], tm_ffn)
    hs = _ffn(hs, g[2], wg, wu, wd, g[3], ms)

    g = gains[1]
    w_qkv = c_w_qkv[0].astype(BF16)
    w_o = c_w_o[0].astype(BF16)
    keep = min(MAX_WINDOW, seq)
    *qkv_p, kv_tail = _qkv_prompt(hp, g[0], w_qkv, batch, seq, keep, tm_p)
    k_prompt = kv_tail[:, :, :ATT_WIDTH].reshape(1, batch, keep, ATT_HEADS, HEAD_DIM)
    v_prompt = kv_tail[:, :, ATT_WIDTH:].reshape(1, batch, keep, ATT_HEADS, HEAD_DIM)
    n_dil = len(ATT_DILATIONS)
    outs, lses = [], []
    for pat, dil in enumerate(ATT_DILATIONS):
        o, lse = _attn_prompt(qkv_p[pat], qkv_p[n_dil + pat], qkv_p[2 * n_dil + pat], dil)
        outs.append(o)
        lses.append(lse)
    hp = _attn_out(outs, lses, w_o, hp, g[1], seq, tm_p)

    scale = HEAD_DIM ** -0.5
    q_s, k_s, v_s = _norm_matmul(
        hs, g[0], w_qkv,
        ((0, ATT_WIDTH, scale), (ATT_WIDTH, ATT_WIDTH, 1.0), (2 * ATT_WIDTH, ATT_WIDTH, 1.0)),
        (F32, F32, F32), ms)
    k_sample = k_s.reshape(1, n_dec, t_dec, ATT_HEADS, HEAD_DIM)
    v_sample = v_s.reshape(1, n_dec, t_dec, ATT_HEADS, HEAD_DIM)
    head_major = lambda a, rows: jnp.pad(
        jnp.swapaxes(a.reshape(n_dec, t_dec, ATT_HEADS, HEAD_DIM), 1, 2),
        ((0, 0), (0, 0), (0, rows - t_dec), (0, 0)))
    cache_kt = jnp.transpose(cache_k[0], (0, 2, 3, 1))
    cache_vt = jnp.transpose(cache_v[0], (0, 2, 3, 1))
    o_s = _attn_sample(head_major(q_s, SUBLANES), head_major(k_s, LANES), head_major(v_s, LANES),
                       cache_kt, cache_vt, t_dec)
    hs = _proj_out(o_s.reshape(ms, ATT_WIDTH), w_o, hs, g[1], ms)

    wg, wu, wd = ffn_w_gate[1].astype(BF16), ffn_w_up[1].astype(BF16), ffn_w_down[1].astype(BF16)
    hp = _ffn(hp, g[2], wg, wu, wd, g[3], tm_ffn)
    hs = _ffn(hs, g[2], wg, wu, wd, g[3], ms)

    return (hp.reshape(batch, seq, d), hs.reshape(n_dec, t_dec, d), pool_prompt, s5_prompt,
            k_prompt, v_prompt, pool_sample, s5_sample, k_sample, v_sample)
```

```python
import functools
import math

import jax
import jax.numpy as jnp
from jax import lax
from jax.experimental import pallas as pl
from jax.experimental.pallas import tpu as pltpu

F32 = jnp.float32
BF16 = jnp.bfloat16

D_MODEL = 1024
PAST_LEN = 16384
POOL_WIDTH = 512
POOL_WINDOWS = (2, 4, 8, 16)
POOL_GROUP = 128
POOL_STATE = 15
POOL_LEAD = 16
SSM_WIDTH = 512
SSM_GROUP = 16
SSM_GROUPS = 32
SSM_STATE = 64
SSM_STATES = SSM_GROUPS * SSM_STATE
ATT_HEADS = 16
HEAD_DIM = 64
ATT_WIDTH = ATT_HEADS * HEAD_DIM
DILATED_PATTERNS = ((128, 1), (512, 4), (2048, 16))
MAX_WINDOW = 2048
ATT_BLOCK = 128
ATT_SPAN = 128
FFN_HIDDEN = 2816
RMS_EPS = 1e-6

SUBLANES = 8
LANES = 128
VMEM_LIMIT = 56 * 1024 * 1024


def _cparams(n_axes):
    return pltpu.CompilerParams(
        dimension_semantics=("arbitrary",) * n_axes, vmem_limit_bytes=VMEM_LIMIT)


def _rms(x, g):
    ms = jnp.mean(x * x, axis=-1, keepdims=True)
    return (x * lax.rsqrt(ms + RMS_EPS)) * g


def _const_spec(shape):
    zeros = (0,) * len(shape)
    return pl.BlockSpec(shape, lambda *_: zeros, pipeline_mode=pl.Buffered(1))


def _norm_matmul_kernel(x_ref, g_ref, w_ref, *out_refs, splits):
    xb = _rms(x_ref[...], g_ref[...]).astype(BF16)
    for (c0, width, scale), o_ref in zip(splits, out_refs):
        y = jnp.dot(xb, w_ref[:, c0:c0 + width], preferred_element_type=F32)
        if scale != 1.0:
            y = y * scale
        o_ref[...] = y.astype(o_ref.dtype)


def _norm_matmul(x, g, w, splits, dtypes, tm):
    m, d = x.shape
    n = w.shape[1]
    return pl.pallas_call(
        functools.partial(_norm_matmul_kernel, splits=splits),
        grid=(m // tm,),
        in_specs=[pl.BlockSpec((tm, d), lambda i: (i, 0)), _const_spec((1, d)), _const_spec((d, n))],
        out_specs=[pl.BlockSpec((tm, width), lambda i: (i, 0)) for _, width, _ in splits],
        out_shape=[jax.ShapeDtypeStruct((m, width), dt) for (_, width, _), dt in zip(splits, dtypes)],
        compiler_params=_cparams(1),
        name="norm_matmul",
    )(x, g, w)


ATT_DILATIONS = tuple(dil for _, dil in DILATED_PATTERNS)
COL_TILES = ATT_WIDTH // LANES


def _qkv_prompt_kernel(x_ref, g_ref, w_ref, perm_ref, *refs, q_scale, tm, tiles, first_tail):
    n_dil = len(ATT_DILATIONS)
    out_refs = refs[:3 * n_dil]
    tail_refs = refs[3 * n_dil:]
    in_tail = (pl.program_id(0) % tiles) >= first_tail
    xb = _rms(x_ref[...], g_ref[...]).astype(BF16)
    for which in range(3):
        y = jnp.dot(xb, w_ref[:, which * ATT_WIDTH:(which + 1) * ATT_WIDTH], preferred_element_type=F32)
        if which == 0:
            y = y * q_scale
        else:
            @pl.when(in_tail)
            def _(y=y, tail_ref=tail_refs[which - 1]):
                tail_ref[0] = y.T
        yb = y.astype(BF16)
        for pat, (dil, o_ref) in enumerate(zip(ATT_DILATIONS, out_refs[which * n_dil:(which + 1) * n_dil])):
            if dil == 1:
                o_ref[0, 0] = yb
                continue
            moved = jnp.dot(perm_ref[pat - 1], yb, preferred_element_type=F32).astype(BF16)
            rows = tm // dil
            for res in range(dil):
                o_ref[0, res] = moved[res * rows:(res + 1) * rows]


def _qkv_prompt(x, g, w, batch, seq, keep, tm):
    m, d = x.shape
    tiles = seq // tm
    first_tail = tiles - keep // tm

    def tail_map(i):
        return (i // tiles, 0, jnp.maximum(i % tiles - first_tail, 0))

    def dil_spec(dil):
        return pl.BlockSpec((1, dil, tm // dil, ATT_WIDTH), lambda i: (i // tiles, 0, i % tiles, 0))

    def dil_shape(dil):
        return jax.ShapeDtypeStruct((batch, dil, seq // dil, ATT_WIDTH), BF16)

    assert ATT_DILATIONS[0] == 1
    src = jnp.arange(tm)[None, :]
    dst = jnp.arange(tm)[:, None]
    perm = jnp.stack([(src == (dst % (tm // dil)) * dil + dst // (tm // dil)).astype(BF16)
                      for dil in ATT_DILATIONS[1:]])
    return pl.pallas_call(
        functools.partial(_qkv_prompt_kernel, q_scale=HEAD_DIM ** -0.5, tm=tm, tiles=tiles,
                          first_tail=first_tail),
        grid=(m // tm,),
        in_specs=[pl.BlockSpec((tm, d), lambda i: (i, 0)), _const_spec((1, d)),
                  _const_spec((d, 3 * ATT_WIDTH)), _const_spec(perm.shape)],
        out_specs=[dil_spec(dil) for _ in range(3) for dil in ATT_DILATIONS]
        + [pl.BlockSpec((1, ATT_WIDTH, tm), tail_map)] * 2,
        out_shape=[dil_shape(dil) for _ in range(3) for dil in ATT_DILATIONS]
        + [jax.ShapeDtypeStruct((batch, ATT_WIDTH, keep), F32)] * 2,
        compiler_params=_cparams(1),
        name="qkv_prompt",
    )(x, g, w, perm)


def _pool_tile(u, j, lead_ref, w_ref, scale_ref, y_ref, ext, *, tt, pos0, carry, seq=0):
    @pl.when(j == 0)
    def _():
        ext[0:POOL_LEAD, :] = lead_ref[seq]

    ext[POOL_LEAD:POOL_LEAD + tt, :] = u
    pos = pos0 + j * tt + lax.broadcasted_iota(jnp.int32, (tt, POOL_GROUP), 0)
    for grp, window in enumerate(POOL_WINDOWS):
        sl = slice(grp * POOL_GROUP, (grp + 1) * POOL_GROUP)
        u_g = u[:, sl]
        win_sum = u_g
        for k in range(1, window):
            win_sum = win_sum + ext[POOL_LEAD - k:POOL_LEAD - k + tt, sl]
        count = jnp.minimum(pos + 1, window).astype(F32)
        diff = win_sum / count - u_g
        y = jnp.dot(diff.astype(BF16), w_ref[grp], preferred_element_type=F32)
        y_ref[seq, :, sl] = (y * scale_ref[:, sl]).astype(y_ref.dtype)
    if carry:
        ext[0:POOL_LEAD, :] = ext[tt:tt + POOL_LEAD, :]


def _pool_kernel(u_ref, lead_ref, w_ref, scale_ref, y_ref, ext, *, tt, pos0, carry, n_seq):
    for seq in range(n_seq):
        _pool_tile(u_ref[seq], pl.program_id(1), lead_ref, w_ref, scale_ref, y_ref, ext,
                   tt=tt, pos0=pos0, carry=carry, seq=seq)


def _in_proj_pool_kernel(x_ref, g_ref, w_ref, lead_ref, pw_ref, scale_ref, ussm_ref, ypool_ref, tail_ref, ext,
                         *, tt, pos0):
    xb = _rms(x_ref[0], g_ref[...]).astype(BF16)
    ussm_ref[0] = jnp.dot(xb, w_ref[:, POOL_WIDTH:], preferred_element_type=F32)
    u_pool = jnp.dot(xb, w_ref[:, 0:POOL_WIDTH], preferred_element_type=F32)
    _pool_tile(u_pool, pl.program_id(1), lead_ref, pw_ref, scale_ref, ypool_ref, ext, tt=tt, pos0=pos0, carry=True)
    tail_ref[0] = ext[0:POOL_LEAD, :]


def _in_proj_pool(x, g, w, lead, pool_w, pool_scale, tt, pos0):
    n, t, d = x.shape
    seq_block = lambda width: pl.BlockSpec((1, tt, width), lambda b, j: (b, j, 0))
    per_seq = pl.BlockSpec((1, POOL_LEAD, POOL_WIDTH), lambda b, j: (b, 0, 0))
    return pl.pallas_call(
        functools.partial(_in_proj_pool_kernel, tt=tt, pos0=pos0),
        grid=(n, t // tt),
        in_specs=[seq_block(d), _const_spec(g.shape), _const_spec(w.shape), per_seq,
                  _const_spec(pool_w.shape), _const_spec(pool_scale.shape)],
        out_specs=[seq_block(SSM_WIDTH), seq_block(POOL_WIDTH), per_seq],
        out_shape=[jax.ShapeDtypeStruct((n, t, SSM_WIDTH), F32), jax.ShapeDtypeStruct((n, t, POOL_WIDTH), BF16),
                   jax.ShapeDtypeStruct((n, POOL_LEAD, POOL_WIDTH), F32)],
        scratch_shapes=[pltpu.VMEM((POOL_LEAD + tt, POOL_WIDTH), F32)],
        compiler_params=_cparams(2),
        name="in_proj_pool",
    )(x, g, w, lead, pool_w, pool_scale)


def _pool_mixer(proj, lead, w, scale, tt, pos0):
    n, t, _ = proj.shape
    steps = t // tt
    n_seq = 1 if steps > 1 else math.gcd(n, 8)
    return pl.pallas_call(
        functools.partial(_pool_kernel, tt=tt, pos0=pos0, carry=steps > 1, n_seq=n_seq),
        grid=(n // n_seq, steps),
        in_specs=[pl.BlockSpec((n_seq, tt, POOL_WIDTH), lambda b, j: (b, j, 0)),
                  pl.BlockSpec((n_seq, POOL_LEAD, POOL_WIDTH), lambda b, j: (b, 0, 0)),
                  _const_spec((len(POOL_WINDOWS), POOL_GROUP, POOL_GROUP)),
                  _const_spec((1, POOL_WIDTH))],
        out_specs=pl.BlockSpec((n_seq, tt, POOL_WIDTH), lambda b, j: (b, j, 0)),
        out_shape=jax.ShapeDtypeStruct((n, t, POOL_WIDTH), BF16),
        scratch_shapes=[pltpu.VMEM((POOL_LEAD + tt, POOL_WIDTH), F32)],
        compiler_params=_cparams(2),
        name="pool_mixer",
    )(proj, lead, w, scale)


S5_IN_HALF = SSM_WIDTH // 2
S5_STATE_HALF = SSM_STATES // 2
S5_OUT_TILE = LANES
S5_STATE_TILE = SSM_STATES // (SSM_WIDTH // S5_OUT_TILE)
S5_SCAN_LANES = 512


def _s5_input(ub, bre_ref, bim_ref):
    re, im = [], []
    for half in range(2):
        uk = ub[:, half * S5_IN_HALF:(half + 1) * S5_IN_HALF]
        re.append(jnp.dot(uk, bre_ref[half], preferred_element_type=F32))
        im.append(jnp.dot(uk, bim_ref[half], preferred_element_type=F32))
    return re, im


def _s5_output(h_re, h_im, cre_ref, cim_ref, tile):
    return (jnp.dot(h_re, cre_ref[tile], preferred_element_type=F32)
            - jnp.dot(h_im, cim_ref[tile], preferred_element_type=F32))


S5_CHUNK = 16
S5_ROW = S5_CHUNK * SSM_GROUP
S5_SUB = 256
UNIT = SSM_GROUP
UNITS = LANES // UNIT
S5_PAIRS = SSM_GROUPS // 2


def _unit_transpose(pieces):
    unit = lax.broadcasted_iota(jnp.int32, pieces[0].shape, 1) >> (UNIT.bit_length() - 1)
    cur = list(pieces)
    k = UNITS // 2
    while k:
        high = (unit & k) != 0
        nxt = list(cur)
        for a in range(UNITS):
            if a & k:
                continue
            lo_piece, hi_piece = cur[a], cur[a | k]
            nxt[a] = jnp.where(high, pltpu.roll(hi_piece, k * UNIT, 1), lo_piece)
            nxt[a | k] = jnp.where(high, hi_piece, pltpu.roll(lo_piece, LANES - k * UNIT, 1))
        cur = nxt
        k //= 2
    return cur


def _s5_prompt_kernel(u_ref, perm_ref, toep_ref, wre_ref, wim_ref, vre_ref, vim_ref, pw_ref, d_ref,
                      y_ref, hre_ref, him_ref, u2, y2, s_re, s_im, c_re, c_im, y_nat, *, tt):
    j = pl.program_id(1)
    chunks = tt // S5_CHUNK
    sub_chunks = S5_SUB // S5_CHUNK

    @pl.when(j == 0)
    def _():
        c_re[...] = jnp.zeros_like(c_re)
        c_im[...] = jnp.zeros_like(c_im)

    def relayout_in(sub, carry):
        r0 = pl.multiple_of(sub * S5_SUB, S5_SUB)
        c0 = pl.multiple_of(sub * sub_chunks, sub_chunks)
        ub = u_ref[0, pl.ds(r0, S5_SUB), :].astype(BF16)
        xs = jnp.dot(perm_ref[...], ub, preferred_element_type=F32).astype(BF16)
        for tile in range(SSM_WIDTH // LANES):
            for half in range(S5_CHUNK // UNITS):
                pieces = [pltpu.bitcast(
                    xs[(half * UNITS + a) * sub_chunks:(half * UNITS + a + 1) * sub_chunks,
                       tile * LANES:(tile + 1) * LANES], jnp.uint32) for a in range(UNITS)]
                outs = _unit_transpose(pieces)
                for gl in range(UNITS):
                    u2[tile * UNITS + gl, pl.ds(c0, sub_chunks), half * LANES:(half + 1) * LANES] = (
                        pltpu.bitcast(outs[gl], BF16))
        return carry

    lax.fori_loop(0, tt // S5_SUB, relayout_in, 0, unroll=2)

    s_re[0:SUBLANES, :] = c_re[...]
    s_im[0:SUBLANES, :] = c_im[...]
    for pair in range(S5_PAIRS):
        sl = slice(pair * LANES, (pair + 1) * LANES)
        ua = u2[2 * pair]
        ub = u2[2 * pair + 1]
        s_re[SUBLANES:SUBLANES + chunks, sl] = (
            jnp.dot(ua, wre_ref[2 * pair], preferred_element_type=F32)
            + jnp.dot(ub, wre_ref[2 * pair + 1], preferred_element_type=F32))
        s_im[SUBLANES:SUBLANES + chunks, sl] = (
            jnp.dot(ua, wim_ref[2 * pair], preferred_element_type=F32)
            + jnp.dot(ub, wim_ref[2 * pair + 1], preferred_element_type=F32))

    for chunk in range(SSM_STATES // S5_SCAN_LANES):
        sl = slice(chunk * S5_SCAN_LANES, (chunk + 1) * S5_SCAN_LANES)

        def body(r, carry, sl=sl):
            in_re, in_im = carry
            row = pl.multiple_of((r + 1) * SUBLANES, SUBLANES)
            re = s_re[pl.ds(row, SUBLANES), sl]
            im = s_im[pl.ds(row, SUBLANES), sl]
            for level, shift in enumerate((1, 2, 4)):
                a_re = pw_ref[2 * level, :, sl]
                a_im = pw_ref[2 * level + 1, :, sl]
                sh_re = pltpu.roll(re, shift, 0)
                sh_im = pltpu.roll(im, shift, 0)
                re, im = (re + a_re * sh_re - a_im * sh_im,
                          im + a_re * sh_im + a_im * sh_re)
            p_re = pw_ref[6, :, sl]
            p_im = pw_ref[7, :, sl]
            re, im = (re + p_re * in_re - p_im * in_im,
                      im + p_re * in_im + p_im * in_re)
            s_re[pl.ds(row, SUBLANES), sl] = re
            s_im[pl.ds(row, SUBLANES), sl] = im
            last = SUBLANES - 1
            return (jnp.broadcast_to(re[last:last + 1, :], re.shape),
                    jnp.broadcast_to(im[last:last + 1, :], im.shape))

        out_re, out_im = lax.fori_loop(0, chunks // SUBLANES, body, (c_re[:, sl], c_im[:, sl]))
        c_re[:, sl] = out_re
        c_im[:, sl] = out_im

    hre_ref[0] = c_re[...]
    him_ref[0] = c_im[...]

    for pair in range(S5_PAIRS):
        sl = slice(pair * LANES, (pair + 1) * LANES)
        h_re = s_re[SUBLANES - 1:SUBLANES - 1 + chunks, sl].astype(BF16)
        h_im = s_im[SUBLANES - 1:SUBLANES - 1 + chunks, sl].astype(BF16)
        carried = (jnp.dot(h_re, vre_ref[pair], preferred_element_type=F32)
                   + jnp.dot(h_im, vim_ref[pair], preferred_element_type=F32))
        for k in range(2):
            grp = 2 * pair + k
            y2[grp] = (jnp.dot(u2[grp], toep_ref[grp], preferred_element_type=F32)
                       + carried[:, k * S5_ROW:(k + 1) * S5_ROW])

    def relayout_out(blk, carry):
        r0 = pl.multiple_of(blk * S5_SUB, S5_SUB)
        c0 = pl.multiple_of(blk * sub_chunks, sub_chunks)
        for tile in range(SSM_WIDTH // LANES):
            ot = slice(tile * LANES, (tile + 1) * LANES)
            for half in range(S5_CHUNK // UNITS):
                pieces = [y2[tile * UNITS + gl, pl.ds(c0, sub_chunks), half * LANES:(half + 1) * LANES]
                          for gl in range(UNITS)]
                outs = _unit_transpose(pieces)
                for a in range(UNITS):
                    y_nat[tile, pl.ds(half * UNITS + a, sub_chunks, stride=S5_CHUNK), :] = outs[a]
            y_ref[0, pl.ds(r0, S5_SUB), ot] = y_nat[tile] + d_ref[:, ot] * u_ref[0, pl.ds(r0, S5_SUB), ot]
        return carry

    lax.fori_loop(0, tt // S5_SUB, relayout_out, 0)


def _s5_prompt(u, mats, tt):
    b, t, _ = u.shape
    chunks = tt // S5_CHUNK
    state = jax.ShapeDtypeStruct((b, SUBLANES, SSM_STATES), F32)
    state_spec = pl.BlockSpec((1, SUBLANES, SSM_STATES), lambda i, j: (i, 0, 0))
    return pl.pallas_call(
        functools.partial(_s5_prompt_kernel, tt=tt),
        grid=(b, t // tt),
        in_specs=[pl.BlockSpec((1, tt, SSM_WIDTH), lambda i, j: (i, j, 0))]
        + [_const_spec(a.shape) for a in mats],
        out_specs=[pl.BlockSpec((1, tt, SSM_WIDTH), lambda i, j: (i, j, 0)), state_spec, state_spec],
        out_shape=[jax.ShapeDtypeStruct((b, t, SSM_WIDTH), F32), state, state],
        scratch_shapes=[pltpu.VMEM((SSM_GROUPS, chunks, S5_ROW), BF16),
                        pltpu.VMEM((SSM_GROUPS, chunks, S5_ROW), F32),
                        pltpu.VMEM((SUBLANES + chunks, SSM_STATES), F32),
                        pltpu.VMEM((SUBLANES + chunks, SSM_STATES), F32),
                        pltpu.VMEM((SUBLANES, SSM_STATES), F32), pltpu.VMEM((SUBLANES, SSM_STATES), F32),
                        pltpu.VMEM((SSM_WIDTH // LANES, S5_SUB, LANES), F32)],
        compiler_params=_cparams(2),
        name="s5_prompt",
    )(u, *mats)


def _s5_sample_kernel(u_ref, h0re_ref, h0im_ref, bre_ref, bim_ref, cre_ref, cim_ref, lam_ref, d_ref,
                      y_ref, hre_ref, him_ref, *, steps):
    h_re = h0re_ref[...]
    h_im = h0im_ref[...]
    lam_re = lam_ref[0:1, :]
    lam_im = lam_ref[1:2, :]
    for t in range(steps):
        u = u_ref[t]
        bu_re, bu_im = _s5_input(u.astype(BF16), bre_ref, bim_ref)
        bu_re = jnp.concatenate(bu_re, axis=1)
        bu_im = jnp.concatenate(bu_im, axis=1)
        h_re, h_im = (lam_re * h_re - lam_im * h_im + bu_re,
                      lam_re * h_im + lam_im * h_re + bu_im)
        hb_re = h_re.astype(BF16)
        hb_im = h_im.astype(BF16)
        for tile in range(SSM_WIDTH // S5_OUT_TILE):
            st = slice(tile * S5_STATE_TILE, (tile + 1) * S5_STATE_TILE)
            ot = slice(tile * S5_OUT_TILE, (tile + 1) * S5_OUT_TILE)
            y = _s5_output(hb_re[:, st], hb_im[:, st], cre_ref, cim_ref, tile)
            y_ref[t, :, ot] = y + d_ref[:, ot] * u[:, ot]
    hre_ref[...] = h_re
    him_ref[...] = h_im


def _s5_sample(u_tb, h0_re, h0_im, mats, lam):
    steps, n, _ = u_tb.shape
    bre, bim, cre, cim, d_skip = mats
    state = jax.ShapeDtypeStruct((n, SSM_STATES), F32)
    args = (u_tb, h0_re, h0_im, bre, bim, cre, cim, lam, d_skip)
    return pl.pallas_call(
        functools.partial(_s5_sample_kernel, steps=steps),
        grid=(1,),
        in_specs=[_const_spec(a.shape) for a in args],
        out_specs=[_const_spec(u_tb.shape), _const_spec((n, SSM_STATES)), _const_spec((n, SSM_STATES))],
        out_shape=[jax.ShapeDtypeStruct(u_tb.shape, F32), state, state],
        compiler_params=_cparams(1),
        name="s5_sample",
    )(*args)


def _s5_matrices(lam_re, lam_im, log_dt, b_re, b_im, c_re, c_im, d_skip):
    lam = lax.complex(lam_re, lam_im)
    dt = jnp.exp(log_dt)[:, None]
    lam_bar = jnp.exp(lam * dt)
    b_bar = ((lam_bar - 1.0) / lam)[..., None] * lax.complex(b_re, b_im)
    eye_half = jnp.eye(SSM_GROUPS // 2, dtype=F32)

    def in_blocks(x):
        x = x.reshape(2, SSM_GROUPS // 2, SSM_STATE, SSM_GROUP)
        return jnp.einsum("kgpi,gh->kgihp", x, eye_half).reshape(2, S5_IN_HALF, S5_STATE_HALF).astype(BF16)

    tiles = SSM_WIDTH // S5_OUT_TILE
    groups_per_tile = SSM_GROUPS // tiles
    eye_tile = jnp.eye(groups_per_tile, dtype=F32)

    def out_blocks(x):
        x = x.reshape(tiles, groups_per_tile, SSM_GROUP, SSM_STATE)
        return jnp.einsum("kgop,gh->kgpho", x, eye_tile).reshape(tiles, S5_STATE_TILE, S5_OUT_TILE).astype(BF16)

    lam_flat = lam_bar.reshape(1, SSM_STATES)
    lam_rows = jnp.concatenate([lam_flat.real, lam_flat.imag], axis=0).astype(F32)
    d_row = d_skip.reshape(1, SSM_WIDTH)
    sample_mats = (in_blocks(b_bar.real), in_blocks(b_bar.imag), out_blocks(c_re), out_blocks(c_im), d_row)

    c_mat = lax.complex(c_re, c_im)
    lam_pow = jnp.concatenate([jnp.ones((1,) + lam_bar.shape, lam_bar.dtype),
                               jnp.cumprod(jnp.broadcast_to(lam_bar, (S5_CHUNK,) + lam_bar.shape), axis=0)])
    taps = jnp.einsum("gop,kgp,gpi->gkoi", c_mat, lam_pow[:S5_CHUNK], b_bar,
                      precision=lax.Precision.HIGHEST).real
    src = jnp.arange(S5_CHUNK)[None, :, None]
    dst = jnp.arange(S5_CHUNK)[None, None, :]
    lag_is = (dst - src == jnp.arange(S5_CHUNK)[:, None, None]).astype(F32)
    toep = jnp.einsum("kst,gkoi->gsito", lag_is, taps, precision=lax.Precision.HIGHEST)
    toep = toep.reshape(SSM_GROUPS, S5_ROW, S5_ROW).astype(BF16)
    w_state = jnp.einsum("sgp,gpi->gsip", lam_pow[S5_CHUNK - 1::-1][:S5_CHUNK], b_bar)
    w_state = w_state.reshape(SSM_GROUPS, S5_ROW, SSM_STATE)
    second = (jnp.arange(SSM_GROUPS) % 2 == 1)[:, None, None]
    zeros = jnp.zeros_like(w_state.real)

    def pair_cols(x):
        return jnp.where(second, jnp.concatenate([zeros, x], -1), jnp.concatenate([x, zeros], -1)).astype(BF16)

    v_out = jnp.einsum("gop,tgp->gpto", c_mat, lam_pow[1:]).reshape(SSM_GROUPS, SSM_STATE, S5_ROW)
    eye_pair = jnp.eye(2, dtype=F32)

    def pair_blocks(x):
        x = x.reshape(S5_PAIRS, 2, SSM_STATE, S5_ROW)
        return jnp.einsum("kapc,ab->kapbc", x, eye_pair).reshape(S5_PAIRS, LANES, 2 * S5_ROW).astype(BF16)

    lam_chunk = lam_pow[S5_CHUNK].reshape(1, SSM_STATES)
    rows = jnp.arange(SUBLANES)[:, None]
    planes = []
    for shift in (1, 2, 4):
        a = jnp.where(rows >= shift, lam_chunk ** shift, 0.0)
        planes += [a.real, a.imag]
    carry = jnp.cumprod(jnp.broadcast_to(lam_chunk, (SUBLANES, SSM_STATES)), axis=0)
    planes += [carry.real, carry.imag]
    powers = jnp.stack(planes).astype(F32)
    pos = jnp.arange(S5_SUB)
    perm = (jnp.arange(S5_SUB)[None, :] == ((pos % (S5_SUB // S5_CHUNK)) * S5_CHUNK
                                            + pos // (S5_SUB // S5_CHUNK))[:, None]).astype(BF16)
    prompt_mats = (perm, toep, pair_cols(w_state.real), pair_cols(w_state.imag),
                   pair_blocks(v_out.real), pair_blocks(-v_out.imag), powers, d_row)
    return prompt_mats, sample_mats, lam_rows


def _ab_out_kernel(yp_ref, ys_ref, wglu_ref, bglu_ref, wout_ref, h_ref, g_ref, o_ref):
    z = jax.nn.gelu(ys_ref[...])
    gate = jnp.dot(z.astype(BF16), wglu_ref[...], preferred_element_type=F32) + bglu_ref[...]
    y_ssm = z * jax.nn.sigmoid(gate)
    y = (jnp.dot(yp_ref[...], wout_ref[0:POOL_WIDTH, :], preferred_element_type=F32)
         + jnp.dot(y_ssm.astype(BF16), wout_ref[POOL_WIDTH:, :], preferred_element_type=F32))
    o_ref[...] = h_ref[...] + _rms(y, g_ref[...])


def _ab_out(y_pool, y_ssm, w_glu, b_glu, w_out, h, g, tm):
    m = h.shape[0]
    row = lambda width: pl.BlockSpec((tm, width), lambda i: (i, 0))
    return pl.pallas_call(
        _ab_out_kernel,
        grid=(m // tm,),
        in_specs=[row(POOL_WIDTH), row(SSM_WIDTH), _const_spec(w_glu.shape), _const_spec(b_glu.shape),
                  _const_spec(w_out.shape), row(D_MODEL), _const_spec(g.shape)],
        out_specs=row(D_MODEL),
        out_shape=jax.ShapeDtypeStruct((m, D_MODEL), F32),
        compiler_params=_cparams(1),
        name="ab_out",
    )(y_pool, y_ssm, w_glu, b_glu, w_out, h, g)


def _ffn_kernel(h_ref, gin_ref, wg_ref, wu_ref, wd_ref, gout_ref, o_ref):
    h = h_ref[...]
    xb = _rms(h, gin_ref[...]).astype(BF16)
    gate = jnp.dot(xb, wg_ref[...], preferred_element_type=F32)
    up = jnp.dot(xb, wu_ref[...], preferred_element_type=F32)
    act = (gate * jax.nn.sigmoid(gate) * up).astype(BF16)
    y = jnp.dot(act, wd_ref[...], preferred_element_type=F32)
    o_ref[...] = h + _rms(y, gout_ref[...])


def _ffn(h, g_in, w_gate, w_up, w_down, g_out, tm):
    m = h.shape[0]
    row = pl.BlockSpec((tm, D_MODEL), lambda i: (i, 0))
    return pl.pallas_call(
        _ffn_kernel,
        grid=(m // tm,),
        in_specs=[row, _const_spec(g_in.shape), _const_spec(w_gate.shape), _const_spec(w_up.shape),
                  _const_spec(w_down.shape), _const_spec(g_out.shape)],
        out_specs=row,
        out_shape=jax.ShapeDtypeStruct((m, D_MODEL), F32),
        compiler_params=_cparams(1),
        name="ffn",
    )(h, g_in, w_gate, w_up, w_down, g_out)


def _attn_prompt_kernel(q_ref, kc_ref, kp_ref, vc_ref, vp_ref, o_ref, lse_ref):
    blk = pl.program_id(2)
    rows = 2 * ATT_BLOCK
    qi = lax.broadcasted_iota(jnp.int32, (rows, 2 * ATT_BLOCK), 0) & (ATT_BLOCK - 1)
    kj = lax.broadcasted_iota(jnp.int32, (rows, 2 * ATT_BLOCK), 1)
    dist = ATT_BLOCK + qi - kj
    valid = (dist >= 0) & (dist <= ATT_SPAN) & ((kj >= ATT_BLOCK) | (blk > 0))
    lane = lax.broadcasted_iota(jnp.int32, (ATT_BLOCK, LANES), 1)
    first_head = lane < HEAD_DIM
    lse_all = jnp.zeros((ATT_BLOCK, LANES), F32)
    for pair in range(ATT_HEADS // 2):
        sl = slice(pair * LANES, (pair + 1) * LANES)
        q2 = q_ref[:, sl]
        zero = jnp.zeros_like(q2)
        qs = jnp.concatenate([jnp.where(first_head, q2, zero), jnp.where(first_head, zero, q2)], axis=0)
        k2 = jnp.concatenate([kp_ref[:, sl], kc_ref[:, sl]], axis=0)
        v2 = jnp.concatenate([vp_ref[:, sl], vc_ref[:, sl]], axis=0)
        s = lax.dot_general(qs, k2, (((1,), (1,)), ((), ())), preferred_element_type=F32)
        s = jnp.where(valid, s, -jnp.inf)
        m = jnp.max(s, axis=-1, keepdims=True)
        p = jnp.exp(s - m)
        den = jnp.sum(p, axis=-1, keepdims=True)
        o = jnp.dot(p.astype(BF16), v2, preferred_element_type=F32) / den
        lse = m + jnp.log(den)
        o_ref[:, sl] = jnp.where(first_head, o[:ATT_BLOCK], o[ATT_BLOCK:]).astype(o_ref.dtype)
        lse_all = jnp.where(lane == 2 * pair, lse[:ATT_BLOCK],
                            jnp.where(lane == 2 * pair + 1, lse[ATT_BLOCK:], lse_all))
    lse_ref[...] = lse_all


def _attn_prompt(q, k, v, dil):
    batch, _, sub, _ = q.shape
    nblk = sub // ATT_BLOCK
    cur = pl.BlockSpec((None, None, ATT_BLOCK, ATT_WIDTH), lambda b, r, i: (b, r, i, 0))
    prev = pl.BlockSpec((None, None, ATT_BLOCK, ATT_WIDTH), lambda b, r, i: (b, r, jnp.maximum(i - 1, 0), 0))
    return pl.pallas_call(
        _attn_prompt_kernel,
        grid=(batch, dil, nblk),
        in_specs=[cur, cur, prev, cur, prev],
        out_specs=[cur, pl.BlockSpec((None, None, ATT_BLOCK, LANES), lambda b, r, i: (b, r, i, 0))],
        out_shape=[jax.ShapeDtypeStruct((batch, dil, sub, ATT_WIDTH), BF16),
                   jax.ShapeDtypeStruct((batch, dil, sub, LANES), F32)],
        compiler_params=_cparams(3),
        name="attn_prompt",
    )(q, k, k, v, v)


def _pattern_multiplicity(dist):
    mult = jnp.zeros(dist.shape, F32)
    for window, dil in DILATED_PATTERNS:
        hit = (dist >= 0) & (dist <= window) & ((dist & (dil - 1)) == 0)
        mult = mult + hit.astype(F32)
    return mult


def _attn_sample_kernel(q_ref, knew_ref, vnew_ref, kt_ref, vt_ref, o_ref, *, steps):
    t_cache = lax.broadcasted_iota(jnp.int32, (SUBLANES, MAX_WINDOW), 0)
    pos = lax.broadcasted_iota(jnp.int32, (SUBLANES, MAX_WINDOW), 1)
    mult = _pattern_multiplicity(MAX_WINDOW + t_cache - pos)
    t_new = lax.broadcasted_iota(jnp.int32, (SUBLANES, LANES), 0)
    j_new = lax.broadcasted_iota(jnp.int32, (SUBLANES, LANES), 1)
    mult_new = jnp.where(j_new < steps, _pattern_multiplicity(t_new - j_new), 0.0)
    nt = (((1,), (1,)), ((), ()))
    for head in range(ATT_HEADS):
        q = q_ref[0, head].astype(BF16)
        s = jnp.dot(q, kt_ref[0, head].astype(BF16), preferred_element_type=F32)
        s_new = lax.dot_general(q, knew_ref[0, head].astype(BF16), nt, preferred_element_type=F32)
        s = jnp.where(mult > 0.0, s, -jnp.inf)
        s_new = jnp.where(mult_new > 0.0, s_new, -jnp.inf)
        m = jnp.maximum(jnp.max(s, axis=-1, keepdims=True), jnp.max(s_new, axis=-1, keepdims=True))
        p = mult * jnp.exp(s - m)
        p_new = mult_new * jnp.exp(s_new - m)
        den = jnp.sum(p, axis=-1, keepdims=True) + jnp.sum(p_new, axis=-1, keepdims=True)
        o = lax.dot_general(p.astype(BF16), vt_ref[0, head].astype(BF16), nt, preferred_element_type=F32)
        o = o + jnp.dot(p_new.astype(BF16), vnew_ref[0, head].astype(BF16), preferred_element_type=F32)
        o_ref[0, :, head * HEAD_DIM:(head + 1) * HEAD_DIM] = (o / den)[:steps]


def _attn_sample(q, k_new, v_new, cache_kt, cache_vt, steps):
    n = q.shape[0]
    assert steps <= SUBLANES
    per = lambda a: pl.BlockSpec((1,) + a.shape[1:], lambda b: (b, 0, 0, 0))
    args = (q, k_new, v_new, cache_kt, cache_vt)
    return pl.pallas_call(
        functools.partial(_attn_sample_kernel, steps=steps),
        grid=(n,),
        in_specs=[per(a) for a in args],
        out_specs=pl.BlockSpec((1, steps, ATT_WIDTH), lambda b: (b, 0, 0)),
        out_shape=jax.ShapeDtypeStruct((n, steps, ATT_WIDTH), F32),
        compiler_params=_cparams(1),
        name="attn_sample",
    )(*args)


def _attn_out_kernel(*refs, tm):
    n_pat = len(ATT_DILATIONS)
    o_refs = refs[:n_pat]
    lse_refs = refs[n_pat:2 * n_pat]
    w_ref, expand_ref, h_ref, g_ref, out_ref, o_nat, lse_nat = refs[2 * n_pat:]
    for pat, dil in enumerate(ATT_DILATIONS):
        rows = tm // dil
        for res in range(dil):
            dst = pl.ds(res, rows, stride=dil) if dil > 1 else pl.ds(0, rows)
            lse_nat[pat, dst, :] = lse_refs[pat][0, res]
            for c in range(COL_TILES):
                o_nat[pat * COL_TILES + c, dst, :] = o_refs[pat][0, res, :, c * LANES:(c + 1) * LANES].astype(F32)
    lses = [lse_nat[pat] for pat in range(n_pat)]
    top = functools.reduce(jnp.maximum, lses)
    es = [jnp.exp(l - top) for l in lses]
    total = functools.reduce(lambda x, y: x + y, es)
    spread = []
    for e in es[:-1]:
        wgt = e / total
        hi = wgt.astype(BF16)
        lo = (wgt - hi.astype(F32)).astype(BF16)
        spread.append(jnp.dot(hi, expand_ref[...], preferred_element_type=F32)
                      + jnp.dot(lo, expand_ref[...], preferred_element_type=F32))
    spread.append(1.0 - functools.reduce(lambda x, y: x + y, spread))
    pieces = []
    for c in range(COL_TILES):
        acc = None
        for pat in range(n_pat):
            term = spread[pat][:, c * LANES:(c + 1) * LANES] * o_nat[pat * COL_TILES + c]
            acc = term if acc is None else acc + term
        pieces.append(acc.astype(BF16))
    a = jnp.concatenate(pieces, axis=1)
    y = jnp.dot(a, w_ref[...], preferred_element_type=F32)
    out_ref[...] = h_ref[...] + _rms(y, g_ref[...])


def _attn_out(outs, lses, w_o, h, g, seq, tm):
    m = h.shape[0]
    tiles = seq // tm
    n_pat = len(ATT_DILATIONS)

    def dil_spec(dil, width):
        return pl.BlockSpec((1, dil, tm // dil, width), lambda i: (i // tiles, 0, i % tiles, 0))

    row = pl.BlockSpec((tm, D_MODEL), lambda i: (i, 0))
    expand = (jnp.arange(LANES)[:, None] == jnp.arange(ATT_WIDTH)[None, :] // HEAD_DIM).astype(BF16)
    return pl.pallas_call(
        functools.partial(_attn_out_kernel, tm=tm),
        grid=(m // tm,),
        in_specs=[dil_spec(dil, ATT_WIDTH) for dil in ATT_DILATIONS]
        + [dil_spec(dil, LANES) for dil in ATT_DILATIONS]
        + [_const_spec(w_o.shape), _const_spec(expand.shape), row, _const_spec(g.shape)],
        out_specs=row,
        out_shape=jax.ShapeDtypeStruct((m, D_MODEL), F32),
        scratch_shapes=[pltpu.VMEM((n_pat * COL_TILES, tm, LANES), F32), pltpu.VMEM((n_pat, tm, LANES), F32)],
        compiler_params=_cparams(1),
        name="attn_out",
    )(*outs, *lses, w_o, expand, h, g)


def _proj_out_kernel(a_ref, w_ref, h_ref, g_ref, out_ref):
    y = jnp.dot(a_ref[...].astype(BF16), w_ref[...], preferred_element_type=F32)
    out_ref[...] = h_ref[...] + _rms(y, g_ref[...])


def _proj_out(a, w, h, g, tm):
    m = h.shape[0]
    return pl.pallas_call(
        _proj_out_kernel,
        grid=(m // tm,),
        in_specs=[pl.BlockSpec((tm, a.shape[1]), lambda i: (i, 0)), _const_spec(w.shape),
                  pl.BlockSpec((tm, D_MODEL), lambda i: (i, 0)), _const_spec(g.shape)],
        out_specs=pl.BlockSpec((tm, D_MODEL), lambda i: (i, 0)),
        out_shape=jax.ShapeDtypeStruct((m, D_MODEL), F32),
        compiler_params=_cparams(1),
        name="proj_out",
    )(a, w, h, g)


def kernel(x_prompt, x_sample, state_pool, state_s5, cache_k, cache_v, norm_gains, ab_w_in, ab_pool_w, ab_pool_scale, ab_lambda_re, ab_lambda_im, ab_log_dt, ab_b_re, ab_b_im, ab_c_re, ab_c_im, ab_d, ab_w_glu, ab_b_glu, ab_w_out, c_w_qkv, c_w_o, ffn_w_gate, ffn_w_up, ffn_w_down):
    batch, seq, d = x_prompt.shape
    n_dec, t_dec, _ = x_sample.shape
    mp, ms = batch * seq, n_dec * t_dec
    tm_p = 256
    tm_ffn = 512
    gains = norm_gains.reshape(norm_gains.shape[0], 4, 1, d)

    hp = x_prompt.reshape(mp, d)
    hs = x_sample.reshape(ms, d)

    g = gains[0]
    w_in = ab_w_in[0].astype(BF16)
    pool_w = ab_pool_w[0].astype(BF16)
    pool_scale = ab_pool_scale[0].reshape(1, POOL_WIDTH)
    lead_p = jnp.zeros((batch, POOL_LEAD, POOL_WIDTH), F32)
    lead_s = jnp.pad(state_pool[0], ((0, 0), (POOL_LEAD - POOL_STATE, 0), (0, 0)))
    ussm_p, ypool_p, pool_tail = _in_proj_pool(x_prompt, g[0], w_in, lead_p, pool_w, pool_scale, tm_ffn, 0)
    (proj_s,) = _norm_matmul(hs, g[0], w_in, ((0, d, 1.0),), (F32,), ms)
    proj_s = proj_s.reshape(n_dec, t_dec, d)
    ypool_s = _pool_mixer(proj_s, lead_s, pool_w, pool_scale, t_dec, PAST_LEN)
    pool_prompt = pool_tail[:, POOL_LEAD - POOL_STATE:][None]
    pool_sample = jnp.concatenate([state_pool[0], proj_s[:, :, :POOL_WIDTH]], axis=1)[:, -POOL_STATE:][None]

    prompt_mats, sample_mats, lam_rows = _s5_matrices(
        ab_lambda_re[0], ab_lambda_im[0], ab_log_dt[0], ab_b_re[0], ab_b_im[0], ab_c_re[0], ab_c_im[0], ab_d[0])
    yssm_p, hre_p, him_p = _s5_prompt(ussm_p, prompt_mats, 2048)
    u_tb = jnp.swapaxes(proj_s[:, :, POOL_WIDTH:], 0, 1)
    h0 = state_s5[0].reshape(n_dec, SSM_STATES, 2)
    yssm_tb, hre_s, him_s = _s5_sample(u_tb, h0[..., 0], h0[..., 1], sample_mats, lam_rows)
    yssm_s = jnp.swapaxes(yssm_tb, 0, 1)
    s5_prompt = jnp.stack([hre_p[:, 0], him_p[:, 0]], axis=-1).reshape(1, batch, SSM_GROUPS, SSM_STATE, 2)
    s5_sample = jnp.stack([hre_s, him_s], axis=-1).reshape(1, n_dec, SSM_GROUPS, SSM_STATE, 2)

    w_glu = ab_w_glu[0].astype(BF16)
    b_glu = ab_b_glu[0].reshape(1, SSM_WIDTH)
    w_out = ab_w_out[0].astype(BF16)
    hp = _ab_out(ypool_p.reshape(mp, POOL_WIDTH), yssm_p.reshape(mp, SSM_WIDTH), w_glu, b_glu, w_out, hp, g[1], tm_ffn)
    hs = _ab_out(ypool_s.reshape(ms, POOL_WIDTH), yssm_s.reshape(ms, SSM_WIDTH), w_glu, b_glu, w_out, hs, g[1], ms)

    wg, wu, wd = ffn_w_gate[0].astype(BF16), ffn_w_up[0].astype(BF16), ffn_w_down[0].astype(BF16)
    hp = _ffn(hp, g[2], wg, wu, wd, g[3], tm_ffn)
    hs = _ffn(hs, g[2], wg, wu, wd, g[3], ms)

    g = gains[1]
    w_qkv = c_w_qkv[0].astype(BF16)
    w_o = c_w_o[0].astype(BF16)
    keep = min(MAX_WINDOW, seq)
    *qkv_p, kt_tail, vt_tail = _qkv_prompt(hp, g[0], w_qkv, batch, seq, keep, tm_p)
    positions_first = lambda a: jnp.transpose(a.reshape(batch, ATT_HEADS, HEAD_DIM, keep), (0, 3, 1, 2))[None]
    k_prompt = positions_first(kt_tail)
    v_prompt = positions_first(vt_tail)
    n_dil = len(ATT_DILATIONS)
    outs, lses = [], []
    for pat, dil in enumerate(ATT_DILATIONS):
        o, lse = _attn_prompt(qkv_p[pat], qkv_p[n_dil + pat], qkv_p[2 * n_dil + pat], dil)
        outs.append(o)
        lses.append(lse)
    hp = _attn_out(outs, lses, w_o, hp, g[1], seq, tm_p)

    scale = HEAD_DIM ** -0.5
    q_s, k_s, v_s = _norm_matmul(
        hs, g[0], w_qkv,
        ((0, ATT_WIDTH, scale), (ATT_WIDTH, ATT_WIDTH, 1.0), (2 * ATT_WIDTH, ATT_WIDTH, 1.0)),
        (F32, F32, F32), ms)
    k_sample = k_s.reshape(1, n_dec, t_dec, ATT_HEADS, HEAD_DIM)
    v_sample = v_s.reshape(1, n_dec, t_dec, ATT_HEADS, HEAD_DIM)
    head_major = lambda a, rows: jnp.pad(
        jnp.swapaxes(a.reshape(n_dec, t_dec, ATT_HEADS, HEAD_DIM), 1, 2),
        ((0, 0), (0, 0), (0, rows - t_dec), (0, 0)))
    cache_kt = jnp.transpose(cache_k[0], (0, 2, 3, 1))
    cache_vt = jnp.transpose(cache_v[0], (0, 2, 3, 1))
    o_s = _attn_sample(head_major(q_s, SUBLANES), head_major(k_s, LANES), head_major(v_s, LANES),
                       cache_kt, cache_vt, t_dec)
    hs = _proj_out(o_s.reshape(ms, ATT_WIDTH), w_o, hs, g[1], ms)

    wg, wu, wd = ffn_w_gate[1].astype(BF16), ffn_w_up[1].astype(BF16), ffn_w_down[1].astype(BF16)
    hp = _ffn(hp, g[2], wg, wu, wd, g[3], tm_ffn)
    hs = _ffn(hs, g[2], wg, wu, wd, g[3], ms)

    return (hp.reshape(batch, seq, d), hs.reshape(n_dec, t_dec, d), pool_prompt, s5_prompt,
            k_prompt, v_prompt, pool_sample, s5_sample, k_sample, v_sample)
```

```python
import functools
import math

import jax
import jax.numpy as jnp
from jax import lax
from jax.experimental import pallas as pl
from jax.experimental.pallas import tpu as pltpu

F32 = jnp.float32
BF16 = jnp.bfloat16

D_MODEL = 1024
PAST_LEN = 16384
POOL_WIDTH = 512
POOL_WINDOWS = (2, 4, 8, 16)
POOL_GROUP = 128
POOL_STATE = 15
POOL_LEAD = 16
SSM_WIDTH = 512
SSM_GROUP = 16
SSM_GROUPS = 32
SSM_STATE = 64
SSM_STATES = SSM_GROUPS * SSM_STATE
ATT_HEADS = 16
HEAD_DIM = 64
ATT_WIDTH = ATT_HEADS * HEAD_DIM
DILATED_PATTERNS = ((128, 1), (512, 4), (2048, 16))
MAX_WINDOW = 2048
ATT_BLOCK = 128
ATT_SPAN = 128
FFN_HIDDEN = 2816
RMS_EPS = 1e-6

SUBLANES = 8
LANES = 128
VMEM_LIMIT = 56 * 1024 * 1024


def _cparams(n_axes):
    return pltpu.CompilerParams(
        dimension_semantics=("arbitrary",) * n_axes, vmem_limit_bytes=VMEM_LIMIT)


def _rms(x, g):
    ms = jnp.mean(x * x, axis=-1, keepdims=True)
    return (x * lax.rsqrt(ms + RMS_EPS)) * g


def _const_spec(shape):
    zeros = (0,) * len(shape)
    return pl.BlockSpec(shape, lambda *_: zeros, pipeline_mode=pl.Buffered(1))


def _norm_matmul_kernel(x_ref, g_ref, w_ref, *out_refs, splits):
    xb = _rms(x_ref[...], g_ref[...]).astype(BF16)
    for (c0, width, scale), o_ref in zip(splits, out_refs):
        y = jnp.dot(xb, w_ref[:, c0:c0 + width], preferred_element_type=F32)
        if scale != 1.0:
            y = y * scale
        o_ref[...] = y.astype(o_ref.dtype)


def _norm_matmul(x, g, w, splits, dtypes, tm):
    m, d = x.shape
    n = w.shape[1]
    return pl.pallas_call(
        functools.partial(_norm_matmul_kernel, splits=splits),
        grid=(m // tm,),
        in_specs=[pl.BlockSpec((tm, d), lambda i: (i, 0)), _const_spec((1, d)), _const_spec((d, n))],
        out_specs=[pl.BlockSpec((tm, width), lambda i: (i, 0)) for _, width, _ in splits],
        out_shape=[jax.ShapeDtypeStruct((m, width), dt) for (_, width, _), dt in zip(splits, dtypes)],
        compiler_params=_cparams(1),
        name="norm_matmul",
    )(x, g, w)


ATT_DILATIONS = tuple(dil for _, dil in DILATED_PATTERNS)
ATT_ORDERS = (ATT_DILATIONS[0], ATT_DILATIONS[2])
PATTERN_ORDER = (ATT_DILATIONS[0], ATT_DILATIONS[2], ATT_DILATIONS[2])
MID_RATIO = ATT_DILATIONS[2] // ATT_DILATIONS[1]
MID_BLOCK = ATT_BLOCK // MID_RATIO
COL_TILES = ATT_WIDTH // LANES


def _qkv_prompt_kernel(x_ref, g_ref, w_ref, perm_ref, *refs, q_scale, tm, tiles, first_tail):
    n_dil = len(ATT_ORDERS)
    out_refs = refs[:3 * n_dil]
    tail_refs = refs[3 * n_dil:]
    in_tail = (pl.program_id(0) % tiles) >= first_tail
    xb = _rms(x_ref[...], g_ref[...]).astype(BF16)
    for which in range(3):
        y = jnp.dot(xb, w_ref[:, which * ATT_WIDTH:(which + 1) * ATT_WIDTH], preferred_element_type=F32)
        if which == 0:
            y = y * q_scale
        else:
            @pl.when(in_tail)
            def _(y=y, tail_ref=tail_refs[which - 1]):
                tail_ref[0] = y.T
        yb = y.astype(BF16)
        for pat, (dil, o_ref) in enumerate(zip(ATT_ORDERS, out_refs[which * n_dil:(which + 1) * n_dil])):
            if dil == 1:
                o_ref[0, 0] = yb
                continue
            moved = jnp.dot(perm_ref[pat - 1], yb, preferred_element_type=F32).astype(BF16)
            rows = tm // dil
            for res in range(dil):
                o_ref[0, res] = moved[res * rows:(res + 1) * rows]


def _qkv_prompt(x, g, w, batch, seq, keep, tm):
    m, d = x.shape
    tiles = seq // tm
    first_tail = tiles - keep // tm

    def tail_map(i):
        return (i // tiles, 0, jnp.maximum(i % tiles - first_tail, 0))

    def dil_spec(dil):
        return pl.BlockSpec((1, dil, tm // dil, ATT_WIDTH), lambda i: (i // tiles, 0, i % tiles, 0))

    def dil_shape(dil):
        return jax.ShapeDtypeStruct((batch, dil, seq // dil, ATT_WIDTH), BF16)

    assert ATT_ORDERS[0] == 1
    src = jnp.arange(tm)[None, :]
    dst = jnp.arange(tm)[:, None]
    perm = jnp.stack([(src == (dst % (tm // dil)) * dil + dst // (tm // dil)).astype(BF16)
                      for dil in ATT_ORDERS[1:]])
    return pl.pallas_call(
        functools.partial(_qkv_prompt_kernel, q_scale=HEAD_DIM ** -0.5, tm=tm, tiles=tiles,
                          first_tail=first_tail),
        grid=(m // tm,),
        in_specs=[pl.BlockSpec((tm, d), lambda i: (i, 0)), _const_spec((1, d)),
                  _const_spec((d, 3 * ATT_WIDTH)), _const_spec(perm.shape)],
        out_specs=[dil_spec(dil) for _ in range(3) for dil in ATT_ORDERS]
        + [pl.BlockSpec((1, ATT_WIDTH, tm), tail_map)] * 2,
        out_shape=[dil_shape(dil) for _ in range(3) for dil in ATT_ORDERS]
        + [jax.ShapeDtypeStruct((batch, ATT_WIDTH, keep), F32)] * 2,
        compiler_params=_cparams(1),
        name="qkv_prompt",
    )(x, g, w, perm)


def _pool_tile(u, j, lead_ref, w_ref, scale_ref, y_ref, ext, *, tt, pos0, carry, seq=0):
    @pl.when(j == 0)
    def _():
        ext[0:POOL_LEAD, :] = lead_ref[seq]

    ext[POOL_LEAD:POOL_LEAD + tt, :] = u
    pos = pos0 + j * tt + lax.broadcasted_iota(jnp.int32, (tt, POOL_GROUP), 0)
    for grp, window in enumerate(POOL_WINDOWS):
        sl = slice(grp * POOL_GROUP, (grp + 1) * POOL_GROUP)
        u_g = u[:, sl]
        win_sum = u_g
        for k in range(1, window):
            win_sum = win_sum + ext[POOL_LEAD - k:POOL_LEAD - k + tt, sl]
        count = jnp.minimum(pos + 1, window).astype(F32)
        diff = win_sum / count - u_g
        y = jnp.dot(diff.astype(BF16), w_ref[grp], preferred_element_type=F32)
        y_ref[seq, :, sl] = (y * scale_ref[:, sl]).astype(y_ref.dtype)
    if carry:
        ext[0:POOL_LEAD, :] = ext[tt:tt + POOL_LEAD, :]


def _pool_kernel(u_ref, lead_ref, w_ref, scale_ref, y_ref, ext, *, tt, pos0, carry, n_seq):
    for seq in range(n_seq):
        _pool_tile(u_ref[seq], pl.program_id(1), lead_ref, w_ref, scale_ref, y_ref, ext,
                   tt=tt, pos0=pos0, carry=carry, seq=seq)


def _in_proj_pool_kernel(x_ref, g_ref, w_ref, lead_ref, pw_ref, scale_ref, ussm_ref, ypool_ref, tail_ref, ext,
                         *, tt, pos0):
    xb = _rms(x_ref[0], g_ref[...]).astype(BF16)
    ussm_ref[0] = jnp.dot(xb, w_ref[:, POOL_WIDTH:], preferred_element_type=F32)
    u_pool = jnp.dot(xb, w_ref[:, 0:POOL_WIDTH], preferred_element_type=F32)
    _pool_tile(u_pool, pl.program_id(1), lead_ref, pw_ref, scale_ref, ypool_ref, ext, tt=tt, pos0=pos0, carry=True)
    tail_ref[0] = ext[0:POOL_LEAD, :]


def _in_proj_pool(x, g, w, lead, pool_w, pool_scale, tt, pos0):
    n, t, d = x.shape
    seq_block = lambda width: pl.BlockSpec((1, tt, width), lambda b, j: (b, j, 0))
    per_seq = pl.BlockSpec((1, POOL_LEAD, POOL_WIDTH), lambda b, j: (b, 0, 0))
    return pl.pallas_call(
        functools.partial(_in_proj_pool_kernel, tt=tt, pos0=pos0),
        grid=(n, t // tt),
        in_specs=[seq_block(d), _const_spec(g.shape), _const_spec(w.shape), per_seq,
                  _const_spec(pool_w.shape), _const_spec(pool_scale.shape)],
        out_specs=[seq_block(SSM_WIDTH), seq_block(POOL_WIDTH), per_seq],
        out_shape=[jax.ShapeDtypeStruct((n, t, SSM_WIDTH), F32), jax.ShapeDtypeStruct((n, t, POOL_WIDTH), BF16),
                   jax.ShapeDtypeStruct((n, POOL_LEAD, POOL_WIDTH), F32)],
        scratch_shapes=[pltpu.VMEM((POOL_LEAD + tt, POOL_WIDTH), F32)],
        compiler_params=_cparams(2),
        name="in_proj_pool",
    )(x, g, w, lead, pool_w, pool_scale)


def _pool_mixer(proj, lead, w, scale, tt, pos0):
    n, t, _ = proj.shape
    steps = t // tt
    n_seq = 1 if steps > 1 else math.gcd(n, 8)
    return pl.pallas_call(
        functools.partial(_pool_kernel, tt=tt, pos0=pos0, carry=steps > 1, n_seq=n_seq),
        grid=(n // n_seq, steps),
        in_specs=[pl.BlockSpec((n_seq, tt, POOL_WIDTH), lambda b, j: (b, j, 0)),
                  pl.BlockSpec((n_seq, POOL_LEAD, POOL_WIDTH), lambda b, j: (b, 0, 0)),
                  _const_spec((len(POOL_WINDOWS), POOL_GROUP, POOL_GROUP)),
                  _const_spec((1, POOL_WIDTH))],
        out_specs=pl.BlockSpec((n_seq, tt, POOL_WIDTH), lambda b, j: (b, j, 0)),
        out_shape=jax.ShapeDtypeStruct((n, t, POOL_WIDTH), BF16),
        scratch_shapes=[pltpu.VMEM((POOL_LEAD + tt, POOL_WIDTH), F32)],
        compiler_params=_cparams(2),
        name="pool_mixer",
    )(proj, lead, w, scale)


S5_IN_HALF = SSM_WIDTH // 2
S5_STATE_HALF = SSM_STATES // 2
S5_OUT_TILE = LANES
S5_STATE_TILE = SSM_STATES // (SSM_WIDTH // S5_OUT_TILE)
S5_SCAN_LANES = 512


def _s5_input(ub, bre_ref, bim_ref):
    re, im = [], []
    for half in range(2):
        uk = ub[:, half * S5_IN_HALF:(half + 1) * S5_IN_HALF]
        re.append(jnp.dot(uk, bre_ref[half], preferred_element_type=F32))
        im.append(jnp.dot(uk, bim_ref[half], preferred_element_type=F32))
    return re, im


def _s5_output(h_re, h_im, cre_ref, cim_ref, tile):
    return (jnp.dot(h_re, cre_ref[tile], preferred_element_type=F32)
            - jnp.dot(h_im, cim_ref[tile], preferred_element_type=F32))


S5_CHUNK = 16
S5_ROW = S5_CHUNK * SSM_GROUP
S5_SUB = 256
UNIT = SSM_GROUP
UNITS = LANES // UNIT
S5_PAIRS = SSM_GROUPS // 2


def _unit_transpose(pieces):
    unit = lax.broadcasted_iota(jnp.int32, pieces[0].shape, 1) >> (UNIT.bit_length() - 1)
    cur = list(pieces)
    k = UNITS // 2
    while k:
        high = (unit & k) != 0
        nxt = list(cur)
        for a in range(UNITS):
            if a & k:
                continue
            lo_piece, hi_piece = cur[a], cur[a | k]
            nxt[a] = jnp.where(high, pltpu.roll(hi_piece, k * UNIT, 1), lo_piece)
            nxt[a | k] = jnp.where(high, hi_piece, pltpu.roll(lo_piece, LANES - k * UNIT, 1))
        cur = nxt
        k //= 2
    return cur


def _s5_prompt_kernel(u_ref, perm_ref, toep_ref, wre_ref, wim_ref, vre_ref, vim_ref, pw_ref, d_ref,
                      y_ref, hre_ref, him_ref, u2, y2, s_re, s_im, c_re, c_im, y_nat, *, tt):
    j = pl.program_id(1)
    chunks = tt // S5_CHUNK
    sub_chunks = S5_SUB // S5_CHUNK

    @pl.when(j == 0)
    def _():
        c_re[...] = jnp.zeros_like(c_re)
        c_im[...] = jnp.zeros_like(c_im)

    def relayout_in(sub, carry):
        r0 = pl.multiple_of(sub * S5_SUB, S5_SUB)
        c0 = pl.multiple_of(sub * sub_chunks, sub_chunks)
        ub = u_ref[0, pl.ds(r0, S5_SUB), :].astype(BF16)
        xs = jnp.dot(perm_ref[...], ub, preferred_element_type=F32).astype(BF16)
        for tile in range(SSM_WIDTH // LANES):
            for half in range(S5_CHUNK // UNITS):
                pieces = [pltpu.bitcast(
                    xs[(half * UNITS + a) * sub_chunks:(half * UNITS + a + 1) * sub_chunks,
                       tile * LANES:(tile + 1) * LANES], jnp.uint32) for a in range(UNITS)]
                outs = _unit_transpose(pieces)
                for gl in range(UNITS):
                    u2[tile * UNITS + gl, pl.ds(c0, sub_chunks), half * LANES:(half + 1) * LANES] = (
                        pltpu.bitcast(outs[gl], BF16))
        return carry

    lax.fori_loop(0, tt // S5_SUB, relayout_in, 0, unroll=2)

    s_re[0:SUBLANES, :] = c_re[...]
    s_im[0:SUBLANES, :] = c_im[...]
    for pair in range(S5_PAIRS):
        sl = slice(pair * LANES, (pair + 1) * LANES)
        ua = u2[2 * pair]
        ub = u2[2 * pair + 1]
        s_re[SUBLANES:SUBLANES + chunks, sl] = (
            jnp.dot(ua, wre_ref[2 * pair], preferred_element_type=F32)
            + jnp.dot(ub, wre_ref[2 * pair + 1], preferred_element_type=F32))
        s_im[SUBLANES:SUBLANES + chunks, sl] = (
            jnp.dot(ua, wim_ref[2 * pair], preferred_element_type=F32)
            + jnp.dot(ub, wim_ref[2 * pair + 1], preferred_element_type=F32))

    for chunk in range(SSM_STATES // S5_SCAN_LANES):
        sl = slice(chunk * S5_SCAN_LANES, (chunk + 1) * S5_SCAN_LANES)

        def body(r, carry, sl=sl):
            in_re, in_im = carry
            row = pl.multiple_of((r + 1) * SUBLANES, SUBLANES)
            re = s_re[pl.ds(row, SUBLANES), sl]
            im = s_im[pl.ds(row, SUBLANES), sl]
            for level, shift in enumerate((1, 2, 4)):
                a_re = pw_ref[2 * level, :, sl]
                a_im = pw_ref[2 * level + 1, :, sl]
                sh_re = pltpu.roll(re, shift, 0)
                sh_im = pltpu.roll(im, shift, 0)
                re, im = (re + a_re * sh_re - a_im * sh_im,
                          im + a_re * sh_im + a_im * sh_re)
            p_re = pw_ref[6, :, sl]
            p_im = pw_ref[7, :, sl]
            re, im = (re + p_re * in_re - p_im * in_im,
                      im + p_re * in_im + p_im * in_re)
            s_re[pl.ds(row, SUBLANES), sl] = re
            s_im[pl.ds(row, SUBLANES), sl] = im
            last = SUBLANES - 1
            return (jnp.broadcast_to(re[last:last + 1, :], re.shape),
                    jnp.broadcast_to(im[last:last + 1, :], im.shape))

        out_re, out_im = lax.fori_loop(0, chunks // SUBLANES, body, (c_re[:, sl], c_im[:, sl]))
        c_re[:, sl] = out_re
        c_im[:, sl] = out_im

    hre_ref[0] = c_re[...]
    him_ref[0] = c_im[...]

    for pair in range(S5_PAIRS):
        sl = slice(pair * LANES, (pair + 1) * LANES)
        h_re = s_re[SUBLANES - 1:SUBLANES - 1 + chunks, sl].astype(BF16)
        h_im = s_im[SUBLANES - 1:SUBLANES - 1 + chunks, sl].astype(BF16)
        carried = (jnp.dot(h_re, vre_ref[pair], preferred_element_type=F32)
                   + jnp.dot(h_im, vim_ref[pair], preferred_element_type=F32))
        for k in range(2):
            grp = 2 * pair + k
            y2[grp] = (jnp.dot(u2[grp], toep_ref[grp], preferred_element_type=F32)
                       + carried[:, k * S5_ROW:(k + 1) * S5_ROW])

    def relayout_out(blk, carry):
        r0 = pl.multiple_of(blk * S5_SUB, S5_SUB)
        c0 = pl.multiple_of(blk * sub_chunks, sub_chunks)
        for tile in range(SSM_WIDTH // LANES):
            ot = slice(tile * LANES, (tile + 1) * LANES)
            for half in range(S5_CHUNK // UNITS):
                pieces = [y2[tile * UNITS + gl, pl.ds(c0, sub_chunks), half * LANES:(half + 1) * LANES]
                          for gl in range(UNITS)]
                outs = _unit_transpose(pieces)
                for a in range(UNITS):
                    y_nat[tile, pl.ds(half * UNITS + a, sub_chunks, stride=S5_CHUNK), :] = outs[a]
            y_ref[0, pl.ds(r0, S5_SUB), ot] = y_nat[tile] + d_ref[:, ot] * u_ref[0, pl.ds(r0, S5_SUB), ot]
        return carry

    lax.fori_loop(0, tt // S5_SUB, relayout_out, 0)


def _s5_prompt(u, mats, tt):
    b, t, _ = u.shape
    chunks = tt // S5_CHUNK
    state = jax.ShapeDtypeStruct((b, SUBLANES, SSM_STATES), F32)
    state_spec = pl.BlockSpec((1, SUBLANES, SSM_STATES), lambda i, j: (i, 0, 0))
    return pl.pallas_call(
        functools.partial(_s5_prompt_kernel, tt=tt),
        grid=(b, t // tt),
        in_specs=[pl.BlockSpec((1, tt, SSM_WIDTH), lambda i, j: (i, j, 0))]
        + [_const_spec(a.shape) for a in mats],
        out_specs=[pl.BlockSpec((1, tt, SSM_WIDTH), lambda i, j: (i, j, 0)), state_spec, state_spec],
        out_shape=[jax.ShapeDtypeStruct((b, t, SSM_WIDTH), F32), state, state],
        scratch_shapes=[pltpu.VMEM((SSM_GROUPS, chunks, S5_ROW), BF16),
                        pltpu.VMEM((SSM_GROUPS, chunks, S5_ROW), F32),
                        pltpu.VMEM((SUBLANES + chunks, SSM_STATES), F32),
                        pltpu.VMEM((SUBLANES + chunks, SSM_STATES), F32),
                        pltpu.VMEM((SUBLANES, SSM_STATES), F32), pltpu.VMEM((SUBLANES, SSM_STATES), F32),
                        pltpu.VMEM((SSM_WIDTH // LANES, S5_SUB, LANES), F32)],
        compiler_params=_cparams(2),
        name="s5_prompt",
    )(u, *mats)


def _s5_sample_kernel(u_ref, h0re_ref, h0im_ref, bre_ref, bim_ref, cre_ref, cim_ref, lam_ref, d_ref,
                      y_ref, hre_ref, him_ref, *, steps):
    h_re = h0re_ref[...]
    h_im = h0im_ref[...]
    lam_re = lam_ref[0:1, :]
    lam_im = lam_ref[1:2, :]
    for t in range(steps):
        u = u_ref[t]
        bu_re, bu_im = _s5_input(u.astype(BF16), bre_ref, bim_ref)
        bu_re = jnp.concatenate(bu_re, axis=1)
        bu_im = jnp.concatenate(bu_im, axis=1)
        h_re, h_im = (lam_re * h_re - lam_im * h_im + bu_re,
                      lam_re * h_im + lam_im * h_re + bu_im)
        hb_re = h_re.astype(BF16)
        hb_im = h_im.astype(BF16)
        for tile in range(SSM_WIDTH // S5_OUT_TILE):
            st = slice(tile * S5_STATE_TILE, (tile + 1) * S5_STATE_TILE)
            ot = slice(tile * S5_OUT_TILE, (tile + 1) * S5_OUT_TILE)
            y = _s5_output(hb_re[:, st], hb_im[:, st], cre_ref, cim_ref, tile)
            y_ref[t, :, ot] = y + d_ref[:, ot] * u[:, ot]
    hre_ref[...] = h_re
    him_ref[...] = h_im


def _s5_sample(u_tb, h0_re, h0_im, mats, lam):
    steps, n, _ = u_tb.shape
    bre, bim, cre, cim, d_skip = mats
    state = jax.ShapeDtypeStruct((n, SSM_STATES), F32)
    args = (u_tb, h0_re, h0_im, bre, bim, cre, cim, lam, d_skip)
    return pl.pallas_call(
        functools.partial(_s5_sample_kernel, steps=steps),
        grid=(1,),
        in_specs=[_const_spec(a.shape) for a in args],
        out_specs=[_const_spec(u_tb.shape), _const_spec((n, SSM_STATES)), _const_spec((n, SSM_STATES))],
        out_shape=[jax.ShapeDtypeStruct(u_tb.shape, F32), state, state],
        compiler_params=_cparams(1),
        name="s5_sample",
    )(*args)


def _s5_matrices(lam_re, lam_im, log_dt, b_re, b_im, c_re, c_im, d_skip):
    lam = lax.complex(lam_re, lam_im)
    dt = jnp.exp(log_dt)[:, None]
    lam_bar = jnp.exp(lam * dt)
    b_bar = ((lam_bar - 1.0) / lam)[..., None] * lax.complex(b_re, b_im)
    eye_half = jnp.eye(SSM_GROUPS // 2, dtype=F32)

    def in_blocks(x):
        x = x.reshape(2, SSM_GROUPS // 2, SSM_STATE, SSM_GROUP)
        return jnp.einsum("kgpi,gh->kgihp", x, eye_half).reshape(2, S5_IN_HALF, S5_STATE_HALF).astype(BF16)

    tiles = SSM_WIDTH // S5_OUT_TILE
    groups_per_tile = SSM_GROUPS // tiles
    eye_tile = jnp.eye(groups_per_tile, dtype=F32)

    def out_blocks(x):
        x = x.reshape(tiles, groups_per_tile, SSM_GROUP, SSM_STATE)
        return jnp.einsum("kgop,gh->kgpho", x, eye_tile).reshape(tiles, S5_STATE_TILE, S5_OUT_TILE).astype(BF16)

    lam_flat = lam_bar.reshape(1, SSM_STATES)
    lam_rows = jnp.concatenate([lam_flat.real, lam_flat.imag], axis=0).astype(F32)
    d_row = d_skip.reshape(1, SSM_WIDTH)
    sample_mats = (in_blocks(b_bar.real), in_blocks(b_bar.imag), out_blocks(c_re), out_blocks(c_im), d_row)

    c_mat = lax.complex(c_re, c_im)
    lam_pow = jnp.concatenate([jnp.ones((1,) + lam_bar.shape, lam_bar.dtype),
                               jnp.cumprod(jnp.broadcast_to(lam_bar, (S5_CHUNK,) + lam_bar.shape), axis=0)])
    taps = jnp.einsum("gop,kgp,gpi->gkoi", c_mat, lam_pow[:S5_CHUNK], b_bar,
                      precision=lax.Precision.HIGHEST).real
    src = jnp.arange(S5_CHUNK)[None, :, None]
    dst = jnp.arange(S5_CHUNK)[None, None, :]
    lag_is = (dst - src == jnp.arange(S5_CHUNK)[:, None, None]).astype(F32)
    toep = jnp.einsum("kst,gkoi->gsito", lag_is, taps, precision=lax.Precision.HIGHEST)
    toep = toep.reshape(SSM_GROUPS, S5_ROW, S5_ROW).astype(BF16)
    w_state = jnp.einsum("sgp,gpi->gsip", lam_pow[S5_CHUNK - 1::-1][:S5_CHUNK], b_bar)
    w_state = w_state.reshape(SSM_GROUPS, S5_ROW, SSM_STATE)
    second = (jnp.arange(SSM_GROUPS) % 2 == 1)[:, None, None]
    zeros = jnp.zeros_like(w_state.real)

    def pair_cols(x):
        return jnp.where(second, jnp.concatenate([zeros, x], -1), jnp.concatenate([x, zeros], -1)).astype(BF16)

    v_out = jnp.einsum("gop,tgp->gpto", c_mat, lam_pow[1:]).reshape(SSM_GROUPS, SSM_STATE, S5_ROW)
    eye_pair = jnp.eye(2, dtype=F32)

    def pair_blocks(x):
        x = x.reshape(S5_PAIRS, 2, SSM_STATE, S5_ROW)
        return jnp.einsum("kapc,ab->kapbc", x, eye_pair).reshape(S5_PAIRS, LANES, 2 * S5_ROW).astype(BF16)

    lam_chunk = lam_pow[S5_CHUNK].reshape(1, SSM_STATES)
    rows = jnp.arange(SUBLANES)[:, None]
    planes = []
    for shift in (1, 2, 4):
        a = jnp.where(rows >= shift, lam_chunk ** shift, 0.0)
        planes += [a.real, a.imag]
    carry = jnp.cumprod(jnp.broadcast_to(lam_chunk, (SUBLANES, SSM_STATES)), axis=0)
    planes += [carry.real, carry.imag]
    powers = jnp.stack(planes).astype(F32)
    pos = jnp.arange(S5_SUB)
    perm = (jnp.arange(S5_SUB)[None, :] == ((pos % (S5_SUB // S5_CHUNK)) * S5_CHUNK
                                            + pos // (S5_SUB // S5_CHUNK))[:, None]).astype(BF16)
    prompt_mats = (perm, toep, pair_cols(w_state.real), pair_cols(w_state.imag),
                   pair_blocks(v_out.real), pair_blocks(-v_out.imag), powers, d_row)
    return prompt_mats, sample_mats, lam_rows


def _ab_out_kernel(yp_ref, ys_ref, wglu_ref, bglu_ref, wout_ref, h_ref, g_ref, o_ref):
    z = jax.nn.gelu(ys_ref[...])
    gate = jnp.dot(z.astype(BF16), wglu_ref[...], preferred_element_type=F32) + bglu_ref[...]
    y_ssm = z * jax.nn.sigmoid(gate)
    y = (jnp.dot(yp_ref[...], wout_ref[0:POOL_WIDTH, :], preferred_element_type=F32)
         + jnp.dot(y_ssm.astype(BF16), wout_ref[POOL_WIDTH:, :], preferred_element_type=F32))
    o_ref[...] = h_ref[...] + _rms(y, g_ref[...])


def _ab_out(y_pool, y_ssm, w_glu, b_glu, w_out, h, g, tm):
    m = h.shape[0]
    row = lambda width: pl.BlockSpec((tm, width), lambda i: (i, 0))
    return pl.pallas_call(
        _ab_out_kernel,
        grid=(m // tm,),
        in_specs=[row(POOL_WIDTH), row(SSM_WIDTH), _const_spec(w_glu.shape), _const_spec(b_glu.shape),
                  _const_spec(w_out.shape), row(D_MODEL), _const_spec(g.shape)],
        out_specs=row(D_MODEL),
        out_shape=jax.ShapeDtypeStruct((m, D_MODEL), F32),
        compiler_params=_cparams(1),
        name="ab_out",
    )(y_pool, y_ssm, w_glu, b_glu, w_out, h, g)


def _ffn_kernel(h_ref, gin_ref, wg_ref, wu_ref, wd_ref, gout_ref, o_ref):
    h = h_ref[...]
    xb = _rms(h, gin_ref[...]).astype(BF16)
    gate = jnp.dot(xb, wg_ref[...], preferred_element_type=F32)
    up = jnp.dot(xb, wu_ref[...], preferred_element_type=F32)
    act = (gate * jax.nn.sigmoid(gate) * up).astype(BF16)
    y = jnp.dot(act, wd_ref[...], preferred_element_type=F32)
    o_ref[...] = h + _rms(y, gout_ref[...])


def _ffn(h, g_in, w_gate, w_up, w_down, g_out, tm):
    m = h.shape[0]
    row = pl.BlockSpec((tm, D_MODEL), lambda i: (i, 0))
    return pl.pallas_call(
        _ffn_kernel,
        grid=(m // tm,),
        in_specs=[row, _const_spec(g_in.shape), _const_spec(w_gate.shape), _const_spec(w_up.shape),
                  _const_spec(w_down.shape), _const_spec(g_out.shape)],
        out_specs=row,
        out_shape=jax.ShapeDtypeStruct((m, D_MODEL), F32),
        compiler_params=_cparams(1),
        name="ffn",
    )(h, g_in, w_gate, w_up, w_down, g_out)


def _attn_prompt_kernel(q_ref, kc_ref, kp_ref, vc_ref, vp_ref, o_ref, lse_ref, *, parts):
    blk = pl.program_id(2)
    rows = 2 * ATT_BLOCK
    rpp = ATT_BLOCK // parts
    shift = rpp.bit_length() - 1
    qi = lax.broadcasted_iota(jnp.int32, (rows, 2 * ATT_BLOCK), 0) & (ATT_BLOCK - 1)
    kj = lax.broadcasted_iota(jnp.int32, (rows, 2 * ATT_BLOCK), 1)
    k_in = kj & (ATT_BLOCK - 1)
    is_cur = kj >> (ATT_BLOCK.bit_length() - 1)
    q_row = rpp + (qi & (rpp - 1))
    k_row = is_cur * rpp + (k_in & (rpp - 1))
    dist = parts * (q_row - k_row) + ((qi >> shift) - (k_in >> shift))
    valid = (dist >= 0) & (dist <= ATT_SPAN) & ((kj >= ATT_BLOCK) | (blk > 0))
    lane = lax.broadcasted_iota(jnp.int32, (ATT_BLOCK, LANES), 1)
    first_head = lane < HEAD_DIM
    lse_all = jnp.zeros((ATT_BLOCK, LANES), F32)
    gather = lambda ref, sl: jnp.concatenate([ref[c, :, sl] for c in range(parts)], axis=0)
    for pair in range(ATT_HEADS // 2):
        sl = slice(pair * LANES, (pair + 1) * LANES)
        q2 = gather(q_ref, sl)
        zero = jnp.zeros_like(q2)
        qs = jnp.concatenate([jnp.where(first_head, q2, zero), jnp.where(first_head, zero, q2)], axis=0)
        k2 = jnp.concatenate([gather(kp_ref, sl), gather(kc_ref, sl)], axis=0)
        v2 = jnp.concatenate([gather(vp_ref, sl), gather(vc_ref, sl)], axis=0)
        s = lax.dot_general(qs, k2, (((1,), (1,)), ((), ())), preferred_element_type=F32)
        s = jnp.where(valid, s, -jnp.inf)
        m = jnp.max(s, axis=-1, keepdims=True)
        p = jnp.exp(s - m)
        den = jnp.sum(p, axis=-1, keepdims=True)
        o = jnp.dot(p.astype(BF16), v2, preferred_element_type=F32) / den
        lse = m + jnp.log(den)
        o_pair = jnp.where(first_head, o[:ATT_BLOCK], o[ATT_BLOCK:]).astype(o_ref.dtype)
        for c in range(parts):
            o_ref[c, :, sl] = o_pair[c * rpp:(c + 1) * rpp]
        lse_all = jnp.where(lane == 2 * pair, lse[:ATT_BLOCK],
                            jnp.where(lane == 2 * pair + 1, lse[ATT_BLOCK:], lse_all))
    for c in range(parts):
        lse_ref[c] = lse_all[c * rpp:(c + 1) * rpp]


def _attn_prompt(q, k, v, parts):
    batch, n_streams, sub, _ = q.shape
    pat_streams = n_streams // parts
    rpp = ATT_BLOCK // parts
    nblk = sub // rpp
    view = lambda a: a.reshape(batch, parts, pat_streams, sub, a.shape[-1])
    cur = pl.BlockSpec((None, parts, None, rpp, ATT_WIDTH), lambda b, r, i: (b, 0, r, i, 0))
    prev = pl.BlockSpec((None, parts, None, rpp, ATT_WIDTH), lambda b, r, i: (b, 0, r, jnp.maximum(i - 1, 0), 0))
    o, lse = pl.pallas_call(
        functools.partial(_attn_prompt_kernel, parts=parts),
        grid=(batch, pat_streams, nblk),
        in_specs=[cur, cur, prev, cur, prev],
        out_specs=[cur, pl.BlockSpec((None, parts, None, rpp, LANES), lambda b, r, i: (b, 0, r, i, 0))],
        out_shape=[jax.ShapeDtypeStruct((batch, parts, pat_streams, sub, ATT_WIDTH), BF16),
                   jax.ShapeDtypeStruct((batch, parts, pat_streams, sub, LANES), F32)],
        compiler_params=_cparams(3),
        name="attn_prompt",
    )(view(q), view(k), view(k), view(v), view(v))
    return o.reshape(batch, n_streams, sub, ATT_WIDTH), lse.reshape(batch, n_streams, sub, LANES)


def _pattern_multiplicity(dist):
    mult = jnp.zeros(dist.shape, F32)
    for window, dil in DILATED_PATTERNS:
        hit = (dist >= 0) & (dist <= window) & ((dist & (dil - 1)) == 0)
        mult = mult + hit.astype(F32)
    return mult


def _attn_sample_kernel(q_ref, knew_ref, vnew_ref, kt_ref, vt_ref, o_ref, *, steps):
    t_cache = lax.broadcasted_iota(jnp.int32, (SUBLANES, MAX_WINDOW), 0)
    pos = lax.broadcasted_iota(jnp.int32, (SUBLANES, MAX_WINDOW), 1)
    mult = _pattern_multiplicity(MAX_WINDOW + t_cache - pos)
    t_new = lax.broadcasted_iota(jnp.int32, (SUBLANES, LANES), 0)
    j_new = lax.broadcasted_iota(jnp.int32, (SUBLANES, LANES), 1)
    mult_new = jnp.where(j_new < steps, _pattern_multiplicity(t_new - j_new), 0.0)
    nt = (((1,), (1,)), ((), ()))
    for head in range(ATT_HEADS):
        q = q_ref[0, head].astype(BF16)
        s = jnp.dot(q, kt_ref[0, head].astype(BF16), preferred_element_type=F32)
        s_new = lax.dot_general(q, knew_ref[0, head].astype(BF16), nt, preferred_element_type=F32)
        s = jnp.where(mult > 0.0, s, -jnp.inf)
        s_new = jnp.where(mult_new > 0.0, s_new, -jnp.inf)
        m = jnp.maximum(jnp.max(s, axis=-1, keepdims=True), jnp.max(s_new, axis=-1, keepdims=True))
        p = mult * jnp.exp(s - m)
        p_new = mult_new * jnp.exp(s_new - m)
        den = jnp.sum(p, axis=-1, keepdims=True) + jnp.sum(p_new, axis=-1, keepdims=True)
        o = lax.dot_general(p.astype(BF16), vt_ref[0, head].astype(BF16), nt, preferred_element_type=F32)
        o = o + jnp.dot(p_new.astype(BF16), vnew_ref[0, head].astype(BF16), preferred_element_type=F32)
        o_ref[0, :, head * HEAD_DIM:(head + 1) * HEAD_DIM] = (o / den)[:steps]


def _attn_sample(q, k_new, v_new, cache_kt, cache_vt, steps):
    n = q.shape[0]
    assert steps <= SUBLANES
    per = lambda a: pl.BlockSpec((1,) + a.shape[1:], lambda b: (b, 0, 0, 0))
    args = (q, k_new, v_new, cache_kt, cache_vt)
    return pl.pallas_call(
        functools.partial(_attn_sample_kernel, steps=steps),
        grid=(n,),
        in_specs=[per(a) for a in args],
        out_specs=pl.BlockSpec((1, steps, ATT_WIDTH), lambda b: (b, 0, 0)),
        out_shape=jax.ShapeDtypeStruct((n, steps, ATT_WIDTH), F32),
        compiler_params=_cparams(1),
        name="attn_sample",
    )(*args)


def _attn_out_kernel(*refs, tm):
    n_pat = len(PATTERN_ORDER)
    o_refs = refs[:n_pat]
    lse_refs = refs[n_pat:2 * n_pat]
    w_ref, expand_ref, h_ref, g_ref, out_ref, o_nat, lse_nat = refs[2 * n_pat:]
    for pat, dil in enumerate(PATTERN_ORDER):
        rows = tm // dil
        for res in range(dil):
            dst = pl.ds(res, rows, stride=dil) if dil > 1 else pl.ds(0, rows)
            lse_nat[pat, dst, :] = lse_refs[pat][0, res]
            for c in range(COL_TILES):
                o_nat[pat * COL_TILES + c, dst, :] = o_refs[pat][0, res, :, c * LANES:(c + 1) * LANES].astype(F32)
    lses = [lse_nat[pat] for pat in range(n_pat)]
    top = functools.reduce(jnp.maximum, lses)
    es = [jnp.exp(l - top) for l in lses]
    total = functools.reduce(lambda x, y: x + y, es)
    spread = []
    for e in es[:-1]:
        wgt = e / total
        hi = wgt.astype(BF16)
        lo = (wgt - hi.astype(F32)).astype(BF16)
        spread.append(jnp.dot(hi, expand_ref[...], preferred_element_type=F32)
                      + jnp.dot(lo, expand_ref[...], preferred_element_type=F32))
    spread.append(1.0 - functools.reduce(lambda x, y: x + y, spread))
    pieces = []
    for c in range(COL_TILES):
        acc = None
        for pat in range(n_pat):
            term = spread[pat][:, c * LANES:(c + 1) * LANES] * o_nat[pat * COL_TILES + c]
            acc = term if acc is None else acc + term
        pieces.append(acc.astype(BF16))
    a = jnp.concatenate(pieces, axis=1)
    y = jnp.dot(a, w_ref[...], preferred_element_type=F32)
    out_ref[...] = h_ref[...] + _rms(y, g_ref[...])


def _attn_out(outs, lses, w_o, h, g, seq, tm):
    m = h.shape[0]
    tiles = seq // tm
    n_pat = len(PATTERN_ORDER)

    def dil_spec(dil, width):
        return pl.BlockSpec((1, dil, tm // dil, width), lambda i: (i // tiles, 0, i % tiles, 0))

    row = pl.BlockSpec((tm, D_MODEL), lambda i: (i, 0))
    expand = (jnp.arange(LANES)[:, None] == jnp.arange(ATT_WIDTH)[None, :] // HEAD_DIM).astype(BF16)
    return pl.pallas_call(
        functools.partial(_attn_out_kernel, tm=tm),
        grid=(m // tm,),
        in_specs=[dil_spec(dil, ATT_WIDTH) for dil in PATTERN_ORDER]
        + [dil_spec(dil, LANES) for dil in PATTERN_ORDER]
        + [_const_spec(w_o.shape), _const_spec(expand.shape), row, _const_spec(g.shape)],
        out_specs=row,
        out_shape=jax.ShapeDtypeStruct((m, D_MODEL), F32),
        scratch_shapes=[pltpu.VMEM((n_pat * COL_TILES, tm, LANES), F32), pltpu.VMEM((n_pat, tm, LANES), F32)],
        compiler_params=_cparams(1),
        name="attn_out",
    )(*outs, *lses, w_o, expand, h, g)


def _proj_out_kernel(a_ref, w_ref, h_ref, g_ref, out_ref):
    y = jnp.dot(a_ref[...].astype(BF16), w_ref[...], preferred_element_type=F32)
    out_ref[...] = h_ref[...] + _rms(y, g_ref[...])


def _proj_out(a, w, h, g, tm):
    m = h.shape[0]
    return pl.pallas_call(
        _proj_out_kernel,
        grid=(m // tm,),
        in_specs=[pl.BlockSpec((tm, a.shape[1]), lambda i: (i, 0)), _const_spec(w.shape),
                  pl.BlockSpec((tm, D_MODEL), lambda i: (i, 0)), _const_spec(g.shape)],
        out_specs=pl.BlockSpec((tm, D_MODEL), lambda i: (i, 0)),
        out_shape=jax.ShapeDtypeStruct((m, D_MODEL), F32),
        compiler_params=_cparams(1),
        name="proj_out",
    )(a, w, h, g)


def kernel(x_prompt, x_sample, state_pool, state_s5, cache_k, cache_v, norm_gains, ab_w_in, ab_pool_w, ab_pool_scale, ab_lambda_re, ab_lambda_im, ab_log_dt, ab_b_re, ab_b_im, ab_c_re, ab_c_im, ab_d, ab_w_glu, ab_b_glu, ab_w_out, c_w_qkv, c_w_o, ffn_w_gate, ffn_w_up, ffn_w_down):
    batch, seq, d = x_prompt.shape
    n_dec, t_dec, _ = x_sample.shape
    mp, ms = batch * seq, n_dec * t_dec
    tm_p = 256
    tm_ffn = 512
    gains = norm_gains.reshape(norm_gains.shape[0], 4, 1, d)

    hp = x_prompt.reshape(mp, d)
    hs = x_sample.reshape(ms, d)

    g = gains[0]
    w_in = ab_w_in[0].astype(BF16)
    pool_w = ab_pool_w[0].astype(BF16)
    pool_scale = ab_pool_scale[0].reshape(1, POOL_WIDTH)
    lead_p = jnp.zeros((batch, POOL_LEAD, POOL_WIDTH), F32)
    lead_s = jnp.pad(state_pool[0], ((0, 0), (POOL_LEAD - POOL_STATE, 0), (0, 0)))
    ussm_p, ypool_p, pool_tail = _in_proj_pool(x_prompt, g[0], w_in, lead_p, pool_w, pool_scale, tm_ffn, 0)
    (proj_s,) = _norm_matmul(hs, g[0], w_in, ((0, d, 1.0),), (F32,), ms)
    proj_s = proj_s.reshape(n_dec, t_dec, d)
    ypool_s = _pool_mixer(proj_s, lead_s, pool_w, pool_scale, t_dec, PAST_LEN)
    pool_prompt = pool_tail[:, POOL_LEAD - POOL_STATE:][None]
    pool_sample = jnp.concatenate([state_pool[0], proj_s[:, :, :POOL_WIDTH]], axis=1)[:, -POOL_STATE:][None]

    prompt_mats, sample_mats, lam_rows = _s5_matrices(
        ab_lambda_re[0], ab_lambda_im[0], ab_log_dt[0], ab_b_re[0], ab_b_im[0], ab_c_re[0], ab_c_im[0], ab_d[0])
    yssm_p, hre_p, him_p = _s5_prompt(ussm_p, prompt_mats, 2048)
    u_tb = jnp.swapaxes(proj_s[:, :, POOL_WIDTH:], 0, 1)
    h0 = state_s5[0].reshape(n_dec, SSM_STATES, 2)
    yssm_tb, hre_s, him_s = _s5_sample(u_tb, h0[..., 0], h0[..., 1], sample_mats, lam_rows)
    yssm_s = jnp.swapaxes(yssm_tb, 0, 1)
    s5_prompt = jnp.stack([hre_p[:, 0], him_p[:, 0]], axis=-1).reshape(1, batch, SSM_GROUPS, SSM_STATE, 2)
    s5_sample = jnp.stack([hre_s, him_s], axis=-1).reshape(1, n_dec, SSM_GROUPS, SSM_STATE, 2)

    w_glu = ab_w_glu[0].astype(BF16)
    b_glu = ab_b_glu[0].reshape(1, SSM_WIDTH)
    w_out = ab_w_out[0].astype(BF16)
    hp = _ab_out(ypool_p.reshape(mp, POOL_WIDTH), yssm_p.reshape(mp, SSM_WIDTH), w_glu, b_glu, w_out, hp, g[1], tm_ffn)
    hs = _ab_out(ypool_s.reshape(ms, POOL_WIDTH), yssm_s.reshape(ms, SSM_WIDTH), w_glu, b_glu, w_out, hs, g[1], ms)

    wg, wu, wd = ffn_w_gate[0].astype(BF16), ffn_w_up[0].astype(BF16), ffn_w_down[0].astype(BF16)
    hp = _ffn(hp, g[2], wg, wu, wd, g[3], tm_ffn)
    hs = _ffn(hs, g[2], wg, wu, wd, g[3], ms)

    g = gains[1]
    w_qkv = c_w_qkv[0].astype(BF16)
    w_o = c_w_o[0].astype(BF16)
    keep = min(MAX_WINDOW, seq)
    *qkv_p, kt_tail, vt_tail = _qkv_prompt(hp, g[0], w_qkv, batch, seq, keep, tm_p)
    positions_first = lambda a: jnp.transpose(a.reshape(batch, ATT_HEADS, HEAD_DIM, keep), (0, 3, 1, 2))[None]
    k_prompt = positions_first(kt_tail)
    v_prompt = positions_first(vt_tail)
    q_nat, q_res, k_nat, k_res, v_nat, v_res = qkv_p
    outs, lses = [], []
    for q, k, v, parts in ((q_nat, k_nat, v_nat, 1), (q_res, k_res, v_res, MID_RATIO), (q_res, k_res, v_res, 1)):
        o, lse = _attn_prompt(q, k, v, parts)
        outs.append(o)
        lses.append(lse)
    hp = _attn_out(outs, lses, w_o, hp, g[1], seq, tm_p)

    scale = HEAD_DIM ** -0.5
    q_s, k_s, v_s = _norm_matmul(
        hs, g[0], w_qkv,
        ((0, ATT_WIDTH, scale), (ATT_WIDTH, ATT_WIDTH, 1.0), (2 * ATT_WIDTH, ATT_WIDTH, 1.0)),
        (F32, F32, F32), ms)
    k_sample = k_s.reshape(1, n_dec, t_dec, ATT_HEADS, HEAD_DIM)
    v_sample = v_s.reshape(1, n_dec, t_dec, ATT_HEADS, HEAD_DIM)
    head_major = lambda a, rows: jnp.pad(
        jnp.swapaxes(a.reshape(n_dec, t_dec, ATT_HEADS, HEAD_DIM), 1, 2),
        ((0, 0), (0, 0), (0, rows - t_dec), (0, 0)))
    cache_kt = jnp.transpose(cache_k[0], (0, 2, 3, 1))
    cache_vt = jnp.transpose(cache_v[0], (0, 2, 3, 1))
    o_s = _attn_sample(head_major(q_s, SUBLANES), head_major(k_s, LANES), head_major(v_s, LANES),
                       cache_kt, cache_vt, t_dec)
    hs = _proj_out(o_s.reshape(ms, ATT_WIDTH), w_o, hs, g[1], ms)

    wg, wu, wd = ffn_w_gate[1].astype(BF16), ffn_w_up[1].astype(BF16), ffn_w_down[1].astype(BF16)
    hp = _ffn(hp, g[2], wg, wu, wd, g[3], tm_ffn)
    hs = _ffn(hs, g[2], wg, wu, wd, g[3], ms)

    return (hp.reshape(batch, seq, d), hs.reshape(n_dec, t_dec, d), pool_prompt, s5_prompt,
            k_prompt, v_prompt, pool_sample, s5_sample, k_sample, v_sample)
```

```python
import functools
import math

import jax
import jax.numpy as jnp
from jax import lax
from jax.experimental import pallas as pl
from jax.experimental.pallas import tpu as pltpu

F32 = jnp.float32
BF16 = jnp.bfloat16

D_MODEL = 1024
PAST_LEN = 16384
POOL_WIDTH = 512
POOL_WINDOWS = (2, 4, 8, 16)
POOL_GROUP = 128
POOL_STATE = 15
POOL_LEAD = 16
SSM_WIDTH = 512
SSM_GROUP = 16
SSM_GROUPS = 32
SSM_STATE = 64
SSM_STATES = SSM_GROUPS * SSM_STATE
ATT_HEADS = 16
HEAD_DIM = 64
ATT_WIDTH = ATT_HEADS * HEAD_DIM
DILATED_PATTERNS = ((128, 1), (512, 4), (2048, 16))
MAX_WINDOW = 2048
ATT_BLOCK = 128
ATT_SPAN = 128
FFN_HIDDEN = 2816
RMS_EPS = 1e-6

SUBLANES = 8
LANES = 128
VMEM_LIMIT = 56 * 1024 * 1024


def _cparams(n_axes):
    return pltpu.CompilerParams(
        dimension_semantics=("arbitrary",) * n_axes, vmem_limit_bytes=VMEM_LIMIT)


def _rms(x, g):
    ms = jnp.mean(x * x, axis=-1, keepdims=True)
    return (x * lax.rsqrt(ms + RMS_EPS)) * g


def _const_spec(shape):
    zeros = (0,) * len(shape)
    return pl.BlockSpec(shape, lambda *_: zeros, pipeline_mode=pl.Buffered(1))


def _norm_matmul_kernel(x_ref, g_ref, w_ref, *out_refs, splits):
    xb = _rms(x_ref[...], g_ref[...]).astype(BF16)
    for (c0, width, scale), o_ref in zip(splits, out_refs):
        y = jnp.dot(xb, w_ref[:, c0:c0 + width], preferred_element_type=F32)
        if scale != 1.0:
            y = y * scale
        o_ref[...] = y.astype(o_ref.dtype)


def _norm_matmul(x, g, w, splits, dtypes, tm):
    m, d = x.shape
    n = w.shape[1]
    return pl.pallas_call(
        functools.partial(_norm_matmul_kernel, splits=splits),
        grid=(m // tm,),
        in_specs=[pl.BlockSpec((tm, d), lambda i: (i, 0)), _const_spec((1, d)), _const_spec((d, n))],
        out_specs=[pl.BlockSpec((tm, width), lambda i: (i, 0)) for _, width, _ in splits],
        out_shape=[jax.ShapeDtypeStruct((m, width), dt) for (_, width, _), dt in zip(splits, dtypes)],
        compiler_params=_cparams(1),
        name="norm_matmul",
    )(x, g, w)


ATT_DILATIONS = tuple(dil for _, dil in DILATED_PATTERNS)
ATT_ORDERS = (ATT_DILATIONS[0], ATT_DILATIONS[2])
PATTERN_ORDER = (ATT_DILATIONS[0], ATT_DILATIONS[2], ATT_DILATIONS[2])
MID_RATIO = ATT_DILATIONS[2] // ATT_DILATIONS[1]
COL_TILES = ATT_WIDTH // LANES


def _qkv_prompt_kernel(x_ref, g_ref, w_ref, perm_ref, *refs, q_scale, tm, tiles, first_tail):
    n_dil = len(ATT_ORDERS)
    out_refs = refs[:3 * n_dil]
    tail_refs = refs[3 * n_dil:]
    in_tail = (pl.program_id(0) % tiles) >= first_tail
    xb = _rms(x_ref[...], g_ref[...]).astype(BF16)
    for which in range(3):
        y = jnp.dot(xb, w_ref[:, which * ATT_WIDTH:(which + 1) * ATT_WIDTH], preferred_element_type=F32)
        if which == 0:
            y = y * q_scale
        else:
            @pl.when(in_tail)
            def _(y=y, tail_ref=tail_refs[which - 1]):
                tail_ref[0] = y.T
        yb = y.astype(BF16)
        for pat, (dil, o_ref) in enumerate(zip(ATT_ORDERS, out_refs[which * n_dil:(which + 1) * n_dil])):
            if dil == 1:
                o_ref[0, 0] = yb
                continue
            moved = jnp.dot(perm_ref[pat - 1], yb, preferred_element_type=F32).astype(BF16)
            rows = tm // dil
            for res in range(dil):
                o_ref[0, res] = moved[res * rows:(res + 1) * rows]


def _qkv_prompt(x, g, w, batch, seq, keep, tm):
    m, d = x.shape
    tiles = seq // tm
    first_tail = tiles - keep // tm

    def tail_map(i):
        return (i // tiles, 0, jnp.maximum(i % tiles - first_tail, 0))

    def dil_spec(dil):
        return pl.BlockSpec((1, dil, tm // dil, ATT_WIDTH), lambda i: (i // tiles, 0, i % tiles, 0))

    def dil_shape(dil):
        return jax.ShapeDtypeStruct((batch, dil, seq // dil, ATT_WIDTH), BF16)

    assert ATT_ORDERS[0] == 1
    src = jnp.arange(tm)[None, :]
    dst = jnp.arange(tm)[:, None]
    perm = jnp.stack([(src == (dst % (tm // dil)) * dil + dst // (tm // dil)).astype(BF16)
                      for dil in ATT_ORDERS[1:]])
    return pl.pallas_call(
        functools.partial(_qkv_prompt_kernel, q_scale=HEAD_DIM ** -0.5, tm=tm, tiles=tiles,
                          first_tail=first_tail),
        grid=(m // tm,),
        in_specs=[pl.BlockSpec((tm, d), lambda i: (i, 0)), _const_spec((1, d)),
                  _const_spec((d, 3 * ATT_WIDTH)), _const_spec(perm.shape)],
        out_specs=[dil_spec(dil) for _ in range(3) for dil in ATT_ORDERS]
        + [pl.BlockSpec((1, ATT_WIDTH, tm), tail_map)] * 2,
        out_shape=[dil_shape(dil) for _ in range(3) for dil in ATT_ORDERS]
        + [jax.ShapeDtypeStruct((batch, ATT_WIDTH, keep), F32)] * 2,
        compiler_params=_cparams(1),
        name="qkv_prompt",
    )(x, g, w, perm)


def _pool_tile(u, j, lead_ref, w_ref, scale_ref, y_ref, ext, *, tt, pos0, carry, seq=0):
    @pl.when(j == 0)
    def _():
        ext[0:POOL_LEAD, :] = lead_ref[seq]

    ext[POOL_LEAD:POOL_LEAD + tt, :] = u
    pos = pos0 + j * tt + lax.broadcasted_iota(jnp.int32, (tt, POOL_GROUP), 0)
    for grp, window in enumerate(POOL_WINDOWS):
        sl = slice(grp * POOL_GROUP, (grp + 1) * POOL_GROUP)
        u_g = u[:, sl]
        win_sum = u_g
        for k in range(1, window):
            win_sum = win_sum + ext[POOL_LEAD - k:POOL_LEAD - k + tt, sl]
        count = jnp.minimum(pos + 1, window).astype(F32)
        diff = win_sum / count - u_g
        y = jnp.dot(diff.astype(BF16), w_ref[grp], preferred_element_type=F32)
        y_ref[seq, :, sl] = (y * scale_ref[:, sl]).astype(y_ref.dtype)
    if carry:
        ext[0:POOL_LEAD, :] = ext[tt:tt + POOL_LEAD, :]


def _pool_kernel(u_ref, lead_ref, w_ref, scale_ref, y_ref, ext, *, tt, pos0, carry, n_seq):
    for seq in range(n_seq):
        _pool_tile(u_ref[seq], pl.program_id(1), lead_ref, w_ref, scale_ref, y_ref, ext,
                   tt=tt, pos0=pos0, carry=carry, seq=seq)


def _in_proj_pool_kernel(x_ref, g_ref, w_ref, lead_ref, pw_ref, scale_ref, ussm_ref, ypool_ref, tail_ref, ext,
                         *, tt, pos0):
    xb = _rms(x_ref[0], g_ref[...]).astype(BF16)
    ussm_ref[0] = jnp.dot(xb, w_ref[:, POOL_WIDTH:], preferred_element_type=F32)
    u_pool = jnp.dot(xb, w_ref[:, 0:POOL_WIDTH], preferred_element_type=F32)
    _pool_tile(u_pool, pl.program_id(1), lead_ref, pw_ref, scale_ref, ypool_ref, ext, tt=tt, pos0=pos0, carry=True)
    tail_ref[0] = ext[0:POOL_LEAD, :]


def _in_proj_pool(x, g, w, lead, pool_w, pool_scale, tt, pos0):
    n, t, d = x.shape
    seq_block = lambda width: pl.BlockSpec((1, tt, width), lambda b, j: (b, j, 0))
    per_seq = pl.BlockSpec((1, POOL_LEAD, POOL_WIDTH), lambda b, j: (b, 0, 0))
    return pl.pallas_call(
        functools.partial(_in_proj_pool_kernel, tt=tt, pos0=pos0),
        grid=(n, t // tt),
        in_specs=[seq_block(d), _const_spec(g.shape), _const_spec(w.shape), per_seq,
                  _const_spec(pool_w.shape), _const_spec(pool_scale.shape)],
        out_specs=[seq_block(SSM_WIDTH), seq_block(POOL_WIDTH), per_seq],
        out_shape=[jax.ShapeDtypeStruct((n, t, SSM_WIDTH), F32), jax.ShapeDtypeStruct((n, t, POOL_WIDTH), BF16),
                   jax.ShapeDtypeStruct((n, POOL_LEAD, POOL_WIDTH), F32)],
        scratch_shapes=[pltpu.VMEM((POOL_LEAD + tt, POOL_WIDTH), F32)],
        compiler_params=_cparams(2),
        name="in_proj_pool",
    )(x, g, w, lead, pool_w, pool_scale)


def _pool_mixer(proj, lead, w, scale, tt, pos0):
    n, t, _ = proj.shape
    steps = t // tt
    n_seq = 1 if steps > 1 else math.gcd(n, 8)
    return pl.pallas_call(
        functools.partial(_pool_kernel, tt=tt, pos0=pos0, carry=steps > 1, n_seq=n_seq),
        grid=(n // n_seq, steps),
        in_specs=[pl.BlockSpec((n_seq, tt, POOL_WIDTH), lambda b, j: (b, j, 0)),
                  pl.BlockSpec((n_seq, POOL_LEAD, POOL_WIDTH), lambda b, j: (b, 0, 0)),
                  _const_spec((len(POOL_WINDOWS), POOL_GROUP, POOL_GROUP)),
                  _const_spec((1, POOL_WIDTH))],
        out_specs=pl.BlockSpec((n_seq, tt, POOL_WIDTH), lambda b, j: (b, j, 0)),
        out_shape=jax.ShapeDtypeStruct((n, t, POOL_WIDTH), BF16),
        scratch_shapes=[pltpu.VMEM((POOL_LEAD + tt, POOL_WIDTH), F32)],
        compiler_params=_cparams(2),
        name="pool_mixer",
    )(proj, lead, w, scale)


S5_IN_HALF = SSM_WIDTH // 2
S5_STATE_HALF = SSM_STATES // 2
S5_OUT_TILE = LANES
S5_STATE_TILE = SSM_STATES // (SSM_WIDTH // S5_OUT_TILE)
S5_SCAN_LANES = 512


def _s5_input(ub, bre_ref, bim_ref):
    re, im = [], []
    for half in range(2):
        uk = ub[:, half * S5_IN_HALF:(half + 1) * S5_IN_HALF]
        re.append(jnp.dot(uk, bre_ref[half], preferred_element_type=F32))
        im.append(jnp.dot(uk, bim_ref[half], preferred_element_type=F32))
    return re, im


def _s5_output(h_re, h_im, cre_ref, cim_ref, tile):
    return (jnp.dot(h_re, cre_ref[tile], preferred_element_type=F32)
            - jnp.dot(h_im, cim_ref[tile], preferred_element_type=F32))


S5_CHUNK = 16
S5_ROW = S5_CHUNK * SSM_GROUP
S5_SUB = 256
UNIT = SSM_GROUP
UNITS = LANES // UNIT
S5_PAIRS = SSM_GROUPS // 2


def _unit_transpose(pieces):
    unit = lax.broadcasted_iota(jnp.int32, pieces[0].shape, 1) >> (UNIT.bit_length() - 1)
    cur = list(pieces)
    k = UNITS // 2
    while k:
        high = (unit & k) != 0
        nxt = list(cur)
        for a in range(UNITS):
            if a & k:
                continue
            lo_piece, hi_piece = cur[a], cur[a | k]
            nxt[a] = jnp.where(high, pltpu.roll(hi_piece, k * UNIT, 1), lo_piece)
            nxt[a | k] = jnp.where(high, hi_piece, pltpu.roll(lo_piece, LANES - k * UNIT, 1))
        cur = nxt
        k //= 2
    return cur


def _s5_prompt_kernel(u_ref, perm_ref, toep_ref, wre_ref, wim_ref, vre_ref, vim_ref, pw_ref, d_ref,
                      y_ref, hre_ref, him_ref, u2, y2, s_re, s_im, c_re, c_im, y_nat, *, tt):
    j = pl.program_id(1)
    chunks = tt // S5_CHUNK
    sub_chunks = S5_SUB // S5_CHUNK

    @pl.when(j == 0)
    def _():
        c_re[...] = jnp.zeros_like(c_re)
        c_im[...] = jnp.zeros_like(c_im)

    def relayout_in(sub, carry):
        r0 = pl.multiple_of(sub * S5_SUB, S5_SUB)
        c0 = pl.multiple_of(sub * sub_chunks, sub_chunks)
        ub = u_ref[0, pl.ds(r0, S5_SUB), :].astype(BF16)
        xs = jnp.dot(perm_ref[...], ub, preferred_element_type=F32).astype(BF16)
        for tile in range(SSM_WIDTH // LANES):
            for half in range(S5_CHUNK // UNITS):
                pieces = [pltpu.bitcast(
                    xs[(half * UNITS + a) * sub_chunks:(half * UNITS + a + 1) * sub_chunks,
                       tile * LANES:(tile + 1) * LANES], jnp.uint32) for a in range(UNITS)]
                outs = _unit_transpose(pieces)
                for gl in range(UNITS):
                    u2[tile * UNITS + gl, pl.ds(c0, sub_chunks), half * LANES:(half + 1) * LANES] = (
                        pltpu.bitcast(outs[gl], BF16))
        return carry

    lax.fori_loop(0, tt // S5_SUB, relayout_in, 0, unroll=2)

    s_re[0:SUBLANES, :] = c_re[...]
    s_im[0:SUBLANES, :] = c_im[...]
    for pair in range(S5_PAIRS):
        sl = slice(pair * LANES, (pair + 1) * LANES)
        ua = u2[2 * pair]
        ub = u2[2 * pair + 1]
        s_re[SUBLANES:SUBLANES + chunks, sl] = (
            jnp.dot(ua, wre_ref[2 * pair], preferred_element_type=F32)
            + jnp.dot(ub, wre_ref[2 * pair + 1], preferred_element_type=F32))
        s_im[SUBLANES:SUBLANES + chunks, sl] = (
            jnp.dot(ua, wim_ref[2 * pair], preferred_element_type=F32)
            + jnp.dot(ub, wim_ref[2 * pair + 1], preferred_element_type=F32))

    for chunk in range(SSM_STATES // S5_SCAN_LANES):
        sl = slice(chunk * S5_SCAN_LANES, (chunk + 1) * S5_SCAN_LANES)

        def body(r, carry, sl=sl):
            in_re, in_im = carry
            row = pl.multiple_of((r + 1) * SUBLANES, SUBLANES)
            re = s_re[pl.ds(row, SUBLANES), sl]
            im = s_im[pl.ds(row, SUBLANES), sl]
            for level, shift in enumerate((1, 2, 4)):
                a_re = pw_ref[2 * level, :, sl]
                a_im = pw_ref[2 * level + 1, :, sl]
                sh_re = pltpu.roll(re, shift, 0)
                sh_im = pltpu.roll(im, shift, 0)
                re, im = (re + a_re * sh_re - a_im * sh_im,
                          im + a_re * sh_im + a_im * sh_re)
            p_re = pw_ref[6, :, sl]
            p_im = pw_ref[7, :, sl]
            re, im = (re + p_re * in_re - p_im * in_im,
                      im + p_re * in_im + p_im * in_re)
            s_re[pl.ds(row, SUBLANES), sl] = re
            s_im[pl.ds(row, SUBLANES), sl] = im
            last = SUBLANES - 1
            return (jnp.broadcast_to(re[last:last + 1, :], re.shape),
                    jnp.broadcast_to(im[last:last + 1, :], im.shape))

        out_re, out_im = lax.fori_loop(0, chunks // SUBLANES, body, (c_re[:, sl], c_im[:, sl]))
        c_re[:, sl] = out_re
        c_im[:, sl] = out_im

    hre_ref[0] = c_re[...]
    him_ref[0] = c_im[...]

    for pair in range(S5_PAIRS):
        sl = slice(pair * LANES, (pair + 1) * LANES)
        h_re = s_re[SUBLANES - 1:SUBLANES - 1 + chunks, sl].astype(BF16)
        h_im = s_im[SUBLANES - 1:SUBLANES - 1 + chunks, sl].astype(BF16)
        carried = (jnp.dot(h_re, vre_ref[pair], preferred_element_type=F32)
                   + jnp.dot(h_im, vim_ref[pair], preferred_element_type=F32))
        for k in range(2):
            grp = 2 * pair + k
            y2[grp] = (jnp.dot(u2[grp], toep_ref[grp], preferred_element_type=F32)
                       + carried[:, k * S5_ROW:(k + 1) * S5_ROW])

    def relayout_out(blk, carry):
        r0 = pl.multiple_of(blk * S5_SUB, S5_SUB)
        c0 = pl.multiple_of(blk * sub_chunks, sub_chunks)
        for tile in range(SSM_WIDTH // LANES):
            ot = slice(tile * LANES, (tile + 1) * LANES)
            for half in range(S5_CHUNK // UNITS):
                pieces = [y2[tile * UNITS + gl, pl.ds(c0, sub_chunks), half * LANES:(half + 1) * LANES]
                          for gl in range(UNITS)]
                outs = _unit_transpose(pieces)
                for a in range(UNITS):
                    y_nat[tile, pl.ds(half * UNITS + a, sub_chunks, stride=S5_CHUNK), :] = outs[a]
            y_ref[0, pl.ds(r0, S5_SUB), ot] = y_nat[tile] + d_ref[:, ot] * u_ref[0, pl.ds(r0, S5_SUB), ot]
        return carry

    lax.fori_loop(0, tt // S5_SUB, relayout_out, 0)


def _s5_prompt(u, mats, tt):
    b, t, _ = u.shape
    chunks = tt // S5_CHUNK
    state = jax.ShapeDtypeStruct((b, SUBLANES, SSM_STATES), F32)
    state_spec = pl.BlockSpec((1, SUBLANES, SSM_STATES), lambda i, j: (i, 0, 0))
    return pl.pallas_call(
        functools.partial(_s5_prompt_kernel, tt=tt),
        grid=(b, t // tt),
        in_specs=[pl.BlockSpec((1, tt, SSM_WIDTH), lambda i, j: (i, j, 0))]
        + [_const_spec(a.shape) for a in mats],
        out_specs=[pl.BlockSpec((1, tt, SSM_WIDTH), lambda i, j: (i, j, 0)), state_spec, state_spec],
        out_shape=[jax.ShapeDtypeStruct((b, t, SSM_WIDTH), F32), state, state],
        scratch_shapes=[pltpu.VMEM((SSM_GROUPS, chunks, S5_ROW), BF16),
                        pltpu.VMEM((SSM_GROUPS, chunks, S5_ROW), F32),
                        pltpu.VMEM((SUBLANES + chunks, SSM_STATES), F32),
                        pltpu.VMEM((SUBLANES + chunks, SSM_STATES), F32),
                        pltpu.VMEM((SUBLANES, SSM_STATES), F32), pltpu.VMEM((SUBLANES, SSM_STATES), F32),
                        pltpu.VMEM((SSM_WIDTH // LANES, S5_SUB, LANES), F32)],
        compiler_params=_cparams(2),
        name="s5_prompt",
    )(u, *mats)


def _s5_sample_kernel(u_ref, h0re_ref, h0im_ref, bre_ref, bim_ref, cre_ref, cim_ref, lam_ref, d_ref,
                      y_ref, hre_ref, him_ref, *, steps):
    h_re = h0re_ref[...]
    h_im = h0im_ref[...]
    lam_re = lam_ref[0:1, :]
    lam_im = lam_ref[1:2, :]
    for t in range(steps):
        u = u_ref[t]
        bu_re, bu_im = _s5_input(u.astype(BF16), bre_ref, bim_ref)
        bu_re = jnp.concatenate(bu_re, axis=1)
        bu_im = jnp.concatenate(bu_im, axis=1)
        h_re, h_im = (lam_re * h_re - lam_im * h_im + bu_re,
                      lam_re * h_im + lam_im * h_re + bu_im)
        hb_re = h_re.astype(BF16)
        hb_im = h_im.astype(BF16)
        for tile in range(SSM_WIDTH // S5_OUT_TILE):
            st = slice(tile * S5_STATE_TILE, (tile + 1) * S5_STATE_TILE)
            ot = slice(tile * S5_OUT_TILE, (tile + 1) * S5_OUT_TILE)
            y = _s5_output(hb_re[:, st], hb_im[:, st], cre_ref, cim_ref, tile)
            y_ref[t, :, ot] = y + d_ref[:, ot] * u[:, ot]
    hre_ref[...] = h_re
    him_ref[...] = h_im


def _s5_sample(u_tb, h0_re, h0_im, mats, lam):
    steps, n, _ = u_tb.shape
    bre, bim, cre, cim, d_skip = mats
    state = jax.ShapeDtypeStruct((n, SSM_STATES), F32)
    args = (u_tb, h0_re, h0_im, bre, bim, cre, cim, lam, d_skip)
    return pl.pallas_call(
        functools.partial(_s5_sample_kernel, steps=steps),
        grid=(1,),
        in_specs=[_const_spec(a.shape) for a in args],
        out_specs=[_const_spec(u_tb.shape), _const_spec((n, SSM_STATES)), _const_spec((n, SSM_STATES))],
        out_shape=[jax.ShapeDtypeStruct(u_tb.shape, F32), state, state],
        compiler_params=_cparams(1),
        name="s5_sample",
    )(*args)


def _s5_matrices(lam_re, lam_im, log_dt, b_re, b_im, c_re, c_im, d_skip):
    lam = lax.complex(lam_re, lam_im)
    dt = jnp.exp(log_dt)[:, None]
    lam_bar = jnp.exp(lam * dt)
    b_bar = ((lam_bar - 1.0) / lam)[..., None] * lax.complex(b_re, b_im)
    eye_half = jnp.eye(SSM_GROUPS // 2, dtype=F32)

    def in_blocks(x):
        x = x.reshape(2, SSM_GROUPS // 2, SSM_STATE, SSM_GROUP)
        return jnp.einsum("kgpi,gh->kgihp", x, eye_half).reshape(2, S5_IN_HALF, S5_STATE_HALF).astype(BF16)

    tiles = SSM_WIDTH // S5_OUT_TILE
    groups_per_tile = SSM_GROUPS // tiles
    eye_tile = jnp.eye(groups_per_tile, dtype=F32)

    def out_blocks(x):
        x = x.reshape(tiles, groups_per_tile, SSM_GROUP, SSM_STATE)
        return jnp.einsum("kgop,gh->kgpho", x, eye_tile).reshape(tiles, S5_STATE_TILE, S5_OUT_TILE).astype(BF16)

    lam_flat = lam_bar.reshape(1, SSM_STATES)
    lam_rows = jnp.concatenate([lam_flat.real, lam_flat.imag], axis=0).astype(F32)
    d_row = d_skip.reshape(1, SSM_WIDTH)
    sample_mats = (in_blocks(b_bar.real), in_blocks(b_bar.imag), out_blocks(c_re), out_blocks(c_im), d_row)

    c_mat = lax.complex(c_re, c_im)
    lam_pow = jnp.concatenate([jnp.ones((1,) + lam_bar.shape, lam_bar.dtype),
                               jnp.cumprod(jnp.broadcast_to(lam_bar, (S5_CHUNK,) + lam_bar.shape), axis=0)])
    taps = jnp.einsum("gop,kgp,gpi->gkoi", c_mat, lam_pow[:S5_CHUNK], b_bar,
                      precision=lax.Precision.HIGHEST).real
    src = jnp.arange(S5_CHUNK)[None, :, None]
    dst = jnp.arange(S5_CHUNK)[None, None, :]
    lag_is = (dst - src == jnp.arange(S5_CHUNK)[:, None, None]).astype(F32)
    toep = jnp.einsum("kst,gkoi->gsito", lag_is, taps, precision=lax.Precision.HIGHEST)
    toep = toep.reshape(SSM_GROUPS, S5_ROW, S5_ROW).astype(BF16)
    w_state = jnp.einsum("sgp,gpi->gsip", lam_pow[S5_CHUNK - 1::-1][:S5_CHUNK], b_bar)
    w_state = w_state.reshape(SSM_GROUPS, S5_ROW, SSM_STATE)
    second = (jnp.arange(SSM_GROUPS) % 2 == 1)[:, None, None]
    zeros = jnp.zeros_like(w_state.real)

    def pair_cols(x):
        return jnp.where(second, jnp.concatenate([zeros, x], -1), jnp.concatenate([x, zeros], -1)).astype(BF16)

    v_out = jnp.einsum("gop,tgp->gpto", c_mat, lam_pow[1:]).reshape(SSM_GROUPS, SSM_STATE, S5_ROW)
    eye_pair = jnp.eye(2, dtype=F32)

    def pair_blocks(x):
        x = x.reshape(S5_PAIRS, 2, SSM_STATE, S5_ROW)
        return jnp.einsum("kapc,ab->kapbc", x, eye_pair).reshape(S5_PAIRS, LANES, 2 * S5_ROW).astype(BF16)

    lam_chunk = lam_pow[S5_CHUNK].reshape(1, SSM_STATES)
    rows = jnp.arange(SUBLANES)[:, None]
    planes = []
    for shift in (1, 2, 4):
        a = jnp.where(rows >= shift, lam_chunk ** shift, 0.0)
        planes += [a.real, a.imag]
    carry = jnp.cumprod(jnp.broadcast_to(lam_chunk, (SUBLANES, SSM_STATES)), axis=0)
    planes += [carry.real, carry.imag]
    powers = jnp.stack(planes).astype(F32)
    pos = jnp.arange(S5_SUB)
    perm = (jnp.arange(S5_SUB)[None, :] == ((pos % (S5_SUB // S5_CHUNK)) * S5_CHUNK
                                            + pos // (S5_SUB // S5_CHUNK))[:, None]).astype(BF16)
    prompt_mats = (perm, toep, pair_cols(w_state.real), pair_cols(w_state.imag),
                   pair_blocks(v_out.real), pair_blocks(-v_out.imag), powers, d_row)
    return prompt_mats, sample_mats, lam_rows


def _ab_out_kernel(yp_ref, ys_ref, wglu_ref, bglu_ref, wout_ref, h_ref, g_ref, o_ref):
    z = jax.nn.gelu(ys_ref[...])
    gate = jnp.dot(z.astype(BF16), wglu_ref[...], preferred_element_type=F32) + bglu_ref[...]
    y_ssm = z * jax.nn.sigmoid(gate)
    y = (jnp.dot(yp_ref[...], wout_ref[0:POOL_WIDTH, :], preferred_element_type=F32)
         + jnp.dot(y_ssm.astype(BF16), wout_ref[POOL_WIDTH:, :], preferred_element_type=F32))
    o_ref[...] = h_ref[...] + _rms(y, g_ref[...])


def _ab_out(y_pool, y_ssm, w_glu, b_glu, w_out, h, g, tm):
    m = h.shape[0]
    row = lambda width: pl.BlockSpec((tm, width), lambda i: (i, 0))
    return pl.pallas_call(
        _ab_out_kernel,
        grid=(m // tm,),
        in_specs=[row(POOL_WIDTH), row(SSM_WIDTH), _const_spec(w_glu.shape), _const_spec(b_glu.shape),
                  _const_spec(w_out.shape), row(D_MODEL), _const_spec(g.shape)],
        out_specs=row(D_MODEL),
        out_shape=jax.ShapeDtypeStruct((m, D_MODEL), F32),
        compiler_params=_cparams(1),
        name="ab_out",
    )(y_pool, y_ssm, w_glu, b_glu, w_out, h, g)


def _ffn_kernel(h_ref, gin_ref, wg_ref, wu_ref, wd_ref, gout_ref, o_ref):
    h = h_ref[...]
    xb = _rms(h, gin_ref[...]).astype(BF16)
    gate = jnp.dot(xb, wg_ref[...], preferred_element_type=F32)
    up = jnp.dot(xb, wu_ref[...], preferred_element_type=F32)
    act = (gate * jax.nn.sigmoid(gate) * up).astype(BF16)
    y = jnp.dot(act, wd_ref[...], preferred_element_type=F32)
    o_ref[...] = h + _rms(y, gout_ref[...])


def _ffn(h, g_in, w_gate, w_up, w_down, g_out, tm):
    m = h.shape[0]
    row = pl.BlockSpec((tm, D_MODEL), lambda i: (i, 0))
    return pl.pallas_call(
        _ffn_kernel,
        grid=(m // tm,),
        in_specs=[row, _const_spec(g_in.shape), _const_spec(w_gate.shape), _const_spec(w_up.shape),
                  _const_spec(w_down.shape), _const_spec(g_out.shape)],
        out_specs=row,
        out_shape=jax.ShapeDtypeStruct((m, D_MODEL), F32),
        compiler_params=_cparams(1),
        name="ffn",
    )(h, g_in, w_gate, w_up, w_down, g_out)


def _attn_prompt_kernel(q_ref, kc_ref, kp_ref, vc_ref, vp_ref, o_ref, lse_ref, *, parts):
    blk = pl.program_id(2)
    rows = 2 * ATT_BLOCK
    rpp = ATT_BLOCK // parts
    shift = rpp.bit_length() - 1
    qi = lax.broadcasted_iota(jnp.int32, (rows, 2 * ATT_BLOCK), 0) & (ATT_BLOCK - 1)
    kj = lax.broadcasted_iota(jnp.int32, (rows, 2 * ATT_BLOCK), 1)
    k_in = kj & (ATT_BLOCK - 1)
    is_cur = kj >> (ATT_BLOCK.bit_length() - 1)
    q_row = rpp + (qi & (rpp - 1))
    k_row = is_cur * rpp + (k_in & (rpp - 1))
    dist = parts * (q_row - k_row) + ((qi >> shift) - (k_in >> shift))
    valid = (dist >= 0) & (dist <= ATT_SPAN) & ((kj >= ATT_BLOCK) | (blk > 0))
    lane = lax.broadcasted_iota(jnp.int32, (ATT_BLOCK, LANES), 1)
    first_head = lane < HEAD_DIM
    lse_all = jnp.zeros((ATT_BLOCK, LANES), F32)
    gather = lambda ref, sl: jnp.concatenate([ref[c, :, sl] for c in range(parts)], axis=0)
    for pair in range(ATT_HEADS // 2):
        sl = slice(pair * LANES, (pair + 1) * LANES)
        q2 = gather(q_ref, sl)
        zero = jnp.zeros_like(q2)
        qs = jnp.concatenate([jnp.where(first_head, q2, zero), jnp.where(first_head, zero, q2)], axis=0)
        k2 = jnp.concatenate([gather(kp_ref, sl), gather(kc_ref, sl)], axis=0)
        v2 = jnp.concatenate([gather(vp_ref, sl), gather(vc_ref, sl)], axis=0)
        s = lax.dot_general(qs, k2, (((1,), (1,)), ((), ())), preferred_element_type=F32)
        s = jnp.where(valid, s, -jnp.inf)
        m = jnp.max(s, axis=-1, keepdims=True)
        p = jnp.exp(s - m)
        den = jnp.sum(p, axis=-1, keepdims=True)
        o = jnp.dot(p.astype(BF16), v2, preferred_element_type=F32) / den
        lse = m + jnp.log(den)
        o_pair = jnp.where(first_head, o[:ATT_BLOCK], o[ATT_BLOCK:]).astype(o_ref.dtype)
        for c in range(parts):
            o_ref[c, :, sl] = o_pair[c * rpp:(c + 1) * rpp]
        lse_all = jnp.where(lane == 2 * pair, lse[:ATT_BLOCK],
                            jnp.where(lane == 2 * pair + 1, lse[ATT_BLOCK:], lse_all))
    for c in range(parts):
        lse_ref[c] = lse_all[c * rpp:(c + 1) * rpp]


def _attn_prompt(q, k, v, parts):
    batch, n_streams, sub, _ = q.shape
    pat_streams = n_streams // parts
    rpp = ATT_BLOCK // parts
    nblk = sub // rpp
    view = lambda a: a.reshape(batch, parts, pat_streams, sub, a.shape[-1])
    cur = pl.BlockSpec((None, parts, None, rpp, ATT_WIDTH), lambda b, r, i: (b, 0, r, i, 0))
    prev = pl.BlockSpec((None, parts, None, rpp, ATT_WIDTH), lambda b, r, i: (b, 0, r, jnp.maximum(i - 1, 0), 0))
    o, lse = pl.pallas_call(
        functools.partial(_attn_prompt_kernel, parts=parts),
        grid=(batch, pat_streams, nblk),
        in_specs=[cur, cur, prev, cur, prev],
        out_specs=[cur, pl.BlockSpec((None, parts, None, rpp, LANES), lambda b, r, i: (b, 0, r, i, 0))],
        out_shape=[jax.ShapeDtypeStruct((batch, parts, pat_streams, sub, ATT_WIDTH), BF16),
                   jax.ShapeDtypeStruct((batch, parts, pat_streams, sub, LANES), F32)],
        compiler_params=_cparams(3),
        name="attn_prompt",
    )(view(q), view(k), view(k), view(v), view(v))
    return o.reshape(batch, n_streams, sub, ATT_WIDTH), lse.reshape(batch, n_streams, sub, LANES)


def _pattern_multiplicity(dist):
    mult = jnp.zeros(dist.shape, F32)
    for window, dil in DILATED_PATTERNS:
        hit = (dist >= 0) & (dist <= window) & ((dist & (dil - 1)) == 0)
        mult = mult + hit.astype(F32)
    return mult


def _attn_sample_kernel(q_ref, knew_ref, vnew_ref, kt_ref, vt_ref, o_ref, *, steps):
    t_cache = lax.broadcasted_iota(jnp.int32, (SUBLANES, MAX_WINDOW), 0)
    pos = lax.broadcasted_iota(jnp.int32, (SUBLANES, MAX_WINDOW), 1)
    mult = _pattern_multiplicity(MAX_WINDOW + t_cache - pos)
    t_new = lax.broadcasted_iota(jnp.int32, (SUBLANES, LANES), 0)
    j_new = lax.broadcasted_iota(jnp.int32, (SUBLANES, LANES), 1)
    mult_new = jnp.where(j_new < steps, _pattern_multiplicity(t_new - j_new), 0.0)
    nt = (((1,), (1,)), ((), ()))
    for head in range(ATT_HEADS):
        q = q_ref[0, head].astype(BF16)
        s = jnp.dot(q, kt_ref[0, head].astype(BF16), preferred_element_type=F32)
        s_new = lax.dot_general(q, knew_ref[0, head].astype(BF16), nt, preferred_element_type=F32)
        s = jnp.where(mult > 0.0, s, -jnp.inf)
        s_new = jnp.where(mult_new > 0.0, s_new, -jnp.inf)
        m = jnp.maximum(jnp.max(s, axis=-1, keepdims=True), jnp.max(s_new, axis=-1, keepdims=True))
        p = mult * jnp.exp(s - m)
        p_new = mult_new * jnp.exp(s_new - m)
        den = jnp.sum(p, axis=-1, keepdims=True) + jnp.sum(p_new, axis=-1, keepdims=True)
        o = lax.dot_general(p.astype(BF16), vt_ref[0, head].astype(BF16), nt, preferred_element_type=F32)
        o = o + jnp.dot(p_new.astype(BF16), vnew_ref[0, head].astype(BF16), preferred_element_type=F32)
        o_ref[0, :, head * HEAD_DIM:(head + 1) * HEAD_DIM] = (o / den)[:steps]


def _attn_sample(q, k_new, v_new, cache_kt, cache_vt, steps):
    n = q.shape[0]
    assert steps <= SUBLANES
    per = lambda a: pl.BlockSpec((1,) + a.shape[1:], lambda b: (b, 0, 0, 0))
    args = (q, k_new, v_new, cache_kt, cache_vt)
    return pl.pallas_call(
        functools.partial(_attn_sample_kernel, steps=steps),
        grid=(n,),
        in_specs=[per(a) for a in args],
        out_specs=pl.BlockSpec((1, steps, ATT_WIDTH), lambda b: (b, 0, 0)),
        out_shape=jax.ShapeDtypeStruct((n, steps, ATT_WIDTH), F32),
        compiler_params=_cparams(1),
        name="attn_sample",
    )(*args)


def _attn_out_kernel(*refs, tm):
    n_pat = len(PATTERN_ORDER)
    o_refs = refs[:n_pat]
    lse_refs = refs[n_pat:2 * n_pat]
    w_ref, expand_ref, unperm_ref, h_ref, g_ref, out_ref, lse_nat = refs[2 * n_pat:]
    o_nat = []
    for pat, dil in enumerate(PATTERN_ORDER):
        if dil == 1:
            lse_nat[pat] = lse_refs[pat][0, 0]
            o_nat.append(o_refs[pat][0, 0].astype(F32))
            continue
        rows = tm // dil
        for res in range(dil):
            lse_nat[pat, pl.ds(res, rows, stride=dil), :] = lse_refs[pat][0, res]
        stacked = jnp.concatenate([o_refs[pat][0, res] for res in range(dil)], axis=0)
        o_nat.append(jnp.dot(unperm_ref[...], stacked, preferred_element_type=F32))
    lses = [lse_nat[pat] for pat in range(n_pat)]
    top = functools.reduce(jnp.maximum, lses)
    es = [jnp.exp(l - top) for l in lses]
    total = functools.reduce(lambda x, y: x + y, es)
    spread = []
    for e in es[:-1]:
        wgt = e / total
        hi = wgt.astype(BF16)
        lo = (wgt - hi.astype(F32)).astype(BF16)
        spread.append(jnp.dot(jnp.concatenate([hi, lo], axis=1), expand_ref[...],
                              preferred_element_type=F32))
    spread.append(1.0 - functools.reduce(lambda x, y: x + y, spread))
    pieces = []
    for c in range(COL_TILES):
        acc = None
        for pat in range(n_pat):
            term = spread[pat][:, c * LANES:(c + 1) * LANES] * o_nat[pat][:, c * LANES:(c + 1) * LANES]
            acc = term if acc is None else acc + term
        pieces.append(acc.astype(BF16))
    a = jnp.concatenate(pieces, axis=1)
    y = jnp.dot(a, w_ref[...], preferred_element_type=F32)
    out_ref[...] = h_ref[...] + _rms(y, g_ref[...])


def _attn_out(outs, lses, w_o, h, g, seq, tm):
    m = h.shape[0]
    tiles = seq // tm
    n_pat = len(PATTERN_ORDER)

    def dil_spec(dil, width):
        return pl.BlockSpec((1, dil, tm // dil, width), lambda i: (i // tiles, 0, i % tiles, 0))

    row = pl.BlockSpec((tm, D_MODEL), lambda i: (i, 0))
    expand = (jnp.arange(LANES)[:, None] == jnp.arange(ATT_WIDTH)[None, :] // HEAD_DIM).astype(BF16)
    expand = jnp.concatenate([expand, expand], axis=0)
    order = ATT_ORDERS[1]
    assert all(dil in (1, order) for dil in PATTERN_ORDER)
    nat = jnp.arange(tm)
    unperm = (jnp.arange(tm)[None, :] == ((nat % order) * (tm // order) + nat // order)[:, None]).astype(BF16)
    return pl.pallas_call(
        functools.partial(_attn_out_kernel, tm=tm),
        grid=(m // tm,),
        in_specs=[dil_spec(dil, ATT_WIDTH) for dil in PATTERN_ORDER]
        + [dil_spec(dil, LANES) for dil in PATTERN_ORDER]
        + [_const_spec(w_o.shape), _const_spec(expand.shape), _const_spec(unperm.shape), row,
           _const_spec(g.shape)],
        out_specs=row,
        out_shape=jax.ShapeDtypeStruct((m, D_MODEL), F32),
        scratch_shapes=[pltpu.VMEM((n_pat, tm, LANES), F32)],
        compiler_params=_cparams(1),
        name="attn_out",
    )(*outs, *lses, w_o, expand, unperm, h, g)


def _proj_out_kernel(a_ref, w_ref, h_ref, g_ref, out_ref):
    y = jnp.dot(a_ref[...].astype(BF16), w_ref[...], preferred_element_type=F32)
    out_ref[...] = h_ref[...] + _rms(y, g_ref[...])


def _proj_out(a, w, h, g, tm):
    m = h.shape[0]
    return pl.pallas_call(
        _proj_out_kernel,
        grid=(m // tm,),
        in_specs=[pl.BlockSpec((tm, a.shape[1]), lambda i: (i, 0)), _const_spec(w.shape),
                  pl.BlockSpec((tm, D_MODEL), lambda i: (i, 0)), _const_spec(g.shape)],
        out_specs=pl.BlockSpec((tm, D_MODEL), lambda i: (i, 0)),
        out_shape=jax.ShapeDtypeStruct((m, D_MODEL), F32),
        compiler_params=_cparams(1),
        name="proj_out",
    )(a, w, h, g)


def kernel(x_prompt, x_sample, state_pool, state_s5, cache_k, cache_v, norm_gains, ab_w_in, ab_pool_w, ab_pool_scale, ab_lambda_re, ab_lambda_im, ab_log_dt, ab_b_re, ab_b_im, ab_c_re, ab_c_im, ab_d, ab_w_glu, ab_b_glu, ab_w_out, c_w_qkv, c_w_o, ffn_w_gate, ffn_w_up, ffn_w_down):
    batch, seq, d = x_prompt.shape
    n_dec, t_dec, _ = x_sample.shape
    mp, ms = batch * seq, n_dec * t_dec
    tm_p = 256
    tm_ffn = 512
    gains = norm_gains.reshape(norm_gains.shape[0], 4, 1, d)

    hp = x_prompt.reshape(mp, d)
    hs = x_sample.reshape(ms, d)

    g = gains[0]
    w_in = ab_w_in[0].astype(BF16)
    pool_w = ab_pool_w[0].astype(BF16)
    pool_scale = ab_pool_scale[0].reshape(1, POOL_WIDTH)
    lead_p = jnp.zeros((batch, POOL_LEAD, POOL_WIDTH), F32)
    lead_s = jnp.pad(state_pool[0], ((0, 0), (POOL_LEAD - POOL_STATE, 0), (0, 0)))
    ussm_p, ypool_p, pool_tail = _in_proj_pool(x_prompt, g[0], w_in, lead_p, pool_w, pool_scale, tm_ffn, 0)
    (proj_s,) = _norm_matmul(hs, g[0], w_in, ((0, d, 1.0),), (F32,), ms)
    proj_s = proj_s.reshape(n_dec, t_dec, d)
    ypool_s = _pool_mixer(proj_s, lead_s, pool_w, pool_scale, t_dec, PAST_LEN)
    pool_prompt = pool_tail[:, POOL_LEAD - POOL_STATE:][None]
    pool_sample = jnp.concatenate([state_pool[0], proj_s[:, :, :POOL_WIDTH]], axis=1)[:, -POOL_STATE:][None]

    prompt_mats, sample_mats, lam_rows = _s5_matrices(
        ab_lambda_re[0], ab_lambda_im[0], ab_log_dt[0], ab_b_re[0], ab_b_im[0], ab_c_re[0], ab_c_im[0], ab_d[0])
    yssm_p, hre_p, him_p = _s5_prompt(ussm_p, prompt_mats, 2048)
    u_tb = jnp.swapaxes(proj_s[:, :, POOL_WIDTH:], 0, 1)
    h0 = state_s5[0].reshape(n_dec, SSM_STATES, 2)
    yssm_tb, hre_s, him_s = _s5_sample(u_tb, h0[..., 0], h0[..., 1], sample_mats, lam_rows)
    yssm_s = jnp.swapaxes(yssm_tb, 0, 1)
    s5_prompt = jnp.stack([hre_p[:, 0], him_p[:, 0]], axis=-1).reshape(1, batch, SSM_GROUPS, SSM_STATE, 2)
    s5_sample = jnp.stack([hre_s, him_s], axis=-1).reshape(1, n_dec, SSM_GROUPS, SSM_STATE, 2)

    w_glu = ab_w_glu[0].astype(BF16)
    b_glu = ab_b_glu[0].reshape(1, SSM_WIDTH)
    w_out = ab_w_out[0].astype(BF16)
    hp = _ab_out(ypool_p.reshape(mp, POOL_WIDTH), yssm_p.reshape(mp, SSM_WIDTH), w_glu, b_glu, w_out, hp, g[1], tm_ffn)
    hs = _ab_out(ypool_s.reshape(ms, POOL_WIDTH), yssm_s.reshape(ms, SSM_WIDTH), w_glu, b_glu, w_out, hs, g[1], ms)

    wg, wu, wd = ffn_w_gate[0].astype(BF16), ffn_w_up[0].astype(BF16), ffn_w_down[0].astype(BF16)
    hp = _ffn(hp, g[2], wg, wu, wd, g[3], tm_ffn)
    hs = _ffn(hs, g[2], wg, wu, wd, g[3], ms)

    g = gains[1]
    w_qkv = c_w_qkv[0].astype(BF16)
    w_o = c_w_o[0].astype(BF16)
    keep = min(MAX_WINDOW, seq)
    *qkv_p, kt_tail, vt_tail = _qkv_prompt(hp, g[0], w_qkv, batch, seq, keep, tm_p)
    positions_first = lambda a: jnp.transpose(a.reshape(batch, ATT_HEADS, HEAD_DIM, keep), (0, 3, 1, 2))[None]
    k_prompt = positions_first(kt_tail)
    v_prompt = positions_first(vt_tail)
    q_nat, q_res, k_nat, k_res, v_nat, v_res = qkv_p
    outs, lses = [], []
    for q, k, v, parts in ((q_nat, k_nat, v_nat, 1), (q_res, k_res, v_res, MID_RATIO), (q_res, k_res, v_res, 1)):
        o, lse = _attn_prompt(q, k, v, parts)
        outs.append(o)
        lses.append(lse)
    hp = _attn_out(outs, lses, w_o, hp, g[1], seq, tm_p)

    scale = HEAD_DIM ** -0.5
    q_s, k_s, v_s = _norm_matmul(
        hs, g[0], w_qkv,
        ((0, ATT_WIDTH, scale), (ATT_WIDTH, ATT_WIDTH, 1.0), (2 * ATT_WIDTH, ATT_WIDTH, 1.0)),
        (F32, F32, F32), ms)
    k_sample = k_s.reshape(1, n_dec, t_dec, ATT_HEADS, HEAD_DIM)
    v_sample = v_s.reshape(1, n_dec, t_dec, ATT_HEADS, HEAD_DIM)
    head_major = lambda a, rows: jnp.pad(
        jnp.swapaxes(a.reshape(n_dec, t_dec, ATT_HEADS, HEAD_DIM), 1, 2),
        ((0, 0), (0, 0), (0, rows - t_dec), (0, 0)))
    cache_kt = jnp.transpose(cache_k[0], (0, 2, 3, 1))
    cache_vt = jnp.transpose(cache_v[0], (0, 2, 3, 1))
    o_s = _attn_sample(head_major(q_s, SUBLANES), head_major(k_s, LANES), head_major(v_s, LANES),
                       cache_kt, cache_vt, t_dec)
    hs = _proj_out(o_s.reshape(ms, ATT_WIDTH), w_o, hs, g[1], ms)

    wg, wu, wd = ffn_w_gate[1].astype(BF16), ffn_w_up[1].astype(BF16), ffn_w_down[1].astype(BF16)
    hp = _ffn(hp, g[2], wg, wu, wd, g[3], tm_ffn)
    hs = _ffn(hs, g[2], wg, wu, wd, g[3], ms)

    return (hp.reshape(batch, seq, d), hs.reshape(n_dec, t_dec, d), pool_prompt, s5_prompt,
            k_prompt, v_prompt, pool_sample, s5_sample, k_sample, v_sample)
```

```python
import functools
import math

import jax
import jax.numpy as jnp
from jax import lax
from jax.experimental import pallas as pl
from jax.experimental.pallas import tpu as pltpu

F32 = jnp.float32
BF16 = jnp.bfloat16

D_MODEL = 1024
PAST_LEN = 16384
POOL_WIDTH = 512
POOL_WINDOWS = (2, 4, 8, 16)
POOL_GROUP = 128
POOL_STATE = 15
POOL_LEAD = 16
POOL_SEQS_PER_STEP = 8
SSM_WIDTH = 512
SSM_GROUP = 16
SSM_GROUPS = 32
SSM_STATE = 64
SSM_STATES = SSM_GROUPS * SSM_STATE
ATT_HEADS = 16
HEAD_DIM = 64
ATT_WIDTH = ATT_HEADS * HEAD_DIM
DILATED_PATTERNS = ((128, 1), (512, 4), (2048, 16))
MAX_WINDOW = 2048
ATT_BLOCK = 128
ATT_SPAN = 128
FFN_HIDDEN = 2816
RMS_EPS = 1e-6

SUBLANES = 8
LANES = 128
VMEM_LIMIT = 56 * 1024 * 1024
PROMPT_ROW_TILE = 512
S5_ROW_TILE = 2048


def _cparams(n_axes):
    return pltpu.CompilerParams(
        dimension_semantics=("arbitrary",) * n_axes, vmem_limit_bytes=VMEM_LIMIT)


def _rms(x, g):
    ms = jnp.mean(x * x, axis=-1, keepdims=True)
    return (x * lax.rsqrt(ms + RMS_EPS)) * g


def _const_spec(shape):
    zeros = (0,) * len(shape)
    return pl.BlockSpec(shape, lambda *_: zeros, pipeline_mode=pl.Buffered(1))


def _norm_matmul_kernel(x_ref, g_ref, w_ref, *out_refs, splits):
    xb = _rms(x_ref[...], g_ref[...]).astype(BF16)
    for (c0, width, scale), o_ref in zip(splits, out_refs):
        y = jnp.dot(xb, w_ref[:, c0:c0 + width], preferred_element_type=F32)
        if scale != 1.0:
            y = y * scale
        o_ref[...] = y.astype(o_ref.dtype)


def _norm_matmul(x, g, w, splits, dtypes, tm):
    m, d = x.shape
    n = w.shape[1]
    return pl.pallas_call(
        functools.partial(_norm_matmul_kernel, splits=splits),
        grid=(m // tm,),
        in_specs=[pl.BlockSpec((tm, d), lambda i: (i, 0)), _const_spec((1, d)), _const_spec((d, n))],
        out_specs=[pl.BlockSpec((tm, width), lambda i: (i, 0)) for _, width, _ in splits],
        out_shape=[jax.ShapeDtypeStruct((m, width), dt) for (_, width, _), dt in zip(splits, dtypes)],
        compiler_params=_cparams(1),
        name="norm_matmul",
    )(x, g, w)


ATT_DILATIONS = tuple(dil for _, dil in DILATED_PATTERNS)
ATT_ORDERS = (ATT_DILATIONS[0], ATT_DILATIONS[2])
PATTERN_ORDER = (ATT_DILATIONS[0], ATT_DILATIONS[2], ATT_DILATIONS[2])
MID_RATIO = ATT_DILATIONS[2] // ATT_DILATIONS[1]
COL_TILES = ATT_WIDTH // LANES
PERM_TILE = 256


def _qkv_prompt_kernel(x_ref, g_ref, w_ref, perm_ref, *refs, q_scale, tm, tiles, first_tail):
    n_dil = len(ATT_ORDERS)
    out_refs = refs[:3 * n_dil]
    tail_refs = refs[3 * n_dil:]
    in_tail = (pl.program_id(0) % tiles) >= first_tail
    xb = _rms(x_ref[...], g_ref[...]).astype(BF16)
    for which in range(3):
        y = jnp.dot(xb, w_ref[:, which * ATT_WIDTH:(which + 1) * ATT_WIDTH], preferred_element_type=F32)
        if which == 0:
            y = y * q_scale
        else:
            @pl.when(in_tail)
            def _(y=y, tail_ref=tail_refs[which - 1]):
                tail_ref[0] = y.T
        yb = y.astype(BF16)
        for pat, (dil, o_ref) in enumerate(zip(ATT_ORDERS, out_refs[which * n_dil:(which + 1) * n_dil])):
            if dil == 1:
                o_ref[0, 0] = yb
                continue
            rows = PERM_TILE // dil
            for part in range(tm // PERM_TILE):
                moved = jnp.dot(perm_ref[pat - 1], yb[part * PERM_TILE:(part + 1) * PERM_TILE],
                                preferred_element_type=F32).astype(BF16)
                for res in range(dil):
                    o_ref[0, res, part * rows:(part + 1) * rows] = moved[res * rows:(res + 1) * rows]


def _qkv_prompt(x, g, w, batch, seq, keep, tm):
    m, d = x.shape
    assert tm % PERM_TILE == 0
    tiles = seq // tm
    first_tail = tiles - keep // tm

    def tail_map(i):
        return (i // tiles, 0, jnp.maximum(i % tiles - first_tail, 0))

    def dil_spec(dil):
        return pl.BlockSpec((1, dil, tm // dil, ATT_WIDTH), lambda i: (i // tiles, 0, i % tiles, 0))

    def dil_shape(dil):
        return jax.ShapeDtypeStruct((batch, dil, seq // dil, ATT_WIDTH), BF16)

    assert ATT_ORDERS[0] == 1
    src = jnp.arange(PERM_TILE)[None, :]
    dst = jnp.arange(PERM_TILE)[:, None]
    perm = jnp.stack([(src == (dst % (PERM_TILE // dil)) * dil + dst // (PERM_TILE // dil)).astype(BF16)
                      for dil in ATT_ORDERS[1:]])
    return pl.pallas_call(
        functools.partial(_qkv_prompt_kernel, q_scale=HEAD_DIM ** -0.5, tm=tm, tiles=tiles,
                          first_tail=first_tail),
        grid=(m // tm,),
        in_specs=[pl.BlockSpec((tm, d), lambda i: (i, 0)), _const_spec((1, d)),
                  _const_spec((d, 3 * ATT_WIDTH)), _const_spec(perm.shape)],
        out_specs=[dil_spec(dil) for _ in range(3) for dil in ATT_ORDERS]
        + [pl.BlockSpec((1, ATT_WIDTH, tm), tail_map)] * 2,
        out_shape=[dil_shape(dil) for _ in range(3) for dil in ATT_ORDERS]
        + [jax.ShapeDtypeStruct((batch, ATT_WIDTH, keep), F32)] * 2,
        compiler_params=_cparams(1),
        name="qkv_prompt",
    )(x, g, w, perm)


def _pool_tile(u, j, lead_ref, w_ref, scale_ref, y_ref, ext, *, tt, pos0, carry, seq=0):
    @pl.when(j == 0)
    def _():
        ext[0:POOL_LEAD, :] = lead_ref[seq]

    ext[POOL_LEAD:POOL_LEAD + tt, :] = u
    pos = pos0 + j * tt + lax.broadcasted_iota(jnp.int32, (tt, POOL_GROUP), 0)
    for grp, window in enumerate(POOL_WINDOWS):
        sl = slice(grp * POOL_GROUP, (grp + 1) * POOL_GROUP)
        u_g = u[:, sl]
        win_sum = u_g
        for k in range(1, window):
            win_sum = win_sum + ext[POOL_LEAD - k:POOL_LEAD - k + tt, sl]
        count = jnp.minimum(pos + 1, window).astype(F32)
        diff = win_sum / count - u_g
        y = jnp.dot(diff.astype(BF16), w_ref[grp], preferred_element_type=F32)
        y_ref[seq, :, sl] = (y * scale_ref[:, sl]).astype(y_ref.dtype)
    if carry:
        ext[0:POOL_LEAD, :] = ext[tt:tt + POOL_LEAD, :]


def _pool_kernel(u_ref, lead_ref, w_ref, scale_ref, y_ref, ext, *, tt, pos0, carry, n_seq):
    for seq in range(n_seq):
        _pool_tile(u_ref[seq], pl.program_id(1), lead_ref, w_ref, scale_ref, y_ref, ext,
                   tt=tt, pos0=pos0, carry=carry, seq=seq)


def _in_proj_pool_kernel(x_ref, g_ref, w_ref, lead_ref, pw_ref, scale_ref, ussm_ref, ypool_ref, tail_ref, ext,
                         *, tt, pos0):
    xb = _rms(x_ref[0], g_ref[...]).astype(BF16)
    ussm_ref[0] = jnp.dot(xb, w_ref[:, POOL_WIDTH:], preferred_element_type=F32)
    u_pool = jnp.dot(xb, w_ref[:, 0:POOL_WIDTH], preferred_element_type=F32)
    _pool_tile(u_pool, pl.program_id(1), lead_ref, pw_ref, scale_ref, ypool_ref, ext, tt=tt, pos0=pos0, carry=True)
    tail_ref[0] = ext[0:POOL_LEAD, :]


def _in_proj_pool(x, g, w, lead, pool_w, pool_scale, tt, pos0):
    n, t, d = x.shape
    seq_block = lambda width: pl.BlockSpec((1, tt, width), lambda b, j: (b, j, 0))
    per_seq = pl.BlockSpec((1, POOL_LEAD, POOL_WIDTH), lambda b, j: (b, 0, 0))
    return pl.pallas_call(
        functools.partial(_in_proj_pool_kernel, tt=tt, pos0=pos0),
        grid=(n, t // tt),
        in_specs=[seq_block(d), _const_spec(g.shape), _const_spec(w.shape), per_seq,
                  _const_spec(pool_w.shape), _const_spec(pool_scale.shape)],
        out_specs=[seq_block(SSM_WIDTH), seq_block(POOL_WIDTH), per_seq],
        out_shape=[jax.ShapeDtypeStruct((n, t, SSM_WIDTH), F32), jax.ShapeDtypeStruct((n, t, POOL_WIDTH), BF16),
                   jax.ShapeDtypeStruct((n, POOL_LEAD, POOL_WIDTH), F32)],
        scratch_shapes=[pltpu.VMEM((POOL_LEAD + tt, POOL_WIDTH), F32)],
        compiler_params=_cparams(2),
        name="in_proj_pool",
    )(x, g, w, lead, pool_w, pool_scale)


def _pool_mixer(proj, lead, w, scale, tt, pos0):
    n, t, _ = proj.shape
    steps = t // tt
    n_seq = 1 if steps > 1 else math.gcd(n, POOL_SEQS_PER_STEP)
    return pl.pallas_call(
        functools.partial(_pool_kernel, tt=tt, pos0=pos0, carry=steps > 1, n_seq=n_seq),
        grid=(n // n_seq, steps),
        in_specs=[pl.BlockSpec((n_seq, tt, POOL_WIDTH), lambda b, j: (b, j, 0)),
                  pl.BlockSpec((n_seq, POOL_LEAD, POOL_WIDTH), lambda b, j: (b, 0, 0)),
                  _const_spec((len(POOL_WINDOWS), POOL_GROUP, POOL_GROUP)),
                  _const_spec((1, POOL_WIDTH))],
        out_specs=pl.BlockSpec((n_seq, tt, POOL_WIDTH), lambda b, j: (b, j, 0)),
        out_shape=jax.ShapeDtypeStruct((n, t, POOL_WIDTH), BF16),
        scratch_shapes=[pltpu.VMEM((POOL_LEAD + tt, POOL_WIDTH), F32)],
        compiler_params=_cparams(2),
        name="pool_mixer",
    )(proj, lead, w, scale)


S5_IN_HALF = SSM_WIDTH // 2
S5_STATE_HALF = SSM_STATES // 2
S5_OUT_TILE = LANES
S5_STATE_TILE = SSM_STATES // (SSM_WIDTH // S5_OUT_TILE)
S5_SCAN_LANES = 512


def _s5_input(ub, bre_ref, bim_ref):
    re, im = [], []
    for half in range(2):
        uk = ub[:, half * S5_IN_HALF:(half + 1) * S5_IN_HALF]
        re.append(jnp.dot(uk, bre_ref[half], preferred_element_type=F32))
        im.append(jnp.dot(uk, bim_ref[half], preferred_element_type=F32))
    return re, im


def _s5_output(h_re, h_im, cre_ref, cim_ref, tile):
    return (jnp.dot(h_re, cre_ref[tile], preferred_element_type=F32)
            - jnp.dot(h_im, cim_ref[tile], preferred_element_type=F32))


S5_CHUNK = 16
S5_ROW = S5_CHUNK * SSM_GROUP
S5_SUB = 256
UNIT = SSM_GROUP
UNITS = LANES // UNIT
S5_PAIRS = SSM_GROUPS // 2


def _unit_transpose(pieces):
    unit = lax.broadcasted_iota(jnp.int32, pieces[0].shape, 1) >> (UNIT.bit_length() - 1)
    cur = list(pieces)
    k = UNITS // 2
    while k:
        high = (unit & k) != 0
        nxt = list(cur)
        for a in range(UNITS):
            if a & k:
                continue
            lo_piece, hi_piece = cur[a], cur[a | k]
            nxt[a] = jnp.where(high, pltpu.roll(hi_piece, k * UNIT, 1), lo_piece)
            nxt[a | k] = jnp.where(high, hi_piece, pltpu.roll(lo_piece, LANES - k * UNIT, 1))
        cur = nxt
        k //= 2
    return cur


def _s5_prompt_kernel(u_ref, perm_ref, toep_ref, wre_ref, wim_ref, vre_ref, vim_ref, pw_ref, d_ref,
                      y_ref, hre_ref, him_ref, u2, y2, s_re, s_im, c_re, c_im, y_nat, *, tt):
    j = pl.program_id(1)
    chunks = tt // S5_CHUNK
    sub_chunks = S5_SUB // S5_CHUNK

    @pl.when(j == 0)
    def _():
        c_re[...] = jnp.zeros_like(c_re)
        c_im[...] = jnp.zeros_like(c_im)

    def relayout_in(sub, carry):
        r0 = pl.multiple_of(sub * S5_SUB, S5_SUB)
        c0 = pl.multiple_of(sub * sub_chunks, sub_chunks)
        ub = u_ref[0, pl.ds(r0, S5_SUB), :].astype(BF16)
        xs = jnp.dot(perm_ref[...], ub, preferred_element_type=F32).astype(BF16)
        for tile in range(SSM_WIDTH // LANES):
            for half in range(S5_CHUNK // UNITS):
                pieces = [pltpu.bitcast(
                    xs[(half * UNITS + a) * sub_chunks:(half * UNITS + a + 1) * sub_chunks,
                       tile * LANES:(tile + 1) * LANES], jnp.uint32) for a in range(UNITS)]
                outs = _unit_transpose(pieces)
                for gl in range(UNITS):
                    u2[tile * UNITS + gl, pl.ds(c0, sub_chunks), half * LANES:(half + 1) * LANES] = (
                        pltpu.bitcast(outs[gl], BF16))
        return carry

    lax.fori_loop(0, tt // S5_SUB, relayout_in, 0, unroll=2)

    s_re[0:SUBLANES, :] = c_re[...]
    s_im[0:SUBLANES, :] = c_im[...]
    for pair in range(S5_PAIRS):
        sl = slice(pair * LANES, (pair + 1) * LANES)
        ua = u2[2 * pair]
        ub = u2[2 * pair + 1]
        s_re[SUBLANES:SUBLANES + chunks, sl] = (
            jnp.dot(ua, wre_ref[2 * pair], preferred_element_type=F32)
            + jnp.dot(ub, wre_ref[2 * pair + 1], preferred_element_type=F32))
        s_im[SUBLANES:SUBLANES + chunks, sl] = (
            jnp.dot(ua, wim_ref[2 * pair], preferred_element_type=F32)
            + jnp.dot(ub, wim_ref[2 * pair + 1], preferred_element_type=F32))

    for chunk in range(SSM_STATES // S5_SCAN_LANES):
        sl = slice(chunk * S5_SCAN_LANES, (chunk + 1) * S5_SCAN_LANES)

        def body(r, carry, sl=sl):
            in_re, in_im = carry
            row = pl.multiple_of((r + 1) * SUBLANES, SUBLANES)
            re = s_re[pl.ds(row, SUBLANES), sl]
            im = s_im[pl.ds(row, SUBLANES), sl]
            for level, shift in enumerate((1, 2, 4)):
                a_re = pw_ref[2 * level, :, sl]
                a_im = pw_ref[2 * level + 1, :, sl]
                sh_re = pltpu.roll(re, shift, 0)
                sh_im = pltpu.roll(im, shift, 0)
                re, im = (re + a_re * sh_re - a_im * sh_im,
                          im + a_re * sh_im + a_im * sh_re)
            p_re = pw_ref[6, :, sl]
            p_im = pw_ref[7, :, sl]
            re, im = (re + p_re * in_re - p_im * in_im,
                      im + p_re * in_im + p_im * in_re)
            s_re[pl.ds(row, SUBLANES), sl] = re
            s_im[pl.ds(row, SUBLANES), sl] = im
            last = SUBLANES - 1
            return (jnp.broadcast_to(re[last:last + 1, :], re.shape),
                    jnp.broadcast_to(im[last:last + 1, :], im.shape))

        out_re, out_im = lax.fori_loop(0, chunks // SUBLANES, body, (c_re[:, sl], c_im[:, sl]))
        c_re[:, sl] = out_re
        c_im[:, sl] = out_im

    hre_ref[0] = c_re[...]
    him_ref[0] = c_im[...]

    for pair in range(S5_PAIRS):
        sl = slice(pair * LANES, (pair + 1) * LANES)
        h_re = s_re[SUBLANES - 1:SUBLANES - 1 + chunks, sl].astype(BF16)
        h_im = s_im[SUBLANES - 1:SUBLANES - 1 + chunks, sl].astype(BF16)
        carried = (jnp.dot(h_re, vre_ref[pair], preferred_element_type=F32)
                   + jnp.dot(h_im, vim_ref[pair], preferred_element_type=F32))
        for k in range(2):
            grp = 2 * pair + k
            y2[grp] = (jnp.dot(u2[grp], toep_ref[grp], preferred_element_type=F32)
                       + carried[:, k * S5_ROW:(k + 1) * S5_ROW])

    def relayout_out(blk, carry):
        r0 = pl.multiple_of(blk * S5_SUB, S5_SUB)
        c0 = pl.multiple_of(blk * sub_chunks, sub_chunks)
        for tile in range(SSM_WIDTH // LANES):
            ot = slice(tile * LANES, (tile + 1) * LANES)
            for half in range(S5_CHUNK // UNITS):
                pieces = [y2[tile * UNITS + gl, pl.ds(c0, sub_chunks), half * LANES:(half + 1) * LANES]
                          for gl in range(UNITS)]
                outs = _unit_transpose(pieces)
                for a in range(UNITS):
                    y_nat[tile, pl.ds(half * UNITS + a, sub_chunks, stride=S5_CHUNK), :] = outs[a]
            y_ref[0, pl.ds(r0, S5_SUB), ot] = y_nat[tile] + d_ref[:, ot] * u_ref[0, pl.ds(r0, S5_SUB), ot]
        return carry

    lax.fori_loop(0, tt // S5_SUB, relayout_out, 0)


def _s5_prompt(u, mats, tt):
    b, t, _ = u.shape
    chunks = tt // S5_CHUNK
    state = jax.ShapeDtypeStruct((b, SUBLANES, SSM_STATES), F32)
    state_spec = pl.BlockSpec((1, SUBLANES, SSM_STATES), lambda i, j: (i, 0, 0))
    return pl.pallas_call(
        functools.partial(_s5_prompt_kernel, tt=tt),
        grid=(b, t // tt),
        in_specs=[pl.BlockSpec((1, tt, SSM_WIDTH), lambda i, j: (i, j, 0))]
        + [_const_spec(a.shape) for a in mats],
        out_specs=[pl.BlockSpec((1, tt, SSM_WIDTH), lambda i, j: (i, j, 0)), state_spec, state_spec],
        out_shape=[jax.ShapeDtypeStruct((b, t, SSM_WIDTH), F32), state, state],
        scratch_shapes=[pltpu.VMEM((SSM_GROUPS, chunks, S5_ROW), BF16),
                        pltpu.VMEM((SSM_GROUPS, chunks, S5_ROW), F32),
                        pltpu.VMEM((SUBLANES + chunks, SSM_STATES), F32),
                        pltpu.VMEM((SUBLANES + chunks, SSM_STATES), F32),
                        pltpu.VMEM((SUBLANES, SSM_STATES), F32), pltpu.VMEM((SUBLANES, SSM_STATES), F32),
                        pltpu.VMEM((SSM_WIDTH // LANES, S5_SUB, LANES), F32)],
        compiler_params=_cparams(2),
        name="s5_prompt",
    )(u, *mats)


def _s5_sample_kernel(u_ref, h0re_ref, h0im_ref, bre_ref, bim_ref, cre_ref, cim_ref, lam_ref, d_ref,
                      y_ref, hre_ref, him_ref, *, steps):
    h_re = h0re_ref[...]
    h_im = h0im_ref[...]
    lam_re = lam_ref[0:1, :]
    lam_im = lam_ref[1:2, :]
    for t in range(steps):
        u = u_ref[t]
        bu_re, bu_im = _s5_input(u.astype(BF16), bre_ref, bim_ref)
        bu_re = jnp.concatenate(bu_re, axis=1)
        bu_im = jnp.concatenate(bu_im, axis=1)
        h_re, h_im = (lam_re * h_re - lam_im * h_im + bu_re,
                      lam_re * h_im + lam_im * h_re + bu_im)
        hb_re = h_re.astype(BF16)
        hb_im = h_im.astype(BF16)
        for tile in range(SSM_WIDTH // S5_OUT_TILE):
            st = slice(tile * S5_STATE_TILE, (tile + 1) * S5_STATE_TILE)
            ot = slice(tile * S5_OUT_TILE, (tile + 1) * S5_OUT_TILE)
            y = _s5_output(hb_re[:, st], hb_im[:, st], cre_ref, cim_ref, tile)
            y_ref[t, :, ot] = y + d_ref[:, ot] * u[:, ot]
    hre_ref[...] = h_re
    him_ref[...] = h_im


def _s5_sample(u_tb, h0_re, h0_im, mats, lam):
    steps, n, _ = u_tb.shape
    bre, bim, cre, cim, d_skip = mats
    state = jax.ShapeDtypeStruct((n, SSM_STATES), F32)
    args = (u_tb, h0_re, h0_im, bre, bim, cre, cim, lam, d_skip)
    return pl.pallas_call(
        functools.partial(_s5_sample_kernel, steps=steps),
        grid=(1,),
        in_specs=[_const_spec(a.shape) for a in args],
        out_specs=[_const_spec(u_tb.shape), _const_spec((n, SSM_STATES)), _const_spec((n, SSM_STATES))],
        out_shape=[jax.ShapeDtypeStruct(u_tb.shape, F32), state, state],
        compiler_params=_cparams(1),
        name="s5_sample",
    )(*args)


def _s5_matrices(lam_re, lam_im, log_dt, b_re, b_im, c_re, c_im, d_skip):
    lam = lax.complex(lam_re, lam_im)
    dt = jnp.exp(log_dt)[:, None]
    lam_bar = jnp.exp(lam * dt)
    b_bar = ((lam_bar - 1.0) / lam)[..., None] * lax.complex(b_re, b_im)
    eye_half = jnp.eye(SSM_GROUPS // 2, dtype=F32)

    def in_blocks(x):
        x = x.reshape(2, SSM_GROUPS // 2, SSM_STATE, SSM_GROUP)
        return jnp.einsum("kgpi,gh->kgihp", x, eye_half).reshape(2, S5_IN_HALF, S5_STATE_HALF).astype(BF16)

    tiles = SSM_WIDTH // S5_OUT_TILE
    groups_per_tile = SSM_GROUPS // tiles
    eye_tile = jnp.eye(groups_per_tile, dtype=F32)

    def out_blocks(x):
        x = x.reshape(tiles, groups_per_tile, SSM_GROUP, SSM_STATE)
        return jnp.einsum("kgop,gh->kgpho", x, eye_tile).reshape(tiles, S5_STATE_TILE, S5_OUT_TILE).astype(BF16)

    lam_flat = lam_bar.reshape(1, SSM_STATES)
    lam_rows = jnp.concatenate([lam_flat.real, lam_flat.imag], axis=0).astype(F32)
    d_row = d_skip.reshape(1, SSM_WIDTH)
    sample_mats = (in_blocks(b_bar.real), in_blocks(b_bar.imag), out_blocks(c_re), out_blocks(c_im), d_row)

    c_mat = lax.complex(c_re, c_im)
    lam_pow = jnp.concatenate([jnp.ones((1,) + lam_bar.shape, lam_bar.dtype),
                               jnp.cumprod(jnp.broadcast_to(lam_bar, (S5_CHUNK,) + lam_bar.shape), axis=0)])
    taps = jnp.einsum("gop,kgp,gpi->gkoi", c_mat, lam_pow[:S5_CHUNK], b_bar,
                      precision=lax.Precision.HIGHEST).real
    src = jnp.arange(S5_CHUNK)[None, :, None]
    dst = jnp.arange(S5_CHUNK)[None, None, :]
    lag_is = (dst - src == jnp.arange(S5_CHUNK)[:, None, None]).astype(F32)
    toep = jnp.einsum("kst,gkoi->gsito", lag_is, taps, precision=lax.Precision.HIGHEST)
    toep = toep.reshape(SSM_GROUPS, S5_ROW, S5_ROW).astype(BF16)
    w_state = jnp.einsum("sgp,gpi->gsip", lam_pow[S5_CHUNK - 1::-1][:S5_CHUNK], b_bar)
    w_state = w_state.reshape(SSM_GROUPS, S5_ROW, SSM_STATE)
    second = (jnp.arange(SSM_GROUPS) % 2 == 1)[:, None, None]
    zeros = jnp.zeros_like(w_state.real)

    def pair_cols(x):
        return jnp.where(second, jnp.concatenate([zeros, x], -1), jnp.concatenate([x, zeros], -1)).astype(BF16)

    v_out = jnp.einsum("gop,tgp->gpto", c_mat, lam_pow[1:]).reshape(SSM_GROUPS, SSM_STATE, S5_ROW)
    eye_pair = jnp.eye(2, dtype=F32)

    def pair_blocks(x):
        x = x.reshape(S5_PAIRS, 2, SSM_STATE, S5_ROW)
        return jnp.einsum("kapc,ab->kapbc", x, eye_pair).reshape(S5_PAIRS, LANES, 2 * S5_ROW).astype(BF16)

    lam_chunk = lam_pow[S5_CHUNK].reshape(1, SSM_STATES)
    rows = jnp.arange(SUBLANES)[:, None]
    planes = []
    for shift in (1, 2, 4):
        a = jnp.where(rows >= shift, lam_chunk ** shift, 0.0)
        planes += [a.real, a.imag]
    carry = jnp.cumprod(jnp.broadcast_to(lam_chunk, (SUBLANES, SSM_STATES)), axis=0)
    planes += [carry.real, carry.imag]
    powers = jnp.stack(planes).astype(F32)
    pos = jnp.arange(S5_SUB)
    perm = (jnp.arange(S5_SUB)[None, :] == ((pos % (S5_SUB // S5_CHUNK)) * S5_CHUNK
                                            + pos // (S5_SUB // S5_CHUNK))[:, None]).astype(BF16)
    prompt_mats = (perm, toep, pair_cols(w_state.real), pair_cols(w_state.imag),
                   pair_blocks(v_out.real), pair_blocks(-v_out.imag), powers, d_row)
    return prompt_mats, sample_mats, lam_rows


def _ab_out_kernel(yp_ref, ys_ref, wglu_ref, bglu_ref, wout_ref, h_ref, g_ref, o_ref):
    z = jax.nn.gelu(ys_ref[...])
    gate = jnp.dot(z.astype(BF16), wglu_ref[...], preferred_element_type=F32) + bglu_ref[...]
    y_ssm = z * jax.nn.sigmoid(gate)
    y = (jnp.dot(yp_ref[...], wout_ref[0:POOL_WIDTH, :], preferred_element_type=F32)
         + jnp.dot(y_ssm.astype(BF16), wout_ref[POOL_WIDTH:, :], preferred_element_type=F32))
    o_ref[...] = h_ref[...] + _rms(y, g_ref[...])


def _ab_out(y_pool, y_ssm, w_glu, b_glu, w_out, h, g, tm):
    m = h.shape[0]
    row = lambda width: pl.BlockSpec((tm, width), lambda i: (i, 0))
    return pl.pallas_call(
        _ab_out_kernel,
        grid=(m // tm,),
        in_specs=[row(POOL_WIDTH), row(SSM_WIDTH), _const_spec(w_glu.shape), _const_spec(b_glu.shape),
                  _const_spec(w_out.shape), row(D_MODEL), _const_spec(g.shape)],
        out_specs=row(D_MODEL),
        out_shape=jax.ShapeDtypeStruct((m, D_MODEL), F32),
        compiler_params=_cparams(1),
        name="ab_out",
    )(y_pool, y_ssm, w_glu, b_glu, w_out, h, g)


def _ffn_kernel(h_ref, gin_ref, wg_ref, wu_ref, wd_ref, gout_ref, o_ref):
    h = h_ref[...]
    xb = _rms(h, gin_ref[...]).astype(BF16)
    gate = jnp.dot(xb, wg_ref[...], preferred_element_type=F32)
    up = jnp.dot(xb, wu_ref[...], preferred_element_type=F32)
    act = (gate * jax.nn.sigmoid(gate) * up).astype(BF16)
    y = jnp.dot(act, wd_ref[...], preferred_element_type=F32)
    o_ref[...] = h + _rms(y, gout_ref[...])


def _ffn(h, g_in, w_gate, w_up, w_down, g_out, tm):
    m = h.shape[0]
    row = pl.BlockSpec((tm, D_MODEL), lambda i: (i, 0))
    return pl.pallas_call(
        _ffn_kernel,
        grid=(m // tm,),
        in_specs=[row, _const_spec(g_in.shape), _const_spec(w_gate.shape), _const_spec(w_up.shape),
                  _const_spec(w_down.shape), _const_spec(g_out.shape)],
        out_specs=row,
        out_shape=jax.ShapeDtypeStruct((m, D_MODEL), F32),
        compiler_params=_cparams(1),
        name="ffn",
    )(h, g_in, w_gate, w_up, w_down, g_out)


def _attn_prompt_kernel(q_ref, kc_ref, kp_ref, vc_ref, vp_ref, o_ref, lse_ref, *, parts):
    blk = pl.program_id(2)
    rows = 2 * ATT_BLOCK
    rpp = ATT_BLOCK // parts
    shift = rpp.bit_length() - 1
    qi = lax.broadcasted_iota(jnp.int32, (rows, 2 * ATT_BLOCK), 0) & (ATT_BLOCK - 1)
    kj = lax.broadcasted_iota(jnp.int32, (rows, 2 * ATT_BLOCK), 1)
    k_in = kj & (ATT_BLOCK - 1)
    is_cur = kj >> (ATT_BLOCK.bit_length() - 1)
    q_row = rpp + (qi & (rpp - 1))
    k_row = is_cur * rpp + (k_in & (rpp - 1))
    dist = parts * (q_row - k_row) + ((qi >> shift) - (k_in >> shift))
    valid = (dist >= 0) & (dist <= ATT_SPAN) & ((kj >= ATT_BLOCK) | (blk > 0))
    lane = lax.broadcasted_iota(jnp.int32, (ATT_BLOCK, LANES), 1)
    first_head = lane < HEAD_DIM
    lse_all = jnp.zeros((ATT_BLOCK, LANES), F32)
    gather = lambda ref, sl: jnp.concatenate([ref[c, :, sl] for c in range(parts)], axis=0)
    for pair in range(ATT_HEADS // 2):
        sl = slice(pair * LANES, (pair + 1) * LANES)
        q2 = gather(q_ref, sl)
        zero = jnp.zeros_like(q2)
        qs = jnp.concatenate([jnp.where(first_head, q2, zero), jnp.where(first_head, zero, q2)], axis=0)
        k2 = jnp.concatenate([gather(kp_ref, sl), gather(kc_ref, sl)], axis=0)
        v2 = jnp.concatenate([gather(vp_ref, sl), gather(vc_ref, sl)], axis=0)
        s = lax.dot_general(qs, k2, (((1,), (1,)), ((), ())), preferred_element_type=F32)
        s = jnp.where(valid, s, -jnp.inf)
        m = jnp.max(s, axis=-1, keepdims=True)
        p = jnp.exp(s - m)
        den = jnp.sum(p, axis=-1, keepdims=True)
        o = jnp.dot(p.astype(BF16), v2, preferred_element_type=F32) / den
        lse = m + jnp.log(den)
        o_pair = jnp.where(first_head, o[:ATT_BLOCK], o[ATT_BLOCK:]).astype(o_ref.dtype)
        for c in range(parts):
            o_ref[c, :, sl] = o_pair[c * rpp:(c + 1) * rpp]
        lse_all = jnp.where(lane == 2 * pair, lse[:ATT_BLOCK],
                            jnp.where(lane == 2 * pair + 1, lse[ATT_BLOCK:], lse_all))
    for c in range(parts):
        lse_ref[c] = lse_all[c * rpp:(c + 1) * rpp]


def _attn_prompt(q, k, v, parts):
    batch, n_streams, sub, _ = q.shape
    pat_streams = n_streams // parts
    rpp = ATT_BLOCK // parts
    nblk = sub // rpp
    view = lambda a: a.reshape(batch, parts, pat_streams, sub, a.shape[-1])
    cur = pl.BlockSpec((None, parts, None, rpp, ATT_WIDTH), lambda b, r, i: (b, 0, r, i, 0))
    prev = pl.BlockSpec((None, parts, None, rpp, ATT_WIDTH), lambda b, r, i: (b, 0, r, jnp.maximum(i - 1, 0), 0))
    o, lse = pl.pallas_call(
        functools.partial(_attn_prompt_kernel, parts=parts),
        grid=(batch, pat_streams, nblk),
        in_specs=[cur, cur, prev, cur, prev],
        out_specs=[cur, pl.BlockSpec((None, parts, None, rpp, LANES), lambda b, r, i: (b, 0, r, i, 0))],
        out_shape=[jax.ShapeDtypeStruct((batch, parts, pat_streams, sub, ATT_WIDTH), BF16),
                   jax.ShapeDtypeStruct((batch, parts, pat_streams, sub, LANES), F32)],
        compiler_params=_cparams(3),
        name="attn_prompt",
    )(view(q), view(k), view(k), view(v), view(v))
    return o.reshape(batch, n_streams, sub, ATT_WIDTH), lse.reshape(batch, n_streams, sub, LANES)


def _pattern_multiplicity(dist):
    mult = jnp.zeros(dist.shape, F32)
    for window, dil in DILATED_PATTERNS:
        hit = (dist >= 0) & (dist <= window) & ((dist & (dil - 1)) == 0)
        mult = mult + hit.astype(F32)
    return mult


def _attn_sample_kernel(q_ref, knew_ref, vnew_ref, kt_ref, vt_ref, o_ref, *, steps):
    t_cache = lax.broadcasted_iota(jnp.int32, (SUBLANES, MAX_WINDOW), 0)
    pos = lax.broadcasted_iota(jnp.int32, (SUBLANES, MAX_WINDOW), 1)
    mult = _pattern_multiplicity(MAX_WINDOW + t_cache - pos)
    t_new = lax.broadcasted_iota(jnp.int32, (SUBLANES, LANES), 0)
    j_new = lax.broadcasted_iota(jnp.int32, (SUBLANES, LANES), 1)
    mult_new = jnp.where(j_new < steps, _pattern_multiplicity(t_new - j_new), 0.0)
    nt = (((1,), (1,)), ((), ()))
    for head in range(ATT_HEADS):
        q = q_ref[0, head].astype(BF16)
        s = jnp.dot(q, kt_ref[0, head].astype(BF16), preferred_element_type=F32)
        s_new = lax.dot_general(q, knew_ref[0, head].astype(BF16), nt, preferred_element_type=F32)
        s = jnp.where(mult > 0.0, s, -jnp.inf)
        s_new = jnp.where(mult_new > 0.0, s_new, -jnp.inf)
        m = jnp.maximum(jnp.max(s, axis=-1, keepdims=True), jnp.max(s_new, axis=-1, keepdims=True))
        p = mult * jnp.exp(s - m)
        p_new = mult_new * jnp.exp(s_new - m)
        den = jnp.sum(p, axis=-1, keepdims=True) + jnp.sum(p_new, axis=-1, keepdims=True)
        o = lax.dot_general(p.astype(BF16), vt_ref[0, head].astype(BF16), nt, preferred_element_type=F32)
        o = o + jnp.dot(p_new.astype(BF16), vnew_ref[0, head].astype(BF16), preferred_element_type=F32)
        o_ref[0, :, head * HEAD_DIM:(head + 1) * HEAD_DIM] = (o / den)[:steps]


def _attn_sample(q, k_new, v_new, cache_kt, cache_vt, steps):
    n = q.shape[0]
    assert steps <= SUBLANES
    per = lambda a: pl.BlockSpec((1,) + a.shape[1:], lambda b: (b, 0, 0, 0))
    args = (q, k_new, v_new, cache_kt, cache_vt)
    return pl.pallas_call(
        functools.partial(_attn_sample_kernel, steps=steps),
        grid=(n,),
        in_specs=[per(a) for a in args],
        out_specs=pl.BlockSpec((1, steps, ATT_WIDTH), lambda b: (b, 0, 0)),
        out_shape=jax.ShapeDtypeStruct((n, steps, ATT_WIDTH), F32),
        compiler_params=_cparams(1),
        name="attn_sample",
    )(*args)


def _attn_out_kernel(*refs, tm):
    n_pat = len(PATTERN_ORDER)
    o_refs = refs[:n_pat]
    lse_refs = refs[n_pat:2 * n_pat]
    w_ref, expand_ref, unperm_ref, h_ref, g_ref, out_ref, lse_nat = refs[2 * n_pat:]
    o_nat = []
    for pat, dil in enumerate(PATTERN_ORDER):
        if dil == 1:
            lse_nat[pat] = lse_refs[pat][0, 0]
            o_nat.append(o_refs[pat][0, 0].astype(F32))
            continue
        for res in range(dil):
            lse_nat[pat, pl.ds(res, tm // dil, stride=dil), :] = lse_refs[pat][0, res]
        rows = PERM_TILE // dil
        parts = []
        for part in range(tm // PERM_TILE):
            stacked = jnp.concatenate([o_refs[pat][0, res, part * rows:(part + 1) * rows] for res in range(dil)],
                                      axis=0)
            parts.append(jnp.dot(unperm_ref[...], stacked, preferred_element_type=F32))
        o_nat.append(jnp.concatenate(parts, axis=0))
    lses = [lse_nat[pat] for pat in range(n_pat)]
    top = functools.reduce(jnp.maximum, lses)
    es = [jnp.exp(l - top) for l in lses]
    total = functools.reduce(lambda x, y: x + y, es)
    spread = []
    for e in es[:-1]:
        wgt = e / total
        hi = wgt.astype(BF16)
        lo = (wgt - hi.astype(F32)).astype(BF16)
        spread.append(jnp.dot(jnp.concatenate([hi, lo], axis=1), expand_ref[...],
                              preferred_element_type=F32))
    spread.append(1.0 - functools.reduce(lambda x, y: x + y, spread))
    pieces = []
    for c in range(COL_TILES):
        acc = None
        for pat in range(n_pat):
            term = spread[pat][:, c * LANES:(c + 1) * LANES] * o_nat[pat][:, c * LANES:(c + 1) * LANES]
            acc = term if acc is None else acc + term
        pieces.append(acc.astype(BF16))
    a = jnp.concatenate(pieces, axis=1)
    y = jnp.dot(a, w_ref[...], preferred_element_type=F32)
    out_ref[...] = h_ref[...] + _rms(y, g_ref[...])


def _attn_out(outs, lses, w_o, h, g, seq, tm):
    m = h.shape[0]
    tiles = seq // tm
    n_pat = len(PATTERN_ORDER)

    def dil_spec(dil, width):
        return pl.BlockSpec((1, dil, tm // dil, width), lambda i: (i // tiles, 0, i % tiles, 0))

    row = pl.BlockSpec((tm, D_MODEL), lambda i: (i, 0))
    expand = (jnp.arange(LANES)[:, None] == jnp.arange(ATT_WIDTH)[None, :] // HEAD_DIM).astype(BF16)
    expand = jnp.concatenate([expand, expand], axis=0)
    order = ATT_ORDERS[1]
    assert all(dil in (1, order) for dil in PATTERN_ORDER)
    assert tm % PERM_TILE == 0
    nat = jnp.arange(PERM_TILE)
    unperm = (jnp.arange(PERM_TILE)[None, :]
              == ((nat % order) * (PERM_TILE // order) + nat // order)[:, None]).astype(BF16)
    return pl.pallas_call(
        functools.partial(_attn_out_kernel, tm=tm),
        grid=(m // tm,),
        in_specs=[dil_spec(dil, ATT_WIDTH) for dil in PATTERN_ORDER]
        + [dil_spec(dil, LANES) for dil in PATTERN_ORDER]
        + [_const_spec(w_o.shape), _const_spec(expand.shape), _const_spec(unperm.shape), row,
           _const_spec(g.shape)],
        out_specs=row,
        out_shape=jax.ShapeDtypeStruct((m, D_MODEL), F32),
        scratch_shapes=[pltpu.VMEM((n_pat, tm, LANES), F32)],
        compiler_params=_cparams(1),
        name="attn_out",
    )(*outs, *lses, w_o, expand, unperm, h, g)


def _proj_out_kernel(a_ref, w_ref, h_ref, g_ref, out_ref):
    y = jnp.dot(a_ref[...].astype(BF16), w_ref[...], preferred_element_type=F32)
    out_ref[...] = h_ref[...] + _rms(y, g_ref[...])


def _proj_out(a, w, h, g, tm):
    m = h.shape[0]
    return pl.pallas_call(
        _proj_out_kernel,
        grid=(m // tm,),
        in_specs=[pl.BlockSpec((tm, a.shape[1]), lambda i: (i, 0)), _const_spec(w.shape),
                  pl.BlockSpec((tm, D_MODEL), lambda i: (i, 0)), _const_spec(g.shape)],
        out_specs=pl.BlockSpec((tm, D_MODEL), lambda i: (i, 0)),
        out_shape=jax.ShapeDtypeStruct((m, D_MODEL), F32),
        compiler_params=_cparams(1),
        name="proj_out",
    )(a, w, h, g)


def kernel(x_prompt, x_sample, state_pool, state_s5, cache_k, cache_v, norm_gains, ab_w_in, ab_pool_w, ab_pool_scale, ab_lambda_re, ab_lambda_im, ab_log_dt, ab_b_re, ab_b_im, ab_c_re, ab_c_im, ab_d, ab_w_glu, ab_b_glu, ab_w_out, c_w_qkv, c_w_o, ffn_w_gate, ffn_w_up, ffn_w_down):
    batch, seq, d = x_prompt.shape
    n_dec, t_dec, _ = x_sample.shape
    mp, ms = batch * seq, n_dec * t_dec
    tm_p = tm_ffn = PROMPT_ROW_TILE
    gains = norm_gains.reshape(norm_gains.shape[0], 4, 1, d)

    hp = x_prompt.reshape(mp, d)
    hs = x_sample.reshape(ms, d)

    g = gains[0]
    w_in = ab_w_in[0].astype(BF16)
    pool_w = ab_pool_w[0].astype(BF16)
    pool_scale = ab_pool_scale[0].reshape(1, POOL_WIDTH)
    lead_p = jnp.zeros((batch, POOL_LEAD, POOL_WIDTH), F32)
    lead_s = jnp.pad(state_pool[0], ((0, 0), (POOL_LEAD - POOL_STATE, 0), (0, 0)))
    ussm_p, ypool_p, pool_tail = _in_proj_pool(x_prompt, g[0], w_in, lead_p, pool_w, pool_scale, tm_ffn, 0)
    (proj_s,) = _norm_matmul(hs, g[0], w_in, ((0, d, 1.0),), (F32,), ms)
    proj_s = proj_s.reshape(n_dec, t_dec, d)
    ypool_s = _pool_mixer(proj_s, lead_s, pool_w, pool_scale, t_dec, PAST_LEN)
    pool_prompt = pool_tail[:, POOL_LEAD - POOL_STATE:][None]
    pool_sample = jnp.concatenate([state_pool[0], proj_s[:, :, :POOL_WIDTH]], axis=1)[:, -POOL_STATE:][None]

    prompt_mats, sample_mats, lam_rows = _s5_matrices(
        ab_lambda_re[0], ab_lambda_im[0], ab_log_dt[0], ab_b_re[0], ab_b_im[0], ab_c_re[0], ab_c_im[0], ab_d[0])
    yssm_p, hre_p, him_p = _s5_prompt(ussm_p, prompt_mats, S5_ROW_TILE)
    u_tb = jnp.swapaxes(proj_s[:, :, POOL_WIDTH:], 0, 1)
    h0 = state_s5[0].reshape(n_dec, SSM_STATES, 2)
    yssm_tb, hre_s, him_s = _s5_sample(u_tb, h0[..., 0], h0[..., 1], sample_mats, lam_rows)
    yssm_s = jnp.swapaxes(yssm_tb, 0, 1)
    s5_prompt = jnp.stack([hre_p[:, 0], him_p[:, 0]], axis=-1).reshape(1, batch, SSM_GROUPS, SSM_STATE, 2)
    s5_sample = jnp.stack([hre_s, him_s], axis=-1).reshape(1, n_dec, SSM_GROUPS, SSM_STATE, 2)

    w_glu = ab_w_glu[0].astype(BF16)
    b_glu = ab_b_glu[0].reshape(1, SSM_WIDTH)
    w_out = ab_w_out[0].astype(BF16)
    hp = _ab_out(ypool_p.reshape(mp, POOL_WIDTH), yssm_p.reshape(mp, SSM_WIDTH), w_glu, b_glu, w_out, hp, g[1], tm_ffn)
    hs = _ab_out(ypool_s.reshape(ms, POOL_WIDTH), yssm_s.reshape(ms, SSM_WIDTH), w_glu, b_glu, w_out, hs, g[1], ms)

    wg, wu, wd = ffn_w_gate[0].astype(BF16), ffn_w_up[0].astype(BF16), ffn_w_down[0].astype(BF16)
    hp = _ffn(hp, g[2], wg, wu, wd, g[3], tm_ffn)
    hs = _ffn(hs, g[2], wg, wu, wd, g[3], ms)

    g = gains[1]
    w_qkv = c_w_qkv[0].astype(BF16)
    w_o = c_w_o[0].astype(BF16)
    keep = min(MAX_WINDOW, seq)
    *qkv_p, kt_tail, vt_tail = _qkv_prompt(hp, g[0], w_qkv, batch, seq, keep, tm_p)
    positions_first = lambda a: jnp.transpose(a.reshape(batch, ATT_HEADS, HEAD_DIM, keep), (0, 3, 1, 2))[None]
    k_prompt = positions_first(kt_tail)
    v_prompt = positions_first(vt_tail)
    q_nat, q_res, k_nat, k_res, v_nat, v_res = qkv_p
    outs, lses = [], []
    for q, k, v, parts in ((q_nat, k_nat, v_nat, 1), (q_res, k_res, v_res, MID_RATIO), (q_res, k_res, v_res, 1)):
        o, lse = _attn_prompt(q, k, v, parts)
        outs.append(o)
        lses.append(lse)
    hp = _attn_out(outs, lses, w_o, hp, g[1], seq, tm_p)

    scale = HEAD_DIM ** -0.5
    q_s, k_s, v_s = _norm_matmul(
        hs, g[0], w_qkv,
        ((0, ATT_WIDTH, scale), (ATT_WIDTH, ATT_WIDTH, 1.0), (2 * ATT_WIDTH, ATT_WIDTH, 1.0)),
        (F32, F32, F32), ms)
    k_sample = k_s.reshape(1, n_dec, t_dec, ATT_HEADS, HEAD_DIM)
    v_sample = v_s.reshape(1, n_dec, t_dec, ATT_HEADS, HEAD_DIM)
    head_major = lambda a, rows: jnp.pad(
        jnp.swapaxes(a.reshape(n_dec, t_dec, ATT_HEADS, HEAD_DIM), 1, 2),
        ((0, 0), (0, 0), (0, rows - t_dec), (0, 0)))
    cache_kt = jnp.transpose(cache_k[0], (0, 2, 3, 1))
    cache_vt = jnp.transpose(cache_v[0], (0, 2, 3, 1))
    o_s = _attn_sample(head_major(q_s, SUBLANES), head_major(k_s, LANES), head_major(v_s, LANES),
                       cache_kt, cache_vt, t_dec)
    hs = _proj_out(o_s.reshape(ms, ATT_WIDTH), w_o, hs, g[1], ms)

    wg, wu, wd = ffn_w_gate[1].astype(BF16), ffn_w_up[1].astype(BF16), ffn_w_down[1].astype(BF16)
    hp = _ffn(hp, g[2], wg, wu, wd, g[3], tm_ffn)
    hs = _ffn(hs, g[2], wg, wu, wd, g[3], ms)

    return (hp.reshape(batch, seq, d), hs.reshape(n_dec, t_dec, d), pool_prompt, s5_prompt,
            k_prompt, v_prompt, pool_sample, s5_sample, k_sample, v_sample)
```

```python
import functools
import math

import jax
import jax.numpy as jnp
from jax import lax
from jax.experimental import pallas as pl
from jax.experimental.pallas import tpu as pltpu

F32 = jnp.float32
BF16 = jnp.bfloat16

D_MODEL = 1024
PAST_LEN = 16384
POOL_WIDTH = 512
POOL_WINDOWS = (2, 4, 8, 16)
POOL_GROUP = 128
POOL_STATE = 15
POOL_LEAD = 16
POOL_SEQS_PER_STEP = 8
SSM_WIDTH = 512
SSM_GROUP = 16
SSM_GROUPS = 32
SSM_STATE = 64
SSM_STATES = SSM_GROUPS * SSM_STATE
ATT_HEADS = 16
HEAD_DIM = 64
ATT_WIDTH = ATT_HEADS * HEAD_DIM
DILATED_PATTERNS = ((128, 1), (512, 4), (2048, 16))
MAX_WINDOW = 2048
ATT_BLOCK = 128
ATT_SPAN = 128
FFN_HIDDEN = 2816
RMS_EPS = 1e-6

SUBLANES = 8
LANES = 128
VMEM_LIMIT = 56 * 1024 * 1024
PROMPT_ROW_TILE = 512
S5_ROW_TILE = 2048


def _cparams(n_axes):
    return pltpu.CompilerParams(
        dimension_semantics=("arbitrary",) * n_axes, vmem_limit_bytes=VMEM_LIMIT)


def _rms(x, g):
    ms = jnp.mean(x * x, axis=-1, keepdims=True)
    return (x * lax.rsqrt(ms + RMS_EPS)) * g


def _const_spec(shape):
    zeros = (0,) * len(shape)
    return pl.BlockSpec(shape, lambda *_: zeros, pipeline_mode=pl.Buffered(1))


def _norm_matmul_kernel(x_ref, g_ref, w_ref, *out_refs, splits):
    xb = _rms(x_ref[...], g_ref[...]).astype(BF16)
    for (c0, width, scale), o_ref in zip(splits, out_refs):
        y = jnp.dot(xb, w_ref[:, c0:c0 + width], preferred_element_type=F32)
        if scale != 1.0:
            y = y * scale
        o_ref[...] = y.astype(o_ref.dtype)


def _norm_matmul(x, g, w, splits, dtypes, tm):
    m, d = x.shape
    n = w.shape[1]
    return pl.pallas_call(
        functools.partial(_norm_matmul_kernel, splits=splits),
        grid=(m // tm,),
        in_specs=[pl.BlockSpec((tm, d), lambda i: (i, 0)), _const_spec((1, d)), _const_spec((d, n))],
        out_specs=[pl.BlockSpec((tm, width), lambda i: (i, 0)) for _, width, _ in splits],
        out_shape=[jax.ShapeDtypeStruct((m, width), dt) for (_, width, _), dt in zip(splits, dtypes)],
        compiler_params=_cparams(1),
        name="norm_matmul",
    )(x, g, w)


ATT_DILATIONS = tuple(dil for _, dil in DILATED_PATTERNS)
ATT_ORDERS = (ATT_DILATIONS[0], ATT_DILATIONS[2])
PATTERN_ORDER = (ATT_DILATIONS[0], ATT_DILATIONS[2], ATT_DILATIONS[2])
MID_RATIO = ATT_DILATIONS[2] // ATT_DILATIONS[1]
COL_TILES = ATT_WIDTH // LANES
PERM_TILE = 256
ATT_BLOCKS_PER_STEP = 2


def _qkv_prompt_kernel(x_ref, g_ref, w_ref, perm_ref, *refs, q_scale, tm, tiles, first_tail):
    n_dil = len(ATT_ORDERS)
    out_refs = refs[:3 * n_dil]
    tail_refs = refs[3 * n_dil:]
    in_tail = (pl.program_id(0) % tiles) >= first_tail
    xb = _rms(x_ref[...], g_ref[...]).astype(BF16)
    for which in range(3):
        y = jnp.dot(xb, w_ref[:, which * ATT_WIDTH:(which + 1) * ATT_WIDTH], preferred_element_type=F32)
        if which == 0:
            y = y * q_scale
        else:
            @pl.when(in_tail)
            def _(y=y, tail_ref=tail_refs[which - 1]):
                tail_ref[0] = y.T
        yb = y.astype(BF16)
        for pat, (dil, o_ref) in enumerate(zip(ATT_ORDERS, out_refs[which * n_dil:(which + 1) * n_dil])):
            if dil == 1:
                o_ref[0, 0] = yb
                continue
            rows = PERM_TILE // dil
            for part in range(tm // PERM_TILE):
                moved = jnp.dot(perm_ref[pat - 1], yb[part * PERM_TILE:(part + 1) * PERM_TILE],
                                preferred_element_type=F32).astype(BF16)
                for res in range(dil):
                    o_ref[0, res, part * rows:(part + 1) * rows] = moved[res * rows:(res + 1) * rows]


def _qkv_prompt(x, g, w, batch, seq, keep, tm):
    m, d = x.shape
    assert tm % PERM_TILE == 0
    tiles = seq // tm
    first_tail = tiles - keep // tm

    def tail_map(i):
        return (i // tiles, 0, jnp.maximum(i % tiles - first_tail, 0))

    def dil_spec(dil):
        return pl.BlockSpec((1, dil, tm // dil, ATT_WIDTH), lambda i: (i // tiles, 0, i % tiles, 0))

    def dil_shape(dil):
        return jax.ShapeDtypeStruct((batch, dil, seq // dil, ATT_WIDTH), BF16)

    assert ATT_ORDERS[0] == 1
    src = jnp.arange(PERM_TILE)[None, :]
    dst = jnp.arange(PERM_TILE)[:, None]
    perm = jnp.stack([(src == (dst % (PERM_TILE // dil)) * dil + dst // (PERM_TILE // dil)).astype(BF16)
                      for dil in ATT_ORDERS[1:]])
    return pl.pallas_call(
        functools.partial(_qkv_prompt_kernel, q_scale=HEAD_DIM ** -0.5, tm=tm, tiles=tiles,
                          first_tail=first_tail),
        grid=(m // tm,),
        in_specs=[pl.BlockSpec((tm, d), lambda i: (i, 0)), _const_spec((1, d)),
                  _const_spec((d, 3 * ATT_WIDTH)), _const_spec(perm.shape)],
        out_specs=[dil_spec(dil) for _ in range(3) for dil in ATT_ORDERS]
        + [pl.BlockSpec((1, ATT_WIDTH, tm), tail_map)] * 2,
        out_shape=[dil_shape(dil) for _ in range(3) for dil in ATT_ORDERS]
        + [jax.ShapeDtypeStruct((batch, ATT_WIDTH, keep), F32)] * 2,
        compiler_params=_cparams(1),
        name="qkv_prompt",
    )(x, g, w, perm)


def _pool_tile(u, j, lead_ref, w_ref, scale_ref, y_ref, ext, *, tt, pos0, carry, seq=0):
    @pl.when(j == 0)
    def _():
        ext[0:POOL_LEAD, :] = lead_ref[seq]

    ext[POOL_LEAD:POOL_LEAD + tt, :] = u
    pos = pos0 + j * tt + lax.broadcasted_iota(jnp.int32, (tt, POOL_GROUP), 0)
    for grp, window in enumerate(POOL_WINDOWS):
        sl = slice(grp * POOL_GROUP, (grp + 1) * POOL_GROUP)
        u_g = u[:, sl]
        win_sum = u_g
        for k in range(1, window):
            win_sum = win_sum + ext[POOL_LEAD - k:POOL_LEAD - k + tt, sl]
        count = jnp.minimum(pos + 1, window).astype(F32)
        diff = win_sum / count - u_g
        y = jnp.dot(diff.astype(BF16), w_ref[grp], preferred_element_type=F32)
        y_ref[seq, :, sl] = (y * scale_ref[:, sl]).astype(y_ref.dtype)
    if carry:
        ext[0:POOL_LEAD, :] = ext[tt:tt + POOL_LEAD, :]


def _pool_kernel(u_ref, lead_ref, w_ref, scale_ref, y_ref, ext, *, tt, pos0, carry, n_seq):
    for seq in range(n_seq):
        _pool_tile(u_ref[seq], pl.program_id(1), lead_ref, w_ref, scale_ref, y_ref, ext,
                   tt=tt, pos0=pos0, carry=carry, seq=seq)


def _in_proj_pool_kernel(x_ref, g_ref, w_ref, lead_ref, pw_ref, scale_ref, ussm_ref, ypool_ref, tail_ref, ext,
                         *, tt, pos0):
    xb = _rms(x_ref[0], g_ref[...]).astype(BF16)
    ussm_ref[0] = jnp.dot(xb, w_ref[:, POOL_WIDTH:], preferred_element_type=F32)
    u_pool = jnp.dot(xb, w_ref[:, 0:POOL_WIDTH], preferred_element_type=F32)
    _pool_tile(u_pool, pl.program_id(1), lead_ref, pw_ref, scale_ref, ypool_ref, ext, tt=tt, pos0=pos0, carry=True)
    tail_ref[0] = ext[0:POOL_LEAD, :]


def _in_proj_pool(x, g, w, lead, pool_w, pool_scale, tt, pos0):
    n, t, d = x.shape
    seq_block = lambda width: pl.BlockSpec((1, tt, width), lambda b, j: (b, j, 0))
    per_seq = pl.BlockSpec((1, POOL_LEAD, POOL_WIDTH), lambda b, j: (b, 0, 0))
    return pl.pallas_call(
        functools.partial(_in_proj_pool_kernel, tt=tt, pos0=pos0),
        grid=(n, t // tt),
        in_specs=[seq_block(d), _const_spec(g.shape), _const_spec(w.shape), per_seq,
                  _const_spec(pool_w.shape), _const_spec(pool_scale.shape)],
        out_specs=[seq_block(SSM_WIDTH), seq_block(POOL_WIDTH), per_seq],
        out_shape=[jax.ShapeDtypeStruct((n, t, SSM_WIDTH), F32), jax.ShapeDtypeStruct((n, t, POOL_WIDTH), BF16),
                   jax.ShapeDtypeStruct((n, POOL_LEAD, POOL_WIDTH), F32)],
        scratch_shapes=[pltpu.VMEM((POOL_LEAD + tt, POOL_WIDTH), F32)],
        compiler_params=_cparams(2),
        name="in_proj_pool",
    )(x, g, w, lead, pool_w, pool_scale)


def _pool_mixer(proj, lead, w, scale, tt, pos0):
    n, t, _ = proj.shape
    steps = t // tt
    n_seq = 1 if steps > 1 else math.gcd(n, POOL_SEQS_PER_STEP)
    return pl.pallas_call(
        functools.partial(_pool_kernel, tt=tt, pos0=pos0, carry=steps > 1, n_seq=n_seq),
        grid=(n // n_seq, steps),
        in_specs=[pl.BlockSpec((n_seq, tt, POOL_WIDTH), lambda b, j: (b, j, 0)),
                  pl.BlockSpec((n_seq, POOL_LEAD, POOL_WIDTH), lambda b, j: (b, 0, 0)),
                  _const_spec((len(POOL_WINDOWS), POOL_GROUP, POOL_GROUP)),
                  _const_spec((1, POOL_WIDTH))],
        out_specs=pl.BlockSpec((n_seq, tt, POOL_WIDTH), lambda b, j: (b, j, 0)),
        out_shape=jax.ShapeDtypeStruct((n, t, POOL_WIDTH), BF16),
        scratch_shapes=[pltpu.VMEM((POOL_LEAD + tt, POOL_WIDTH), F32)],
        compiler_params=_cparams(2),
        name="pool_mixer",
    )(proj, lead, w, scale)


S5_IN_HALF = SSM_WIDTH // 2
S5_STATE_HALF = SSM_STATES // 2
S5_OUT_TILE = LANES
S5_STATE_TILE = SSM_STATES // (SSM_WIDTH // S5_OUT_TILE)
S5_SCAN_LANES = 512


def _s5_input(ub, bre_ref, bim_ref):
    re, im = [], []
    for half in range(2):
        uk = ub[:, half * S5_IN_HALF:(half + 1) * S5_IN_HALF]
        re.append(jnp.dot(uk, bre_ref[half], preferred_element_type=F32))
        im.append(jnp.dot(uk, bim_ref[half], preferred_element_type=F32))
    return re, im


def _s5_output(h_re, h_im, cre_ref, cim_ref, tile):
    return (jnp.dot(h_re, cre_ref[tile], preferred_element_type=F32)
            - jnp.dot(h_im, cim_ref[tile], preferred_element_type=F32))


S5_CHUNK = 16
S5_ROW = S5_CHUNK * SSM_GROUP
S5_SUB = 256
UNIT = SSM_GROUP
UNITS = LANES // UNIT
S5_PAIRS = SSM_GROUPS // 2


def _unit_transpose(pieces):
    unit = lax.broadcasted_iota(jnp.int32, pieces[0].shape, 1) >> (UNIT.bit_length() - 1)
    cur = list(pieces)
    k = UNITS // 2
    while k:
        high = (unit & k) != 0
        nxt = list(cur)
        for a in range(UNITS):
            if a & k:
                continue
            lo_piece, hi_piece = cur[a], cur[a | k]
            nxt[a] = jnp.where(high, pltpu.roll(hi_piece, k * UNIT, 1), lo_piece)
            nxt[a | k] = jnp.where(high, hi_piece, pltpu.roll(lo_piece, LANES - k * UNIT, 1))
        cur = nxt
        k //= 2
    return cur


def _s5_prompt_kernel(u_ref, perm_ref, toep_ref, wre_ref, wim_ref, vre_ref, vim_ref, pw_ref, d_ref,
                      y_ref, hre_ref, him_ref, u2, y2, s_re, s_im, c_re, c_im, y_nat, *, tt):
    j = pl.program_id(1)
    chunks = tt // S5_CHUNK
    sub_chunks = S5_SUB // S5_CHUNK

    @pl.when(j == 0)
    def _():
        c_re[...] = jnp.zeros_like(c_re)
        c_im[...] = jnp.zeros_like(c_im)

    def relayout_in(sub, carry):
        r0 = pl.multiple_of(sub * S5_SUB, S5_SUB)
        c0 = pl.multiple_of(sub * sub_chunks, sub_chunks)
        ub = u_ref[0, pl.ds(r0, S5_SUB), :].astype(BF16)
        xs = jnp.dot(perm_ref[...], ub, preferred_element_type=F32).astype(BF16)
        for tile in range(SSM_WIDTH // LANES):
            for half in range(S5_CHUNK // UNITS):
                pieces = [pltpu.bitcast(
                    xs[(half * UNITS + a) * sub_chunks:(half * UNITS + a + 1) * sub_chunks,
                       tile * LANES:(tile + 1) * LANES], jnp.uint32) for a in range(UNITS)]
                outs = _unit_transpose(pieces)
                for gl in range(UNITS):
                    u2[tile * UNITS + gl, pl.ds(c0, sub_chunks), half * LANES:(half + 1) * LANES] = (
                        pltpu.bitcast(outs[gl], BF16))
        return carry

    lax.fori_loop(0, tt // S5_SUB, relayout_in, 0, unroll=2)

    s_re[0:SUBLANES, :] = c_re[...]
    s_im[0:SUBLANES, :] = c_im[...]
    for pair in range(S5_PAIRS):
        sl = slice(pair * LANES, (pair + 1) * LANES)
        ua = u2[2 * pair]
        ub = u2[2 * pair + 1]
        s_re[SUBLANES:SUBLANES + chunks, sl] = (
            jnp.dot(ua, wre_ref[2 * pair], preferred_element_type=F32)
            + jnp.dot(ub, wre_ref[2 * pair + 1], preferred_element_type=F32))
        s_im[SUBLANES:SUBLANES + chunks, sl] = (
            jnp.dot(ua, wim_ref[2 * pair], preferred_element_type=F32)
            + jnp.dot(ub, wim_ref[2 * pair + 1], preferred_element_type=F32))

    for chunk in range(SSM_STATES // S5_SCAN_LANES):
        sl = slice(chunk * S5_SCAN_LANES, (chunk + 1) * S5_SCAN_LANES)

        def body(r, carry, sl=sl):
            in_re, in_im = carry
            row = pl.multiple_of((r + 1) * SUBLANES, SUBLANES)
            re = s_re[pl.ds(row, SUBLANES), sl]
            im = s_im[pl.ds(row, SUBLANES), sl]
            for level, shift in enumerate((1, 2, 4)):
                a_re = pw_ref[2 * level, :, sl]
                a_im = pw_ref[2 * level + 1, :, sl]
                sh_re = pltpu.roll(re, shift, 0)
                sh_im = pltpu.roll(im, shift, 0)
                re, im = (re + a_re * sh_re - a_im * sh_im,
                          im + a_re * sh_im + a_im * sh_re)
            p_re = pw_ref[6, :, sl]
            p_im = pw_ref[7, :, sl]
            re, im = (re + p_re * in_re - p_im * in_im,
                      im + p_re * in_im + p_im * in_re)
            s_re[pl.ds(row, SUBLANES), sl] = re
            s_im[pl.ds(row, SUBLANES), sl] = im
            last = SUBLANES - 1
            return (jnp.broadcast_to(re[last:last + 1, :], re.shape),
                    jnp.broadcast_to(im[last:last + 1, :], im.shape))

        out_re, out_im = lax.fori_loop(0, chunks // SUBLANES, body, (c_re[:, sl], c_im[:, sl]))
        c_re[:, sl] = out_re
        c_im[:, sl] = out_im

    hre_ref[0] = c_re[...]
    him_ref[0] = c_im[...]

    for pair in range(S5_PAIRS):
        sl = slice(pair * LANES, (pair + 1) * LANES)
        h_re = s_re[SUBLANES - 1:SUBLANES - 1 + chunks, sl].astype(BF16)
        h_im = s_im[SUBLANES - 1:SUBLANES - 1 + chunks, sl].astype(BF16)
        carried = (jnp.dot(h_re, vre_ref[pair], preferred_element_type=F32)
                   + jnp.dot(h_im, vim_ref[pair], preferred_element_type=F32))
        for k in range(2):
            grp = 2 * pair + k
            y2[grp] = (jnp.dot(u2[grp], toep_ref[grp], preferred_element_type=F32)
                       + carried[:, k * S5_ROW:(k + 1) * S5_ROW])

    def relayout_out(blk, carry):
        r0 = pl.multiple_of(blk * S5_SUB, S5_SUB)
        c0 = pl.multiple_of(blk * sub_chunks, sub_chunks)
        for tile in range(SSM_WIDTH // LANES):
            ot = slice(tile * LANES, (tile + 1) * LANES)
            for half in range(S5_CHUNK // UNITS):
                pieces = [y2[tile * UNITS + gl, pl.ds(c0, sub_chunks), half * LANES:(half + 1) * LANES]
                          for gl in range(UNITS)]
                outs = _unit_transpose(pieces)
                for a in range(UNITS):
                    y_nat[tile, pl.ds(half * UNITS + a, sub_chunks, stride=S5_CHUNK), :] = outs[a]
            y_ref[0, pl.ds(r0, S5_SUB), ot] = y_nat[tile] + d_ref[:, ot] * u_ref[0, pl.ds(r0, S5_SUB), ot]
        return carry

    lax.fori_loop(0, tt // S5_SUB, relayout_out, 0)


def _s5_prompt(u, mats, tt):
    b, t, _ = u.shape
    chunks = tt // S5_CHUNK
    state = jax.ShapeDtypeStruct((b, SUBLANES, SSM_STATES), F32)
    state_spec = pl.BlockSpec((1, SUBLANES, SSM_STATES), lambda i, j: (i, 0, 0))
    return pl.pallas_call(
        functools.partial(_s5_prompt_kernel, tt=tt),
        grid=(b, t // tt),
        in_specs=[pl.BlockSpec((1, tt, SSM_WIDTH), lambda i, j: (i, j, 0))]
        + [_const_spec(a.shape) for a in mats],
        out_specs=[pl.BlockSpec((1, tt, SSM_WIDTH), lambda i, j: (i, j, 0)), state_spec, state_spec],
        out_shape=[jax.ShapeDtypeStruct((b, t, SSM_WIDTH), F32), state, state],
        scratch_shapes=[pltpu.VMEM((SSM_GROUPS, chunks, S5_ROW), BF16),
                        pltpu.VMEM((SSM_GROUPS, chunks, S5_ROW), F32),
                        pltpu.VMEM((SUBLANES + chunks, SSM_STATES), F32),
                        pltpu.VMEM((SUBLANES + chunks, SSM_STATES), F32),
                        pltpu.VMEM((SUBLANES, SSM_STATES), F32), pltpu.VMEM((SUBLANES, SSM_STATES), F32),
                        pltpu.VMEM((SSM_WIDTH // LANES, S5_SUB, LANES), F32)],
        compiler_params=_cparams(2),
        name="s5_prompt",
    )(u, *mats)


def _s5_sample_kernel(u_ref, h0re_ref, h0im_ref, bre_ref, bim_ref, cre_ref, cim_ref, lam_ref, d_ref,
                      y_ref, hre_ref, him_ref, *, steps):
    h_re = h0re_ref[...]
    h_im = h0im_ref[...]
    lam_re = lam_ref[0:1, :]
    lam_im = lam_ref[1:2, :]
    for t in range(steps):
        u = u_ref[t]
        bu_re, bu_im = _s5_input(u.astype(BF16), bre_ref, bim_ref)
        bu_re = jnp.concatenate(bu_re, axis=1)
        bu_im = jnp.concatenate(bu_im, axis=1)
        h_re, h_im = (lam_re * h_re - lam_im * h_im + bu_re,
                      lam_re * h_im + lam_im * h_re + bu_im)
        hb_re = h_re.astype(BF16)
        hb_im = h_im.astype(BF16)
        for tile in range(SSM_WIDTH // S5_OUT_TILE):
            st = slice(tile * S5_STATE_TILE, (tile + 1) * S5_STATE_TILE)
            ot = slice(tile * S5_OUT_TILE, (tile + 1) * S5_OUT_TILE)
            y = _s5_output(hb_re[:, st], hb_im[:, st], cre_ref, cim_ref, tile)
            y_ref[t, :, ot] = y + d_ref[:, ot] * u[:, ot]
    hre_ref[...] = h_re
    him_ref[...] = h_im


def _s5_sample(u_tb, h0_re, h0_im, mats, lam):
    steps, n, _ = u_tb.shape
    bre, bim, cre, cim, d_skip = mats
    state = jax.ShapeDtypeStruct((n, SSM_STATES), F32)
    args = (u_tb, h0_re, h0_im, bre, bim, cre, cim, lam, d_skip)
    return pl.pallas_call(
        functools.partial(_s5_sample_kernel, steps=steps),
        grid=(1,),
        in_specs=[_const_spec(a.shape) for a in args],
        out_specs=[_const_spec(u_tb.shape), _const_spec((n, SSM_STATES)), _const_spec((n, SSM_STATES))],
        out_shape=[jax.ShapeDtypeStruct(u_tb.shape, F32), state, state],
        compiler_params=_cparams(1),
        name="s5_sample",
    )(*args)


def _s5_matrices(lam_re, lam_im, log_dt, b_re, b_im, c_re, c_im, d_skip):
    lam = lax.complex(lam_re, lam_im)
    dt = jnp.exp(log_dt)[:, None]
    lam_bar = jnp.exp(lam * dt)
    b_bar = ((lam_bar - 1.0) / lam)[..., None] * lax.complex(b_re, b_im)
    eye_half = jnp.eye(SSM_GROUPS // 2, dtype=F32)

    def in_blocks(x):
        x = x.reshape(2, SSM_GROUPS // 2, SSM_STATE, SSM_GROUP)
        return jnp.einsum("kgpi,gh->kgihp", x, eye_half).reshape(2, S5_IN_HALF, S5_STATE_HALF).astype(BF16)

    tiles = SSM_WIDTH // S5_OUT_TILE
    groups_per_tile = SSM_GROUPS // tiles
    eye_tile = jnp.eye(groups_per_tile, dtype=F32)

    def out_blocks(x):
        x = x.reshape(tiles, groups_per_tile, SSM_GROUP, SSM_STATE)
        return jnp.einsum("kgop,gh->kgpho", x, eye_tile).reshape(tiles, S5_STATE_TILE, S5_OUT_TILE).astype(BF16)

    lam_flat = lam_bar.reshape(1, SSM_STATES)
    lam_rows = jnp.concatenate([lam_flat.real, lam_flat.imag], axis=0).astype(F32)
    d_row = d_skip.reshape(1, SSM_WIDTH)
    sample_mats = (in_blocks(b_bar.real), in_blocks(b_bar.imag), out_blocks(c_re), out_blocks(c_im), d_row)

    c_mat = lax.complex(c_re, c_im)
    lam_pow = jnp.concatenate([jnp.ones((1,) + lam_bar.shape, lam_bar.dtype),
                               jnp.cumprod(jnp.broadcast_to(lam_bar, (S5_CHUNK,) + lam_bar.shape), axis=0)])
    taps = jnp.einsum("gop,kgp,gpi->gkoi", c_mat, lam_pow[:S5_CHUNK], b_bar,
                      precision=lax.Precision.HIGHEST).real
    src = jnp.arange(S5_CHUNK)[None, :, None]
    dst = jnp.arange(S5_CHUNK)[None, None, :]
    lag_is = (dst - src == jnp.arange(S5_CHUNK)[:, None, None]).astype(F32)
    toep = jnp.einsum("kst,gkoi->gsito", lag_is, taps, precision=lax.Precision.HIGHEST)
    toep = toep.reshape(SSM_GROUPS, S5_ROW, S5_ROW).astype(BF16)
    w_state = jnp.einsum("sgp,gpi->gsip", lam_pow[S5_CHUNK - 1::-1][:S5_CHUNK], b_bar)
    w_state = w_state.reshape(SSM_GROUPS, S5_ROW, SSM_STATE)
    second = (jnp.arange(SSM_GROUPS) % 2 == 1)[:, None, None]
    zeros = jnp.zeros_like(w_state.real)

    def pair_cols(x):
        return jnp.where(second, jnp.concatenate([zeros, x], -1), jnp.concatenate([x, zeros], -1)).astype(BF16)

    v_out = jnp.einsum("gop,tgp->gpto", c_mat, lam_pow[1:]).reshape(SSM_GROUPS, SSM_STATE, S5_ROW)
    eye_pair = jnp.eye(2, dtype=F32)

    def pair_blocks(x):
        x = x.reshape(S5_PAIRS, 2, SSM_STATE, S5_ROW)
        return jnp.einsum("kapc,ab->kapbc", x, eye_pair).reshape(S5_PAIRS, LANES, 2 * S5_ROW).astype(BF16)

    lam_chunk = lam_pow[S5_CHUNK].reshape(1, SSM_STATES)
    rows = jnp.arange(SUBLANES)[:, None]
    planes = []
    for shift in (1, 2, 4):
        a = jnp.where(rows >= shift, lam_chunk ** shift, 0.0)
        planes += [a.real, a.imag]
    carry = jnp.cumprod(jnp.broadcast_to(lam_chunk, (SUBLANES, SSM_STATES)), axis=0)
    planes += [carry.real, carry.imag]
    powers = jnp.stack(planes).astype(F32)
    pos = jnp.arange(S5_SUB)
    perm = (jnp.arange(S5_SUB)[None, :] == ((pos % (S5_SUB // S5_CHUNK)) * S5_CHUNK
                                            + pos // (S5_SUB // S5_CHUNK))[:, None]).astype(BF16)
    prompt_mats = (perm, toep, pair_cols(w_state.real), pair_cols(w_state.imag),
                   pair_blocks(v_out.real), pair_blocks(-v_out.imag), powers, d_row)
    return prompt_mats, sample_mats, lam_rows


def _ab_out_kernel(yp_ref, ys_ref, wglu_ref, bglu_ref, wout_ref, h_ref, g_ref, o_ref):
    z = jax.nn.gelu(ys_ref[...])
    gate = jnp.dot(z.astype(BF16), wglu_ref[...], preferred_element_type=F32) + bglu_ref[...]
    y_ssm = z * jax.nn.sigmoid(gate)
    y = (jnp.dot(yp_ref[...], wout_ref[0:POOL_WIDTH, :], preferred_element_type=F32)
         + jnp.dot(y_ssm.astype(BF16), wout_ref[POOL_WIDTH:, :], preferred_element_type=F32))
    o_ref[...] = h_ref[...] + _rms(y, g_ref[...])


def _ab_out(y_pool, y_ssm, w_glu, b_glu, w_out, h, g, tm):
    m = h.shape[0]
    row = lambda width: pl.BlockSpec((tm, width), lambda i: (i, 0))
    return pl.pallas_call(
        _ab_out_kernel,
        grid=(m // tm,),
        in_specs=[row(POOL_WIDTH), row(SSM_WIDTH), _const_spec(w_glu.shape), _const_spec(b_glu.shape),
                  _const_spec(w_out.shape), row(D_MODEL), _const_spec(g.shape)],
        out_specs=row(D_MODEL),
        out_shape=jax.ShapeDtypeStruct((m, D_MODEL), F32),
        compiler_params=_cparams(1),
        name="ab_out",
    )(y_pool, y_ssm, w_glu, b_glu, w_out, h, g)


def _ffn_kernel(h_ref, gin_ref, wg_ref, wu_ref, wd_ref, gout_ref, o_ref):
    h = h_ref[...]
    xb = _rms(h, gin_ref[...]).astype(BF16)
    gate = jnp.dot(xb, wg_ref[...], preferred_element_type=F32)
    up = jnp.dot(xb, wu_ref[...], preferred_element_type=F32)
    act = (gate * jax.nn.sigmoid(gate) * up).astype(BF16)
    y = jnp.dot(act, wd_ref[...], preferred_element_type=F32)
    o_ref[...] = h + _rms(y, gout_ref[...])


def _ffn(h, g_in, w_gate, w_up, w_down, g_out, tm):
    m = h.shape[0]
    row = pl.BlockSpec((tm, D_MODEL), lambda i: (i, 0))
    return pl.pallas_call(
        _ffn_kernel,
        grid=(m // tm,),
        in_specs=[row, _const_spec(g_in.shape), _const_spec(w_gate.shape), _const_spec(w_up.shape),
                  _const_spec(w_down.shape), _const_spec(g_out.shape)],
        out_specs=row,
        out_shape=jax.ShapeDtypeStruct((m, D_MODEL), F32),
        compiler_params=_cparams(1),
        name="ffn",
    )(h, g_in, w_gate, w_up, w_down, g_out)


def _attn_prompt_kernel(q_ref, kc_ref, kp_ref, vc_ref, vp_ref, o_ref, lse_ref, *, parts):
    blk = pl.program_id(2)
    rows = 2 * ATT_BLOCK
    rpp = ATT_BLOCK // parts
    shift = rpp.bit_length() - 1
    qi = lax.broadcasted_iota(jnp.int32, (rows, 2 * ATT_BLOCK), 0) & (ATT_BLOCK - 1)
    kj = lax.broadcasted_iota(jnp.int32, (rows, 2 * ATT_BLOCK), 1)
    k_in = kj & (ATT_BLOCK - 1)
    is_cur = kj >> (ATT_BLOCK.bit_length() - 1)
    q_row = rpp + (qi & (rpp - 1))
    k_row = is_cur * rpp + (k_in & (rpp - 1))
    dist = parts * (q_row - k_row) + ((qi >> shift) - (k_in >> shift))
    in_band = (dist >= 0) & (dist <= ATT_SPAN)
    lane = lax.broadcasted_iota(jnp.int32, (ATT_BLOCK, LANES), 1)
    first_head = lane < HEAD_DIM

    def gather(ref, first, sl):
        return jnp.concatenate([ref[c, first:first + rpp, sl] for c in range(parts)], axis=0)

    for sub in range(ATT_BLOCKS_PER_STEP):
        here = sub * rpp
        valid = in_band if sub else in_band & ((kj >= ATT_BLOCK) | (blk > 0))
        lse_all = jnp.zeros((ATT_BLOCK, LANES), F32)
        for pair in range(ATT_HEADS // 2):
            sl = slice(pair * LANES, (pair + 1) * LANES)
            q2 = gather(q_ref, here, sl)
            zero = jnp.zeros_like(q2)
            qs = jnp.concatenate([jnp.where(first_head, q2, zero), jnp.where(first_head, zero, q2)], axis=0)
            k_before = gather(kc_ref, here - rpp, sl) if sub else gather(kp_ref, 0, sl)
            v_before = gather(vc_ref, here - rpp, sl) if sub else gather(vp_ref, 0, sl)
            k2 = jnp.concatenate([k_before, gather(kc_ref, here, sl)], axis=0)
            v2 = jnp.concatenate([v_before, gather(vc_ref, here, sl)], axis=0)
            s = lax.dot_general(qs, k2, (((1,), (1,)), ((), ())), preferred_element_type=F32)
            s = jnp.where(valid, s, -jnp.inf)
            m = jnp.max(s, axis=-1, keepdims=True)
            p = jnp.exp(s - m)
            den = jnp.sum(p, axis=-1, keepdims=True)
            o = jnp.dot(p.astype(BF16), v2, preferred_element_type=F32) / den
            lse = m + jnp.log(den)
            o_pair = jnp.where(first_head, o[:ATT_BLOCK], o[ATT_BLOCK:]).astype(o_ref.dtype)
            for c in range(parts):
                o_ref[c, here:here + rpp, sl] = o_pair[c * rpp:(c + 1) * rpp]
            lse_all = jnp.where(lane == 2 * pair, lse[:ATT_BLOCK],
                                jnp.where(lane == 2 * pair + 1, lse[ATT_BLOCK:], lse_all))
        for c in range(parts):
            lse_ref[c, here:here + rpp, :] = lse_all[c * rpp:(c + 1) * rpp]


def _attn_prompt(q, k, v, parts):
    batch, n_streams, sub, _ = q.shape
    pat_streams = n_streams // parts
    rpp = ATT_BLOCK // parts
    step_rows = ATT_BLOCKS_PER_STEP * rpp
    nblk = sub // step_rows
    view = lambda a: a.reshape(batch, parts, pat_streams, sub, a.shape[-1])
    cur = pl.BlockSpec((None, parts, None, step_rows, ATT_WIDTH), lambda b, r, i: (b, 0, r, i, 0))
    prev = pl.BlockSpec((None, parts, None, rpp, ATT_WIDTH),
                        lambda b, r, i: (b, 0, r, jnp.maximum(i * ATT_BLOCKS_PER_STEP - 1, 0), 0))
    o, lse = pl.pallas_call(
        functools.partial(_attn_prompt_kernel, parts=parts),
        grid=(batch, pat_streams, nblk),
        in_specs=[cur, cur, prev, cur, prev],
        out_specs=[cur, pl.BlockSpec((None, parts, None, step_rows, LANES), lambda b, r, i: (b, 0, r, i, 0))],
        out_shape=[jax.ShapeDtypeStruct((batch, parts, pat_streams, sub, ATT_WIDTH), BF16),
                   jax.ShapeDtypeStruct((batch, parts, pat_streams, sub, LANES), F32)],
        compiler_params=_cparams(3),
        name="attn_prompt",
    )(view(q), view(k), view(k), view(v), view(v))
    return o.reshape(batch, n_streams, sub, ATT_WIDTH), lse.reshape(batch, n_streams, sub, LANES)


def _pattern_multiplicity(dist):
    mult = jnp.zeros(dist.shape, F32)
    for window, dil in DILATED_PATTERNS:
        hit = (dist >= 0) & (dist <= window) & ((dist & (dil - 1)) == 0)
        mult = mult + hit.astype(F32)
    return mult


def _attn_sample_kernel(q_ref, knew_ref, vnew_ref, kt_ref, vt_ref, o_ref, *, steps):
    t_cache = lax.broadcasted_iota(jnp.int32, (SUBLANES, MAX_WINDOW), 0)
    pos = lax.broadcasted_iota(jnp.int32, (SUBLANES, MAX_WINDOW), 1)
    mult = _pattern_multiplicity(MAX_WINDOW + t_cache - pos)
    t_new = lax.broadcasted_iota(jnp.int32, (SUBLANES, LANES), 0)
    j_new = lax.broadcasted_iota(jnp.int32, (SUBLANES, LANES), 1)
    mult_new = jnp.where(j_new < steps, _pattern_multiplicity(t_new - j_new), 0.0)
    nt = (((1,), (1,)), ((), ()))
    for head in range(ATT_HEADS):
        q = q_ref[0, head].astype(BF16)
        s = jnp.dot(q, kt_ref[0, head].astype(BF16), preferred_element_type=F32)
        s_new = lax.dot_general(q, knew_ref[0, head].astype(BF16), nt, preferred_element_type=F32)
        s = jnp.where(mult > 0.0, s, -jnp.inf)
        s_new = jnp.where(mult_new > 0.0, s_new, -jnp.inf)
        m = jnp.maximum(jnp.max(s, axis=-1, keepdims=True), jnp.max(s_new, axis=-1, keepdims=True))
        p = mult * jnp.exp(s - m)
        p_new = mult_new * jnp.exp(s_new - m)
        den = jnp.sum(p, axis=-1, keepdims=True) + jnp.sum(p_new, axis=-1, keepdims=True)
        o = lax.dot_general(p.astype(BF16), vt_ref[0, head].astype(BF16), nt, preferred_element_type=F32)
        o = o + jnp.dot(p_new.astype(BF16), vnew_ref[0, head].astype(BF16), preferred_element_type=F32)
        o_ref[0, :, head * HEAD_DIM:(head + 1) * HEAD_DIM] = (o / den)[:steps]


def _attn_sample(q, k_new, v_new, cache_kt, cache_vt, steps):
    n = q.shape[0]
    assert steps <= SUBLANES
    per = lambda a: pl.BlockSpec((1,) + a.shape[1:], lambda b: (b, 0, 0, 0))
    args = (q, k_new, v_new, cache_kt, cache_vt)
    return pl.pallas_call(
        functools.partial(_attn_sample_kernel, steps=steps),
        grid=(n,),
        in_specs=[per(a) for a in args],
        out_specs=pl.BlockSpec((1, steps, ATT_WIDTH), lambda b: (b, 0, 0)),
        out_shape=jax.ShapeDtypeStruct((n, steps, ATT_WIDTH), F32),
        compiler_params=_cparams(1),
        name="attn_sample",
    )(*args)


def _attn_out_kernel(*refs, tm):
    n_pat = len(PATTERN_ORDER)
    o_refs = refs[:n_pat]
    lse_refs = refs[n_pat:2 * n_pat]
    w_ref, expand_ref, unperm_ref, h_ref, g_ref, out_ref, lse_nat = refs[2 * n_pat:]
    o_nat = []
    for pat, dil in enumerate(PATTERN_ORDER):
        if dil == 1:
            lse_nat[pat] = lse_refs[pat][0, 0]
            o_nat.append(o_refs[pat][0, 0].astype(F32))
            continue
        for res in range(dil):
            lse_nat[pat, pl.ds(res, tm // dil, stride=dil), :] = lse_refs[pat][0, res]
        rows = PERM_TILE // dil
        parts = []
        for part in range(tm // PERM_TILE):
            stacked = jnp.concatenate([o_refs[pat][0, res, part * rows:(part + 1) * rows] for res in range(dil)],
                                      axis=0)
            parts.append(jnp.dot(unperm_ref[...], stacked, preferred_element_type=F32))
        o_nat.append(jnp.concatenate(parts, axis=0))
    lses = [lse_nat[pat] for pat in range(n_pat)]
    top = functools.reduce(jnp.maximum, lses)
    es = [jnp.exp(l - top) for l in lses]
    total = functools.reduce(lambda x, y: x + y, es)
    spread = []
    for e in es[:-1]:
        wgt = e / total
        hi = wgt.astype(BF16)
        lo = (wgt - hi.astype(F32)).astype(BF16)
        spread.append(jnp.dot(jnp.concatenate([hi, lo], axis=1), expand_ref[...],
                              preferred_element_type=F32))
    spread.append(1.0 - functools.reduce(lambda x, y: x + y, spread))
    pieces = []
    for c in range(COL_TILES):
        acc = None
        for pat in range(n_pat):
            term = spread[pat][:, c * LANES:(c + 1) * LANES] * o_nat[pat][:, c * LANES:(c + 1) * LANES]
            acc = term if acc is None else acc + term
        pieces.append(acc.astype(BF16))
    a = jnp.concatenate(pieces, axis=1)
    y = jnp.dot(a, w_ref[...], preferred_element_type=F32)
    out_ref[...] = h_ref[...] + _rms(y, g_ref[...])


def _attn_out(outs, lses, w_o, h, g, seq, tm):
    m = h.shape[0]
    tiles = seq // tm
    n_pat = len(PATTERN_ORDER)

    def dil_spec(dil, width):
        return pl.BlockSpec((1, dil, tm // dil, width), lambda i: (i // tiles, 0, i % tiles, 0))

    row = pl.BlockSpec((tm, D_MODEL), lambda i: (i, 0))
    expand = (jnp.arange(LANES)[:, None] == jnp.arange(ATT_WIDTH)[None, :] // HEAD_DIM).astype(BF16)
    expand = jnp.concatenate([expand, expand], axis=0)
    order = ATT_ORDERS[1]
    assert all(dil in (1, order) for dil in PATTERN_ORDER)
    assert tm % PERM_TILE == 0
    nat = jnp.arange(PERM_TILE)
    unperm = (jnp.arange(PERM_TILE)[None, :]
              == ((nat % order) * (PERM_TILE // order) + nat // order)[:, None]).astype(BF16)
    return pl.pallas_call(
        functools.partial(_attn_out_kernel, tm=tm),
        grid=(m // tm,),
        in_specs=[dil_spec(dil, ATT_WIDTH) for dil in PATTERN_ORDER]
        + [dil_spec(dil, LANES) for dil in PATTERN_ORDER]
        + [_const_spec(w_o.shape), _const_spec(expand.shape), _const_spec(unperm.shape), row,
           _const_spec(g.shape)],
        out_specs=row,
        out_shape=jax.ShapeDtypeStruct((m, D_MODEL), F32),
        scratch_shapes=[pltpu.VMEM((n_pat, tm, LANES), F32)],
        compiler_params=_cparams(1),
        name="attn_out",
    )(*outs, *lses, w_o, expand, unperm, h, g)


def _proj_out_kernel(a_ref, w_ref, h_ref, g_ref, out_ref):
    y = jnp.dot(a_ref[...].astype(BF16), w_ref[...], preferred_element_type=F32)
    out_ref[...] = h_ref[...] + _rms(y, g_ref[...])


def _proj_out(a, w, h, g, tm):
    m = h.shape[0]
    return pl.pallas_call(
        _proj_out_kernel,
        grid=(m // tm,),
        in_specs=[pl.BlockSpec((tm, a.shape[1]), lambda i: (i, 0)), _const_spec(w.shape),
                  pl.BlockSpec((tm, D_MODEL), lambda i: (i, 0)), _const_spec(g.shape)],
        out_specs=pl.BlockSpec((tm, D_MODEL), lambda i: (i, 0)),
        out_shape=jax.ShapeDtypeStruct((m, D_MODEL), F32),
        compiler_params=_cparams(1),
        name="proj_out",
    )(a, w, h, g)


def kernel(x_prompt, x_sample, state_pool, state_s5, cache_k, cache_v, norm_gains, ab_w_in, ab_pool_w, ab_pool_scale, ab_lambda_re, ab_lambda_im, ab_log_dt, ab_b_re, ab_b_im, ab_c_re, ab_c_im, ab_d, ab_w_glu, ab_b_glu, ab_w_out, c_w_qkv, c_w_o, ffn_w_gate, ffn_w_up, ffn_w_down):
    batch, seq, d = x_prompt.shape
    n_dec, t_dec, _ = x_sample.shape
    mp, ms = batch * seq, n_dec * t_dec
    tm_p = tm_ffn = PROMPT_ROW_TILE
    gains = norm_gains.reshape(norm_gains.shape[0], 4, 1, d)

    hp = x_prompt.reshape(mp, d)
    hs = x_sample.reshape(ms, d)

    g = gains[0]
    w_in = ab_w_in[0].astype(BF16)
    pool_w = ab_pool_w[0].astype(BF16)
    pool_scale = ab_pool_scale[0].reshape(1, POOL_WIDTH)
    lead_p = jnp.zeros((batch, POOL_LEAD, POOL_WIDTH), F32)
    lead_s = jnp.pad(state_pool[0], ((0, 0), (POOL_LEAD - POOL_STATE, 0), (0, 0)))
    ussm_p, ypool_p, pool_tail = _in_proj_pool(x_prompt, g[0], w_in, lead_p, pool_w, pool_scale, tm_ffn, 0)
    (proj_s,) = _norm_matmul(hs, g[0], w_in, ((0, d, 1.0),), (F32,), ms)
    proj_s = proj_s.reshape(n_dec, t_dec, d)
    ypool_s = _pool_mixer(proj_s, lead_s, pool_w, pool_scale, t_dec, PAST_LEN)
    pool_prompt = pool_tail[:, POOL_LEAD - POOL_STATE:][None]
    pool_sample = jnp.concatenate([state_pool[0], proj_s[:, :, :POOL_WIDTH]], axis=1)[:, -POOL_STATE:][None]

    prompt_mats, sample_mats, lam_rows = _s5_matrices(
        ab_lambda_re[0], ab_lambda_im[0], ab_log_dt[0], ab_b_re[0], ab_b_im[0], ab_c_re[0], ab_c_im[0], ab_d[0])
    yssm_p, hre_p, him_p = _s5_prompt(ussm_p, prompt_mats, S5_ROW_TILE)
    u_tb = jnp.swapaxes(proj_s[:, :, POOL_WIDTH:], 0, 1)
    h0 = state_s5[0].reshape(n_dec, SSM_STATES, 2)
    yssm_tb, hre_s, him_s = _s5_sample(u_tb, h0[..., 0], h0[..., 1], sample_mats, lam_rows)
    yssm_s = jnp.swapaxes(yssm_tb, 0, 1)
    s5_prompt = jnp.stack([hre_p[:, 0], him_p[:, 0]], axis=-1).reshape(1, batch, SSM_GROUPS, SSM_STATE, 2)
    s5_sample = jnp.stack([hre_s, him_s], axis=-1).reshape(1, n_dec, SSM_GROUPS, SSM_STATE, 2)

    w_glu = ab_w_glu[0].astype(BF16)
    b_glu = ab_b_glu[0].reshape(1, SSM_WIDTH)
    w_out = ab_w_out[0].astype(BF16)
    hp = _ab_out(ypool_p.reshape(mp, POOL_WIDTH), yssm_p.reshape(mp, SSM_WIDTH), w_glu, b_glu, w_out, hp, g[1], tm_ffn)
    hs = _ab_out(ypool_s.reshape(ms, POOL_WIDTH), yssm_s.reshape(ms, SSM_WIDTH), w_glu, b_glu, w_out, hs, g[1], ms)

    wg, wu, wd = ffn_w_gate[0].astype(BF16), ffn_w_up[0].astype(BF16), ffn_w_down[0].astype(BF16)
    hp = _ffn(hp, g[2], wg, wu, wd, g[3], tm_ffn)
    hs = _ffn(hs, g[2], wg, wu, wd, g[3], ms)

    g = gains[1]
    w_qkv = c_w_qkv[0].astype(BF16)
    w_o = c_w_o[0].astype(BF16)
    keep = min(MAX_WINDOW, seq)
    *qkv_p, kt_tail, vt_tail = _qkv_prompt(hp, g[0], w_qkv, batch, seq, keep, tm_p)
    positions_first = lambda a: jnp.transpose(a.reshape(batch, ATT_HEADS, HEAD_DIM, keep), (0, 3, 1, 2))[None]
    k_prompt = positions_first(kt_tail)
    v_prompt = positions_first(vt_tail)
    q_nat, q_res, k_nat, k_res, v_nat, v_res = qkv_p
    outs, lses = [], []
    for q, k, v, parts in ((q_nat, k_nat, v_nat, 1), (q_res, k_res, v_res, MID_RATIO), (q_res, k_res, v_res, 1)):
        o, lse = _attn_prompt(q, k, v, parts)
        outs.append(o)
        lses.append(lse)
    hp = _attn_out(outs, lses, w_o, hp, g[1], seq, tm_p)

    scale = HEAD_DIM ** -0.5
    q_s, k_s, v_s = _norm_matmul(
        hs, g[0], w_qkv,
        ((0, ATT_WIDTH, scale), (ATT_WIDTH, ATT_WIDTH, 1.0), (2 * ATT_WIDTH, ATT_WIDTH, 1.0)),
        (F32, F32, F32), ms)
    k_sample = k_s.reshape(1, n_dec, t_dec, ATT_HEADS, HEAD_DIM)
    v_sample = v_s.reshape(1, n_dec, t_dec, ATT_HEADS, HEAD_DIM)
    head_major = lambda a, rows: jnp.pad(
        jnp.swapaxes(a.reshape(n_dec, t_dec, ATT_HEADS, HEAD_DIM), 1, 2),
        ((0, 0), (0, 0), (0, rows - t_dec), (0, 0)))
    cache_kt = jnp.transpose(cache_k[0], (0, 2, 3, 1))
    cache_vt = jnp.transpose(cache_v[0], (0, 2, 3, 1))
    o_s = _attn_sample(head_major(q_s, SUBLANES), head_major(k_s, LANES), head_major(v_s, LANES),
                       cache_kt, cache_vt, t_dec)
    hs = _proj_out(o_s.reshape(ms, ATT_WIDTH), w_o, hs, g[1], ms)

    wg, wu, wd = ffn_w_gate[1].astype(BF16), ffn_w_up[1].astype(BF16), ffn_w_down[1].astype(BF16)
    hp = _ffn(hp, g[2], wg, wu, wd, g[3], tm_ffn)
    hs = _ffn(hs, g[2], wg, wu, wd, g[3], ms)

    return (hp.reshape(batch, seq, d), hs.reshape(n_dec, t_dec, d), pool_prompt, s5_prompt,
            k_prompt, v_prompt, pool_sample, s5_sample, k_sample, v_sample)
```

```python
import functools
import math

import jax
import jax.numpy as jnp
from jax import lax
from jax.experimental import pallas as pl
from jax.experimental.pallas import tpu as pltpu

F32 = jnp.float32
BF16 = jnp.bfloat16

D_MODEL = 1024
PAST_LEN = 16384
POOL_WIDTH = 512
POOL_WINDOWS = (2, 4, 8, 16)
POOL_GROUP = 128
POOL_STATE = 15
POOL_LEAD = 16
POOL_SEQS_PER_STEP = 8
SSM_WIDTH = 512
SSM_GROUP = 16
SSM_GROUPS = 32
SSM_STATE = 64
SSM_STATES = SSM_GROUPS * SSM_STATE
ATT_HEADS = 16
HEAD_DIM = 64
ATT_WIDTH = ATT_HEADS * HEAD_DIM
DILATED_PATTERNS = ((128, 1), (512, 4), (2048, 16))
MAX_WINDOW = 2048
ATT_BLOCK = 128
ATT_SPAN = 128
FFN_HIDDEN = 2816
RMS_EPS = 1e-6

SUBLANES = 8
LANES = 128
VMEM_LIMIT = 56 * 1024 * 1024
PROMPT_ROW_TILE = 512
MIXER_ROW_TILE = 1024
S5_ROW_TILE = 2048


def _cparams(n_axes):
    return pltpu.CompilerParams(
        dimension_semantics=("arbitrary",) * n_axes, vmem_limit_bytes=VMEM_LIMIT)


def _rms(x, g):
    ms = jnp.mean(x * x, axis=-1, keepdims=True)
    return (x * lax.rsqrt(ms + RMS_EPS)) * g


def _const_spec(shape):
    zeros = (0,) * len(shape)
    return pl.BlockSpec(shape, lambda *_: zeros, pipeline_mode=pl.Buffered(1))


def _norm_matmul_kernel(x_ref, g_ref, w_ref, *out_refs, splits):
    xb = _rms(x_ref[...], g_ref[...]).astype(BF16)
    for (c0, width, scale), o_ref in zip(splits, out_refs):
        y = jnp.dot(xb, w_ref[:, c0:c0 + width], preferred_element_type=F32)
        if scale != 1.0:
            y = y * scale
        o_ref[...] = y.astype(o_ref.dtype)


def _norm_matmul(x, g, w, splits, dtypes, tm):
    m, d = x.shape
    n = w.shape[1]
    return pl.pallas_call(
        functools.partial(_norm_matmul_kernel, splits=splits),
        grid=(m // tm,),
        in_specs=[pl.BlockSpec((tm, d), lambda i: (i, 0)), _const_spec((1, d)), _const_spec((d, n))],
        out_specs=[pl.BlockSpec((tm, width), lambda i: (i, 0)) for _, width, _ in splits],
        out_shape=[jax.ShapeDtypeStruct((m, width), dt) for (_, width, _), dt in zip(splits, dtypes)],
        compiler_params=_cparams(1),
        name="norm_matmul",
    )(x, g, w)


ATT_DILATIONS = tuple(dil for _, dil in DILATED_PATTERNS)
ATT_ORDERS = (ATT_DILATIONS[0], ATT_DILATIONS[2])
PATTERN_ORDER = (ATT_DILATIONS[0], ATT_DILATIONS[2], ATT_DILATIONS[2])
MID_RATIO = ATT_DILATIONS[2] // ATT_DILATIONS[1]
COL_TILES = ATT_WIDTH // LANES
PERM_TILE = 256
ATT_BLOCKS_PER_STEP = 4


def _qkv_prompt_kernel(x_ref, g_ref, w_ref, perm_ref, *refs, q_scale, tm, tiles, first_tail):
    n_dil = len(ATT_ORDERS)
    out_refs = refs[:3 * n_dil]
    tail_refs = refs[3 * n_dil:]
    in_tail = (pl.program_id(0) % tiles) >= first_tail
    xb = _rms(x_ref[...], g_ref[...]).astype(BF16)
    for which in range(3):
        y = jnp.dot(xb, w_ref[:, which * ATT_WIDTH:(which + 1) * ATT_WIDTH], preferred_element_type=F32)
        if which == 0:
            y = y * q_scale
        else:
            @pl.when(in_tail)
            def _(y=y, tail_ref=tail_refs[which - 1]):
                tail_ref[0] = y.T
        yb = y.astype(BF16)
        for pat, (dil, o_ref) in enumerate(zip(ATT_ORDERS, out_refs[which * n_dil:(which + 1) * n_dil])):
            if dil == 1:
                o_ref[0, 0] = yb
                continue
            rows = PERM_TILE // dil
            for part in range(tm // PERM_TILE):
                moved = jnp.dot(perm_ref[pat - 1], yb[part * PERM_TILE:(part + 1) * PERM_TILE],
                                preferred_element_type=F32).astype(BF16)
                for res in range(dil):
                    o_ref[0, res, part * rows:(part + 1) * rows] = moved[res * rows:(res + 1) * rows]


def _qkv_prompt(x, g, w, batch, seq, keep, tm):
    m, d = x.shape
    assert tm % PERM_TILE == 0
    tiles = seq // tm
    first_tail = tiles - keep // tm

    def tail_map(i):
        return (i // tiles, 0, jnp.maximum(i % tiles - first_tail, 0))

    def dil_spec(dil):
        return pl.BlockSpec((1, dil, tm // dil, ATT_WIDTH), lambda i: (i // tiles, 0, i % tiles, 0))

    def dil_shape(dil):
        return jax.ShapeDtypeStruct((batch, dil, seq // dil, ATT_WIDTH), BF16)

    assert ATT_ORDERS[0] == 1
    src = jnp.arange(PERM_TILE)[None, :]
    dst = jnp.arange(PERM_TILE)[:, None]
    perm = jnp.stack([(src == (dst % (PERM_TILE // dil)) * dil + dst // (PERM_TILE // dil)).astype(BF16)
                      for dil in ATT_ORDERS[1:]])
    return pl.pallas_call(
        functools.partial(_qkv_prompt_kernel, q_scale=HEAD_DIM ** -0.5, tm=tm, tiles=tiles,
                          first_tail=first_tail),
        grid=(m // tm,),
        in_specs=[pl.BlockSpec((tm, d), lambda i: (i, 0)), _const_spec((1, d)),
                  _const_spec((d, 3 * ATT_WIDTH)), _const_spec(perm.shape)],
        out_specs=[dil_spec(dil) for _ in range(3) for dil in ATT_ORDERS]
        + [pl.BlockSpec((1, ATT_WIDTH, tm), tail_map)] * 2,
        out_shape=[dil_shape(dil) for _ in range(3) for dil in ATT_ORDERS]
        + [jax.ShapeDtypeStruct((batch, ATT_WIDTH, keep), F32)] * 2,
        compiler_params=_cparams(1),
        name="qkv_prompt",
    )(x, g, w, perm)


def _pool_tile(u, j, lead_ref, w_ref, scale_ref, y_ref, ext, *, tt, pos0, carry, seq=0):
    @pl.when(j == 0)
    def _():
        ext[0:POOL_LEAD, :] = lead_ref[seq]

    ext[POOL_LEAD:POOL_LEAD + tt, :] = u
    pos = pos0 + j * tt + lax.broadcasted_iota(jnp.int32, (tt, POOL_GROUP), 0)
    for grp, window in enumerate(POOL_WINDOWS):
        sl = slice(grp * POOL_GROUP, (grp + 1) * POOL_GROUP)
        u_g = u[:, sl]
        win_sum = u_g
        for k in range(1, window):
            win_sum = win_sum + ext[POOL_LEAD - k:POOL_LEAD - k + tt, sl]
        count = jnp.minimum(pos + 1, window).astype(F32)
        diff = win_sum / count - u_g
        y = jnp.dot(diff.astype(BF16), w_ref[grp], preferred_element_type=F32)
        y_ref[seq, :, sl] = (y * scale_ref[:, sl]).astype(y_ref.dtype)
    if carry:
        ext[0:POOL_LEAD, :] = ext[tt:tt + POOL_LEAD, :]


def _pool_kernel(u_ref, lead_ref, w_ref, scale_ref, y_ref, ext, *, tt, pos0, carry, n_seq):
    for seq in range(n_seq):
        _pool_tile(u_ref[seq], pl.program_id(1), lead_ref, w_ref, scale_ref, y_ref, ext,
                   tt=tt, pos0=pos0, carry=carry, seq=seq)


def _in_proj_pool_kernel(x_ref, g_ref, w_ref, lead_ref, pw_ref, scale_ref, ussm_ref, ypool_ref, tail_ref, ext,
                         *, tt, pos0):
    xb = _rms(x_ref[0], g_ref[...]).astype(BF16)
    ussm_ref[0] = jnp.dot(xb, w_ref[:, POOL_WIDTH:], preferred_element_type=F32)
    u_pool = jnp.dot(xb, w_ref[:, 0:POOL_WIDTH], preferred_element_type=F32)
    _pool_tile(u_pool, pl.program_id(1), lead_ref, pw_ref, scale_ref, ypool_ref, ext, tt=tt, pos0=pos0, carry=True)
    tail_ref[0] = ext[0:POOL_LEAD, :]


def _in_proj_pool(x, g, w, lead, pool_w, pool_scale, tt, pos0):
    n, t, d = x.shape
    seq_block = lambda width: pl.BlockSpec((1, tt, width), lambda b, j: (b, j, 0))
    per_seq = pl.BlockSpec((1, POOL_LEAD, POOL_WIDTH), lambda b, j: (b, 0, 0))
    return pl.pallas_call(
        functools.partial(_in_proj_pool_kernel, tt=tt, pos0=pos0),
        grid=(n, t // tt),
        in_specs=[seq_block(d), _const_spec(g.shape), _const_spec(w.shape), per_seq,
                  _const_spec(pool_w.shape), _const_spec(pool_scale.shape)],
        out_specs=[seq_block(SSM_WIDTH), seq_block(POOL_WIDTH), per_seq],
        out_shape=[jax.ShapeDtypeStruct((n, t, SSM_WIDTH), F32), jax.ShapeDtypeStruct((n, t, POOL_WIDTH), BF16),
                   jax.ShapeDtypeStruct((n, POOL_LEAD, POOL_WIDTH), F32)],
        scratch_shapes=[pltpu.VMEM((POOL_LEAD + tt, POOL_WIDTH), F32)],
        compiler_params=_cparams(2),
        name="in_proj_pool",
    )(x, g, w, lead, pool_w, pool_scale)


def _pool_mixer(proj, lead, w, scale, tt, pos0):
    n, t, _ = proj.shape
    steps = t // tt
    n_seq = 1 if steps > 1 else math.gcd(n, POOL_SEQS_PER_STEP)
    return pl.pallas_call(
        functools.partial(_pool_kernel, tt=tt, pos0=pos0, carry=steps > 1, n_seq=n_seq),
        grid=(n // n_seq, steps),
        in_specs=[pl.BlockSpec((n_seq, tt, POOL_WIDTH), lambda b, j: (b, j, 0)),
                  pl.BlockSpec((n_seq, POOL_LEAD, POOL_WIDTH), lambda b, j: (b, 0, 0)),
                  _const_spec((len(POOL_WINDOWS), POOL_GROUP, POOL_GROUP)),
                  _const_spec((1, POOL_WIDTH))],
        out_specs=pl.BlockSpec((n_seq, tt, POOL_WIDTH), lambda b, j: (b, j, 0)),
        out_shape=jax.ShapeDtypeStruct((n, t, POOL_WIDTH), BF16),
        scratch_shapes=[pltpu.VMEM((POOL_LEAD + tt, POOL_WIDTH), F32)],
        compiler_params=_cparams(2),
        name="pool_mixer",
    )(proj, lead, w, scale)


S5_IN_HALF = SSM_WIDTH // 2
S5_STATE_HALF = SSM_STATES // 2
S5_OUT_TILE = LANES
S5_STATE_TILE = SSM_STATES // (SSM_WIDTH // S5_OUT_TILE)
S5_SCAN_LANES = 512


def _s5_input(ub, bre_ref, bim_ref):
    re, im = [], []
    for half in range(2):
        uk = ub[:, half * S5_IN_HALF:(half + 1) * S5_IN_HALF]
        re.append(jnp.dot(uk, bre_ref[half], preferred_element_type=F32))
        im.append(jnp.dot(uk, bim_ref[half], preferred_element_type=F32))
    return re, im


def _s5_output(h_re, h_im, cre_ref, cim_ref, tile):
    return (jnp.dot(h_re, cre_ref[tile], preferred_element_type=F32)
            - jnp.dot(h_im, cim_ref[tile], preferred_element_type=F32))


S5_CHUNK = 16
S5_ROW = S5_CHUNK * SSM_GROUP
S5_SUB = 256
UNIT = SSM_GROUP
UNITS = LANES // UNIT
S5_PAIRS = SSM_GROUPS // 2


def _unit_transpose(pieces):
    unit = lax.broadcasted_iota(jnp.int32, pieces[0].shape, 1) >> (UNIT.bit_length() - 1)
    cur = list(pieces)
    k = UNITS // 2
    while k:
        high = (unit & k) != 0
        nxt = list(cur)
        for a in range(UNITS):
            if a & k:
                continue
            lo_piece, hi_piece = cur[a], cur[a | k]
            nxt[a] = jnp.where(high, pltpu.roll(hi_piece, k * UNIT, 1), lo_piece)
            nxt[a | k] = jnp.where(high, hi_piece, pltpu.roll(lo_piece, LANES - k * UNIT, 1))
        cur = nxt
        k //= 2
    return cur


def _s5_prompt_kernel(u_ref, perm_ref, toep_ref, wre_ref, wim_ref, vre_ref, vim_ref, pw_ref, d_ref,
                      y_ref, hre_ref, him_ref, u2, y2, s_re, s_im, c_re, c_im, y_nat, *, tt):
    j = pl.program_id(1)
    chunks = tt // S5_CHUNK
    sub_chunks = S5_SUB // S5_CHUNK

    @pl.when(j == 0)
    def _():
        c_re[...] = jnp.zeros_like(c_re)
        c_im[...] = jnp.zeros_like(c_im)

    def relayout_in(sub, carry):
        r0 = pl.multiple_of(sub * S5_SUB, S5_SUB)
        c0 = pl.multiple_of(sub * sub_chunks, sub_chunks)
        ub = u_ref[0, pl.ds(r0, S5_SUB), :].astype(BF16)
        xs = jnp.dot(perm_ref[...], ub, preferred_element_type=F32).astype(BF16)
        for tile in range(SSM_WIDTH // LANES):
            for half in range(S5_CHUNK // UNITS):
                pieces = [pltpu.bitcast(
                    xs[(half * UNITS + a) * sub_chunks:(half * UNITS + a + 1) * sub_chunks,
                       tile * LANES:(tile + 1) * LANES], jnp.uint32) for a in range(UNITS)]
                outs = _unit_transpose(pieces)
                for gl in range(UNITS):
                    u2[tile * UNITS + gl, pl.ds(c0, sub_chunks), half * LANES:(half + 1) * LANES] = (
                        pltpu.bitcast(outs[gl], BF16))
        return carry

    lax.fori_loop(0, tt // S5_SUB, relayout_in, 0, unroll=2)

    s_re[0:SUBLANES, :] = c_re[...]
    s_im[0:SUBLANES, :] = c_im[...]
    for pair in range(S5_PAIRS):
        sl = slice(pair * LANES, (pair + 1) * LANES)
        ua = u2[2 * pair]
        ub = u2[2 * pair + 1]
        s_re[SUBLANES:SUBLANES + chunks, sl] = (
            jnp.dot(ua, wre_ref[2 * pair], preferred_element_type=F32)
            + jnp.dot(ub, wre_ref[2 * pair + 1], preferred_element_type=F32))
        s_im[SUBLANES:SUBLANES + chunks, sl] = (
            jnp.dot(ua, wim_ref[2 * pair], preferred_element_type=F32)
            + jnp.dot(ub, wim_ref[2 * pair + 1], preferred_element_type=F32))

    for chunk in range(SSM_STATES // S5_SCAN_LANES):
        sl = slice(chunk * S5_SCAN_LANES, (chunk + 1) * S5_SCAN_LANES)

        def body(r, carry, sl=sl):
            in_re, in_im = carry
            row = pl.multiple_of((r + 1) * SUBLANES, SUBLANES)
            re = s_re[pl.ds(row, SUBLANES), sl]
            im = s_im[pl.ds(row, SUBLANES), sl]
            for level, shift in enumerate((1, 2, 4)):
                a_re = pw_ref[2 * level, :, sl]
                a_im = pw_ref[2 * level + 1, :, sl]
                sh_re = pltpu.roll(re, shift, 0)
                sh_im = pltpu.roll(im, shift, 0)
                re, im = (re + a_re * sh_re - a_im * sh_im,
                          im + a_re * sh_im + a_im * sh_re)
            p_re = pw_ref[6, :, sl]
            p_im = pw_ref[7, :, sl]
            re, im = (re + p_re * in_re - p_im * in_im,
                      im + p_re * in_im + p_im * in_re)
            s_re[pl.ds(row, SUBLANES), sl] = re
            s_im[pl.ds(row, SUBLANES), sl] = im
            last = SUBLANES - 1
            return (jnp.broadcast_to(re[last:last + 1, :], re.shape),
                    jnp.broadcast_to(im[last:last + 1, :], im.shape))

        out_re, out_im = lax.fori_loop(0, chunks // SUBLANES, body, (c_re[:, sl], c_im[:, sl]))
        c_re[:, sl] = out_re
        c_im[:, sl] = out_im

    hre_ref[0] = c_re[...]
    him_ref[0] = c_im[...]

    for pair in range(S5_PAIRS):
        sl = slice(pair * LANES, (pair + 1) * LANES)
        h_re = s_re[SUBLANES - 1:SUBLANES - 1 + chunks, sl].astype(BF16)
        h_im = s_im[SUBLANES - 1:SUBLANES - 1 + chunks, sl].astype(BF16)
        carried = (jnp.dot(h_re, vre_ref[pair], preferred_element_type=F32)
                   + jnp.dot(h_im, vim_ref[pair], preferred_element_type=F32))
        for k in range(2):
            grp = 2 * pair + k
            y2[grp] = (jnp.dot(u2[grp], toep_ref[grp], preferred_element_type=F32)
                       + carried[:, k * S5_ROW:(k + 1) * S5_ROW])

    def relayout_out(blk, carry):
        r0 = pl.multiple_of(blk * S5_SUB, S5_SUB)
        c0 = pl.multiple_of(blk * sub_chunks, sub_chunks)
        for tile in range(SSM_WIDTH // LANES):
            ot = slice(tile * LANES, (tile + 1) * LANES)
            for half in range(S5_CHUNK // UNITS):
                pieces = [y2[tile * UNITS + gl, pl.ds(c0, sub_chunks), half * LANES:(half + 1) * LANES]
                          for gl in range(UNITS)]
                outs = _unit_transpose(pieces)
                for a in range(UNITS):
                    y_nat[tile, pl.ds(half * UNITS + a, sub_chunks, stride=S5_CHUNK), :] = outs[a]
            y_ref[0, pl.ds(r0, S5_SUB), ot] = y_nat[tile] + d_ref[:, ot] * u_ref[0, pl.ds(r0, S5_SUB), ot]
        return carry

    lax.fori_loop(0, tt // S5_SUB, relayout_out, 0)


def _s5_prompt(u, mats, tt):
    b, t, _ = u.shape
    chunks = tt // S5_CHUNK
    state = jax.ShapeDtypeStruct((b, SUBLANES, SSM_STATES), F32)
    state_spec = pl.BlockSpec((1, SUBLANES, SSM_STATES), lambda i, j: (i, 0, 0))
    return pl.pallas_call(
        functools.partial(_s5_prompt_kernel, tt=tt),
        grid=(b, t // tt),
        in_specs=[pl.BlockSpec((1, tt, SSM_WIDTH), lambda i, j: (i, j, 0))]
        + [_const_spec(a.shape) for a in mats],
        out_specs=[pl.BlockSpec((1, tt, SSM_WIDTH), lambda i, j: (i, j, 0)), state_spec, state_spec],
        out_shape=[jax.ShapeDtypeStruct((b, t, SSM_WIDTH), F32), state, state],
        scratch_shapes=[pltpu.VMEM((SSM_GROUPS, chunks, S5_ROW), BF16),
                        pltpu.VMEM((SSM_GROUPS, chunks, S5_ROW), F32),
                        pltpu.VMEM((SUBLANES + chunks, SSM_STATES), F32),
                        pltpu.VMEM((SUBLANES + chunks, SSM_STATES), F32),
                        pltpu.VMEM((SUBLANES, SSM_STATES), F32), pltpu.VMEM((SUBLANES, SSM_STATES), F32),
                        pltpu.VMEM((SSM_WIDTH // LANES, S5_SUB, LANES), F32)],
        compiler_params=_cparams(2),
        name="s5_prompt",
    )(u, *mats)


def _s5_sample_kernel(u_ref, h0re_ref, h0im_ref, bre_ref, bim_ref, cre_ref, cim_ref, lam_ref, d_ref,
                      y_ref, hre_ref, him_ref, *, steps):
    h_re = h0re_ref[...]
    h_im = h0im_ref[...]
    lam_re = lam_ref[0:1, :]
    lam_im = lam_ref[1:2, :]
    for t in range(steps):
        u = u_ref[t]
        bu_re, bu_im = _s5_input(u.astype(BF16), bre_ref, bim_ref)
        bu_re = jnp.concatenate(bu_re, axis=1)
        bu_im = jnp.concatenate(bu_im, axis=1)
        h_re, h_im = (lam_re * h_re - lam_im * h_im + bu_re,
                      lam_re * h_im + lam_im * h_re + bu_im)
        hb_re = h_re.astype(BF16)
        hb_im = h_im.astype(BF16)
        for tile in range(SSM_WIDTH // S5_OUT_TILE):
            st = slice(tile * S5_STATE_TILE, (tile + 1) * S5_STATE_TILE)
            ot = slice(tile * S5_OUT_TILE, (tile + 1) * S5_OUT_TILE)
            y = _s5_output(hb_re[:, st], hb_im[:, st], cre_ref, cim_ref, tile)
            y_ref[t, :, ot] = y + d_ref[:, ot] * u[:, ot]
    hre_ref[...] = h_re
    him_ref[...] = h_im


def _s5_sample(u_tb, h0_re, h0_im, mats, lam):
    steps, n, _ = u_tb.shape
    bre, bim, cre, cim, d_skip = mats
    state = jax.ShapeDtypeStruct((n, SSM_STATES), F32)
    args = (u_tb, h0_re, h0_im, bre, bim, cre, cim, lam, d_skip)
    return pl.pallas_call(
        functools.partial(_s5_sample_kernel, steps=steps),
        grid=(1,),
        in_specs=[_const_spec(a.shape) for a in args],
        out_specs=[_const_spec(u_tb.shape), _const_spec((n, SSM_STATES)), _const_spec((n, SSM_STATES))],
        out_shape=[jax.ShapeDtypeStruct(u_tb.shape, F32), state, state],
        compiler_params=_cparams(1),
        name="s5_sample",
    )(*args)


def _s5_matrices(lam_re, lam_im, log_dt, b_re, b_im, c_re, c_im, d_skip):
    lam = lax.complex(lam_re, lam_im)
    dt = jnp.exp(log_dt)[:, None]
    lam_bar = jnp.exp(lam * dt)
    b_bar = ((lam_bar - 1.0) / lam)[..., None] * lax.complex(b_re, b_im)
    eye_half = jnp.eye(SSM_GROUPS // 2, dtype=F32)

    def in_blocks(x):
        x = x.reshape(2, SSM_GROUPS // 2, SSM_STATE, SSM_GROUP)
        return jnp.einsum("kgpi,gh->kgihp", x, eye_half).reshape(2, S5_IN_HALF, S5_STATE_HALF).astype(BF16)

    tiles = SSM_WIDTH // S5_OUT_TILE
    groups_per_tile = SSM_GROUPS // tiles
    eye_tile = jnp.eye(groups_per_tile, dtype=F32)

    def out_blocks(x):
        x = x.reshape(tiles, groups_per_tile, SSM_GROUP, SSM_STATE)
        return jnp.einsum("kgop,gh->kgpho", x, eye_tile).reshape(tiles, S5_STATE_TILE, S5_OUT_TILE).astype(BF16)

    lam_flat = lam_bar.reshape(1, SSM_STATES)
    lam_rows = jnp.concatenate([lam_flat.real, lam_flat.imag], axis=0).astype(F32)
    d_row = d_skip.reshape(1, SSM_WIDTH)
    sample_mats = (in_blocks(b_bar.real), in_blocks(b_bar.imag), out_blocks(c_re), out_blocks(c_im), d_row)

    c_mat = lax.complex(c_re, c_im)
    lam_pow = jnp.concatenate([jnp.ones((1,) + lam_bar.shape, lam_bar.dtype),
                               jnp.cumprod(jnp.broadcast_to(lam_bar, (S5_CHUNK,) + lam_bar.shape), axis=0)])
    taps = jnp.einsum("gop,kgp,gpi->gkoi", c_mat, lam_pow[:S5_CHUNK], b_bar,
                      precision=lax.Precision.HIGHEST).real
    src = jnp.arange(S5_CHUNK)[None, :, None]
    dst = jnp.arange(S5_CHUNK)[None, None, :]
    lag_is = (dst - src == jnp.arange(S5_CHUNK)[:, None, None]).astype(F32)
    toep = jnp.einsum("kst,gkoi->gsito", lag_is, taps, precision=lax.Precision.HIGHEST)
    toep = toep.reshape(SSM_GROUPS, S5_ROW, S5_ROW).astype(BF16)
    w_state = jnp.einsum("sgp,gpi->gsip", lam_pow[S5_CHUNK - 1::-1][:S5_CHUNK], b_bar)
    w_state = w_state.reshape(SSM_GROUPS, S5_ROW, SSM_STATE)
    second = (jnp.arange(SSM_GROUPS) % 2 == 1)[:, None, None]
    zeros = jnp.zeros_like(w_state.real)

    def pair_cols(x):
        return jnp.where(second, jnp.concatenate([zeros, x], -1), jnp.concatenate([x, zeros], -1)).astype(BF16)

    v_out = jnp.einsum("gop,tgp->gpto", c_mat, lam_pow[1:]).reshape(SSM_GROUPS, SSM_STATE, S5_ROW)
    eye_pair = jnp.eye(2, dtype=F32)

    def pair_blocks(x):
        x = x.reshape(S5_PAIRS, 2, SSM_STATE, S5_ROW)
        return jnp.einsum("kapc,ab->kapbc", x, eye_pair).reshape(S5_PAIRS, LANES, 2 * S5_ROW).astype(BF16)

    lam_chunk = lam_pow[S5_CHUNK].reshape(1, SSM_STATES)
    rows = jnp.arange(SUBLANES)[:, None]
    planes = []
    for shift in (1, 2, 4):
        a = jnp.where(rows >= shift, lam_chunk ** shift, 0.0)
        planes += [a.real, a.imag]
    carry = jnp.cumprod(jnp.broadcast_to(lam_chunk, (SUBLANES, SSM_STATES)), axis=0)
    planes += [carry.real, carry.imag]
    powers = jnp.stack(planes).astype(F32)
    pos = jnp.arange(S5_SUB)
    perm = (jnp.arange(S5_SUB)[None, :] == ((pos % (S5_SUB // S5_CHUNK)) * S5_CHUNK
                                            + pos // (S5_SUB // S5_CHUNK))[:, None]).astype(BF16)
    prompt_mats = (perm, toep, pair_cols(w_state.real), pair_cols(w_state.imag),
                   pair_blocks(v_out.real), pair_blocks(-v_out.imag), powers, d_row)
    return prompt_mats, sample_mats, lam_rows


def _ab_out_kernel(yp_ref, ys_ref, wglu_ref, bglu_ref, wout_ref, h_ref, g_ref, o_ref):
    z = jax.nn.gelu(ys_ref[...])
    gate = jnp.dot(z.astype(BF16), wglu_ref[...], preferred_element_type=F32) + bglu_ref[...]
    y_ssm = z * jax.nn.sigmoid(gate)
    y = (jnp.dot(yp_ref[...], wout_ref[0:POOL_WIDTH, :], preferred_element_type=F32)
         + jnp.dot(y_ssm.astype(BF16), wout_ref[POOL_WIDTH:, :], preferred_element_type=F32))
    o_ref[...] = h_ref[...] + _rms(y, g_ref[...])


def _ab_out(y_pool, y_ssm, w_glu, b_glu, w_out, h, g, tm):
    m = h.shape[0]
    row = lambda width: pl.BlockSpec((tm, width), lambda i: (i, 0))
    return pl.pallas_call(
        _ab_out_kernel,
        grid=(m // tm,),
        in_specs=[row(POOL_WIDTH), row(SSM_WIDTH), _const_spec(w_glu.shape), _const_spec(b_glu.shape),
                  _const_spec(w_out.shape), row(D_MODEL), _const_spec(g.shape)],
        out_specs=row(D_MODEL),
        out_shape=jax.ShapeDtypeStruct((m, D_MODEL), F32),
        compiler_params=_cparams(1),
        name="ab_out",
    )(y_pool, y_ssm, w_glu, b_glu, w_out, h, g)


def _ffn_kernel(h_ref, gin_ref, wg_ref, wu_ref, wd_ref, gout_ref, o_ref):
    h = h_ref[...]
    xb = _rms(h, gin_ref[...]).astype(BF16)
    gate = jnp.dot(xb, wg_ref[...], preferred_element_type=F32)
    up = jnp.dot(xb, wu_ref[...], preferred_element_type=F32)
    act = (gate * jax.nn.sigmoid(gate) * up).astype(BF16)
    y = jnp.dot(act, wd_ref[...], preferred_element_type=F32)
    o_ref[...] = h + _rms(y, gout_ref[...])


def _ffn(h, g_in, w_gate, w_up, w_down, g_out, tm):
    m = h.shape[0]
    row = pl.BlockSpec((tm, D_MODEL), lambda i: (i, 0))
    return pl.pallas_call(
        _ffn_kernel,
        grid=(m // tm,),
        in_specs=[row, _const_spec(g_in.shape), _const_spec(w_gate.shape), _const_spec(w_up.shape),
                  _const_spec(w_down.shape), _const_spec(g_out.shape)],
        out_specs=row,
        out_shape=jax.ShapeDtypeStruct((m, D_MODEL), F32),
        compiler_params=_cparams(1),
        name="ffn",
    )(h, g_in, w_gate, w_up, w_down, g_out)


def _attn_prompt_kernel(q_ref, kc_ref, kp_ref, vc_ref, vp_ref, o_ref, lse_ref, *, parts):
    blk = pl.program_id(2)
    rows = 2 * ATT_BLOCK
    rpp = ATT_BLOCK // parts
    shift = rpp.bit_length() - 1
    qi = lax.broadcasted_iota(jnp.int32, (rows, 2 * ATT_BLOCK), 0) & (ATT_BLOCK - 1)
    kj = lax.broadcasted_iota(jnp.int32, (rows, 2 * ATT_BLOCK), 1)
    k_in = kj & (ATT_BLOCK - 1)
    is_cur = kj >> (ATT_BLOCK.bit_length() - 1)
    q_row = rpp + (qi & (rpp - 1))
    k_row = is_cur * rpp + (k_in & (rpp - 1))
    dist = parts * (q_row - k_row) + ((qi >> shift) - (k_in >> shift))
    in_band = (dist >= 0) & (dist <= ATT_SPAN)
    lane = lax.broadcasted_iota(jnp.int32, (ATT_BLOCK, LANES), 1)
    first_head = lane < HEAD_DIM

    def gather(ref, first, sl):
        return jnp.concatenate([ref[c, first:first + rpp, sl] for c in range(parts)], axis=0)

    for sub in range(ATT_BLOCKS_PER_STEP):
        here = sub * rpp
        valid = in_band if sub else in_band & ((kj >= ATT_BLOCK) | (blk > 0))
        lse_all = jnp.zeros((ATT_BLOCK, LANES), F32)
        for pair in range(ATT_HEADS // 2):
            sl = slice(pair * LANES, (pair + 1) * LANES)
            q2 = gather(q_ref, here, sl)
            zero = jnp.zeros_like(q2)
            qs = jnp.concatenate([jnp.where(first_head, q2, zero), jnp.where(first_head, zero, q2)], axis=0)
            k_before = gather(kc_ref, here - rpp, sl) if sub else gather(kp_ref, 0, sl)
            v_before = gather(vc_ref, here - rpp, sl) if sub else gather(vp_ref, 0, sl)
            k2 = jnp.concatenate([k_before, gather(kc_ref, here, sl)], axis=0)
            v2 = jnp.concatenate([v_before, gather(vc_ref, here, sl)], axis=0)
            s = lax.dot_general(qs, k2, (((1,), (1,)), ((), ())), preferred_element_type=F32)
            s = jnp.where(valid, s, -jnp.inf)
            m = jnp.max(s, axis=-1, keepdims=True)
            p = jnp.exp(s - m)
            den = jnp.sum(p, axis=-1, keepdims=True)
            o = jnp.dot(p.astype(BF16), v2, preferred_element_type=F32) / den
            lse = m + jnp.log(den)
            o_pair = jnp.where(first_head, o[:ATT_BLOCK], o[ATT_BLOCK:]).astype(o_ref.dtype)
            for c in range(parts):
                o_ref[c, here:here + rpp, sl] = o_pair[c * rpp:(c + 1) * rpp]
            lse_all = jnp.where(lane == 2 * pair, lse[:ATT_BLOCK],
                                jnp.where(lane == 2 * pair + 1, lse[ATT_BLOCK:], lse_all))
        for c in range(parts):
            lse_ref[c, here:here + rpp, :] = lse_all[c * rpp:(c + 1) * rpp]


def _attn_prompt(q, k, v, parts):
    batch, n_streams, sub, _ = q.shape
    pat_streams = n_streams // parts
    rpp = ATT_BLOCK // parts
    step_rows = ATT_BLOCKS_PER_STEP * rpp
    nblk = sub // step_rows
    view = lambda a: a.reshape(batch, parts, pat_streams, sub, a.shape[-1])
    cur = pl.BlockSpec((None, parts, None, step_rows, ATT_WIDTH), lambda b, r, i: (b, 0, r, i, 0))
    prev = pl.BlockSpec((None, parts, None, rpp, ATT_WIDTH),
                        lambda b, r, i: (b, 0, r, jnp.maximum(i * ATT_BLOCKS_PER_STEP - 1, 0), 0))
    o, lse = pl.pallas_call(
        functools.partial(_attn_prompt_kernel, parts=parts),
        grid=(batch, pat_streams, nblk),
        in_specs=[cur, cur, prev, cur, prev],
        out_specs=[cur, pl.BlockSpec((None, parts, None, step_rows, LANES), lambda b, r, i: (b, 0, r, i, 0))],
        out_shape=[jax.ShapeDtypeStruct((batch, parts, pat_streams, sub, ATT_WIDTH), BF16),
                   jax.ShapeDtypeStruct((batch, parts, pat_streams, sub, LANES), F32)],
        compiler_params=_cparams(3),
        name="attn_prompt",
    )(view(q), view(k), view(k), view(v), view(v))
    return o.reshape(batch, n_streams, sub, ATT_WIDTH), lse.reshape(batch, n_streams, sub, LANES)


def _pattern_multiplicity(dist):
    mult = jnp.zeros(dist.shape, F32)
    for window, dil in DILATED_PATTERNS:
        hit = (dist >= 0) & (dist <= window) & ((dist & (dil - 1)) == 0)
        mult = mult + hit.astype(F32)
    return mult


def _attn_sample_kernel(q_ref, knew_ref, vnew_ref, kt_ref, vt_ref, o_ref, *, steps):
    t_cache = lax.broadcasted_iota(jnp.int32, (SUBLANES, MAX_WINDOW), 0)
    pos = lax.broadcasted_iota(jnp.int32, (SUBLANES, MAX_WINDOW), 1)
    mult = _pattern_multiplicity(MAX_WINDOW + t_cache - pos)
    t_new = lax.broadcasted_iota(jnp.int32, (SUBLANES, LANES), 0)
    j_new = lax.broadcasted_iota(jnp.int32, (SUBLANES, LANES), 1)
    mult_new = jnp.where(j_new < steps, _pattern_multiplicity(t_new - j_new), 0.0)
    nt = (((1,), (1,)), ((), ()))
    for head in range(ATT_HEADS):
        q = q_ref[0, head].astype(BF16)
        s = jnp.dot(q, kt_ref[0, head].astype(BF16), preferred_element_type=F32)
        s_new = lax.dot_general(q, knew_ref[0, head].astype(BF16), nt, preferred_element_type=F32)
        s = jnp.where(mult > 0.0, s, -jnp.inf)
        s_new = jnp.where(mult_new > 0.0, s_new, -jnp.inf)
        m = jnp.maximum(jnp.max(s, axis=-1, keepdims=True), jnp.max(s_new, axis=-1, keepdims=True))
        p = mult * jnp.exp(s - m)
        p_new = mult_new * jnp.exp(s_new - m)
        den = jnp.sum(p, axis=-1, keepdims=True) + jnp.sum(p_new, axis=-1, keepdims=True)
        o = lax.dot_general(p.astype(BF16), vt_ref[0, head].astype(BF16), nt, preferred_element_type=F32)
        o = o + jnp.dot(p_new.astype(BF16), vnew_ref[0, head].astype(BF16), preferred_element_type=F32)
        o_ref[0, :, head * HEAD_DIM:(head + 1) * HEAD_DIM] = (o / den)[:steps]


def _attn_sample(q, k_new, v_new, cache_kt, cache_vt, steps):
    n = q.shape[0]
    assert steps <= SUBLANES
    per = lambda a: pl.BlockSpec((1,) + a.shape[1:], lambda b: (b, 0, 0, 0))
    args = (q, k_new, v_new, cache_kt, cache_vt)
    return pl.pallas_call(
        functools.partial(_attn_sample_kernel, steps=steps),
        grid=(n,),
        in_specs=[per(a) for a in args],
        out_specs=pl.BlockSpec((1, steps, ATT_WIDTH), lambda b: (b, 0, 0)),
        out_shape=jax.ShapeDtypeStruct((n, steps, ATT_WIDTH), F32),
        compiler_params=_cparams(1),
        name="attn_sample",
    )(*args)


def _attn_out_kernel(*refs, tm):
    n_pat = len(PATTERN_ORDER)
    o_refs = refs[:n_pat]
    lse_refs = refs[n_pat:2 * n_pat]
    w_ref, expand_ref, unperm_ref, h_ref, g_ref, out_ref, lse_nat = refs[2 * n_pat:]
    o_nat = []
    for pat, dil in enumerate(PATTERN_ORDER):
        if dil == 1:
            lse_nat[pat] = lse_refs[pat][0, 0]
            o_nat.append(o_refs[pat][0, 0].astype(F32))
            continue
        for res in range(dil):
            lse_nat[pat, pl.ds(res, tm // dil, stride=dil), :] = lse_refs[pat][0, res]
        rows = PERM_TILE // dil
        parts = []
        for part in range(tm // PERM_TILE):
            stacked = jnp.concatenate([o_refs[pat][0, res, part * rows:(part + 1) * rows] for res in range(dil)],
                                      axis=0)
            parts.append(jnp.dot(unperm_ref[...], stacked, preferred_element_type=F32))
        o_nat.append(jnp.concatenate(parts, axis=0))
    lses = [lse_nat[pat] for pat in range(n_pat)]
    top = functools.reduce(jnp.maximum, lses)
    es = [jnp.exp(l - top) for l in lses]
    total = functools.reduce(lambda x, y: x + y, es)
    spread = []
    for e in es[:-1]:
        wgt = e / total
        hi = wgt.astype(BF16)
        lo = (wgt - hi.astype(F32)).astype(BF16)
        spread.append(jnp.dot(jnp.concatenate([hi, lo], axis=1), expand_ref[...],
                              preferred_element_type=F32))
    spread.append(1.0 - functools.reduce(lambda x, y: x + y, spread))
    pieces = []
    for c in range(COL_TILES):
        acc = None
        for pat in range(n_pat):
            term = spread[pat][:, c * LANES:(c + 1) * LANES] * o_nat[pat][:, c * LANES:(c + 1) * LANES]
            acc = term if acc is None else acc + term
        pieces.append(acc.astype(BF16))
    a = jnp.concatenate(pieces, axis=1)
    y = jnp.dot(a, w_ref[...], preferred_element_type=F32)
    out_ref[...] = h_ref[...] + _rms(y, g_ref[...])


def _attn_out(outs, lses, w_o, h, g, seq, tm):
    m = h.shape[0]
    tiles = seq // tm
    n_pat = len(PATTERN_ORDER)

    def dil_spec(dil, width):
        return pl.BlockSpec((1, dil, tm // dil, width), lambda i: (i // tiles, 0, i % tiles, 0))

    row = pl.BlockSpec((tm, D_MODEL), lambda i: (i, 0))
    expand = (jnp.arange(LANES)[:, None] == jnp.arange(ATT_WIDTH)[None, :] // HEAD_DIM).astype(BF16)
    expand = jnp.concatenate([expand, expand], axis=0)
    order = ATT_ORDERS[1]
    assert all(dil in (1, order) for dil in PATTERN_ORDER)
    assert tm % PERM_TILE == 0
    nat = jnp.arange(PERM_TILE)
    unperm = (jnp.arange(PERM_TILE)[None, :]
              == ((nat % order) * (PERM_TILE // order) + nat // order)[:, None]).astype(BF16)
    return pl.pallas_call(
        functools.partial(_attn_out_kernel, tm=tm),
        grid=(m // tm,),
        in_specs=[dil_spec(dil, ATT_WIDTH) for dil in PATTERN_ORDER]
        + [dil_spec(dil, LANES) for dil in PATTERN_ORDER]
        + [_const_spec(w_o.shape), _const_spec(expand.shape), _const_spec(unperm.shape), row,
           _const_spec(g.shape)],
        out_specs=row,
        out_shape=jax.ShapeDtypeStruct((m, D_MODEL), F32),
        scratch_shapes=[pltpu.VMEM((n_pat, tm, LANES), F32)],
        compiler_params=_cparams(1),
        name="attn_out",
    )(*outs, *lses, w_o, expand, unperm, h, g)


def _proj_out_kernel(a_ref, w_ref, h_ref, g_ref, out_ref):
    y = jnp.dot(a_ref[...].astype(BF16), w_ref[...], preferred_element_type=F32)
    out_ref[...] = h_ref[...] + _rms(y, g_ref[...])


def _proj_out(a, w, h, g, tm):
    m = h.shape[0]
    return pl.pallas_call(
        _proj_out_kernel,
        grid=(m // tm,),
        in_specs=[pl.BlockSpec((tm, a.shape[1]), lambda i: (i, 0)), _const_spec(w.shape),
                  pl.BlockSpec((tm, D_MODEL), lambda i: (i, 0)), _const_spec(g.shape)],
        out_specs=pl.BlockSpec((tm, D_MODEL), lambda i: (i, 0)),
        out_shape=jax.ShapeDtypeStruct((m, D_MODEL), F32),
        compiler_params=_cparams(1),
        name="proj_out",
    )(a, w, h, g)


def kernel(x_prompt, x_sample, state_pool, state_s5, cache_k, cache_v, norm_gains, ab_w_in, ab_pool_w, ab_pool_scale, ab_lambda_re, ab_lambda_im, ab_log_dt, ab_b_re, ab_b_im, ab_c_re, ab_c_im, ab_d, ab_w_glu, ab_b_glu, ab_w_out, c_w_qkv, c_w_o, ffn_w_gate, ffn_w_up, ffn_w_down):
    batch, seq, d = x_prompt.shape
    n_dec, t_dec, _ = x_sample.shape
    mp, ms = batch * seq, n_dec * t_dec
    tm_p = tm_ffn = PROMPT_ROW_TILE
    gains = norm_gains.reshape(norm_gains.shape[0], 4, 1, d)

    hp = x_prompt.reshape(mp, d)
    hs = x_sample.reshape(ms, d)

    g = gains[0]
    w_in = ab_w_in[0].astype(BF16)
    pool_w = ab_pool_w[0].astype(BF16)
    pool_scale = ab_pool_scale[0].reshape(1, POOL_WIDTH)
    lead_p = jnp.zeros((batch, POOL_LEAD, POOL_WIDTH), F32)
    lead_s = jnp.pad(state_pool[0], ((0, 0), (POOL_LEAD - POOL_STATE, 0), (0, 0)))
    ussm_p, ypool_p, pool_tail = _in_proj_pool(x_prompt, g[0], w_in, lead_p, pool_w, pool_scale, MIXER_ROW_TILE, 0)
    (proj_s,) = _norm_matmul(hs, g[0], w_in, ((0, d, 1.0),), (F32,), ms)
    proj_s = proj_s.reshape(n_dec, t_dec, d)
    ypool_s = _pool_mixer(proj_s, lead_s, pool_w, pool_scale, t_dec, PAST_LEN)
    pool_prompt = pool_tail[:, POOL_LEAD - POOL_STATE:][None]
    pool_sample = jnp.concatenate([state_pool[0], proj_s[:, :, :POOL_WIDTH]], axis=1)[:, -POOL_STATE:][None]

    prompt_mats, sample_mats, lam_rows = _s5_matrices(
        ab_lambda_re[0], ab_lambda_im[0], ab_log_dt[0], ab_b_re[0], ab_b_im[0], ab_c_re[0], ab_c_im[0], ab_d[0])
    yssm_p, hre_p, him_p = _s5_prompt(ussm_p, prompt_mats, S5_ROW_TILE)
    u_tb = jnp.swapaxes(proj_s[:, :, POOL_WIDTH:], 0, 1)
    h0 = state_s5[0].reshape(n_dec, SSM_STATES, 2)
    yssm_tb, hre_s, him_s = _s5_sample(u_tb, h0[..., 0], h0[..., 1], sample_mats, lam_rows)
    yssm_s = jnp.swapaxes(yssm_tb, 0, 1)
    s5_prompt = jnp.stack([hre_p[:, 0], him_p[:, 0]], axis=-1).reshape(1, batch, SSM_GROUPS, SSM_STATE, 2)
    s5_sample = jnp.stack([hre_s, him_s], axis=-1).reshape(1, n_dec, SSM_GROUPS, SSM_STATE, 2)

    w_glu = ab_w_glu[0].astype(BF16)
    b_glu = ab_b_glu[0].reshape(1, SSM_WIDTH)
    w_out = ab_w_out[0].astype(BF16)
    hp = _ab_out(ypool_p.reshape(mp, POOL_WIDTH), yssm_p.reshape(mp, SSM_WIDTH), w_glu, b_glu, w_out, hp, g[1], MIXER_ROW_TILE)
    hs = _ab_out(ypool_s.reshape(ms, POOL_WIDTH), yssm_s.reshape(ms, SSM_WIDTH), w_glu, b_glu, w_out, hs, g[1], ms)

    wg, wu, wd = ffn_w_gate[0].astype(BF16), ffn_w_up[0].astype(BF16), ffn_w_down[0].astype(BF16)
    hp = _ffn(hp, g[2], wg, wu, wd, g[3], tm_ffn)
    hs = _ffn(hs, g[2], wg, wu, wd, g[3], ms)

    g = gains[1]
    w_qkv = c_w_qkv[0].astype(BF16)
    w_o = c_w_o[0].astype(BF16)
    keep = min(MAX_WINDOW, seq)
    *qkv_p, kt_tail, vt_tail = _qkv_prompt(hp, g[0], w_qkv, batch, seq, keep, tm_p)
    positions_first = lambda a: jnp.transpose(a.reshape(batch, ATT_HEADS, HEAD_DIM, keep), (0, 3, 1, 2))[None]
    k_prompt = positions_first(kt_tail)
    v_prompt = positions_first(vt_tail)
    q_nat, q_res, k_nat, k_res, v_nat, v_res = qkv_p
    outs, lses = [], []
    for q, k, v, parts in ((q_nat, k_nat, v_nat, 1), (q_res, k_res, v_res, MID_RATIO), (q_res, k_res, v_res, 1)):
        o, lse = _attn_prompt(q, k, v, parts)
        outs.append(o)
        lses.append(lse)
    hp = _attn_out(outs, lses, w_o, hp, g[1], seq, tm_p)

    scale = HEAD_DIM ** -0.5
    q_s, k_s, v_s = _norm_matmul(
        hs, g[0], w_qkv,
        ((0, ATT_WIDTH, scale), (ATT_WIDTH, ATT_WIDTH, 1.0), (2 * ATT_WIDTH, ATT_WIDTH, 1.0)),
        (F32, F32, F32), ms)
    k_sample = k_s.reshape(1, n_dec, t_dec, ATT_HEADS, HEAD_DIM)
    v_sample = v_s.reshape(1, n_dec, t_dec, ATT_HEADS, HEAD_DIM)
    head_major = lambda a, rows: jnp.pad(
        jnp.swapaxes(a.reshape(n_dec, t_dec, ATT_HEADS, HEAD_DIM), 1, 2),
        ((0, 0), (0, 0), (0, rows - t_dec), (0, 0)))
    cache_kt = jnp.transpose(cache_k[0], (0, 2, 3, 1))
    cache_vt = jnp.transpose(cache_v[0], (0, 2, 3, 1))
    o_s = _attn_sample(head_major(q_s, SUBLANES), head_major(k_s, LANES), head_major(v_s, LANES),
                       cache_kt, cache_vt, t_dec)
    hs = _proj_out(o_s.reshape(ms, ATT_WIDTH), w_o, hs, g[1], ms)

    wg, wu, wd = ffn_w_gate[1].astype(BF16), ffn_w_up[1].astype(BF16), ffn_w_down[1].astype(BF16)
    hp = _ffn(hp, g[2], wg, wu, wd, g[3], tm_ffn)
    hs = _ffn(hs, g[2], wg, wu, wd, g[3], ms)

    return (hp.reshape(batch, seq, d), hs.reshape(n_dec, t_dec, d), pool_prompt, s5_prompt,
            k_prompt, v_prompt, pool_sample, s5_sample, k_sample, v_sample)
```

```python
import functools
import math

import jax
import jax.numpy as jnp
from jax import lax
from jax.experimental import pallas as pl
from jax.experimental.pallas import tpu as pltpu

F32 = jnp.float32
BF16 = jnp.bfloat16

D_MODEL = 1024
PAST_LEN = 16384
POOL_WIDTH = 512
POOL_WINDOWS = (2, 4, 8, 16)
POOL_GROUP = 128
POOL_STATE = 15
POOL_LEAD = 16
POOL_SEQS_PER_STEP = 8
SSM_WIDTH = 512
SSM_GROUP = 16
SSM_GROUPS = 32
SSM_STATE = 64
SSM_STATES = SSM_GROUPS * SSM_STATE
ATT_HEADS = 16
HEAD_DIM = 64
ATT_WIDTH = ATT_HEADS * HEAD_DIM
DILATED_PATTERNS = ((128, 1), (512, 4), (2048, 16))
MAX_WINDOW = 2048
ATT_BLOCK = 128
ATT_SPAN = 128
FFN_HIDDEN = 2816
RMS_EPS = 1e-6

SUBLANES = 8
LANES = 128
VMEM_LIMIT = 56 * 1024 * 1024
PROMPT_ROW_TILE = 512
MIXER_ROW_TILE = 1024
S5_ROW_TILE = 2048


def _cparams(n_axes):
    return pltpu.CompilerParams(
        dimension_semantics=("arbitrary",) * n_axes, vmem_limit_bytes=VMEM_LIMIT)


def _rms(x, g):
    ms = jnp.mean(x * x, axis=-1, keepdims=True)
    return (x * lax.rsqrt(ms + RMS_EPS)) * g


def _const_spec(shape):
    zeros = (0,) * len(shape)
    return pl.BlockSpec(shape, lambda *_: zeros, pipeline_mode=pl.Buffered(1))


def _norm_matmul_kernel(x_ref, g_ref, w_ref, *out_refs, splits):
    xb = _rms(x_ref[...], g_ref[...]).astype(BF16)
    for (c0, width, scale), o_ref in zip(splits, out_refs):
        y = jnp.dot(xb, w_ref[:, c0:c0 + width], preferred_element_type=F32)
        if scale != 1.0:
            y = y * scale
        o_ref[...] = y.astype(o_ref.dtype)


def _norm_matmul(x, g, w, splits, dtypes, tm):
    m, d = x.shape
    n = w.shape[1]
    return pl.pallas_call(
        functools.partial(_norm_matmul_kernel, splits=splits),
        grid=(m // tm,),
        in_specs=[pl.BlockSpec((tm, d), lambda i: (i, 0)), _const_spec((1, d)), _const_spec((d, n))],
        out_specs=[pl.BlockSpec((tm, width), lambda i: (i, 0)) for _, width, _ in splits],
        out_shape=[jax.ShapeDtypeStruct((m, width), dt) for (_, width, _), dt in zip(splits, dtypes)],
        compiler_params=_cparams(1),
        name="norm_matmul",
    )(x, g, w)


ATT_DILATIONS = tuple(dil for _, dil in DILATED_PATTERNS)
ATT_ORDERS = (ATT_DILATIONS[0], ATT_DILATIONS[2])
PATTERN_ORDER = (ATT_DILATIONS[0], ATT_DILATIONS[2], ATT_DILATIONS[2])
MID_RATIO = ATT_DILATIONS[2] // ATT_DILATIONS[1]
COL_TILES = ATT_WIDTH // LANES
PERM_TILE = 256
ATT_BLOCKS_PER_STEP = 8


def _qkv_prompt_kernel(x_ref, g_ref, w_ref, perm_ref, *refs, q_scale, tm, tiles, first_tail):
    n_dil = len(ATT_ORDERS)
    out_refs = refs[:3 * n_dil]
    tail_refs = refs[3 * n_dil:]
    in_tail = (pl.program_id(0) % tiles) >= first_tail
    xb = _rms(x_ref[...], g_ref[...]).astype(BF16)
    for which in range(3):
        y = jnp.dot(xb, w_ref[:, which * ATT_WIDTH:(which + 1) * ATT_WIDTH], preferred_element_type=F32)
        if which == 0:
            y = y * q_scale
        else:
            @pl.when(in_tail)
            def _(y=y, tail_ref=tail_refs[which - 1]):
                tail_ref[0] = y.T
        yb = y.astype(BF16)
        for pat, (dil, o_ref) in enumerate(zip(ATT_ORDERS, out_refs[which * n_dil:(which + 1) * n_dil])):
            if dil == 1:
                o_ref[0, 0] = yb
                continue
            rows = PERM_TILE // dil
            for part in range(tm // PERM_TILE):
                moved = jnp.dot(perm_ref[pat - 1], yb[part * PERM_TILE:(part + 1) * PERM_TILE],
                                preferred_element_type=F32).astype(BF16)
                for res in range(dil):
                    o_ref[0, res, part * rows:(part + 1) * rows] = moved[res * rows:(res + 1) * rows]


def _qkv_prompt(x, g, w, batch, seq, keep, tm):
    m, d = x.shape
    assert tm % PERM_TILE == 0
    tiles = seq // tm
    first_tail = tiles - keep // tm

    def tail_map(i):
        return (i // tiles, 0, jnp.maximum(i % tiles - first_tail, 0))

    def dil_spec(dil):
        return pl.BlockSpec((1, dil, tm // dil, ATT_WIDTH), lambda i: (i // tiles, 0, i % tiles, 0))

    def dil_shape(dil):
        return jax.ShapeDtypeStruct((batch, dil, seq // dil, ATT_WIDTH), BF16)

    assert ATT_ORDERS[0] == 1
    src = jnp.arange(PERM_TILE)[None, :]
    dst = jnp.arange(PERM_TILE)[:, None]
    perm = jnp.stack([(src == (dst % (PERM_TILE // dil)) * dil + dst // (PERM_TILE // dil)).astype(BF16)
                      for dil in ATT_ORDERS[1:]])
    return pl.pallas_call(
        functools.partial(_qkv_prompt_kernel, q_scale=HEAD_DIM ** -0.5, tm=tm, tiles=tiles,
                          first_tail=first_tail),
        grid=(m // tm,),
        in_specs=[pl.BlockSpec((tm, d), lambda i: (i, 0)), _const_spec((1, d)),
                  _const_spec((d, 3 * ATT_WIDTH)), _const_spec(perm.shape)],
        out_specs=[dil_spec(dil) for _ in range(3) for dil in ATT_ORDERS]
        + [pl.BlockSpec((1, ATT_WIDTH, tm), tail_map)] * 2,
        out_shape=[dil_shape(dil) for _ in range(3) for dil in ATT_ORDERS]
        + [jax.ShapeDtypeStruct((batch, ATT_WIDTH, keep), F32)] * 2,
        compiler_params=_cparams(1),
        name="qkv_prompt",
    )(x, g, w, perm)


def _pool_tile(u, j, lead_ref, w_ref, scale_ref, y_ref, ext, *, tt, pos0, carry, seq=0):
    @pl.when(j == 0)
    def _():
        ext[0:POOL_LEAD, :] = lead_ref[seq]

    ext[POOL_LEAD:POOL_LEAD + tt, :] = u
    pos = pos0 + j * tt + lax.broadcasted_iota(jnp.int32, (tt, POOL_GROUP), 0)
    for grp, window in enumerate(POOL_WINDOWS):
        sl = slice(grp * POOL_GROUP, (grp + 1) * POOL_GROUP)
        u_g = u[:, sl]
        win_sum = u_g
        for k in range(1, window):
            win_sum = win_sum + ext[POOL_LEAD - k:POOL_LEAD - k + tt, sl]
        count = jnp.minimum(pos + 1, window).astype(F32)
        diff = win_sum / count - u_g
        y = jnp.dot(diff.astype(BF16), w_ref[grp], preferred_element_type=F32)
        y_ref[seq, :, sl] = (y * scale_ref[:, sl]).astype(y_ref.dtype)
    if carry:
        ext[0:POOL_LEAD, :] = ext[tt:tt + POOL_LEAD, :]


def _pool_kernel(u_ref, lead_ref, w_ref, scale_ref, y_ref, ext, *, tt, pos0, carry, n_seq):
    for seq in range(n_seq):
        _pool_tile(u_ref[seq], pl.program_id(1), lead_ref, w_ref, scale_ref, y_ref, ext,
                   tt=tt, pos0=pos0, carry=carry, seq=seq)


def _in_proj_pool_kernel(x_ref, g_ref, w_ref, lead_ref, pw_ref, scale_ref, ussm_ref, ypool_ref, tail_ref, ext,
                         *, tt, pos0):
    xb = _rms(x_ref[0], g_ref[...]).astype(BF16)
    ussm_ref[0] = jnp.dot(xb, w_ref[:, POOL_WIDTH:], preferred_element_type=F32)
    u_pool = jnp.dot(xb, w_ref[:, 0:POOL_WIDTH], preferred_element_type=F32)
    _pool_tile(u_pool, pl.program_id(1), lead_ref, pw_ref, scale_ref, ypool_ref, ext, tt=tt, pos0=pos0, carry=True)
    tail_ref[0] = ext[0:POOL_LEAD, :]


def _in_proj_pool(x, g, w, lead, pool_w, pool_scale, tt, pos0):
    n, t, d = x.shape
    seq_block = lambda width: pl.BlockSpec((1, tt, width), lambda b, j: (b, j, 0))
    per_seq = pl.BlockSpec((1, POOL_LEAD, POOL_WIDTH), lambda b, j: (b, 0, 0))
    return pl.pallas_call(
        functools.partial(_in_proj_pool_kernel, tt=tt, pos0=pos0),
        grid=(n, t // tt),
        in_specs=[seq_block(d), _const_spec(g.shape), _const_spec(w.shape), per_seq,
                  _const_spec(pool_w.shape), _const_spec(pool_scale.shape)],
        out_specs=[seq_block(SSM_WIDTH), seq_block(POOL_WIDTH), per_seq],
        out_shape=[jax.ShapeDtypeStruct((n, t, SSM_WIDTH), F32), jax.ShapeDtypeStruct((n, t, POOL_WIDTH), BF16),
                   jax.ShapeDtypeStruct((n, POOL_LEAD, POOL_WIDTH), F32)],
        scratch_shapes=[pltpu.VMEM((POOL_LEAD + tt, POOL_WIDTH), F32)],
        compiler_params=_cparams(2),
        name="in_proj_pool",
    )(x, g, w, lead, pool_w, pool_scale)


def _pool_mixer(proj, lead, w, scale, tt, pos0):
    n, t, _ = proj.shape
    steps = t // tt
    n_seq = 1 if steps > 1 else math.gcd(n, POOL_SEQS_PER_STEP)
    return pl.pallas_call(
        functools.partial(_pool_kernel, tt=tt, pos0=pos0, carry=steps > 1, n_seq=n_seq),
        grid=(n // n_seq, steps),
        in_specs=[pl.BlockSpec((n_seq, tt, POOL_WIDTH), lambda b, j: (b, j, 0)),
                  pl.BlockSpec((n_seq, POOL_LEAD, POOL_WIDTH), lambda b, j: (b, 0, 0)),
                  _const_spec((len(POOL_WINDOWS), POOL_GROUP, POOL_GROUP)),
                  _const_spec((1, POOL_WIDTH))],
        out_specs=pl.BlockSpec((n_seq, tt, POOL_WIDTH), lambda b, j: (b, j, 0)),
        out_shape=jax.ShapeDtypeStruct((n, t, POOL_WIDTH), BF16),
        scratch_shapes=[pltpu.VMEM((POOL_LEAD + tt, POOL_WIDTH), F32)],
        compiler_params=_cparams(2),
        name="pool_mixer",
    )(proj, lead, w, scale)


S5_IN_HALF = SSM_WIDTH // 2
S5_STATE_HALF = SSM_STATES // 2
S5_OUT_TILE = LANES
S5_STATE_TILE = SSM_STATES // (SSM_WIDTH // S5_OUT_TILE)
S5_SCAN_LANES = 512


def _s5_input(ub, bre_ref, bim_ref):
    re, im = [], []
    for half in range(2):
        uk = ub[:, half * S5_IN_HALF:(half + 1) * S5_IN_HALF]
        re.append(jnp.dot(uk, bre_ref[half], preferred_element_type=F32))
        im.append(jnp.dot(uk, bim_ref[half], preferred_element_type=F32))
    return re, im


def _s5_output(h_re, h_im, cre_ref, cim_ref, tile):
    return (jnp.dot(h_re, cre_ref[tile], preferred_element_type=F32)
            - jnp.dot(h_im, cim_ref[tile], preferred_element_type=F32))


S5_CHUNK = 16
S5_ROW = S5_CHUNK * SSM_GROUP
S5_SUB = 256
UNIT = SSM_GROUP
UNITS = LANES // UNIT
S5_PAIRS = SSM_GROUPS // 2


def _unit_transpose(pieces):
    unit = lax.broadcasted_iota(jnp.int32, pieces[0].shape, 1) >> (UNIT.bit_length() - 1)
    cur = list(pieces)
    k = UNITS // 2
    while k:
        high = (unit & k) != 0
        nxt = list(cur)
        for a in range(UNITS):
            if a & k:
                continue
            lo_piece, hi_piece = cur[a], cur[a | k]
            nxt[a] = jnp.where(high, pltpu.roll(hi_piece, k * UNIT, 1), lo_piece)
            nxt[a | k] = jnp.where(high, hi_piece, pltpu.roll(lo_piece, LANES - k * UNIT, 1))
        cur = nxt
        k //= 2
    return cur


def _s5_prompt_kernel(u_ref, perm_ref, toep_ref, wre_ref, wim_ref, vre_ref, vim_ref, pw_ref, d_ref,
                      y_ref, hre_ref, him_ref, u2, y2, s_re, s_im, c_re, c_im, y_nat, *, tt):
    j = pl.program_id(1)
    chunks = tt // S5_CHUNK
    sub_chunks = S5_SUB // S5_CHUNK

    @pl.when(j == 0)
    def _():
        c_re[...] = jnp.zeros_like(c_re)
        c_im[...] = jnp.zeros_like(c_im)

    def relayout_in(sub, carry):
        r0 = pl.multiple_of(sub * S5_SUB, S5_SUB)
        c0 = pl.multiple_of(sub * sub_chunks, sub_chunks)
        ub = u_ref[0, pl.ds(r0, S5_SUB), :].astype(BF16)
        xs = jnp.dot(perm_ref[...], ub, preferred_element_type=F32).astype(BF16)
        for tile in range(SSM_WIDTH // LANES):
            for half in range(S5_CHUNK // UNITS):
                pieces = [pltpu.bitcast(
                    xs[(half * UNITS + a) * sub_chunks:(half * UNITS + a + 1) * sub_chunks,
                       tile * LANES:(tile + 1) * LANES], jnp.uint32) for a in range(UNITS)]
                outs = _unit_transpose(pieces)
                for gl in range(UNITS):
                    u2[tile * UNITS + gl, pl.ds(c0, sub_chunks), half * LANES:(half + 1) * LANES] = (
                        pltpu.bitcast(outs[gl], BF16))
        return carry

    lax.fori_loop(0, tt // S5_SUB, relayout_in, 0, unroll=2)

    s_re[0:SUBLANES, :] = c_re[...]
    s_im[0:SUBLANES, :] = c_im[...]
    for pair in range(S5_PAIRS):
        sl = slice(pair * LANES, (pair + 1) * LANES)
        ua = u2[2 * pair]
        ub = u2[2 * pair + 1]
        s_re[SUBLANES:SUBLANES + chunks, sl] = (
            jnp.dot(ua, wre_ref[2 * pair], preferred_element_type=F32)
            + jnp.dot(ub, wre_ref[2 * pair + 1], preferred_element_type=F32))
        s_im[SUBLANES:SUBLANES + chunks, sl] = (
            jnp.dot(ua, wim_ref[2 * pair], preferred_element_type=F32)
            + jnp.dot(ub, wim_ref[2 * pair + 1], preferred_element_type=F32))

    for chunk in range(SSM_STATES // S5_SCAN_LANES):
        sl = slice(chunk * S5_SCAN_LANES, (chunk + 1) * S5_SCAN_LANES)

        def body(r, carry, sl=sl):
            in_re, in_im = carry
            row = pl.multiple_of((r + 1) * SUBLANES, SUBLANES)
            re = s_re[pl.ds(row, SUBLANES), sl]
            im = s_im[pl.ds(row, SUBLANES), sl]
            for level, shift in enumerate((1, 2, 4)):
                a_re = pw_ref[2 * level, :, sl]
                a_im = pw_ref[2 * level + 1, :, sl]
                sh_re = pltpu.roll(re, shift, 0)
                sh_im = pltpu.roll(im, shift, 0)
                re, im = (re + a_re * sh_re - a_im * sh_im,
                          im + a_re * sh_im + a_im * sh_re)
            p_re = pw_ref[6, :, sl]
            p_im = pw_ref[7, :, sl]
            re, im = (re + p_re * in_re - p_im * in_im,
                      im + p_re * in_im + p_im * in_re)
            s_re[pl.ds(row, SUBLANES), sl] = re
            s_im[pl.ds(row, SUBLANES), sl] = im
            last = SUBLANES - 1
            return (jnp.broadcast_to(re[last:last + 1, :], re.shape),
                    jnp.broadcast_to(im[last:last + 1, :], im.shape))

        out_re, out_im = lax.fori_loop(0, chunks // SUBLANES, body, (c_re[:, sl], c_im[:, sl]))
        c_re[:, sl] = out_re
        c_im[:, sl] = out_im

    hre_ref[0] = c_re[...]
    him_ref[0] = c_im[...]

    for pair in range(S5_PAIRS):
        sl = slice(pair * LANES, (pair + 1) * LANES)
        h_re = s_re[SUBLANES - 1:SUBLANES - 1 + chunks, sl].astype(BF16)
        h_im = s_im[SUBLANES - 1:SUBLANES - 1 + chunks, sl].astype(BF16)
        carried = (jnp.dot(h_re, vre_ref[pair], preferred_element_type=F32)
                   + jnp.dot(h_im, vim_ref[pair], preferred_element_type=F32))
        for k in range(2):
            grp = 2 * pair + k
            y2[grp] = (jnp.dot(u2[grp], toep_ref[grp], preferred_element_type=F32)
                       + carried[:, k * S5_ROW:(k + 1) * S5_ROW])

    def relayout_out(blk, carry):
        r0 = pl.multiple_of(blk * S5_SUB, S5_SUB)
        c0 = pl.multiple_of(blk * sub_chunks, sub_chunks)
        for tile in range(SSM_WIDTH // LANES):
            ot = slice(tile * LANES, (tile + 1) * LANES)
            for half in range(S5_CHUNK // UNITS):
                pieces = [y2[tile * UNITS + gl, pl.ds(c0, sub_chunks), half * LANES:(half + 1) * LANES]
                          for gl in range(UNITS)]
                outs = _unit_transpose(pieces)
                for a in range(UNITS):
                    y_nat[tile, pl.ds(half * UNITS + a, sub_chunks, stride=S5_CHUNK), :] = outs[a]
            y_ref[0, pl.ds(r0, S5_SUB), ot] = y_nat[tile] + d_ref[:, ot] * u_ref[0, pl.ds(r0, S5_SUB), ot]
        return carry

    lax.fori_loop(0, tt // S5_SUB, relayout_out, 0)


def _s5_prompt(u, mats, tt):
    b, t, _ = u.shape
    chunks = tt // S5_CHUNK
    state = jax.ShapeDtypeStruct((b, SUBLANES, SSM_STATES), F32)
    state_spec = pl.BlockSpec((1, SUBLANES, SSM_STATES), lambda i, j: (i, 0, 0))
    return pl.pallas_call(
        functools.partial(_s5_prompt_kernel, tt=tt),
        grid=(b, t // tt),
        in_specs=[pl.BlockSpec((1, tt, SSM_WIDTH), lambda i, j: (i, j, 0))]
        + [_const_spec(a.shape) for a in mats],
        out_specs=[pl.BlockSpec((1, tt, SSM_WIDTH), lambda i, j: (i, j, 0)), state_spec, state_spec],
        out_shape=[jax.ShapeDtypeStruct((b, t, SSM_WIDTH), F32), state, state],
        scratch_shapes=[pltpu.VMEM((SSM_GROUPS, chunks, S5_ROW), BF16),
                        pltpu.VMEM((SSM_GROUPS, chunks, S5_ROW), F32),
                        pltpu.VMEM((SUBLANES + chunks, SSM_STATES), F32),
                        pltpu.VMEM((SUBLANES + chunks, SSM_STATES), F32),
                        pltpu.VMEM((SUBLANES, SSM_STATES), F32), pltpu.VMEM((SUBLANES, SSM_STATES), F32),
                        pltpu.VMEM((SSM_WIDTH // LANES, S5_SUB, LANES), F32)],
        compiler_params=_cparams(2),
        name="s5_prompt",
    )(u, *mats)


def _s5_sample_kernel(u_ref, h0re_ref, h0im_ref, bre_ref, bim_ref, cre_ref, cim_ref, lam_ref, d_ref,
                      y_ref, hre_ref, him_ref, *, steps):
    h_re = h0re_ref[...]
    h_im = h0im_ref[...]
    lam_re = lam_ref[0:1, :]
    lam_im = lam_ref[1:2, :]
    for t in range(steps):
        u = u_ref[t]
        bu_re, bu_im = _s5_input(u.astype(BF16), bre_ref, bim_ref)
        bu_re = jnp.concatenate(bu_re, axis=1)
        bu_im = jnp.concatenate(bu_im, axis=1)
        h_re, h_im = (lam_re * h_re - lam_im * h_im + bu_re,
                      lam_re * h_im + lam_im * h_re + bu_im)
        hb_re = h_re.astype(BF16)
        hb_im = h_im.astype(BF16)
        for tile in range(SSM_WIDTH // S5_OUT_TILE):
            st = slice(tile * S5_STATE_TILE, (tile + 1) * S5_STATE_TILE)
            ot = slice(tile * S5_OUT_TILE, (tile + 1) * S5_OUT_TILE)
            y = _s5_output(hb_re[:, st], hb_im[:, st], cre_ref, cim_ref, tile)
            y_ref[t, :, ot] = y + d_ref[:, ot] * u[:, ot]
    hre_ref[...] = h_re
    him_ref[...] = h_im


def _s5_sample(u_tb, h0_re, h0_im, mats, lam):
    steps, n, _ = u_tb.shape
    bre, bim, cre, cim, d_skip = mats
    state = jax.ShapeDtypeStruct((n, SSM_STATES), F32)
    args = (u_tb, h0_re, h0_im, bre, bim, cre, cim, lam, d_skip)
    return pl.pallas_call(
        functools.partial(_s5_sample_kernel, steps=steps),
        grid=(1,),
        in_specs=[_const_spec(a.shape) for a in args],
        out_specs=[_const_spec(u_tb.shape), _const_spec((n, SSM_STATES)), _const_spec((n, SSM_STATES))],
        out_shape=[jax.ShapeDtypeStruct(u_tb.shape, F32), state, state],
        compiler_params=_cparams(1),
        name="s5_sample",
    )(*args)


def _s5_matrices(lam_re, lam_im, log_dt, b_re, b_im, c_re, c_im, d_skip):
    lam = lax.complex(lam_re, lam_im)
    dt = jnp.exp(log_dt)[:, None]
    lam_bar = jnp.exp(lam * dt)
    b_bar = ((lam_bar - 1.0) / lam)[..., None] * lax.complex(b_re, b_im)
    eye_half = jnp.eye(SSM_GROUPS // 2, dtype=F32)

    def in_blocks(x):
        x = x.reshape(2, SSM_GROUPS // 2, SSM_STATE, SSM_GROUP)
        return jnp.einsum("kgpi,gh->kgihp", x, eye_half).reshape(2, S5_IN_HALF, S5_STATE_HALF).astype(BF16)

    tiles = SSM_WIDTH // S5_OUT_TILE
    groups_per_tile = SSM_GROUPS // tiles
    eye_tile = jnp.eye(groups_per_tile, dtype=F32)

    def out_blocks(x):
        x = x.reshape(tiles, groups_per_tile, SSM_GROUP, SSM_STATE)
        return jnp.einsum("kgop,gh->kgpho", x, eye_tile).reshape(tiles, S5_STATE_TILE, S5_OUT_TILE).astype(BF16)

    lam_flat = lam_bar.reshape(1, SSM_STATES)
    lam_rows = jnp.concatenate([lam_flat.real, lam_flat.imag], axis=0).astype(F32)
    d_row = d_skip.reshape(1, SSM_WIDTH)
    sample_mats = (in_blocks(b_bar.real), in_blocks(b_bar.imag), out_blocks(c_re), out_blocks(c_im), d_row)

    c_mat = lax.complex(c_re, c_im)
    lam_pow = jnp.concatenate([jnp.ones((1,) + lam_bar.shape, lam_bar.dtype),
                               jnp.cumprod(jnp.broadcast_to(lam_bar, (S5_CHUNK,) + lam_bar.shape), axis=0)])
    taps = jnp.einsum("gop,kgp,gpi->gkoi", c_mat, lam_pow[:S5_CHUNK], b_bar,
                      precision=lax.Precision.HIGHEST).real
    src = jnp.arange(S5_CHUNK)[None, :, None]
    dst = jnp.arange(S5_CHUNK)[None, None, :]
    lag_is = (dst - src == jnp.arange(S5_CHUNK)[:, None, None]).astype(F32)
    toep = jnp.einsum("kst,gkoi->gsito", lag_is, taps, precision=lax.Precision.HIGHEST)
    toep = toep.reshape(SSM_GROUPS, S5_ROW, S5_ROW).astype(BF16)
    w_state = jnp.einsum("sgp,gpi->gsip", lam_pow[S5_CHUNK - 1::-1][:S5_CHUNK], b_bar)
    w_state = w_state.reshape(SSM_GROUPS, S5_ROW, SSM_STATE)
    second = (jnp.arange(SSM_GROUPS) % 2 == 1)[:, None, None]
    zeros = jnp.zeros_like(w_state.real)

    def pair_cols(x):
        return jnp.where(second, jnp.concatenate([zeros, x], -1), jnp.concatenate([x, zeros], -1)).astype(BF16)

    v_out = jnp.einsum("gop,tgp->gpto", c_mat, lam_pow[1:]).reshape(SSM_GROUPS, SSM_STATE, S5_ROW)
    eye_pair = jnp.eye(2, dtype=F32)

    def pair_blocks(x):
        x = x.reshape(S5_PAIRS, 2, SSM_STATE, S5_ROW)
        return jnp.einsum("kapc,ab->kapbc", x, eye_pair).reshape(S5_PAIRS, LANES, 2 * S5_ROW).astype(BF16)

    lam_chunk = lam_pow[S5_CHUNK].reshape(1, SSM_STATES)
    rows = jnp.arange(SUBLANES)[:, None]
    planes = []
    for shift in (1, 2, 4):
        a = jnp.where(rows >= shift, lam_chunk ** shift, 0.0)
        planes += [a.real, a.imag]
    carry = jnp.cumprod(jnp.broadcast_to(lam_chunk, (SUBLANES, SSM_STATES)), axis=0)
    planes += [carry.real, carry.imag]
    powers = jnp.stack(planes).astype(F32)
    pos = jnp.arange(S5_SUB)
    perm = (jnp.arange(S5_SUB)[None, :] == ((pos % (S5_SUB // S5_CHUNK)) * S5_CHUNK
                                            + pos // (S5_SUB // S5_CHUNK))[:, None]).astype(BF16)
    prompt_mats = (perm, toep, pair_cols(w_state.real), pair_cols(w_state.imag),
                   pair_blocks(v_out.real), pair_blocks(-v_out.imag), powers, d_row)
    return prompt_mats, sample_mats, lam_rows


def _ab_out_kernel(yp_ref, ys_ref, wglu_ref, bglu_ref, wout_ref, h_ref, g_ref, o_ref):
    z = jax.nn.gelu(ys_ref[...])
    gate = jnp.dot(z.astype(BF16), wglu_ref[...], preferred_element_type=F32) + bglu_ref[...]
    y_ssm = z * jax.nn.sigmoid(gate)
    y = (jnp.dot(yp_ref[...], wout_ref[0:POOL_WIDTH, :], preferred_element_type=F32)
         + jnp.dot(y_ssm.astype(BF16), wout_ref[POOL_WIDTH:, :], preferred_element_type=F32))
    o_ref[...] = h_ref[...] + _rms(y, g_ref[...])


def _ab_out(y_pool, y_ssm, w_glu, b_glu, w_out, h, g, tm):
    m = h.shape[0]
    row = lambda width: pl.BlockSpec((tm, width), lambda i: (i, 0))
    return pl.pallas_call(
        _ab_out_kernel,
        grid=(m // tm,),
        in_specs=[row(POOL_WIDTH), row(SSM_WIDTH), _const_spec(w_glu.shape), _const_spec(b_glu.shape),
                  _const_spec(w_out.shape), row(D_MODEL), _const_spec(g.shape)],
        out_specs=row(D_MODEL),
        out_shape=jax.ShapeDtypeStruct((m, D_MODEL), F32),
        compiler_params=_cparams(1),
        name="ab_out",
    )(y_pool, y_ssm, w_glu, b_glu, w_out, h, g)


def _ffn_kernel(h_ref, gin_ref, wg_ref, wu_ref, wd_ref, gout_ref, o_ref):
    h = h_ref[...]
    xb = _rms(h, gin_ref[...]).astype(BF16)
    gate = jnp.dot(xb, wg_ref[...], preferred_element_type=F32)
    up = jnp.dot(xb, wu_ref[...], preferred_element_type=F32)
    act = (gate * jax.nn.sigmoid(gate) * up).astype(BF16)
    y = jnp.dot(act, wd_ref[...], preferred_element_type=F32)
    o_ref[...] = h + _rms(y, gout_ref[...])


def _ffn(h, g_in, w_gate, w_up, w_down, g_out, tm):
    m = h.shape[0]
    row = pl.BlockSpec((tm, D_MODEL), lambda i: (i, 0))
    return pl.pallas_call(
        _ffn_kernel,
        grid=(m // tm,),
        in_specs=[row, _const_spec(g_in.shape), _const_spec(w_gate.shape), _const_spec(w_up.shape),
                  _const_spec(w_down.shape), _const_spec(g_out.shape)],
        out_specs=row,
        out_shape=jax.ShapeDtypeStruct((m, D_MODEL), F32),
        compiler_params=_cparams(1),
        name="ffn",
    )(h, g_in, w_gate, w_up, w_down, g_out)


def _attn_prompt_kernel(q_ref, kc_ref, kp_ref, vc_ref, vp_ref, o_ref, lse_ref, *, parts, blocks):
    blk = pl.program_id(2)
    rows = 2 * ATT_BLOCK
    rpp = ATT_BLOCK // parts
    shift = rpp.bit_length() - 1
    qi = lax.broadcasted_iota(jnp.int32, (rows, 2 * ATT_BLOCK), 0) & (ATT_BLOCK - 1)
    kj = lax.broadcasted_iota(jnp.int32, (rows, 2 * ATT_BLOCK), 1)
    k_in = kj & (ATT_BLOCK - 1)
    is_cur = kj >> (ATT_BLOCK.bit_length() - 1)
    q_row = rpp + (qi & (rpp - 1))
    k_row = is_cur * rpp + (k_in & (rpp - 1))
    dist = parts * (q_row - k_row) + ((qi >> shift) - (k_in >> shift))
    in_band = (dist >= 0) & (dist <= ATT_SPAN)
    lane = lax.broadcasted_iota(jnp.int32, (ATT_BLOCK, LANES), 1)
    first_head = lane < HEAD_DIM

    def gather(ref, first, sl):
        return jnp.concatenate([ref[c, first:first + rpp, sl] for c in range(parts)], axis=0)

    for sub in range(blocks):
        here = sub * rpp
        valid = in_band if sub else in_band & ((kj >= ATT_BLOCK) | (blk > 0))
        lse_all = jnp.zeros((ATT_BLOCK, LANES), F32)
        for pair in range(ATT_HEADS // 2):
            sl = slice(pair * LANES, (pair + 1) * LANES)
            q2 = gather(q_ref, here, sl)
            zero = jnp.zeros_like(q2)
            qs = jnp.concatenate([jnp.where(first_head, q2, zero), jnp.where(first_head, zero, q2)], axis=0)
            k_before = gather(kc_ref, here - rpp, sl) if sub else gather(kp_ref, 0, sl)
            v_before = gather(vc_ref, here - rpp, sl) if sub else gather(vp_ref, 0, sl)
            k2 = jnp.concatenate([k_before, gather(kc_ref, here, sl)], axis=0)
            v2 = jnp.concatenate([v_before, gather(vc_ref, here, sl)], axis=0)
            s = lax.dot_general(qs, k2, (((1,), (1,)), ((), ())), preferred_element_type=F32)
            s = jnp.where(valid, s, -jnp.inf)
            m = jnp.max(s, axis=-1, keepdims=True)
            p = jnp.exp(s - m)
            den = jnp.sum(p, axis=-1, keepdims=True)
            o = jnp.dot(p.astype(BF16), v2, preferred_element_type=F32) / den
            lse = m + jnp.log(den)
            o_pair = jnp.where(first_head, o[:ATT_BLOCK], o[ATT_BLOCK:]).astype(o_ref.dtype)
            for c in range(parts):
                o_ref[c, here:here + rpp, sl] = o_pair[c * rpp:(c + 1) * rpp]
            lse_all = jnp.where(lane == 2 * pair, lse[:ATT_BLOCK],
                                jnp.where(lane == 2 * pair + 1, lse[ATT_BLOCK:], lse_all))
        for c in range(parts):
            lse_ref[c, here:here + rpp, :] = lse_all[c * rpp:(c + 1) * rpp]


def _attn_prompt(q, k, v, parts):
    batch, n_streams, sub, _ = q.shape
    pat_streams = n_streams // parts
    rpp = ATT_BLOCK // parts
    blocks = min(ATT_BLOCKS_PER_STEP, sub // rpp)
    step_rows = blocks * rpp
    nblk = sub // step_rows
    view = lambda a: a.reshape(batch, parts, pat_streams, sub, a.shape[-1])
    cur = pl.BlockSpec((None, parts, None, step_rows, ATT_WIDTH), lambda b, r, i: (b, 0, r, i, 0))
    prev = pl.BlockSpec((None, parts, None, rpp, ATT_WIDTH),
                        lambda b, r, i: (b, 0, r, jnp.maximum(i * blocks - 1, 0), 0))
    o, lse = pl.pallas_call(
        functools.partial(_attn_prompt_kernel, parts=parts, blocks=blocks),
        grid=(batch, pat_streams, nblk),
        in_specs=[cur, cur, prev, cur, prev],
        out_specs=[cur, pl.BlockSpec((None, parts, None, step_rows, LANES), lambda b, r, i: (b, 0, r, i, 0))],
        out_shape=[jax.ShapeDtypeStruct((batch, parts, pat_streams, sub, ATT_WIDTH), BF16),
                   jax.ShapeDtypeStruct((batch, parts, pat_streams, sub, LANES), F32)],
        compiler_params=_cparams(3),
        name="attn_prompt",
    )(view(q), view(k), view(k), view(v), view(v))
    return o.reshape(batch, n_streams, sub, ATT_WIDTH), lse.reshape(batch, n_streams, sub, LANES)


def _pattern_multiplicity(dist):
    mult = jnp.zeros(dist.shape, F32)
    for window, dil in DILATED_PATTERNS:
        hit = (dist >= 0) & (dist <= window) & ((dist & (dil - 1)) == 0)
        mult = mult + hit.astype(F32)
    return mult


def _attn_sample_kernel(q_ref, knew_ref, vnew_ref, kt_ref, vt_ref, o_ref, *, steps):
    t_cache = lax.broadcasted_iota(jnp.int32, (SUBLANES, MAX_WINDOW), 0)
    pos = lax.broadcasted_iota(jnp.int32, (SUBLANES, MAX_WINDOW), 1)
    mult = _pattern_multiplicity(MAX_WINDOW + t_cache - pos)
    t_new = lax.broadcasted_iota(jnp.int32, (SUBLANES, LANES), 0)
    j_new = lax.broadcasted_iota(jnp.int32, (SUBLANES, LANES), 1)
    mult_new = jnp.where(j_new < steps, _pattern_multiplicity(t_new - j_new), 0.0)
    nt = (((1,), (1,)), ((), ()))
    for head in range(ATT_HEADS):
        q = q_ref[0, head].astype(BF16)
        s = jnp.dot(q, kt_ref[0, head].astype(BF16), preferred_element_type=F32)
        s_new = lax.dot_general(q, knew_ref[0, head].astype(BF16), nt, preferred_element_type=F32)
        s = jnp.where(mult > 0.0, s, -jnp.inf)
        s_new = jnp.where(mult_new > 0.0, s_new, -jnp.inf)
        m = jnp.maximum(jnp.max(s, axis=-1, keepdims=True), jnp.max(s_new, axis=-1, keepdims=True))
        p = mult * jnp.exp(s - m)
        p_new = mult_new * jnp.exp(s_new - m)
        den = jnp.sum(p, axis=-1, keepdims=True) + jnp.sum(p_new, axis=-1, keepdims=True)
        o = lax.dot_general(p.astype(BF16), vt_ref[0, head].astype(BF16), nt, preferred_element_type=F32)
        o = o + jnp.dot(p_new.astype(BF16), vnew_ref[0, head].astype(BF16), preferred_element_type=F32)
        o_ref[0, :, head * HEAD_DIM:(head + 1) * HEAD_DIM] = (o / den)[:steps]


def _attn_sample(q, k_new, v_new, cache_kt, cache_vt, steps):
    n = q.shape[0]
    assert steps <= SUBLANES
    per = lambda a: pl.BlockSpec((1,) + a.shape[1:], lambda b: (b, 0, 0, 0))
    args = (q, k_new, v_new, cache_kt, cache_vt)
    return pl.pallas_call(
        functools.partial(_attn_sample_kernel, steps=steps),
        grid=(n,),
        in_specs=[per(a) for a in args],
        out_specs=pl.BlockSpec((1, steps, ATT_WIDTH), lambda b: (b, 0, 0)),
        out_shape=jax.ShapeDtypeStruct((n, steps, ATT_WIDTH), F32),
        compiler_params=_cparams(1),
        name="attn_sample",
    )(*args)


def _attn_out_kernel(*refs, tm):
    n_pat = len(PATTERN_ORDER)
    o_refs = refs[:n_pat]
    lse_refs = refs[n_pat:2 * n_pat]
    w_ref, expand_ref, unperm_ref, h_ref, g_ref, out_ref, lse_nat = refs[2 * n_pat:]
    o_nat = []
    for pat, dil in enumerate(PATTERN_ORDER):
        if dil == 1:
            lse_nat[pat] = lse_refs[pat][0, 0]
            o_nat.append(o_refs[pat][0, 0].astype(F32))
            continue
        for res in range(dil):
            lse_nat[pat, pl.ds(res, tm // dil, stride=dil), :] = lse_refs[pat][0, res]
        rows = PERM_TILE // dil
        parts = []
        for part in range(tm // PERM_TILE):
            stacked = jnp.concatenate([o_refs[pat][0, res, part * rows:(part + 1) * rows] for res in range(dil)],
                                      axis=0)
            parts.append(jnp.dot(unperm_ref[...], stacked, preferred_element_type=F32))
        o_nat.append(jnp.concatenate(parts, axis=0))
    lses = [lse_nat[pat] for pat in range(n_pat)]
    top = functools.reduce(jnp.maximum, lses)
    es = [jnp.exp(l - top) for l in lses]
    total = functools.reduce(lambda x, y: x + y, es)
    spread = []
    for e in es[:-1]:
        wgt = e / total
        hi = wgt.astype(BF16)
        lo = (wgt - hi.astype(F32)).astype(BF16)
        spread.append(jnp.dot(jnp.concatenate([hi, lo], axis=1), expand_ref[...],
                              preferred_element_type=F32))
    spread.append(1.0 - functools.reduce(lambda x, y: x + y, spread))
    pieces = []
    for c in range(COL_TILES):
        acc = None
        for pat in range(n_pat):
            term = spread[pat][:, c * LANES:(c + 1) * LANES] * o_nat[pat][:, c * LANES:(c + 1) * LANES]
            acc = term if acc is None else acc + term
        pieces.append(acc.astype(BF16))
    a = jnp.concatenate(pieces, axis=1)
    y = jnp.dot(a, w_ref[...], preferred_element_type=F32)
    out_ref[...] = h_ref[...] + _rms(y, g_ref[...])


def _attn_out(outs, lses, w_o, h, g, seq, tm):
    m = h.shape[0]
    tiles = seq // tm
    n_pat = len(PATTERN_ORDER)

    def dil_spec(dil, width):
        return pl.BlockSpec((1, dil, tm // dil, width), lambda i: (i // tiles, 0, i % tiles, 0))

    row = pl.BlockSpec((tm, D_MODEL), lambda i: (i, 0))
    expand = (jnp.arange(LANES)[:, None] == jnp.arange(ATT_WIDTH)[None, :] // HEAD_DIM).astype(BF16)
    expand = jnp.concatenate([expand, expand], axis=0)
    order = ATT_ORDERS[1]
    assert all(dil in (1, order) for dil in PATTERN_ORDER)
    assert tm % PERM_TILE == 0
    nat = jnp.arange(PERM_TILE)
    unperm = (jnp.arange(PERM_TILE)[None, :]
              == ((nat % order) * (PERM_TILE // order) + nat // order)[:, None]).astype(BF16)
    return pl.pallas_call(
        functools.partial(_attn_out_kernel, tm=tm),
        grid=(m // tm,),
        in_specs=[dil_spec(dil, ATT_WIDTH) for dil in PATTERN_ORDER]
        + [dil_spec(dil, LANES) for dil in PATTERN_ORDER]
        + [_const_spec(w_o.shape), _const_spec(expand.shape), _const_spec(unperm.shape), row,
           _const_spec(g.shape)],
        out_specs=row,
        out_shape=jax.ShapeDtypeStruct((m, D_MODEL), F32),
        scratch_shapes=[pltpu.VMEM((n_pat, tm, LANES), F32)],
        compiler_params=_cparams(1),
        name="attn_out",
    )(*outs, *lses, w_o, expand, unperm, h, g)


def _proj_out_kernel(a_ref, w_ref, h_ref, g_ref, out_ref):
    y = jnp.dot(a_ref[...].astype(BF16), w_ref[...], preferred_element_type=F32)
    out_ref[...] = h_ref[...] + _rms(y, g_ref[...])


def _proj_out(a, w, h, g, tm):
    m = h.shape[0]
    return pl.pallas_call(
        _proj_out_kernel,
        grid=(m // tm,),
        in_specs=[pl.BlockSpec((tm, a.shape[1]), lambda i: (i, 0)), _const_spec(w.shape),
                  pl.BlockSpec((tm, D_MODEL), lambda i: (i, 0)), _const_spec(g.shape)],
        out_specs=pl.BlockSpec((tm, D_MODEL), lambda i: (i, 0)),
        out_shape=jax.ShapeDtypeStruct((m, D_MODEL), F32),
        compiler_params=_cparams(1),
        name="proj_out",
    )(a, w, h, g)


def kernel(x_prompt, x_sample, state_pool, state_s5, cache_k, cache_v, norm_gains, ab_w_in, ab_pool_w, ab_pool_scale, ab_lambda_re, ab_lambda_im, ab_log_dt, ab_b_re, ab_b_im, ab_c_re, ab_c_im, ab_d, ab_w_glu, ab_b_glu, ab_w_out, c_w_qkv, c_w_o, ffn_w_gate, ffn_w_up, ffn_w_down):
    batch, seq, d = x_prompt.shape
    n_dec, t_dec, _ = x_sample.shape
    mp, ms = batch * seq, n_dec * t_dec
    tm_p = tm_ffn = PROMPT_ROW_TILE
    gains = norm_gains.reshape(norm_gains.shape[0], 4, 1, d)

    hp = x_prompt.reshape(mp, d)
    hs = x_sample.reshape(ms, d)

    g = gains[0]
    w_in = ab_w_in[0].astype(BF16)
    pool_w = ab_pool_w[0].astype(BF16)
    pool_scale = ab_pool_scale[0].reshape(1, POOL_WIDTH)
    lead_p = jnp.zeros((batch, POOL_LEAD, POOL_WIDTH), F32)
    lead_s = jnp.pad(state_pool[0], ((0, 0), (POOL_LEAD - POOL_STATE, 0), (0, 0)))
    ussm_p, ypool_p, pool_tail = _in_proj_pool(x_prompt, g[0], w_in, lead_p, pool_w, pool_scale, MIXER_ROW_TILE, 0)
    (proj_s,) = _norm_matmul(hs, g[0], w_in, ((0, d, 1.0),), (F32,), ms)
    proj_s = proj_s.reshape(n_dec, t_dec, d)
    ypool_s = _pool_mixer(proj_s, lead_s, pool_w, pool_scale, t_dec, PAST_LEN)
    pool_prompt = pool_tail[:, POOL_LEAD - POOL_STATE:][None]
    pool_sample = jnp.concatenate([state_pool[0], proj_s[:, :, :POOL_WIDTH]], axis=1)[:, -POOL_STATE:][None]

    prompt_mats, sample_mats, lam_rows = _s5_matrices(
        ab_lambda_re[0], ab_lambda_im[0], ab_log_dt[0], ab_b_re[0], ab_b_im[0], ab_c_re[0], ab_c_im[0], ab_d[0])
    yssm_p, hre_p, him_p = _s5_prompt(ussm_p, prompt_mats, S5_ROW_TILE)
    u_tb = jnp.swapaxes(proj_s[:, :, POOL_WIDTH:], 0, 1)
    h0 = state_s5[0].reshape(n_dec, SSM_STATES, 2)
    yssm_tb, hre_s, him_s = _s5_sample(u_tb, h0[..., 0], h0[..., 1], sample_mats, lam_rows)
    yssm_s = jnp.swapaxes(yssm_tb, 0, 1)
    s5_prompt = jnp.stack([hre_p[:, 0], him_p[:, 0]], axis=-1).reshape(1, batch, SSM_GROUPS, SSM_STATE, 2)
    s5_sample = jnp.stack([hre_s, him_s], axis=-1).reshape(1, n_dec, SSM_GROUPS, SSM_STATE, 2)

    w_glu = ab_w_glu[0].astype(BF16)
    b_glu = ab_b_glu[0].reshape(1, SSM_WIDTH)
    w_out = ab_w_out[0].astype(BF16)
    hp = _ab_out(ypool_p.reshape(mp, POOL_WIDTH), yssm_p.reshape(mp, SSM_WIDTH), w_glu, b_glu, w_out, hp, g[1], MIXER_ROW_TILE)
    hs = _ab_out(ypool_s.reshape(ms, POOL_WIDTH), yssm_s.reshape(ms, SSM_WIDTH), w_glu, b_glu, w_out, hs, g[1], ms)

    wg, wu, wd = ffn_w_gate[0].astype(BF16), ffn_w_up[0].astype(BF16), ffn_w_down[0].astype(BF16)
    hp = _ffn(hp, g[2], wg, wu, wd, g[3], tm_ffn)
    hs = _ffn(hs, g[2], wg, wu, wd, g[3], ms)

    g = gains[1]
    w_qkv = c_w_qkv[0].astype(BF16)
    w_o = c_w_o[0].astype(BF16)
    keep = min(MAX_WINDOW, seq)
    *qkv_p, kt_tail, vt_tail = _qkv_prompt(hp, g[0], w_qkv, batch, seq, keep, tm_p)
    positions_first = lambda a: jnp.transpose(a.reshape(batch, ATT_HEADS, HEAD_DIM, keep), (0, 3, 1, 2))[None]
    k_prompt = positions_first(kt_tail)
    v_prompt = positions_first(vt_tail)
    q_nat, q_res, k_nat, k_res, v_nat, v_res = qkv_p
    outs, lses = [], []
    for q, k, v, parts in ((q_nat, k_nat, v_nat, 1), (q_res, k_res, v_res, MID_RATIO), (q_res, k_res, v_res, 1)):
        o, lse = _attn_prompt(q, k, v, parts)
        outs.append(o)
        lses.append(lse)
    hp = _attn_out(outs, lses, w_o, hp, g[1], seq, tm_p)

    scale = HEAD_DIM ** -0.5
    q_s, k_s, v_s = _norm_matmul(
        hs, g[0], w_qkv,
        ((0, ATT_WIDTH, scale), (ATT_WIDTH, ATT_WIDTH, 1.0), (2 * ATT_WIDTH, ATT_WIDTH, 1.0)),
        (F32, F32, F32), ms)
    k_sample = k_s.reshape(1, n_dec, t_dec, ATT_HEADS, HEAD_DIM)
    v_sample = v_s.reshape(1, n_dec, t_dec, ATT_HEADS, HEAD_DIM)
    head_major = lambda a, rows: jnp.pad(
        jnp.swapaxes(a.reshape(n_dec, t_dec, ATT_HEADS, HEAD_DIM), 1, 2),
        ((0, 0), (0, 0), (0, rows - t_dec), (0, 0)))
    cache_kt = jnp.transpose(cache_k[0], (0, 2, 3, 1))
    cache_vt = jnp.transpose(cache_v[0], (0, 2, 3, 1))
    o_s = _attn_sample(head_major(q_s, SUBLANES), head_major(k_s, LANES), head_major(v_s, LANES),
                       cache_kt, cache_vt, t_dec)
    hs = _proj_out(o_s.reshape(ms, ATT_WIDTH), w_o, hs, g[1], ms)

    wg, wu, wd = ffn_w_gate[1].astype(BF16), ffn_w_up[1].astype(BF16), ffn_w_down[1].astype(BF16)
    hp = _ffn(hp, g[2], wg, wu, wd, g[3], tm_ffn)
    hs = _ffn(hs, g[2], wg, wu, wd, g[3], ms)

    return (hp.reshape(batch, seq, d), hs.reshape(n_dec, t_dec, d), pool_prompt, s5_prompt,
            k_prompt, v_prompt, pool_sample, s5_sample, k_sample, v_sample)
```

```python
import functools
import math

import jax
import jax.numpy as jnp
from jax import lax
from jax.experimental import pallas as pl
from jax.experimental.pallas import tpu as pltpu

F32 = jnp.float32
BF16 = jnp.bfloat16

D_MODEL = 1024
PAST_LEN = 16384
POOL_WIDTH = 512
POOL_WINDOWS = (2, 4, 8, 16)
POOL_GROUP = 128
POOL_STATE = 15
POOL_LEAD = 16
POOL_SEQS_PER_STEP = 8
SSM_WIDTH = 512
SSM_GROUP = 16
SSM_GROUPS = 32
SSM_STATE = 64
SSM_STATES = SSM_GROUPS * SSM_STATE
ATT_HEADS = 16
HEAD_DIM = 64
ATT_WIDTH = ATT_HEADS * HEAD_DIM
DILATED_PATTERNS = ((128, 1), (512, 4), (2048, 16))
MAX_WINDOW = 2048
ATT_BLOCK = 128
ATT_SPAN = 128
FFN_HIDDEN = 2816
FFN_CHUNK = 256
RMS_EPS = 1e-6

SUBLANES = 8
LANES = 128
VMEM_LIMIT = 56 * 1024 * 1024
PROMPT_ROW_TILE = 512
MIXER_ROW_TILE = 1024
FFN_ROW_TILE = 1024
S5_ROW_TILE = 2048


def _cparams(n_axes):
    return pltpu.CompilerParams(
        dimension_semantics=("arbitrary",) * n_axes, vmem_limit_bytes=VMEM_LIMIT)


def _rms(x, g):
    ms = jnp.mean(x * x, axis=-1, keepdims=True)
    return (x * lax.rsqrt(ms + RMS_EPS)) * g


def _const_spec(shape):
    zeros = (0,) * len(shape)
    return pl.BlockSpec(shape, lambda *_: zeros, pipeline_mode=pl.Buffered(1))


def _norm_matmul_kernel(x_ref, g_ref, w_ref, *out_refs, splits):
    xb = _rms(x_ref[...], g_ref[...]).astype(BF16)
    for (c0, width, scale), o_ref in zip(splits, out_refs):
        y = jnp.dot(xb, w_ref[:, c0:c0 + width], preferred_element_type=F32)
        if scale != 1.0:
            y = y * scale
        o_ref[...] = y.astype(o_ref.dtype)


def _norm_matmul(x, g, w, splits, dtypes, tm):
    m, d = x.shape
    n = w.shape[1]
    return pl.pallas_call(
        functools.partial(_norm_matmul_kernel, splits=splits),
        grid=(m // tm,),
        in_specs=[pl.BlockSpec((tm, d), lambda i: (i, 0)), _const_spec((1, d)), _const_spec((d, n))],
        out_specs=[pl.BlockSpec((tm, width), lambda i: (i, 0)) for _, width, _ in splits],
        out_shape=[jax.ShapeDtypeStruct((m, width), dt) for (_, width, _), dt in zip(splits, dtypes)],
        compiler_params=_cparams(1),
        name="norm_matmul",
    )(x, g, w)


ATT_DILATIONS = tuple(dil for _, dil in DILATED_PATTERNS)
ATT_ORDERS = (ATT_DILATIONS[0], ATT_DILATIONS[2])
PATTERN_ORDER = (ATT_DILATIONS[0], ATT_DILATIONS[2], ATT_DILATIONS[2])
MID_RATIO = ATT_DILATIONS[2] // ATT_DILATIONS[1]
COL_TILES = ATT_WIDTH // LANES
PERM_TILE = 256
ATT_BLOCKS_PER_STEP = 8


def _qkv_prompt_kernel(x_ref, g_ref, w_ref, perm_ref, *refs, q_scale, tm, tiles, first_tail):
    n_dil = len(ATT_ORDERS)
    out_refs = refs[:3 * n_dil]
    tail_refs = refs[3 * n_dil:]
    in_tail = (pl.program_id(0) % tiles) >= first_tail
    xb = _rms(x_ref[...], g_ref[...]).astype(BF16)
    for which in range(3):
        y = jnp.dot(xb, w_ref[:, which * ATT_WIDTH:(which + 1) * ATT_WIDTH], preferred_element_type=F32)
        if which == 0:
            y = y * q_scale
        else:
            @pl.when(in_tail)
            def _(y=y, tail_ref=tail_refs[which - 1]):
                tail_ref[0] = y.T
        yb = y.astype(BF16)
        for pat, (dil, o_ref) in enumerate(zip(ATT_ORDERS, out_refs[which * n_dil:(which + 1) * n_dil])):
            if dil == 1:
                o_ref[0, 0] = yb
                continue
            rows = PERM_TILE // dil
            for part in range(tm // PERM_TILE):
                moved = jnp.dot(perm_ref[pat - 1], yb[part * PERM_TILE:(part + 1) * PERM_TILE],
                                preferred_element_type=F32).astype(BF16)
                for res in range(dil):
                    o_ref[0, res, part * rows:(part + 1) * rows] = moved[res * rows:(res + 1) * rows]


def _qkv_prompt(x, g, w, batch, seq, keep, tm):
    m, d = x.shape
    assert tm % PERM_TILE == 0
    tiles = seq // tm
    first_tail = tiles - keep // tm

    def tail_map(i):
        return (i // tiles, 0, jnp.maximum(i % tiles - first_tail, 0))

    def dil_spec(dil):
        return pl.BlockSpec((1, dil, tm // dil, ATT_WIDTH), lambda i: (i // tiles, 0, i % tiles, 0))

    def dil_shape(dil):
        return jax.ShapeDtypeStruct((batch, dil, seq // dil, ATT_WIDTH), BF16)

    assert ATT_ORDERS[0] == 1
    src = jnp.arange(PERM_TILE)[None, :]
    dst = jnp.arange(PERM_TILE)[:, None]
    perm = jnp.stack([(src == (dst % (PERM_TILE // dil)) * dil + dst // (PERM_TILE // dil)).astype(BF16)
                      for dil in ATT_ORDERS[1:]])
    return pl.pallas_call(
        functools.partial(_qkv_prompt_kernel, q_scale=HEAD_DIM ** -0.5, tm=tm, tiles=tiles,
                          first_tail=first_tail),
        grid=(m // tm,),
        in_specs=[pl.BlockSpec((tm, d), lambda i: (i, 0)), _const_spec((1, d)),
                  _const_spec((d, 3 * ATT_WIDTH)), _const_spec(perm.shape)],
        out_specs=[dil_spec(dil) for _ in range(3) for dil in ATT_ORDERS]
        + [pl.BlockSpec((1, ATT_WIDTH, tm), tail_map)] * 2,
        out_shape=[dil_shape(dil) for _ in range(3) for dil in ATT_ORDERS]
        + [jax.ShapeDtypeStruct((batch, ATT_WIDTH, keep), F32)] * 2,
        compiler_params=_cparams(1),
        name="qkv_prompt",
    )(x, g, w, perm)


def _pool_tile(u, j, lead_ref, w_ref, scale_ref, y_ref, ext, *, tt, pos0, carry, seq=0):
    @pl.when(j == 0)
    def _():
        ext[0:POOL_LEAD, :] = lead_ref[seq]

    ext[POOL_LEAD:POOL_LEAD + tt, :] = u
    pos = pos0 + j * tt + lax.broadcasted_iota(jnp.int32, (tt, POOL_GROUP), 0)
    for grp, window in enumerate(POOL_WINDOWS):
        sl = slice(grp * POOL_GROUP, (grp + 1) * POOL_GROUP)
        u_g = u[:, sl]
        win_sum = u_g
        for k in range(1, window):
            win_sum = win_sum + ext[POOL_LEAD - k:POOL_LEAD - k + tt, sl]
        count = jnp.minimum(pos + 1, window).astype(F32)
        diff = win_sum / count - u_g
        y = jnp.dot(diff.astype(BF16), w_ref[grp], preferred_element_type=F32)
        y_ref[seq, :, sl] = (y * scale_ref[:, sl]).astype(y_ref.dtype)
    if carry:
        ext[0:POOL_LEAD, :] = ext[tt:tt + POOL_LEAD, :]


def _pool_kernel(u_ref, lead_ref, w_ref, scale_ref, y_ref, ext, *, tt, pos0, carry, n_seq):
    for seq in range(n_seq):
        _pool_tile(u_ref[seq], pl.program_id(1), lead_ref, w_ref, scale_ref, y_ref, ext,
                   tt=tt, pos0=pos0, carry=carry, seq=seq)


def _in_proj_pool_kernel(x_ref, g_ref, w_ref, lead_ref, pw_ref, scale_ref, ussm_ref, ypool_ref, tail_ref, ext,
                         *, tt, pos0):
    xb = _rms(x_ref[0], g_ref[...]).astype(BF16)
    ussm_ref[0] = jnp.dot(xb, w_ref[:, POOL_WIDTH:], preferred_element_type=F32)
    u_pool = jnp.dot(xb, w_ref[:, 0:POOL_WIDTH], preferred_element_type=F32)
    _pool_tile(u_pool, pl.program_id(1), lead_ref, pw_ref, scale_ref, ypool_ref, ext, tt=tt, pos0=pos0, carry=True)
    tail_ref[0] = ext[0:POOL_LEAD, :]


def _in_proj_pool(x, g, w, lead, pool_w, pool_scale, tt, pos0):
    n, t, d = x.shape
    seq_block = lambda width: pl.BlockSpec((1, tt, width), lambda b, j: (b, j, 0))
    per_seq = pl.BlockSpec((1, POOL_LEAD, POOL_WIDTH), lambda b, j: (b, 0, 0))
    return pl.pallas_call(
        functools.partial(_in_proj_pool_kernel, tt=tt, pos0=pos0),
        grid=(n, t // tt),
        in_specs=[seq_block(d), _const_spec(g.shape), _const_spec(w.shape), per_seq,
                  _const_spec(pool_w.shape), _const_spec(pool_scale.shape)],
        out_specs=[seq_block(SSM_WIDTH), seq_block(POOL_WIDTH), per_seq],
        out_shape=[jax.ShapeDtypeStruct((n, t, SSM_WIDTH), F32), jax.ShapeDtypeStruct((n, t, POOL_WIDTH), BF16),
                   jax.ShapeDtypeStruct((n, POOL_LEAD, POOL_WIDTH), F32)],
        scratch_shapes=[pltpu.VMEM((POOL_LEAD + tt, POOL_WIDTH), F32)],
        compiler_params=_cparams(2),
        name="in_proj_pool",
    )(x, g, w, lead, pool_w, pool_scale)


def _pool_mixer(proj, lead, w, scale, tt, pos0):
    n, t, _ = proj.shape
    steps = t // tt
    n_seq = 1 if steps > 1 else math.gcd(n, POOL_SEQS_PER_STEP)
    return pl.pallas_call(
        functools.partial(_pool_kernel, tt=tt, pos0=pos0, carry=steps > 1, n_seq=n_seq),
        grid=(n // n_seq, steps),
        in_specs=[pl.BlockSpec((n_seq, tt, POOL_WIDTH), lambda b, j: (b, j, 0)),
                  pl.BlockSpec((n_seq, POOL_LEAD, POOL_WIDTH), lambda b, j: (b, 0, 0)),
                  _const_spec((len(POOL_WINDOWS), POOL_GROUP, POOL_GROUP)),
                  _const_spec((1, POOL_WIDTH))],
        out_specs=pl.BlockSpec((n_seq, tt, POOL_WIDTH), lambda b, j: (b, j, 0)),
        out_shape=jax.ShapeDtypeStruct((n, t, POOL_WIDTH), BF16),
        scratch_shapes=[pltpu.VMEM((POOL_LEAD + tt, POOL_WIDTH), F32)],
        compiler_params=_cparams(2),
        name="pool_mixer",
    )(proj, lead, w, scale)


S5_IN_HALF = SSM_WIDTH // 2
S5_STATE_HALF = SSM_STATES // 2
S5_OUT_TILE = LANES
S5_STATE_TILE = SSM_STATES // (SSM_WIDTH // S5_OUT_TILE)
S5_SCAN_LANES = 512


def _s5_input(ub, bre_ref, bim_ref):
    re, im = [], []
    for half in range(2):
        uk = ub[:, half * S5_IN_HALF:(half + 1) * S5_IN_HALF]
        re.append(jnp.dot(uk, bre_ref[half], preferred_element_type=F32))
        im.append(jnp.dot(uk, bim_ref[half], preferred_element_type=F32))
    return re, im


def _s5_output(h_re, h_im, cre_ref, cim_ref, tile):
    return (jnp.dot(h_re, cre_ref[tile], preferred_element_type=F32)
            - jnp.dot(h_im, cim_ref[tile], preferred_element_type=F32))


S5_CHUNK = 16
S5_ROW = S5_CHUNK * SSM_GROUP
S5_SUB = 256
UNIT = SSM_GROUP
UNITS = LANES // UNIT
S5_PAIRS = SSM_GROUPS // 2


def _unit_transpose(pieces):
    unit = lax.broadcasted_iota(jnp.int32, pieces[0].shape, 1) >> (UNIT.bit_length() - 1)
    cur = list(pieces)
    k = UNITS // 2
    while k:
        high = (unit & k) != 0
        nxt = list(cur)
        for a in range(UNITS):
            if a & k:
                continue
            lo_piece, hi_piece = cur[a], cur[a | k]
            nxt[a] = jnp.where(high, pltpu.roll(hi_piece, k * UNIT, 1), lo_piece)
            nxt[a | k] = jnp.where(high, hi_piece, pltpu.roll(lo_piece, LANES - k * UNIT, 1))
        cur = nxt
        k //= 2
    return cur


def _s5_prompt_kernel(u_ref, perm_ref, toep_ref, wre_ref, wim_ref, vre_ref, vim_ref, pw_ref, d_ref,
                      y_ref, hre_ref, him_ref, u2, y2, s_re, s_im, c_re, c_im, y_nat, *, tt):
    j = pl.program_id(1)
    chunks = tt // S5_CHUNK
    sub_chunks = S5_SUB // S5_CHUNK

    @pl.when(j == 0)
    def _():
        c_re[...] = jnp.zeros_like(c_re)
        c_im[...] = jnp.zeros_like(c_im)

    def relayout_in(sub, carry):
        r0 = pl.multiple_of(sub * S5_SUB, S5_SUB)
        c0 = pl.multiple_of(sub * sub_chunks, sub_chunks)
        ub = u_ref[0, pl.ds(r0, S5_SUB), :].astype(BF16)
        xs = jnp.dot(perm_ref[...], ub, preferred_element_type=F32).astype(BF16)
        for tile in range(SSM_WIDTH // LANES):
            for half in range(S5_CHUNK // UNITS):
                pieces = [pltpu.bitcast(
                    xs[(half * UNITS + a) * sub_chunks:(half * UNITS + a + 1) * sub_chunks,
                       tile * LANES:(tile + 1) * LANES], jnp.uint32) for a in range(UNITS)]
                outs = _unit_transpose(pieces)
                for gl in range(UNITS):
                    u2[tile * UNITS + gl, pl.ds(c0, sub_chunks), half * LANES:(half + 1) * LANES] = (
                        pltpu.bitcast(outs[gl], BF16))
        return carry

    lax.fori_loop(0, tt // S5_SUB, relayout_in, 0, unroll=2)

    s_re[0:SUBLANES, :] = c_re[...]
    s_im[0:SUBLANES, :] = c_im[...]
    for pair in range(S5_PAIRS):
        sl = slice(pair * LANES, (pair + 1) * LANES)
        ua = u2[2 * pair]
        ub = u2[2 * pair + 1]
        s_re[SUBLANES:SUBLANES + chunks, sl] = (
            jnp.dot(ua, wre_ref[2 * pair], preferred_element_type=F32)
            + jnp.dot(ub, wre_ref[2 * pair + 1], preferred_element_type=F32))
        s_im[SUBLANES:SUBLANES + chunks, sl] = (
            jnp.dot(ua, wim_ref[2 * pair], preferred_element_type=F32)
            + jnp.dot(ub, wim_ref[2 * pair + 1], preferred_element_type=F32))

    for chunk in range(SSM_STATES // S5_SCAN_LANES):
        sl = slice(chunk * S5_SCAN_LANES, (chunk + 1) * S5_SCAN_LANES)

        def body(r, carry, sl=sl):
            in_re, in_im = carry
            row = pl.multiple_of((r + 1) * SUBLANES, SUBLANES)
            re = s_re[pl.ds(row, SUBLANES), sl]
            im = s_im[pl.ds(row, SUBLANES), sl]
            for level, shift in enumerate((1, 2, 4)):
                a_re = pw_ref[2 * level, :, sl]
                a_im = pw_ref[2 * level + 1, :, sl]
                sh_re = pltpu.roll(re, shift, 0)
                sh_im = pltpu.roll(im, shift, 0)
                re, im = (re + a_re * sh_re - a_im * sh_im,
                          im + a_re * sh_im + a_im * sh_re)
            p_re = pw_ref[6, :, sl]
            p_im = pw_ref[7, :, sl]
            re, im = (re + p_re * in_re - p_im * in_im,
                      im + p_re * in_im + p_im * in_re)
            s_re[pl.ds(row, SUBLANES), sl] = re
            s_im[pl.ds(row, SUBLANES), sl] = im
            last = SUBLANES - 1
            return (jnp.broadcast_to(re[last:last + 1, :], re.shape),
                    jnp.broadcast_to(im[last:last + 1, :], im.shape))

        out_re, out_im = lax.fori_loop(0, chunks // SUBLANES, body, (c_re[:, sl], c_im[:, sl]))
        c_re[:, sl] = out_re
        c_im[:, sl] = out_im

    hre_ref[0] = c_re[...]
    him_ref[0] = c_im[...]

    for pair in range(S5_PAIRS):
        sl = slice(pair * LANES, (pair + 1) * LANES)
        h_re = s_re[SUBLANES - 1:SUBLANES - 1 + chunks, sl].astype(BF16)
        h_im = s_im[SUBLANES - 1:SUBLANES - 1 + chunks, sl].astype(BF16)
        carried = (jnp.dot(h_re, vre_ref[pair], preferred_element_type=F32)
                   + jnp.dot(h_im, vim_ref[pair], preferred_element_type=F32))
        for k in range(2):
            grp = 2 * pair + k
            y2[grp] = (jnp.dot(u2[grp], toep_ref[grp], preferred_element_type=F32)
                       + carried[:, k * S5_ROW:(k + 1) * S5_ROW])

    def relayout_out(blk, carry):
        r0 = pl.multiple_of(blk * S5_SUB, S5_SUB)
        c0 = pl.multiple_of(blk * sub_chunks, sub_chunks)
        for tile in range(SSM_WIDTH // LANES):
            ot = slice(tile * LANES, (tile + 1) * LANES)
            for half in range(S5_CHUNK // UNITS):
                pieces = [y2[tile * UNITS + gl, pl.ds(c0, sub_chunks), half * LANES:(half + 1) * LANES]
                          for gl in range(UNITS)]
                outs = _unit_transpose(pieces)
                for a in range(UNITS):
                    y_nat[tile, pl.ds(half * UNITS + a, sub_chunks, stride=S5_CHUNK), :] = outs[a]
            y_ref[0, pl.ds(r0, S5_SUB), ot] = y_nat[tile] + d_ref[:, ot] * u_ref[0, pl.ds(r0, S5_SUB), ot]
        return carry

    lax.fori_loop(0, tt // S5_SUB, relayout_out, 0)


def _s5_prompt(u, mats, tt):
    b, t, _ = u.shape
    chunks = tt // S5_CHUNK
    state = jax.ShapeDtypeStruct((b, SUBLANES, SSM_STATES), F32)
    state_spec = pl.BlockSpec((1, SUBLANES, SSM_STATES), lambda i, j: (i, 0, 0))
    return pl.pallas_call(
        functools.partial(_s5_prompt_kernel, tt=tt),
        grid=(b, t // tt),
        in_specs=[pl.BlockSpec((1, tt, SSM_WIDTH), lambda i, j: (i, j, 0))]
        + [_const_spec(a.shape) for a in mats],
        out_specs=[pl.BlockSpec((1, tt, SSM_WIDTH), lambda i, j: (i, j, 0)), state_spec, state_spec],
        out_shape=[jax.ShapeDtypeStruct((b, t, SSM_WIDTH), F32), state, state],
        scratch_shapes=[pltpu.VMEM((SSM_GROUPS, chunks, S5_ROW), BF16),
                        pltpu.VMEM((SSM_GROUPS, chunks, S5_ROW), F32),
                        pltpu.VMEM((SUBLANES + chunks, SSM_STATES), F32),
                        pltpu.VMEM((SUBLANES + chunks, SSM_STATES), F32),
                        pltpu.VMEM((SUBLANES, SSM_STATES), F32), pltpu.VMEM((SUBLANES, SSM_STATES), F32),
                        pltpu.VMEM((SSM_WIDTH // LANES, S5_SUB, LANES), F32)],
        compiler_params=_cparams(2),
        name="s5_prompt",
    )(u, *mats)


def _s5_sample_kernel(u_ref, h0re_ref, h0im_ref, bre_ref, bim_ref, cre_ref, cim_ref, lam_ref, d_ref,
                      y_ref, hre_ref, him_ref, *, steps):
    h_re = h0re_ref[...]
    h_im = h0im_ref[...]
    lam_re = lam_ref[0:1, :]
    lam_im = lam_ref[1:2, :]
    for t in range(steps):
        u = u_ref[t]
        bu_re, bu_im = _s5_input(u.astype(BF16), bre_ref, bim_ref)
        bu_re = jnp.concatenate(bu_re, axis=1)
        bu_im = jnp.concatenate(bu_im, axis=1)
        h_re, h_im = (lam_re * h_re - lam_im * h_im + bu_re,
                      lam_re * h_im + lam_im * h_re + bu_im)
        hb_re = h_re.astype(BF16)
        hb_im = h_im.astype(BF16)
        for tile in range(SSM_WIDTH // S5_OUT_TILE):
            st = slice(tile * S5_STATE_TILE, (tile + 1) * S5_STATE_TILE)
            ot = slice(tile * S5_OUT_TILE, (tile + 1) * S5_OUT_TILE)
            y = _s5_output(hb_re[:, st], hb_im[:, st], cre_ref, cim_ref, tile)
            y_ref[t, :, ot] = y + d_ref[:, ot] * u[:, ot]
    hre_ref[...] = h_re
    him_ref[...] = h_im


def _s5_sample(u_tb, h0_re, h0_im, mats, lam):
    steps, n, _ = u_tb.shape
    bre, bim, cre, cim, d_skip = mats
    state = jax.ShapeDtypeStruct((n, SSM_STATES), F32)
    args = (u_tb, h0_re, h0_im, bre, bim, cre, cim, lam, d_skip)
    return pl.pallas_call(
        functools.partial(_s5_sample_kernel, steps=steps),
        grid=(1,),
        in_specs=[_const_spec(a.shape) for a in args],
        out_specs=[_const_spec(u_tb.shape), _const_spec((n, SSM_STATES)), _const_spec((n, SSM_STATES))],
        out_shape=[jax.ShapeDtypeStruct(u_tb.shape, F32), state, state],
        compiler_params=_cparams(1),
        name="s5_sample",
    )(*args)


def _s5_matrices(lam_re, lam_im, log_dt, b_re, b_im, c_re, c_im, d_skip):
    lam = lax.complex(lam_re, lam_im)
    dt = jnp.exp(log_dt)[:, None]
    lam_bar = jnp.exp(lam * dt)
    b_bar = ((lam_bar - 1.0) / lam)[..., None] * lax.complex(b_re, b_im)
    eye_half = jnp.eye(SSM_GROUPS // 2, dtype=F32)

    def in_blocks(x):
        x = x.reshape(2, SSM_GROUPS // 2, SSM_STATE, SSM_GROUP)
        return jnp.einsum("kgpi,gh->kgihp", x, eye_half).reshape(2, S5_IN_HALF, S5_STATE_HALF).astype(BF16)

    tiles = SSM_WIDTH // S5_OUT_TILE
    groups_per_tile = SSM_GROUPS // tiles
    eye_tile = jnp.eye(groups_per_tile, dtype=F32)

    def out_blocks(x):
        x = x.reshape(tiles, groups_per_tile, SSM_GROUP, SSM_STATE)
        return jnp.einsum("kgop,gh->kgpho", x, eye_tile).reshape(tiles, S5_STATE_TILE, S5_OUT_TILE).astype(BF16)

    lam_flat = lam_bar.reshape(1, SSM_STATES)
    lam_rows = jnp.concatenate([lam_flat.real, lam_flat.imag], axis=0).astype(F32)
    d_row = d_skip.reshape(1, SSM_WIDTH)
    sample_mats = (in_blocks(b_bar.real), in_blocks(b_bar.imag), out_blocks(c_re), out_blocks(c_im), d_row)

    c_mat = lax.complex(c_re, c_im)
    lam_pow = jnp.concatenate([jnp.ones((1,) + lam_bar.shape, lam_bar.dtype),
                               jnp.cumprod(jnp.broadcast_to(lam_bar, (S5_CHUNK,) + lam_bar.shape), axis=0)])
    taps = jnp.einsum("gop,kgp,gpi->gkoi", c_mat, lam_pow[:S5_CHUNK], b_bar,
                      precision=lax.Precision.HIGHEST).real
    src = jnp.arange(S5_CHUNK)[None, :, None]
    dst = jnp.arange(S5_CHUNK)[None, None, :]
    lag_is = (dst - src == jnp.arange(S5_CHUNK)[:, None, None]).astype(F32)
    toep = jnp.einsum("kst,gkoi->gsito", lag_is, taps, precision=lax.Precision.HIGHEST)
    toep = toep.reshape(SSM_GROUPS, S5_ROW, S5_ROW).astype(BF16)
    w_state = jnp.einsum("sgp,gpi->gsip", lam_pow[S5_CHUNK - 1::-1][:S5_CHUNK], b_bar)
    w_state = w_state.reshape(SSM_GROUPS, S5_ROW, SSM_STATE)
    second = (jnp.arange(SSM_GROUPS) % 2 == 1)[:, None, None]
    zeros = jnp.zeros_like(w_state.real)

    def pair_cols(x):
        return jnp.where(second, jnp.concatenate([zeros, x], -1), jnp.concatenate([x, zeros], -1)).astype(BF16)

    v_out = jnp.einsum("gop,tgp->gpto", c_mat, lam_pow[1:]).reshape(SSM_GROUPS, SSM_STATE, S5_ROW)
    eye_pair = jnp.eye(2, dtype=F32)

    def pair_blocks(x):
        x = x.reshape(S5_PAIRS, 2, SSM_STATE, S5_ROW)
        return jnp.einsum("kapc,ab->kapbc", x, eye_pair).reshape(S5_PAIRS, LANES, 2 * S5_ROW).astype(BF16)

    lam_chunk = lam_pow[S5_CHUNK].reshape(1, SSM_STATES)
    rows = jnp.arange(SUBLANES)[:, None]
    planes = []
    for shift in (1, 2, 4):
        a = jnp.where(rows >= shift, lam_chunk ** shift, 0.0)
        planes += [a.real, a.imag]
    carry = jnp.cumprod(jnp.broadcast_to(lam_chunk, (SUBLANES, SSM_STATES)), axis=0)
    planes += [carry.real, carry.imag]
    powers = jnp.stack(planes).astype(F32)
    pos = jnp.arange(S5_SUB)
    perm = (jnp.arange(S5_SUB)[None, :] == ((pos % (S5_SUB // S5_CHUNK)) * S5_CHUNK
                                            + pos // (S5_SUB // S5_CHUNK))[:, None]).astype(BF16)
    prompt_mats = (perm, toep, pair_cols(w_state.real), pair_cols(w_state.imag),
                   pair_blocks(v_out.real), pair_blocks(-v_out.imag), powers, d_row)
    return prompt_mats, sample_mats, lam_rows


def _ab_out_kernel(yp_ref, ys_ref, wglu_ref, bglu_ref, wout_ref, h_ref, g_ref, o_ref):
    z = jax.nn.gelu(ys_ref[...])
    gate = jnp.dot(z.astype(BF16), wglu_ref[...], preferred_element_type=F32) + bglu_ref[...]
    y_ssm = z * jax.nn.sigmoid(gate)
    y = (jnp.dot(yp_ref[...], wout_ref[0:POOL_WIDTH, :], preferred_element_type=F32)
         + jnp.dot(y_ssm.astype(BF16), wout_ref[POOL_WIDTH:, :], preferred_element_type=F32))
    o_ref[...] = h_ref[...] + _rms(y, g_ref[...])


def _ab_out(y_pool, y_ssm, w_glu, b_glu, w_out, h, g, tm):
    m = h.shape[0]
    row = lambda width: pl.BlockSpec((tm, width), lambda i: (i, 0))
    return pl.pallas_call(
        _ab_out_kernel,
        grid=(m // tm,),
        in_specs=[row(POOL_WIDTH), row(SSM_WIDTH), _const_spec(w_glu.shape), _const_spec(b_glu.shape),
                  _const_spec(w_out.shape), row(D_MODEL), _const_spec(g.shape)],
        out_specs=row(D_MODEL),
        out_shape=jax.ShapeDtypeStruct((m, D_MODEL), F32),
        compiler_params=_cparams(1),
        name="ab_out",
    )(y_pool, y_ssm, w_glu, b_glu, w_out, h, g)


def _ffn_kernel(h_ref, gin_ref, wg_ref, wu_ref, wd_ref, gout_ref, o_ref):
    h = h_ref[...]
    xb = _rms(h, gin_ref[...]).astype(BF16)
    y = None
    for c in range(FFN_HIDDEN // FFN_CHUNK):
        cs = slice(c * FFN_CHUNK, (c + 1) * FFN_CHUNK)
        gate = jnp.dot(xb, wg_ref[:, cs], preferred_element_type=F32)
        up = jnp.dot(xb, wu_ref[:, cs], preferred_element_type=F32)
        act = (gate * jax.nn.sigmoid(gate) * up).astype(BF16)
        part = jnp.dot(act, wd_ref[cs, :], preferred_element_type=F32)
        y = part if y is None else y + part
    o_ref[...] = h + _rms(y, gout_ref[...])


def _ffn(h, g_in, w_gate, w_up, w_down, g_out, tm):
    m = h.shape[0]
    row = pl.BlockSpec((tm, D_MODEL), lambda i: (i, 0))
    return pl.pallas_call(
        _ffn_kernel,
        grid=(m // tm,),
        in_specs=[row, _const_spec(g_in.shape), _const_spec(w_gate.shape), _const_spec(w_up.shape),
                  _const_spec(w_down.shape), _const_spec(g_out.shape)],
        out_specs=row,
        out_shape=jax.ShapeDtypeStruct((m, D_MODEL), F32),
        compiler_params=_cparams(1),
        name="ffn",
    )(h, g_in, w_gate, w_up, w_down, g_out)


def _attn_prompt_kernel(q_ref, kc_ref, kp_ref, vc_ref, vp_ref, o_ref, lse_ref, *, parts, blocks):
    blk = pl.program_id(2)
    rows = 2 * ATT_BLOCK
    rpp = ATT_BLOCK // parts
    shift = rpp.bit_length() - 1
    qi = lax.broadcasted_iota(jnp.int32, (rows, 2 * ATT_BLOCK), 0) & (ATT_BLOCK - 1)
    kj = lax.broadcasted_iota(jnp.int32, (rows, 2 * ATT_BLOCK), 1)
    k_in = kj & (ATT_BLOCK - 1)
    is_cur = kj >> (ATT_BLOCK.bit_length() - 1)
    q_row = rpp + (qi & (rpp - 1))
    k_row = is_cur * rpp + (k_in & (rpp - 1))
    dist = parts * (q_row - k_row) + ((qi >> shift) - (k_in >> shift))
    in_band = (dist >= 0) & (dist <= ATT_SPAN)
    lane = lax.broadcasted_iota(jnp.int32, (ATT_BLOCK, LANES), 1)
    first_head = lane < HEAD_DIM

    def gather(ref, first, sl):
        return jnp.concatenate([ref[c, first:first + rpp, sl] for c in range(parts)], axis=0)

    for sub in range(blocks):
        here = sub * rpp
        valid = in_band if sub else in_band & ((kj >= ATT_BLOCK) | (blk > 0))
        lse_all = jnp.zeros((ATT_BLOCK, LANES), F32)
        for pair in range(ATT_HEADS // 2):
            sl = slice(pair * LANES, (pair + 1) * LANES)
            q2 = gather(q_ref, here, sl)
            zero = jnp.zeros_like(q2)
            qs = jnp.concatenate([jnp.where(first_head, q2, zero), jnp.where(first_head, zero, q2)], axis=0)
            k_before = gather(kc_ref, here - rpp, sl) if sub else gather(kp_ref, 0, sl)
            v_before = gather(vc_ref, here - rpp, sl) if sub else gather(vp_ref, 0, sl)
            k2 = jnp.concatenate([k_before, gather(kc_ref, here, sl)], axis=0)
            v2 = jnp.concatenate([v_before, gather(vc_ref, here, sl)], axis=0)
            s = lax.dot_general(qs, k2, (((1,), (1,)), ((), ())), preferred_element_type=F32)
            s = jnp.where(valid, s, -jnp.inf)
            m = jnp.max(s, axis=-1, keepdims=True)
            p = jnp.exp(s - m)
            den = jnp.sum(p, axis=-1, keepdims=True)
            o = jnp.dot(p.astype(BF16), v2, preferred_element_type=F32) / den
            lse = m + jnp.log(den)
            o_pair = jnp.where(first_head, o[:ATT_BLOCK], o[ATT_BLOCK:]).astype(o_ref.dtype)
            for c in range(parts):
                o_ref[c, here:here + rpp, sl] = o_pair[c * rpp:(c + 1) * rpp]
            lse_all = jnp.where(lane == 2 * pair, lse[:ATT_BLOCK],
                                jnp.where(lane == 2 * pair + 1, lse[ATT_BLOCK:], lse_all))
        for c in range(parts):
            lse_ref[c, here:here + rpp, :] = lse_all[c * rpp:(c + 1) * rpp]


def _attn_prompt(q, k, v, parts):
    batch, n_streams, sub, _ = q.shape
    pat_streams = n_streams // parts
    rpp = ATT_BLOCK // parts
    blocks = min(ATT_BLOCKS_PER_STEP, sub // rpp)
    step_rows = blocks * rpp
    nblk = sub // step_rows
    view = lambda a: a.reshape(batch, parts, pat_streams, sub, a.shape[-1])
    cur = pl.BlockSpec((None, parts, None, step_rows, ATT_WIDTH), lambda b, r, i: (b, 0, r, i, 0))
    prev = pl.BlockSpec((None, parts, None, rpp, ATT_WIDTH),
                        lambda b, r, i: (b, 0, r, jnp.maximum(i * blocks - 1, 0), 0))
    o, lse = pl.pallas_call(
        functools.partial(_attn_prompt_kernel, parts=parts, blocks=blocks),
        grid=(batch, pat_streams, nblk),
        in_specs=[cur, cur, prev, cur, prev],
        out_specs=[cur, pl.BlockSpec((None, parts, None, step_rows, LANES), lambda b, r, i: (b, 0, r, i, 0))],
        out_shape=[jax.ShapeDtypeStruct((batch, parts, pat_streams, sub, ATT_WIDTH), BF16),
                   jax.ShapeDtypeStruct((batch, parts, pat_streams, sub, LANES), F32)],
        compiler_params=_cparams(3),
        name="attn_prompt",
    )(view(q), view(k), view(k), view(v), view(v))
    return o.reshape(batch, n_streams, sub, ATT_WIDTH), lse.reshape(batch, n_streams, sub, LANES)


def _pattern_multiplicity(dist):
    mult = jnp.zeros(dist.shape, F32)
    for window, dil in DILATED_PATTERNS:
        hit = (dist >= 0) & (dist <= window) & ((dist & (dil - 1)) == 0)
        mult = mult + hit.astype(F32)
    return mult


def _attn_sample_kernel(q_ref, knew_ref, vnew_ref, kt_ref, vt_ref, o_ref, *, steps):
    t_cache = lax.broadcasted_iota(jnp.int32, (SUBLANES, MAX_WINDOW), 0)
    pos = lax.broadcasted_iota(jnp.int32, (SUBLANES, MAX_WINDOW), 1)
    mult = _pattern_multiplicity(MAX_WINDOW + t_cache - pos)
    t_new = lax.broadcasted_iota(jnp.int32, (SUBLANES, LANES), 0)
    j_new = lax.broadcasted_iota(jnp.int32, (SUBLANES, LANES), 1)
    mult_new = jnp.where(j_new < steps, _pattern_multiplicity(t_new - j_new), 0.0)
    nt = (((1,), (1,)), ((), ()))
    for head in range(ATT_HEADS):
        q = q_ref[0, head].astype(BF16)
        s = jnp.dot(q, kt_ref[0, head].astype(BF16), preferred_element_type=F32)
        s_new = lax.dot_general(q, knew_ref[0, head].astype(BF16), nt, preferred_element_type=F32)
        s = jnp.where(mult > 0.0, s, -jnp.inf)
        s_new = jnp.where(mult_new > 0.0, s_new, -jnp.inf)
        m = jnp.maximum(jnp.max(s, axis=-1, keepdims=True), jnp.max(s_new, axis=-1, keepdims=True))
        p = mult * jnp.exp(s - m)
        p_new = mult_new * jnp.exp(s_new - m)
        den = jnp.sum(p, axis=-1, keepdims=True) + jnp.sum(p_new, axis=-1, keepdims=True)
        o = lax.dot_general(p.astype(BF16), vt_ref[0, head].astype(BF16), nt, preferred_element_type=F32)
        o = o + jnp.dot(p_new.astype(BF16), vnew_ref[0, head].astype(BF16), preferred_element_type=F32)
        o_ref[0, :, head * HEAD_DIM:(head + 1) * HEAD_DIM] = (o / den)[:steps]


def _attn_sample(q, k_new, v_new, cache_kt, cache_vt, steps):
    n = q.shape[0]
    assert steps <= SUBLANES
    per = lambda a: pl.BlockSpec((1,) + a.shape[1:], lambda b: (b, 0, 0, 0))
    args = (q, k_new, v_new, cache_kt, cache_vt)
    return pl.pallas_call(
        functools.partial(_attn_sample_kernel, steps=steps),
        grid=(n,),
        in_specs=[per(a) for a in args],
        out_specs=pl.BlockSpec((1, steps, ATT_WIDTH), lambda b: (b, 0, 0)),
        out_shape=jax.ShapeDtypeStruct((n, steps, ATT_WIDTH), F32),
        compiler_params=_cparams(1),
        name="attn_sample",
    )(*args)


def _attn_out_kernel(*refs, tm):
    n_pat = len(PATTERN_ORDER)
    o_refs = refs[:n_pat]
    lse_refs = refs[n_pat:2 * n_pat]
    w_ref, expand_ref, unperm_ref, h_ref, g_ref, out_ref, lse_nat = refs[2 * n_pat:]
    o_nat = []
    for pat, dil in enumerate(PATTERN_ORDER):
        if dil == 1:
            lse_nat[pat] = lse_refs[pat][0, 0]
            o_nat.append(o_refs[pat][0, 0].astype(F32))
            continue
        for res in range(dil):
            lse_nat[pat, pl.ds(res, tm // dil, stride=dil), :] = lse_refs[pat][0, res]
        rows = PERM_TILE // dil
        parts = []
        for part in range(tm // PERM_TILE):
            stacked = jnp.concatenate([o_refs[pat][0, res, part * rows:(part + 1) * rows] for res in range(dil)],
                                      axis=0)
            parts.append(jnp.dot(unperm_ref[...], stacked, preferred_element_type=F32))
        o_nat.append(jnp.concatenate(parts, axis=0))
    lses = [lse_nat[pat] for pat in range(n_pat)]
    top = functools.reduce(jnp.maximum, lses)
    es = [jnp.exp(l - top) for l in lses]
    total = functools.reduce(lambda x, y: x + y, es)
    spread = []
    for e in es[:-1]:
        wgt = e / total
        hi = wgt.astype(BF16)
        lo = (wgt - hi.astype(F32)).astype(BF16)
        spread.append(jnp.dot(jnp.concatenate([hi, lo], axis=1), expand_ref[...],
                              preferred_element_type=F32))
    spread.append(1.0 - functools.reduce(lambda x, y: x + y, spread))
    pieces = []
    for c in range(COL_TILES):
        acc = None
        for pat in range(n_pat):
            term = spread[pat][:, c * LANES:(c + 1) * LANES] * o_nat[pat][:, c * LANES:(c + 1) * LANES]
            acc = term if acc is None else acc + term
        pieces.append(acc.astype(BF16))
    a = jnp.concatenate(pieces, axis=1)
    y = jnp.dot(a, w_ref[...], preferred_element_type=F32)
    out_ref[...] = h_ref[...] + _rms(y, g_ref[...])


def _attn_out(outs, lses, w_o, h, g, seq, tm):
    m = h.shape[0]
    tiles = seq // tm
    n_pat = len(PATTERN_ORDER)

    def dil_spec(dil, width):
        return pl.BlockSpec((1, dil, tm // dil, width), lambda i: (i // tiles, 0, i % tiles, 0))

    row = pl.BlockSpec((tm, D_MODEL), lambda i: (i, 0))
    expand = (jnp.arange(LANES)[:, None] == jnp.arange(ATT_WIDTH)[None, :] // HEAD_DIM).astype(BF16)
    expand = jnp.concatenate([expand, expand], axis=0)
    order = ATT_ORDERS[1]
    assert all(dil in (1, order) for dil in PATTERN_ORDER)
    assert tm % PERM_TILE == 0
    nat = jnp.arange(PERM_TILE)
    unperm = (jnp.arange(PERM_TILE)[None, :]
              == ((nat % order) * (PERM_TILE // order) + nat // order)[:, None]).astype(BF16)
    return pl.pallas_call(
        functools.partial(_attn_out_kernel, tm=tm),
        grid=(m // tm,),
        in_specs=[dil_spec(dil, ATT_WIDTH) for dil in PATTERN_ORDER]
        + [dil_spec(dil, LANES) for dil in PATTERN_ORDER]
        + [_const_spec(w_o.shape), _const_spec(expand.shape), _const_spec(unperm.shape), row,
           _const_spec(g.shape)],
        out_specs=row,
        out_shape=jax.ShapeDtypeStruct((m, D_MODEL), F32),
        scratch_shapes=[pltpu.VMEM((n_pat, tm, LANES), F32)],
        compiler_params=_cparams(1),
        name="attn_out",
    )(*outs, *lses, w_o, expand, unperm, h, g)


def _proj_out_kernel(a_ref, w_ref, h_ref, g_ref, out_ref):
    y = jnp.dot(a_ref[...].astype(BF16), w_ref[...], preferred_element_type=F32)
    out_ref[...] = h_ref[...] + _rms(y, g_ref[...])


def _proj_out(a, w, h, g, tm):
    m = h.shape[0]
    return pl.pallas_call(
        _proj_out_kernel,
        grid=(m // tm,),
        in_specs=[pl.BlockSpec((tm, a.shape[1]), lambda i: (i, 0)), _const_spec(w.shape),
                  pl.BlockSpec((tm, D_MODEL), lambda i: (i, 0)), _const_spec(g.shape)],
        out_specs=pl.BlockSpec((tm, D_MODEL), lambda i: (i, 0)),
        out_shape=jax.ShapeDtypeStruct((m, D_MODEL), F32),
        compiler_params=_cparams(1),
        name="proj_out",
    )(a, w, h, g)


def kernel(x_prompt, x_sample, state_pool, state_s5, cache_k, cache_v, norm_gains, ab_w_in, ab_pool_w, ab_pool_scale, ab_lambda_re, ab_lambda_im, ab_log_dt, ab_b_re, ab_b_im, ab_c_re, ab_c_im, ab_d, ab_w_glu, ab_b_glu, ab_w_out, c_w_qkv, c_w_o, ffn_w_gate, ffn_w_up, ffn_w_down):
    batch, seq, d = x_prompt.shape
    n_dec, t_dec, _ = x_sample.shape
    mp, ms = batch * seq, n_dec * t_dec
    tm_p = tm_ffn = PROMPT_ROW_TILE
    gains = norm_gains.reshape(norm_gains.shape[0], 4, 1, d)

    hp = x_prompt.reshape(mp, d)
    hs = x_sample.reshape(ms, d)

    g = gains[0]
    w_in = ab_w_in[0].astype(BF16)
    pool_w = ab_pool_w[0].astype(BF16)
    pool_scale = ab_pool_scale[0].reshape(1, POOL_WIDTH)
    lead_p = jnp.zeros((batch, POOL_LEAD, POOL_WIDTH), F32)
    lead_s = jnp.pad(state_pool[0], ((0, 0), (POOL_LEAD - POOL_STATE, 0), (0, 0)))
    ussm_p, ypool_p, pool_tail = _in_proj_pool(x_prompt, g[0], w_in, lead_p, pool_w, pool_scale, MIXER_ROW_TILE, 0)
    (proj_s,) = _norm_matmul(hs, g[0], w_in, ((0, d, 1.0),), (F32,), ms)
    proj_s = proj_s.reshape(n_dec, t_dec, d)
    ypool_s = _pool_mixer(proj_s, lead_s, pool_w, pool_scale, t_dec, PAST_LEN)
    pool_prompt = pool_tail[:, POOL_LEAD - POOL_STATE:][None]
    pool_sample = jnp.concatenate([state_pool[0], proj_s[:, :, :POOL_WIDTH]], axis=1)[:, -POOL_STATE:][None]

    prompt_mats, sample_mats, lam_rows = _s5_matrices(
        ab_lambda_re[0], ab_lambda_im[0], ab_log_dt[0], ab_b_re[0], ab_b_im[0], ab_c_re[0], ab_c_im[0], ab_d[0])
    yssm_p, hre_p, him_p = _s5_prompt(ussm_p, prompt_mats, S5_ROW_TILE)
    u_tb = jnp.swapaxes(proj_s[:, :, POOL_WIDTH:], 0, 1)
    h0 = state_s5[0].reshape(n_dec, SSM_STATES, 2)
    yssm_tb, hre_s, him_s = _s5_sample(u_tb, h0[..., 0], h0[..., 1], sample_mats, lam_rows)
    yssm_s = jnp.swapaxes(yssm_tb, 0, 1)
    s5_prompt = jnp.stack([hre_p[:, 0], him_p[:, 0]], axis=-1).reshape(1, batch, SSM_GROUPS, SSM_STATE, 2)
    s5_sample = jnp.stack([hre_s, him_s], axis=-1).reshape(1, n_dec, SSM_GROUPS, SSM_STATE, 2)

    w_glu = ab_w_glu[0].astype(BF16)
    b_glu = ab_b_glu[0].reshape(1, SSM_WIDTH)
    w_out = ab_w_out[0].astype(BF16)
    hp = _ab_out(ypool_p.reshape(mp, POOL_WIDTH), yssm_p.reshape(mp, SSM_WIDTH), w_glu, b_glu, w_out, hp, g[1], MIXER_ROW_TILE)
    hs = _ab_out(ypool_s.reshape(ms, POOL_WIDTH), yssm_s.reshape(ms, SSM_WIDTH), w_glu, b_glu, w_out, hs, g[1], ms)

    wg, wu, wd = ffn_w_gate[0].astype(BF16), ffn_w_up[0].astype(BF16), ffn_w_down[0].astype(BF16)
    hp = _ffn(hp, g[2], wg, wu, wd, g[3], FFN_ROW_TILE)
    hs = _ffn(hs, g[2], wg, wu, wd, g[3], ms)

    g = gains[1]
    w_qkv = c_w_qkv[0].astype(BF16)
    w_o = c_w_o[0].astype(BF16)
    keep = min(MAX_WINDOW, seq)
    *qkv_p, kt_tail, vt_tail = _qkv_prompt(hp, g[0], w_qkv, batch, seq, keep, tm_p)
    positions_first = lambda a: jnp.transpose(a.reshape(batch, ATT_HEADS, HEAD_DIM, keep), (0, 3, 1, 2))[None]
    k_prompt = positions_first(kt_tail)
    v_prompt = positions_first(vt_tail)
    q_nat, q_res, k_nat, k_res, v_nat, v_res = qkv_p
    outs, lses = [], []
    for q, k, v, parts in ((q_nat, k_nat, v_nat, 1), (q_res, k_res, v_res, MID_RATIO), (q_res, k_res, v_res, 1)):
        o, lse = _attn_prompt(q, k, v, parts)
        outs.append(o)
        lses.append(lse)
    hp = _attn_out(outs, lses, w_o, hp, g[1], seq, tm_p)

    scale = HEAD_DIM ** -0.5
    q_s, k_s, v_s = _norm_matmul(
        hs, g[0], w_qkv,
        ((0, ATT_WIDTH, scale), (ATT_WIDTH, ATT_WIDTH, 1.0), (2 * ATT_WIDTH, ATT_WIDTH, 1.0)),
        (F32, F32, F32), ms)
    k_sample = k_s.reshape(1, n_dec, t_dec, ATT_HEADS, HEAD_DIM)
    v_sample = v_s.reshape(1, n_dec, t_dec, ATT_HEADS, HEAD_DIM)
    head_major = lambda a, rows: jnp.pad(
        jnp.swapaxes(a.reshape(n_dec, t_dec, ATT_HEADS, HEAD_DIM), 1, 2),
        ((0, 0), (0, 0), (0, rows - t_dec), (0, 0)))
    cache_kt = jnp.transpose(cache_k[0], (0, 2, 3, 1))
    cache_vt = jnp.transpose(cache_v[0], (0, 2, 3, 1))
    o_s = _attn_sample(head_major(q_s, SUBLANES), head_major(k_s, LANES), head_major(v_s, LANES),
                       cache_kt, cache_vt, t_dec)
    hs = _proj_out(o_s.reshape(ms, ATT_WIDTH), w_o, hs, g[1], ms)

    wg, wu, wd = ffn_w_gate[1].astype(BF16), ffn_w_up[1].astype(BF16), ffn_w_down[1].astype(BF16)
    hp = _ffn(hp, g[2], wg, wu, wd, g[3], FFN_ROW_TILE)
    hs = _ffn(hs, g[2], wg, wu, wd, g[3], ms)

    return (hp.reshape(batch, seq, d), hs.reshape(n_dec, t_dec, d), pool_prompt, s5_prompt,
            k_prompt, v_prompt, pool_sample, s5_sample, k_sample, v_sample)
```
